```python
import jax
import jax.numpy as jnp
from jax import lax
import numpy as np

D_MODEL = 1024
BATCH = 8
SEQ = 2048
DEPTH = 2

GRID_W = 64
CTX_LEN = 256
BLOCK = 128
HEAD_DIM = 64
ROPE_THETA = 10000.0
EPS = 1e-6
NEG_INF = -1e30

A_Q_HEADS = 8
A_KV_HEADS = 2
A_WINDOW = 128
B_HEADS = 8
B_NOPE = 64
B_ROPE = 32
B_QK = B_NOPE + B_ROPE
B_V = 64
B_Q_RANK = 256
B_KV_RANK = 256
C_HEADS = 4
C_DK = 64
C_DV = 128
D_Q_HEADS = 8
D_KV_HEADS = 2
PEER_HEADS = 8
PEER_N_KEYS = 128
PEER_N_EXPERTS = PEER_N_KEYS * PEER_N_KEYS
PEER_D_KEY = 256
PEER_TOPK = 16
PEER_BLOCK = 128

N_AB = (DEPTH + 1) // 2
N_CD = DEPTH // 2

A_Q_W = A_Q_HEADS * HEAD_DIM
A_KV_W = A_KV_HEADS * HEAD_DIM
AB_SPLITS = (A_Q_W, A_KV_W, A_KV_W, B_Q_RANK, B_KV_RANK, B_ROPE)
AB_IN = sum(AB_SPLITS)
AB_OUT = A_Q_W + B_HEADS * B_V
C_QK_W = C_HEADS * C_DK
C_V_W = C_HEADS * C_DV
D_Q_W = D_Q_HEADS * HEAD_DIM
D_KV_W = D_KV_HEADS * HEAD_DIM
CD_SPLITS = (C_QK_W, C_QK_W, C_V_W, C_V_W, D_Q_W, D_KV_W, D_KV_W)
CD_IN = sum(CD_SPLITS)
CD_OUT = C_V_W + D_Q_W

kernel_name = "hybrid_dit_window_mla_retention_axialgqa_peer"


def rms_norm(x, w):
    xf = x.astype(jnp.float32)
    y = xf * lax.rsqrt(jnp.mean(xf * xf, axis=-1, keepdims=True) + EPS)
    return (y * w.astype(jnp.float32)).astype(x.dtype)


def modulate(x, w, shift, scale):
    return rms_norm(x, w) * (1 + scale) + shift


def split_cols(z, widths):
    offsets = [int(o) for o in np.cumsum(widths)[:-1]]
    return jnp.split(z, offsets, axis=-1)


def grid_positions(n_tokens):
    rows = n_tokens // GRID_W
    row = jnp.repeat(jnp.arange(rows, dtype=jnp.int32), GRID_W)
    col = jnp.tile(jnp.arange(GRID_W, dtype=jnp.int32), rows)
    return row, col


def rope_1d(x, pos):
    half = x.shape[-1] // 2
    freqs = ROPE_THETA ** (-jnp.arange(half, dtype=jnp.float32) / half)
    ang = pos.astype(jnp.float32)[:, None] * freqs[None, :]
    cos = jnp.cos(ang)[:, None, :]
    sin = jnp.sin(ang)[:, None, :]
    xf = x.astype(jnp.float32)
    x1, x2 = xf[..., :half], xf[..., half:]
    return jnp.concatenate([x1 * cos - x2 * sin, x2 * cos + x1 * sin], axis=-1).astype(x.dtype)


def axial_rope(x, row, col):
    d = x.shape[-1] // 2
    return jnp.concatenate([rope_1d(x[..., :d], row), rope_1d(x[..., d:], col)], axis=-1)


def attend(q, k, v, sink=None):
    B, Q, Hq, d = q.shape
    Hkv = k.shape[2]
    G = Hq // Hkv
    qg = q.reshape(B, Q, Hkv, G, d)
    s = jnp.einsum('bqhgd,bkhd->bhgqk', qg, k).astype(jnp.float32) * d ** -0.5
    if sink is not None:
        s_sink = jnp.broadcast_to(sink.astype(jnp.float32).reshape(1, Hkv, G, 1, 1), s.shape[:-1] + (1,))
        p = jax.nn.softmax(jnp.concatenate([s, s_sink], axis=-1), axis=-1)[..., :-1]
    else:
        p = jax.nn.softmax(s, axis=-1)
    o = jnp.einsum('bhgqk,bkhd->bqhgd', p.astype(v.dtype), v)
    return o.reshape(B, Q, Hq * v.shape[-1])


def blocked_attention(q, k, v):
    B, S = q.shape[:2]
    nb = S // BLOCK
    qb = q.reshape((B, nb, BLOCK) + q.shape[2:]).swapaxes(0, 1)
    o = lax.map(lambda qq: attend(qq, k, v), qb)
    return o.swapaxes(0, 1).reshape(B, S, -1)


def window_attention(q, k, v, kc, vc, sink):
    B, S, Hq, d = q.shape
    Hkv = k.shape[2]
    G = Hq // Hkv
    nb = S // BLOCK
    scale = d ** -0.5
    qb = q.reshape(B, nb, BLOCK, Hkv, G, d)
    pad = ((0, 0), (BLOCK, BLOCK), (0, 0), (0, 0))
    kp = jnp.pad(k, pad).reshape(B, nb + 2, BLOCK, Hkv, d)
    vp = jnp.pad(v, pad).reshape(B, nb + 2, BLOCK, Hkv, v.shape[-1])
    kw = jnp.concatenate([kp[:, :-2], kp[:, 1:-1], kp[:, 2:]], axis=2)
    vw = jnp.concatenate([vp[:, :-2], vp[:, 1:-1], vp[:, 2:]], axis=2)
    qi = jnp.arange(BLOCK)[:, None]
    kj = jnp.arange(3 * BLOCK)[None, :]
    kpos = jnp.arange(nb)[:, None, None] * BLOCK + kj[None] - BLOCK
    valid = (jnp.abs(kj - BLOCK - qi)[None] <= A_WINDOW) & (kpos >= 0) & (kpos < S)
    s_loc = jnp.einsum('bnqhgd,bnkhd->bnhgqk', qb, kw).astype(jnp.float32) * scale
    s_loc = jnp.where(valid[None, :, None, None], s_loc, NEG_INF)
    s_ctx = jnp.einsum('bnqhgd,blhd->bnhgql', qb, kc).astype(jnp.float32) * scale
    s_sink = jnp.broadcast_to(sink.astype(jnp.float32).reshape(1, 1, Hkv, G, 1, 1), s_loc.shape[:-1] + (1,))
    p = jax.nn.softmax(jnp.concatenate([s_loc, s_ctx, s_sink], axis=-1), axis=-1).astype(v.dtype)
    nw = 3 * BLOCK
    o = (jnp.einsum('bnhgqk,bnkhd->bnqhgd', p[..., :nw], vw)
         + jnp.einsum('bnhgql,blhd->bnqhgd', p[..., nw:-1], vc))
    return o.reshape(B, S, Hq * v.shape[-1])


def retention_chunked(q, k, v, log_gamma, s0):
    q = q.astype(jnp.float32)
    k = k.astype(jnp.float32)
    v = v.astype(jnp.float32)
    B, S, H, dk = q.shape
    dv = v.shape[-1]
    nc = S // BLOCK
    idx = jnp.arange(BLOCK, dtype=jnp.float32)
    diff = idx[:, None] - idx[None, :]
    lg = log_gamma[:, None, None]
    decay_in = jnp.exp(lg * jnp.maximum(diff, 0.0)) * (diff >= 0)
    q_dec = jnp.exp(log_gamma[:, None] * (idx[None, :] + 1.0))
    k_dec = jnp.exp(log_gamma[:, None] * (BLOCK - 1.0 - idx[None, :]))
    chunk_dec = jnp.exp(log_gamma * BLOCK)
    qc = q.reshape(B, nc, BLOCK, H, dk).transpose(1, 0, 3, 2, 4)
    kc = k.reshape(B, nc, BLOCK, H, dk).transpose(1, 0, 3, 2, 4)
    vc = v.reshape(B, nc, BLOCK, H, dv).transpose(1, 0, 3, 2, 4)

    def step(state, inp):
        qi, ki, vi = inp
        s = jnp.einsum('bhqd,bhkd->bhqk', qi, ki) * decay_in
        inner = jnp.einsum('bhqk,bhkv->bhqv', s, vi)
        cross = jnp.einsum('bhqd,bhdv->bhqv', qi, state) * q_dec[None, :, :, None]
        new_state = (state * chunk_dec[None, :, None, None]
                     + jnp.einsum('bhkd,bhkv->bhdv', ki * k_dec[None, :, :, None], vi))
        return new_state, inner + cross

    s_fin, o = lax.scan(step, s0, (qc, kc, vc))
    return o.transpose(1, 0, 3, 2, 4).reshape(B, S, H, dv), s_fin


def retention_out(o, g, gn_w):
    mu = jnp.mean(o, axis=-1, keepdims=True)
    var = jnp.mean(jnp.square(o - mu), axis=-1, keepdims=True)
    y = ((o - mu) * lax.rsqrt(var + EPS)).reshape(o.shape[0], o.shape[1], C_V_W) * gn_w.astype(jnp.float32)
    return (jax.nn.silu(g.astype(jnp.float32)) * y).astype(g.dtype)


def flip_seq(t):
    return jnp.flip(t, axis=1)


def mixer_ab(h, hc, w_in, w_o, a_qn, a_kn, a_sink, b_qln, b_kvln, b_wuq, b_wukv, b_qn, b_kn,
             row, col, with_ctx_out):
    def project(z, rotate):
        Bz, n = z.shape[:2]
        qa, ka, va, cq, ckv, kr = split_cols(z @ w_in, AB_SPLITS)
        qa = rms_norm(qa.reshape(Bz, n, A_Q_HEADS, HEAD_DIM), a_qn)
        ka = rms_norm(ka.reshape(Bz, n, A_KV_HEADS, HEAD_DIM), a_kn)
        va = va.reshape(Bz, n, A_KV_HEADS, HEAD_DIM)
        qb = rms_norm((rms_norm(cq, b_qln) @ b_wuq).reshape(Bz, n, B_HEADS, B_QK), b_qn)
        kv = (rms_norm(ckv, b_kvln) @ b_wukv).reshape(Bz, n, B_HEADS, B_NOPE + B_V)
        kb = jnp.concatenate([kv[..., :B_NOPE],
                              jnp.broadcast_to(kr[:, :, None, :], (Bz, n, B_HEADS, B_ROPE))], axis=-1)
        kb = rms_norm(kb, b_kn)
        vb = kv[..., B_NOPE:]
        if rotate:
            qa = axial_rope(qa, row, col)
            ka = axial_rope(ka, row, col)
            qb = jnp.concatenate([qb[..., :B_NOPE], axial_rope(qb[..., B_NOPE:], row, col)], axis=-1)
            kb = jnp.concatenate([kb[..., :B_NOPE], axial_rope(kb[..., B_NOPE:], row, col)], axis=-1)
        return qa, ka, va, qb, kb, vb

    qa, ka, va, qb, kb, vb = project(h, True)
    qac, kac, vac, qbc, kbc, vbc = project(hc, False)
    o_a = window_attention(qa, ka, va, kac, vac, a_sink)
    o_b = blocked_attention(qb, jnp.concatenate([kbc, kb], axis=1), jnp.concatenate([vbc, vb], axis=1))
    y = jnp.concatenate([o_a, o_b], axis=-1) @ w_o
    yc = None
    if with_ctx_out:
        yc = jnp.concatenate([attend(qac, kac, vac, a_sink), attend(qbc, kbc, vbc)], axis=-1) @ w_o
    return y, yc


def mixer_cd(h, hc, w_in, w_o, dec_f, dec_b, gn_w, d_qn, d_kn, row, col, with_ctx_out):
    def project(z, rotate):
        Bz, n = z.shape[:2]
        q_r, k_r, v_r, g_r, q_d, k_d, v_d = split_cols(z @ w_in, CD_SPLITS)
        q_r = q_r.reshape(Bz, n, C_HEADS, C_DK)
        k_r = k_r.reshape(Bz, n, C_HEADS, C_DK) * C_DK ** -0.5
        v_r = v_r.reshape(Bz, n, C_HEADS, C_DV)
        q_d = rms_norm(q_d.reshape(Bz, n, D_Q_HEADS, HEAD_DIM), d_qn)
        k_d = rms_norm(k_d.reshape(Bz, n, D_KV_HEADS, HEAD_DIM), d_kn)
        v_d = v_d.reshape(Bz, n, D_KV_HEADS, HEAD_DIM)
        if rotate:
            q_r = axial_rope(q_r, row, col)
            k_r = axial_rope(k_r, row, col)
            q_d = axial_rope(q_d, row, col)
            k_d = axial_rope(k_d, row, col)
        return q_r, k_r, v_r, g_r, q_d, k_d, v_d

    q_r, k_r, v_r, g_r, q_d, k_d, v_d = project(h, True)
    q_rc, k_rc, v_rc, g_rc, q_dc, k_dc, v_dc = project(hc, False)
    lg_f = jax.nn.log_sigmoid(dec_f.astype(jnp.float32))
    lg_b = jax.nn.log_sigmoid(dec_b.astype(jnp.float32))
    zero = jnp.zeros((h.shape[0], C_HEADS, C_DK, C_DV), jnp.float32)
    oc_f, st_f = retention_chunked(q_rc, k_rc, v_rc, lg_f, zero)
    oc_b, st_b = retention_chunked(flip_seq(q_rc), flip_seq(k_rc), flip_seq(v_rc), lg_b, zero)
    o_f, _ = retention_chunked(q_r, k_r, v_r, lg_f, st_f)
    o_b, _ = retention_chunked(flip_seq(q_r), flip_seq(k_r), flip_seq(v_r), lg_b, st_b)
    y_ret = retention_out(o_f + flip_seq(o_b), g_r, gn_w)
    o_d = blocked_attention(q_d, jnp.concatenate([k_dc, k_d], axis=1), jnp.concatenate([v_dc, v_d], axis=1))
    y = jnp.concatenate([y_ret, o_d], axis=-1) @ w_o
    yc = None
    if with_ctx_out:
        yc_ret = retention_out(oc_f + flip_seq(oc_b), g_rc, gn_w)
        yc = jnp.concatenate([yc_ret, attend(q_dc, k_dc, v_dc)], axis=-1) @ w_o
    return y, yc


def peer(h, w_q, sub_keys, u_tab, v_tab):
    T, D = h.shape
    q = (h @ w_q).reshape(T, PEER_HEADS, 2, PEER_D_KEY // 2)
    scores = jnp.einsum('thpd,hpnd->thpn', q, sub_keys).astype(jnp.float32)
    s_top, i_top = lax.top_k(scores, PEER_TOPK)
    cand = s_top[:, :, 0, :, None] + s_top[:, :, 1, None, :]
    cand_idx = i_top[:, :, 0, :, None] * PEER_N_KEYS + i_top[:, :, 1, None, :]
    best, pos = lax.top_k(cand.reshape(T, PEER_HEADS, PEER_TOPK * PEER_TOPK), PEER_TOPK)
    experts = jnp.take_along_axis(cand_idx.reshape(T, PEER_HEADS, PEER_TOPK * PEER_TOPK), pos, axis=-1)
    gates = jax.nn.softmax(best, axis=-1)
    nblk = T // PEER_BLOCK
    hk = PEER_HEADS * PEER_TOPK

    def one_block(args):
        hb, eb, gb = args
        u = u_tab[eb]
        a = jax.nn.gelu(jnp.einsum('cd,ced->ce', hb, u).astype(jnp.float32))
        return jnp.einsum('ce,ced->cd', (gb * a).astype(hb.dtype), v_tab[eb])

    out = lax.map(one_block, (h.reshape(nblk, PEER_BLOCK, D),
                              experts.reshape(nblk, PEER_BLOCK, hk),
                              gates.reshape(nblk, PEER_BLOCK, hk)))
    return out.reshape(T, D)


def setup_inputs(seed: int = 0) -> dict:
    key = jax.random.key(seed)
    ks = iter(jax.random.split(key, 40))

    def normal(shape, scale):
        return jax.random.normal(next(ks), shape, jnp.float32) * scale

    def gain(shape):
        return 1.0 + normal(shape, 0.02)

    base = 1.0 - 2.0 ** (-5.0 - np.arange(C_HEADS))
    decay_logit = jnp.asarray(np.log(base / (1.0 - base)), jnp.float32)
    return {
        "x": normal((BATCH, SEQ, D_MODEL), 1.0),
        "c": normal((BATCH, D_MODEL), 1.0),
        "ctx": normal((BATCH, CTX_LEN, D_MODEL), 1.0),
        "c_ctx": normal((D_MODEL,), 1.0),
        "ada_w": normal((DEPTH, D_MODEL, 6 * D_MODEL), 0.5 * D_MODEL ** -0.5),
        "ada_b": normal((DEPTH, 6 * D_MODEL), 0.02),
        "norm1_w": gain((DEPTH, D_MODEL)),
        "norm2_w": gain((DEPTH, D_MODEL)),
        "ab_w_in": normal((N_AB, D_MODEL, AB_IN), D_MODEL ** -0.5),
        "ab_w_o": normal((N_AB, AB_OUT, D_MODEL), AB_OUT ** -0.5),
        "a_q_norm": gain((N_AB, HEAD_DIM)),
        "a_k_norm": gain((N_AB, HEAD_DIM)),
        "a_sink": normal((N_AB, A_Q_HEADS), 0.5),
        "b_q_lora_norm": gain((N_AB, B_Q_RANK)),
        "b_kv_lora_norm": gain((N_AB, B_KV_RANK)),
        "b_w_uq": normal((N_AB, B_Q_RANK, B_HEADS * B_QK), B_Q_RANK ** -0.5),
        "b_w_ukv": normal((N_AB, B_KV_RANK, B_HEADS * (B_NOPE + B_V)), B_KV_RANK ** -0.5),
        "b_q_norm": gain((N_AB, B_QK)),
        "b_k_norm": gain((N_AB, B_QK)),
        "cd_w_in": normal((N_CD, D_MODEL, CD_IN), D_MODEL ** -0.5),
        "cd_w_o": normal((N_CD, CD_OUT, D_MODEL), CD_OUT ** -0.5),
        "c_decay_fwd": decay_logit + normal((N_CD, C_HEADS), 0.05),
        "c_decay_bwd": decay_logit + normal((N_CD, C_HEADS), 0.05),
        "c_gn_w": gain((N_CD, C_V_W)),
        "d_q_norm": gain((N_CD, HEAD_DIM)),
        "d_k_norm": gain((N_CD, HEAD_DIM)),
        "peer_w_q": normal((DEPTH, D_MODEL, PEER_HEADS * PEER_D_KEY), D_MODEL ** -0.5),
        "peer_keys": normal((DEPTH, PEER_HEADS, 2, PEER_N_KEYS, PEER_D_KEY // 2), (PEER_D_KEY // 2) ** -0.5),
        "peer_u": normal((DEPTH, PEER_N_EXPERTS, D_MODEL), D_MODEL ** -0.5),
        "peer_v": normal((DEPTH, PEER_N_EXPERTS, D_MODEL), 0.1),
    }


def reference(x, c, ctx, c_ctx, ada_w, ada_b, norm1_w, norm2_w, ab_w_in, ab_w_o, a_q_norm, a_k_norm,
              a_sink, b_q_lora_norm, b_kv_lora_norm, b_w_uq, b_w_ukv, b_q_norm, b_k_norm, cd_w_in, cd_w_o,
              c_decay_fwd, c_decay_bwd, c_gn_w, d_q_norm, d_k_norm, peer_w_q, peer_keys, peer_u, peer_v):
    B, S, D = x.shape
    row, col = grid_positions(S)
    xc = ctx
    for l in range(DEPTH):
        last = l == DEPTH - 1
        i = l // 2
        mod = jax.nn.silu(c) @ ada_w[l] + ada_b[l]
        mod_c = jax.nn.silu(c_ctx) @ ada_w[l] + ada_b[l]
        sh1, sc1, g1, sh2, sc2, g2 = jnp.split(mod[:, None, :], 6, axis=-1)
        sh1c, sc1c, g1c, sh2c, sc2c, g2c = jnp.split(mod_c, 6, axis=-1)
        h = modulate(x, norm1_w[l], sh1, sc1)
        hc = modulate(xc, norm1_w[l], sh1c, sc1c)
        if l % 2 == 0:
            y, yc = mixer_ab(h, hc, ab_w_in[i], ab_w_o[i], a_q_norm[i], a_k_norm[i], a_sink[i],
                             b_q_lora_norm[i], b_kv_lora_norm[i], b_w_uq[i], b_w_ukv[i], b_q_norm[i],
                             b_k_norm[i], row, col, not last)
        else:
            y, yc = mixer_cd(h, hc, cd_w_in[i], cd_w_o[i], c_decay_fwd[i], c_decay_bwd[i], c_gn_w[i],
                             d_q_norm[i], d_k_norm[i], row, col, not last)
        x = x + g1 * y
        h2 = modulate(x, norm2_w[l], sh2, sc2)
        if last:
            f = peer(h2.reshape(B * S, D), peer_w_q[l], peer_keys[l], peer_u[l], peer_v[l])
            x = x + g2 * f.reshape(B, S, D)
        else:
            xc = xc + g1c * yc
            h2c = modulate(xc, norm2_w[l], sh2c, sc2c)
            f = peer(jnp.concatenate([h2.reshape(B * S, D), h2c.reshape(-1, D)], axis=0),
                     peer_w_q[l], peer_keys[l], peer_u[l], peer_v[l])
            x = x + g2 * f[:B * S].reshape(B, S, D)
            xc = xc + g2c * f[B * S:].reshape(xc.shape)
    return x
```

```python
import functools

import numpy as np
import jax
import jax.numpy as jnp
from jax import lax
from jax.experimental import pallas as pl
from jax.experimental.pallas import tpu as pltpu

F32 = jnp.float32
BF16 = jnp.bfloat16

GRID_W = 64
ROPE_THETA = 10000.0
EPS = 1e-6
NEG_INF = -1e30
HEAD_DIM = 64
A_Q_HEADS, A_KV_HEADS, A_WINDOW = 8, 2, 128
B_HEADS, B_NOPE, B_ROPE, B_V, B_Q_RANK, B_KV_RANK = 8, 64, 32, 64, 256, 256
B_QK = B_NOPE + B_ROPE
C_HEADS, C_DK, C_DV = 4, 64, 128
D_Q_HEADS, D_KV_HEADS = 8, 2
PEER_HEADS, PEER_N_KEYS, PEER_D_KEY, PEER_TOPK = 8, 128, 256, 16

LANE = 128
SUBLANE = 8
VMEM_LIMIT = 56 * 1024 * 1024

TOK_TILE = 256
CHUNK = 128
PEER_SEL_TILE = 256
PEER_TOK_TILE = 512
PEER_EXP_TILE = 1024


def _cparams(*sem):
    return pltpu.CompilerParams(dimension_semantics=sem, vmem_limit_bytes=VMEM_LIMIT)


def _full(arr):
    nd = arr.ndim
    return pl.BlockSpec(arr.shape, lambda *_: (0,) * nd)


def _pad_cols(w, n_heads, d):
    lead = w.shape[:-1]
    w = w.reshape(lead + (n_heads, d))
    w = jnp.pad(w, [(0, 0)] * len(lead) + [(0, 0), (0, LANE - d)])
    return w.reshape(lead + (n_heads * LANE,))


def _pad_rows(w, n_heads, d):
    n = w.shape[-1]
    w = w.reshape(n_heads, d, n)
    w = jnp.pad(w, [(0, 0), (0, LANE - d), (0, 0)])
    return w.reshape(n_heads * LANE, n)


def _pad_gain(g, d):
    return jnp.pad(g.astype(F32), (0, LANE - d)).reshape(1, LANE)


def _rope_tables(ctx_len, seq, lane_off, d_rot):
    blk = d_rot // 2
    half = blk // 2
    freqs = ROPE_THETA ** (-np.arange(half, dtype=np.float64) / half)
    pos = np.arange(seq)
    total = ctx_len + seq
    cos = np.ones((total, LANE), np.float64)
    sup = np.zeros((total, LANE), np.float64)
    sdn = np.zeros((total, LANE), np.float64)
    for axis, p in enumerate((pos // GRID_W, pos % GRID_W)):
        ang = p[:, None].astype(np.float64) * freqs[None, :]
        c, s = np.cos(ang), np.sin(ang)
        base = lane_off + axis * blk
        cos[ctx_len:, base:base + half] = c
        cos[ctx_len:, base + half:base + blk] = c
        sdn[ctx_len:, base:base + half] = -s
        sup[ctx_len:, base + half:base + blk] = s
    return (jnp.asarray(cos, F32), jnp.asarray(sup, F32), jnp.asarray(sdn, F32)), half


def _rms_rows(x, true_dim):
    return x * lax.rsqrt(jnp.sum(x * x, axis=-1, keepdims=True) * (1.0 / true_dim) + EPS)


def _rope(y, cos, sup, sdn, half):
    return y * cos + pltpu.roll(y, half, 1) * sup + pltpu.roll(y, LANE - half, 1) * sdn


def _ada_kernel(c_ref, w_ref, b_ref, o_ref):
    c = c_ref[...]
    s = c * jax.nn.sigmoid(c)
    o_ref[0] = jnp.dot(s.astype(BF16), w_ref[0].astype(BF16), preferred_element_type=F32) + b_ref[0]


def _ada(c_rows, ada_w, ada_b):
    depth, d, n = ada_w.shape
    rows = c_rows.shape[0]
    tn = 1536
    return pl.pallas_call(
        _ada_kernel,
        grid=(depth, n // tn),
        in_specs=[pl.BlockSpec((rows, d), lambda l, j: (0, 0)),
                  pl.BlockSpec((1, d, tn), lambda l, j: (l, 0, j)),
                  pl.BlockSpec((1, 1, tn), lambda l, j: (l, 0, j))],
        out_specs=pl.BlockSpec((1, rows, tn), lambda l, j: (l, 0, j)),
        out_shape=jax.ShapeDtypeStruct((depth, rows, n), F32),
        compiler_params=_cparams("arbitrary", "arbitrary"),
        name="ada_mod",
    )(c_rows, ada_w, ada_b.reshape(depth, 1, n))


def _mod_spec(layer, chunk, rows, nb, d, tile_axis=1):
    def imap(*ids):
        b, t = ids[0], ids[tile_axis]
        r = jnp.where(t == 0, nb, b)
        return ((layer * rows + r) * 6 + chunk, 0, 0)
    return pl.BlockSpec((1, 1, d), imap)


def _modulate(x, n_ref, sc_ref, sh_ref, d):
    return _rms_rows(x, d) * n_ref[...] * (1.0 + sc_ref[0]) + sh_ref[0]


def _proj0_kernel(x_ref, n1_ref, sh_ref, sc_ref, w_ref, wuq_ref, wuk_ref, wuv_ref,
                  aqn_ref, akn_ref, bqln_ref, bkvln_ref, bqn_ref, bkn_ref,
                  ca_ref, ua_ref, da_ref, cb_ref, ub_ref, db_ref,
                  qa_ref, ka_ref, va_ref, qb_ref, kb_ref, vb_ref, *, d_model, half_a, half_b, offs):
    h = _modulate(x_ref[0], n1_ref, sc_ref, sh_ref, d_model).astype(BF16)
    ca, ua, da = ca_ref[...], ua_ref[...], da_ref[...]
    cb, ub, db = cb_ref[...], ub_ref[...], db_ref[...]
    o_qa, o_ka, o_va, o_cq, o_ckv, o_kr, o_end = offs

    z = jnp.dot(h, w_ref[:, o_qa:o_ka], preferred_element_type=F32)
    for i in range(A_Q_HEADS):
        y = _rms_rows(z[:, i * LANE:(i + 1) * LANE], HEAD_DIM) * aqn_ref[...]
        qa_ref[0, :, i * LANE:(i + 1) * LANE] = _rope(y, ca, ua, da, half_a).astype(BF16)
    z = jnp.dot(h, w_ref[:, o_ka:o_va], preferred_element_type=F32)
    for i in range(A_KV_HEADS):
        y = _rms_rows(z[:, i * LANE:(i + 1) * LANE], HEAD_DIM) * akn_ref[...]
        ka_ref[0, :, i * LANE:(i + 1) * LANE] = _rope(y, ca, ua, da, half_a).astype(BF16)
    va_ref[0] = jnp.dot(h, w_ref[:, o_va:o_cq], preferred_element_type=F32).astype(BF16)

    cq = jnp.dot(h, w_ref[:, o_cq:o_ckv], preferred_element_type=F32)
    cq = (_rms_rows(cq, B_Q_RANK) * bqln_ref[...]).astype(BF16)
    z = jnp.dot(cq, wuq_ref[...], preferred_element_type=F32)
    for i in range(B_HEADS):
        y = _rms_rows(z[:, i * LANE:(i + 1) * LANE], B_QK) * bqn_ref[...]
        qb_ref[0, :, i * LANE:(i + 1) * LANE] = _rope(y, cb, ub, db, half_b).astype(BF16)

    ckv = jnp.dot(h, w_ref[:, o_ckv:o_kr], preferred_element_type=F32)
    ckv = (_rms_rows(ckv, B_KV_RANK) * bkvln_ref[...]).astype(BF16)
    kr = jnp.dot(h, w_ref[:, o_kr:o_end], preferred_element_type=F32)
    z = jnp.dot(ckv, wuk_ref[...], preferred_element_type=F32)
    for i in range(B_HEADS):
        y = _rms_rows(z[:, i * LANE:(i + 1) * LANE] + kr, B_QK) * bkn_ref[...]
        kb_ref[0, :, i * LANE:(i + 1) * LANE] = _rope(y, cb, ub, db, half_b).astype(BF16)
    vb_ref[0] = jnp.dot(ckv, wuv_ref[...], preferred_element_type=F32).astype(BF16)


def _proj0(x_all, mods, rows, norm1, w_in, b_wuq, b_wukv, a_qn, a_kn, b_qln, b_kvln, b_qn, b_kn, tabs_a, tabs_b,
           half_a, half_b):
    nb, total, d = x_all.shape
    tm = TOK_TILE
    wq, wk, wv, wcq, wckv, wkr = jnp.split(
        w_in, np.cumsum([A_Q_HEADS * HEAD_DIM, A_KV_HEADS * HEAD_DIM, A_KV_HEADS * HEAD_DIM, B_Q_RANK, B_KV_RANK])
        .tolist(), axis=1)
    kr_pad = jnp.pad(wkr, ((0, 0), (B_NOPE, LANE - B_QK)))
    parts = [_pad_cols(wq, A_Q_HEADS, HEAD_DIM), _pad_cols(wk, A_KV_HEADS, HEAD_DIM),
             _pad_cols(wv, A_KV_HEADS, HEAD_DIM), wcq, wckv, kr_pad]
    offs = tuple(int(o) for o in np.cumsum([0] + [p.shape[1] for p in parts]))
    w_all = jnp.concatenate(parts, axis=1).astype(BF16)
    wuq = _pad_cols(b_wuq, B_HEADS, B_QK).astype(BF16)
    wukv = b_wukv.reshape(B_KV_RANK, B_HEADS, B_NOPE + B_V)
    wuk = _pad_cols(wukv[..., :B_NOPE].reshape(B_KV_RANK, -1), B_HEADS, B_NOPE).astype(BF16)
    wuv = _pad_cols(wukv[..., B_NOPE:].reshape(B_KV_RANK, -1), B_HEADS, B_V).astype(BF16)
    consts = [w_all, wuq, wuk, wuv, _pad_gain(a_qn, HEAD_DIM), _pad_gain(a_kn, HEAD_DIM),
              b_qln.astype(F32).reshape(1, -1), b_kvln.astype(F32).reshape(1, -1),
              _pad_gain(b_qn, B_QK), _pad_gain(b_kn, B_QK)]
    tab_spec = pl.BlockSpec((tm, LANE), lambda b, t: (t, 0))
    wide = lambda nh: pl.BlockSpec((1, tm, nh * LANE), lambda b, t: (b, t, 0))
    shp = lambda nh: jax.ShapeDtypeStruct((nb, total, nh * LANE), BF16)
    return pl.pallas_call(
        functools.partial(_proj0_kernel, d_model=d, half_a=half_a, half_b=half_b, offs=offs),
        grid=(nb, total // tm),
        in_specs=[pl.BlockSpec((1, tm, d), lambda b, t: (b, t, 0)), _full(norm1),
                  _mod_spec(0, 0, rows, nb, d), _mod_spec(0, 1, rows, nb, d)]
                 + [_full(c) for c in consts] + [tab_spec] * 6,
        out_specs=[wide(A_Q_HEADS), wide(A_KV_HEADS), wide(A_KV_HEADS), wide(B_HEADS), wide(B_HEADS), wide(B_HEADS)],
        out_shape=[shp(A_Q_HEADS), shp(A_KV_HEADS), shp(A_KV_HEADS), shp(B_HEADS), shp(B_HEADS), shp(B_HEADS)],
        compiler_params=_cparams("arbitrary", "arbitrary"),
        name="proj0",
    )(x_all, norm1, mods, mods, *consts, *tabs_a, *tabs_b)


def _attn_dense_kernel(q_ref, k_ref, v_ref, o_ref, *, scale, ctx_len, tq):
    qt = pl.program_id(2)
    s = lax.dot_general(q_ref[0], k_ref[0], (((1,), (1,)), ((), ())), preferred_element_type=F32) * scale
    kpos = lax.broadcasted_iota(jnp.int32, s.shape, 1)
    s = jnp.where(jnp.logical_or(kpos < ctx_len, qt * tq >= ctx_len), s, NEG_INF)
    m = jnp.max(s, axis=-1, keepdims=True)
    p = jnp.exp(s - m)
    den = jnp.sum(p, axis=-1, keepdims=True)
    o = jnp.dot(p.astype(BF16), v_ref[0], preferred_element_type=F32)
    o_ref[0] = (o / den).astype(BF16)


def _attn_dense(q, k, v, *, scale, ctx_len):
    nb, total, qw = q.shape
    hq, hkv = qw // LANE, k.shape[2] // LANE
    grp = hq // hkv
    tq = TOK_TILE
    assert ctx_len == tq
    return pl.pallas_call(
        functools.partial(_attn_dense_kernel, scale=scale, ctx_len=ctx_len, tq=tq),
        grid=(nb, hq, total // tq),
        in_specs=[pl.BlockSpec((1, tq, LANE), lambda b, h, t: (b, t, h)),
                  pl.BlockSpec((1, total, LANE), lambda b, h, t: (b, 0, h // grp)),
                  pl.BlockSpec((1, total, LANE), lambda b, h, t: (b, 0, h // grp))],
        out_specs=pl.BlockSpec((1, tq, LANE), lambda b, h, t: (b, t, h)),
        out_shape=jax.ShapeDtypeStruct(q.shape, BF16),
        compiler_params=_cparams("arbitrary", "arbitrary", "arbitrary"),
        name="attn_dense",
    )(q, k, v)


def _attn_window_kernel(q_ref, k_ref, v_ref, sink_ref, o_ref, *, scale, ctx_len, tq, total, window):
    qt = pl.program_id(2)
    q = q_ref[0]
    slab = 3 * tq
    start = pl.multiple_of(jnp.clip((qt - 1) * tq, ctx_len, total - slab), tq)
    nt = (((1,), (1,)), ((), ()))
    s_c = lax.dot_general(q, k_ref[0, 0:ctx_len, :], nt, preferred_element_type=F32) * scale
    s_l = lax.dot_general(q, k_ref[0, pl.ds(start, slab), :], nt, preferred_element_type=F32) * scale
    qpos = qt * tq + lax.broadcasted_iota(jnp.int32, s_l.shape, 0)
    kpos = start + lax.broadcasted_iota(jnp.int32, s_l.shape, 1)
    ok = jnp.logical_and(jnp.abs(qpos - kpos) <= window, qt * tq >= ctx_len)
    s_l = jnp.where(ok, s_l, NEG_INF)
    sink = sink_ref[0][:, 0:1]
    m = jnp.maximum(jnp.maximum(jnp.max(s_c, axis=-1, keepdims=True), jnp.max(s_l, axis=-1, keepdims=True)), sink)
    p_c = jnp.exp(s_c - m)
    p_l = jnp.exp(s_l - m)
    den = jnp.sum(p_c, axis=-1, keepdims=True) + jnp.sum(p_l, axis=-1, keepdims=True) + jnp.exp(sink - m)
    o = (jnp.dot(p_c.astype(BF16), v_ref[0, 0:ctx_len, :], preferred_element_type=F32)
         + jnp.dot(p_l.astype(BF16), v_ref[0, pl.ds(start, slab), :], preferred_element_type=F32))
    o_ref[0] = (o / den).astype(BF16)


def _attn_window(q, k, v, sink, *, scale, ctx_len):
    nb, total, qw = q.shape
    hq, hkv = qw // LANE, k.shape[2] // LANE
    grp = hq // hkv
    tq = CHUNK
    sink_rows = jnp.broadcast_to(sink.astype(F32).reshape(hq, 1, 1), (hq, 1, LANE))
    return pl.pallas_call(
        functools.partial(_attn_window_kernel, scale=scale, ctx_len=ctx_len, tq=tq, total=total, window=A_WINDOW),
        grid=(nb, hq, total // tq),
        in_specs=[pl.BlockSpec((1, tq, LANE), lambda b, h, t: (b, t, h)),
                  pl.BlockSpec((1, total, LANE), lambda b, h, t: (b, 0, h // grp)),
                  pl.BlockSpec((1, total, LANE), lambda b, h, t: (b, 0, h // grp)),
                  pl.BlockSpec((1, 1, LANE), lambda b, h, t: (h, 0, 0))],
        out_specs=pl.BlockSpec((1, tq, LANE), lambda b, h, t: (b, t, h)),
        out_shape=jax.ShapeDtypeStruct(q.shape, BF16),
        compiler_params=_cparams("arbitrary", "arbitrary", "arbitrary"),
        name="attn_window",
    )(q, k, v, sink_rows)


def _out0_kernel(oa_ref, ob_ref, woa_ref, wob_ref, x_ref, g1_ref, n2_ref, sh2_ref, sc2_ref, xn_ref, h2_ref, *, d_model):
    y = (jnp.dot(oa_ref[0], woa_ref[...], preferred_element_type=F32)
         + jnp.dot(ob_ref[0], wob_ref[...], preferred_element_type=F32))
    xn = x_ref[0] + g1_ref[0] * y
    xn_ref[0] = xn
    h2_ref[0] = _modulate(xn, n2_ref, sc2_ref, sh2_ref, d_model).astype(BF16)


def _out0(oa, ob, w_o, x_all, mods, rows, norm2):
    nb, total, d = x_all.shape
    tm = TOK_TILE
    woa = _pad_rows(w_o[:A_Q_HEADS * HEAD_DIM], A_Q_HEADS, HEAD_DIM).astype(BF16)
    wob = _pad_rows(w_o[A_Q_HEADS * HEAD_DIM:], B_HEADS, B_V).astype(BF16)
    tile = lambda w: pl.BlockSpec((1, tm, w), lambda b, t: (b, t, 0))
    return pl.pallas_call(
        functools.partial(_out0_kernel, d_model=d),
        grid=(nb, total // tm),
        in_specs=[tile(oa.shape[2]), tile(ob.shape[2]), _full(woa), _full(wob), tile(d),
                  _mod_spec(0, 2, rows, nb, d), _full(norm2), _mod_spec(0, 3, rows, nb, d),
                  _mod_spec(0, 4, rows, nb, d)],
        out_specs=[tile(d), tile(d)],
        out_shape=[jax.ShapeDtypeStruct((nb, total, d), F32), jax.ShapeDtypeStruct((nb, total, d), BF16)],
        compiler_params=_cparams("arbitrary", "arbitrary"),
        name="out_proj0",
    )(oa, ob, woa, wob, x_all, mods, norm2, mods, mods)


def _top_rows(sc, rowf, k):
    n = sc.shape[0]
    vals, idxs = [], []
    work = sc
    for _ in range(k):
        m = jnp.max(work, axis=0, keepdims=True)
        idx = jnp.min(jnp.where(work == m, rowf, float(n)), axis=0, keepdims=True)
        vals.append(m)
        idxs.append(idx)
        work = jnp.where(rowf == idx, -jnp.inf, work)
    return vals, idxs


def _peer_select_kernel(h_ref, wq_ref, keys_ref, cc_ref, e0_ref, rb_ref, e1_ref, q_sc, *, n_heads, topk):
    nk = PEER_N_KEYS
    ts = h_ref.shape[0]
    q_sc[...] = lax.dot_general(wq_ref[...], h_ref[...], (((1,), (1,)), ((), ())), preferred_element_type=F32)
    rowf = lax.broadcasted_iota(jnp.int32, (nk, ts), 0).astype(F32)
    row16 = lax.broadcasted_iota(jnp.int32, (topk, ts), 0).astype(F32)
    slab_rows = [topk] + [SUBLANE] * (topk - 1)
    n_cand = sum(slab_rows)
    ci = lax.broadcasted_iota(jnp.int32, (n_cand, ts), 0)
    rest = ci - topk
    flat = jnp.where(ci < topk, ci, (1 + (rest >> 3)) * topk + (rest & 7)).astype(F32)

    def head_body(hd, carry):
        tops = []
        for p in range(2):
            hp = hd * 2 + p
            qhp = q_sc[pl.ds(pl.multiple_of(hp * nk, nk), nk), :].astype(BF16)
            sc = jnp.dot(keys_ref[hp], qhp, preferred_element_type=F32)
            vals, idxs = _top_rows(sc, rowf, topk)
            tops.append((sc, vals, idxs))
        (sc0, v0, i0), (sc1, v1, i1) = tops
        s1 = jnp.zeros((topk, ts), F32)
        for k in range(topk):
            s1 = jnp.where(row16 == float(k), v1[k], s1)
        cand = jnp.concatenate([v0[k1] + s1[0:slab_rows[k1], :] for k1 in range(topk)], axis=0)
        cnt = jnp.zeros((topk, ts), F32)
        zsum = jnp.zeros((1, ts), F32)
        best0 = None
        work = cand
        for k in range(topk):
            m = jnp.max(work, axis=0, keepdims=True)
            idx = jnp.min(jnp.where(work == m, flat, 1e9), axis=0, keepdims=True)
            work = jnp.where(flat == idx, -jnp.inf, work)
            best0 = m if best0 is None else best0
            zsum = zsum + jnp.exp(m - best0)
            cnt = cnt + jnp.where(row16 == jnp.floor(idx * (1.0 / topk)), 1.0, 0.0)
        cc = jnp.zeros((nk, ts), F32)
        rb = jnp.full((nk, ts), 99.0, F32)
        for k in range(topk):
            ck = jnp.sum(jnp.where(row16 == float(k), cnt, 0.0), axis=0, keepdims=True)
            cc = jnp.where(rowf == i0[k], ck, cc)
            rb = jnp.where(rowf == i1[k], float(k), rb)
        cc_ref[hd] = cc
        rb_ref[hd] = rb
        e0_ref[hd] = jnp.exp(sc0 - v0[0])
        e1_ref[hd] = jnp.exp(sc1 - v1[0]) / zsum
        return carry

    lax.fori_loop(0, n_heads, head_body, 0)


def _peer_select(h2, w_q, keys):
    t, d = h2.shape
    ts = PEER_SEL_TILE
    nh, nk = PEER_HEADS, PEER_N_KEYS
    wq_t = w_q.T.astype(BF16)
    keys2 = keys.reshape(nh * 2, nk, PEER_D_KEY // 2).astype(BF16)
    out = jax.ShapeDtypeStruct((nh, nk, t), F32)
    ospec = pl.BlockSpec((nh, nk, ts), lambda i: (0, 0, i))
    return pl.pallas_call(
        functools.partial(_peer_select_kernel, n_heads=nh, topk=PEER_TOPK),
        grid=(t // ts,),
        in_specs=[pl.BlockSpec((ts, d), lambda i: (i, 0)), _full(wq_t), _full(keys2)],
        out_specs=[ospec] * 4,
        out_shape=[out] * 4,
        scratch_shapes=[pltpu.VMEM((wq_t.shape[0], ts), F32)],
        compiler_params=_cparams("arbitrary"),
        name="peer_select",
    )(h2, wq_t, keys2)


def _peer_apply_kernel(*refs, n_heads, final):
    if final:
        h_ref, u_ref, vt_ref, cc_ref, e0_ref, rb_ref, e1_ref, x_ref, g_ref, o_ref, acc_ref, g_sc, p_sc = refs
    else:
        h_ref, u_ref, vt_ref, cc_ref, e0_ref, rb_ref, e1_ref, o_ref, acc_ref, g_sc, p_sc = refs
    et = pl.program_id(1)
    nk = PEER_N_KEYS
    rows_per_tile = cc_ref.shape[1]

    @pl.when(et == 0)
    def _():
        acc_ref[...] = jnp.zeros_like(acc_ref)

    at = lax.dot_general(u_ref[...], h_ref[...], (((1,), (1,)), ((), ())), preferred_element_type=F32)
    g_sc[...] = jax.nn.gelu(at)

    def row_body(ii, carry):
        w = jnp.zeros((nk, h_ref.shape[0]), F32)
        for hd in range(n_heads):
            ccr = cc_ref[hd, pl.ds(ii, 1), :]
            e0r = e0_ref[hd, pl.ds(ii, 1), :]
            w = w + jnp.where(rb_ref[hd] < ccr, e1_ref[hd], 0.0) * e0r
        r0 = pl.multiple_of(ii * nk, nk)
        p_sc[pl.ds(r0, nk), :] = (w * g_sc[pl.ds(r0, nk), :]).astype(BF16)
        return carry

    lax.fori_loop(0, rows_per_tile, row_body, 0)
    acc_ref[...] += jnp.dot(vt_ref[...], p_sc[...], preferred_element_type=F32)

    @pl.when(et == pl.num_programs(1) - 1)
    def _():
        f = acc_ref[...].T
        if final:
            o_ref[...] = x_ref[...] + g_ref[0] * f
        else:
            o_ref[...] = f


def _peer_apply(h2, sel, u_tab, v_tab, x=None, mods=None, mod_index=None, tokens_per_batch=None):
    t, d = h2.shape
    cc, e0, rb, e1 = sel
    nh, nk = PEER_HEADS, PEER_N_KEYS
    tt, te = PEER_TOK_TILE, PEER_EXP_TILE
    ti = te // nk
    n_exp = u_tab.shape[0]
    u = u_tab.astype(BF16)
    vt = v_tab.T.astype(BF16)
    final = x is not None
    row_spec = pl.BlockSpec((nh, ti, tt), lambda i, e: (0, e, i))
    col_spec = pl.BlockSpec((nh, nk, tt), lambda i, e: (0, 0, i))
    in_specs = [pl.BlockSpec((tt, d), lambda i, e: (i, 0)),
                pl.BlockSpec((te, d), lambda i, e: (e, 0)),
                pl.BlockSpec((d, te), lambda i, e: (0, e)),
                row_spec, row_spec, col_spec, col_spec]
    args = [h2, u, vt, cc, e0, rb, e1]
    if final:
        per = tokens_per_batch // tt
        in_specs += [pl.BlockSpec((tt, d), lambda i, e: (i, 0)),
                     pl.BlockSpec((1, 1, d), lambda i, e: (mod_index(i // per), 0, 0))]
        args += [x, mods]
    return pl.pallas_call(
        functools.partial(_peer_apply_kernel, n_heads=nh, final=final),
        grid=(t // tt, n_exp // te),
        in_specs=in_specs,
        out_specs=pl.BlockSpec((tt, d), lambda i, e: (i, 0)),
        out_shape=jax.ShapeDtypeStruct((t, d), F32),
        scratch_shapes=[pltpu.VMEM((d, tt), F32), pltpu.VMEM((te, tt), F32), pltpu.VMEM((te, tt), BF16)],
        compiler_params=_cparams("arbitrary", "arbitrary"),
        name="peer_apply_final" if final else "peer_apply",
    )(*args)


def _proj1_kernel(x_ref, f_ref, g2_ref, n1_ref, sh_ref, sc_ref, w_ref, dqn_ref, dkn_ref, ca_ref, ua_ref, da_ref,
                  xn_ref, qr_ref, kr_ref, vr_ref, gr_ref, qd_ref, kd_ref, vd_ref, *, d_model, half_a, offs):
    xn = x_ref[0] + g2_ref[0] * f_ref[0]
    xn_ref[0] = xn
    h = _modulate(xn, n1_ref, sc_ref, sh_ref, d_model).astype(BF16)
    ca, ua, da = ca_ref[...], ua_ref[...], da_ref[...]
    o_qr, o_kr, o_vr, o_gr, o_qd, o_kd, o_vd, o_end = offs

    z = jnp.dot(h, w_ref[:, o_qr:o_kr], preferred_element_type=F32)
    for i in range(C_HEADS):
        qr_ref[0, :, i * LANE:(i + 1) * LANE] = _rope(z[:, i * LANE:(i + 1) * LANE], ca, ua, da, half_a).astype(BF16)
    z = jnp.dot(h, w_ref[:, o_kr:o_vr], preferred_element_type=F32) * (C_DK ** -0.5)
    for i in range(C_HEADS):
        kr_ref[0, :, i * LANE:(i + 1) * LANE] = _rope(z[:, i * LANE:(i + 1) * LANE], ca, ua, da, half_a).astype(BF16)
    vr_ref[0] = jnp.dot(h, w_ref[:, o_vr:o_gr], preferred_element_type=F32).astype(BF16)
    gr_ref[0] = jnp.dot(h, w_ref[:, o_gr:o_qd], preferred_element_type=F32).astype(BF16)
    z = jnp.dot(h, w_ref[:, o_qd:o_kd], preferred_element_type=F32)
    for i in range(D_Q_HEADS):
        y = _rms_rows(z[:, i * LANE:(i + 1) * LANE], HEAD_DIM) * dqn_ref[...]
        qd_ref[0, :, i * LANE:(i + 1) * LANE] = _rope(y, ca, ua, da, half_a).astype(BF16)
    z = jnp.dot(h, w_ref[:, o_kd:o_vd], preferred_element_type=F32)
    for i in range(D_KV_HEADS):
        y = _rms_rows(z[:, i * LANE:(i + 1) * LANE], HEAD_DIM) * dkn_ref[...]
        kd_ref[0, :, i * LANE:(i + 1) * LANE] = _rope(y, ca, ua, da, half_a).astype(BF16)
    vd_ref[0] = jnp.dot(h, w_ref[:, o_vd:o_end], preferred_element_type=F32).astype(BF16)


def _proj1(x_all, f_all, mods, rows, norm1, w_in, d_qn, d_kn, tabs_a, half_a):
    nb, total, d = x_all.shape
    tm = TOK_TILE
    cqk, cv = C_HEADS * C_DK, C_HEADS * C_DV
    wqr, wkr, wvr, wgr, wqd, wkd, wvd = jnp.split(
        w_in, np.cumsum([cqk, cqk, cv, cv, D_Q_HEADS * HEAD_DIM, D_KV_HEADS * HEAD_DIM]).tolist(), axis=1)
    parts = [_pad_cols(wqr, C_HEADS, C_DK), _pad_cols(wkr, C_HEADS, C_DK), wvr, wgr,
             _pad_cols(wqd, D_Q_HEADS, HEAD_DIM), _pad_cols(wkd, D_KV_HEADS, HEAD_DIM),
             _pad_cols(wvd, D_KV_HEADS, HEAD_DIM)]
    offs = tuple(int(o) for o in np.cumsum([0] + [p.shape[1] for p in parts]))
    w_all = jnp.concatenate(parts, axis=1).astype(BF16)
    consts = [w_all, _pad_gain(d_qn, HEAD_DIM), _pad_gain(d_kn, HEAD_DIM)]
    tab_spec = pl.BlockSpec((tm, LANE), lambda b, t: (t, 0))
    tile = lambda w: pl.BlockSpec((1, tm, w), lambda b, t: (b, t, 0))
    widths = [p.shape[1] for p in parts]
    return pl.pallas_call(
        functools.partial(_proj1_kernel, d_model=d, half_a=half_a, offs=offs),
        grid=(nb, total // tm),
        in_specs=[tile(d), tile(d), _mod_spec(0, 5, rows, nb, d), _full(norm1),
                  _mod_spec(1, 0, rows, nb, d), _mod_spec(1, 1, rows, nb, d)]
                 + [_full(c) for c in consts] + [tab_spec] * 3,
        out_specs=[tile(d)] + [tile(w) for w in widths],
        out_shape=[jax.ShapeDtypeStruct((nb, total, d), F32)]
                  + [jax.ShapeDtypeStruct((nb, total, w), BF16) for w in widths],
        compiler_params=_cparams("arbitrary", "arbitrary"),
        name="proj1",
    )(x_all, f_all, mods, norm1, mods, mods, *consts, *tabs_a)


def _retention_kernel(lg_ref, qf_ref, kf_ref, vf_ref, qb_ref, kb_ref, vb_ref, of_ref, ob_ref, st_ref, *, n_heads):
    step = pl.program_id(1)

    @pl.when(step == 0)
    def _():
        st_ref[...] = jnp.zeros_like(st_ref)

    c = CHUNK
    ri = lax.broadcasted_iota(jnp.int32, (c, LANE), 0).astype(F32)
    diff = ri - lax.broadcasted_iota(jnp.int32, (c, LANE), 1).astype(F32)
    nt = (((1,), (1,)), ((), ()))
    for d, (q_ref, k_ref, v_ref, o_ref) in enumerate(((qf_ref, kf_ref, vf_ref, of_ref),
                                                      (qb_ref, kb_ref, vb_ref, ob_ref))):
        for hd in range(n_heads):
            lg = lg_ref[d, hd]
            sl = slice(hd * LANE, (hd + 1) * LANE)
            q, k, v = q_ref[0, :, sl], k_ref[0, :, sl], v_ref[0, :, sl]
            if d == 0:
                dec = jnp.where(diff >= 0, jnp.exp(lg * jnp.maximum(diff, 0.0)), 0.0)
                q_dec = jnp.exp(lg * (ri + 1.0))
                k_dec = jnp.exp(lg * (c - 1.0 - ri))
            else:
                dec = jnp.where(diff <= 0, jnp.exp(lg * jnp.maximum(-diff, 0.0)), 0.0)
                q_dec = jnp.exp(lg * (c - ri))
                k_dec = jnp.exp(lg * ri)
            s = lax.dot_general(q, k, nt, preferred_element_type=F32) * dec
            inner = jnp.dot(s.astype(BF16), v, preferred_element_type=F32)
            st = st_ref[d, hd]
            cross = jnp.dot(q, st.astype(BF16), preferred_element_type=F32) * q_dec
            o_ref[0, :, sl] = inner + cross
            kd_t = (k.astype(F32) * k_dec).T.astype(BF16)
            st_ref[d, hd] = st * jnp.exp(lg * c) + jnp.dot(kd_t, v, preferred_element_type=F32)


def _retention(qr, kr, vr, lg, ctx_len):
    nb, total, w = qr.shape
    nh = w // LANE
    c = CHUNK
    nc, nctx = total // c, ctx_len // c
    fwd = pl.BlockSpec((1, c, w), lambda b, s: (b, s, 0))

    def bmap(b, s):
        return (b, jnp.where(s < nctx, nctx - 1 - s, nc - 1 - (s - nctx)), 0)

    bwd = pl.BlockSpec((1, c, w), bmap)
    out = jax.ShapeDtypeStruct((nb, total, w), F32)
    return pl.pallas_call(
        functools.partial(_retention_kernel, n_heads=nh),
        grid=(nb, nc),
        in_specs=[pl.BlockSpec(memory_space=pltpu.SMEM), fwd, fwd, fwd, bwd, bwd, bwd],
        out_specs=[fwd, bwd],
        out_shape=[out, out],
        scratch_shapes=[pltpu.VMEM((2, nh, LANE, LANE), F32)],
        compiler_params=_cparams("arbitrary", "arbitrary"),
        name="retention",
    )(lg, qr, kr, vr, qr, kr, vr)


def _out1_kernel(of_ref, ob_ref, gr_ref, gn_ref, od_ref, wor_ref, wod_ref, x_ref, g1_ref, n2_ref, sh2_ref, sc2_ref,
                 xn_ref, h2_ref, *, d_model, n_heads):
    o = of_ref[0] + ob_ref[0]
    g = gr_ref[0].astype(F32)
    gate = g * jax.nn.sigmoid(g)
    gn = gn_ref[...]
    ys = []
    for hd in range(n_heads):
        sl = slice(hd * LANE, (hd + 1) * LANE)
        oh = o[:, sl]
        mu = jnp.mean(oh, axis=-1, keepdims=True)
        var = jnp.mean(jnp.square(oh - mu), axis=-1, keepdims=True)
        ys.append((gate[:, sl] * ((oh - mu) * lax.rsqrt(var + EPS) * gn[:, sl])).astype(BF16))
    y_ret = jnp.concatenate(ys, axis=1)
    y = (jnp.dot(y_ret, wor_ref[...], preferred_element_type=F32)
         + jnp.dot(od_ref[0], wod_ref[...], preferred_element_type=F32))
    xn = x_ref[0] + g1_ref[0] * y
    xn_ref[0] = xn
    h2_ref[0] = _modulate(xn, n2_ref, sc2_ref, sh2_ref, d_model).astype(BF16)


def _out1(o_f, o_b, g_r, gn_w, o_d, w_o, x_all, mods, rows, norm2, ctx_len):
    nb, total, d = x_all.shape
    tm = TOK_TILE
    skip = ctx_len // tm
    seq = total - ctx_len
    wor = w_o[:C_HEADS * C_DV].astype(BF16)
    wod = _pad_rows(w_o[C_HEADS * C_DV:], D_Q_HEADS, HEAD_DIM).astype(BF16)
    gn = gn_w.astype(F32).reshape(1, -1)
    tile_in = lambda w: pl.BlockSpec((1, tm, w), lambda b, t: (b, t + skip, 0))
    tile_out = pl.BlockSpec((1, tm, d), lambda b, t: (b, t, 0))
    mod = lambda chunk: pl.BlockSpec((1, 1, d), lambda b, t: ((rows + b) * 6 + chunk, 0, 0))
    return pl.pallas_call(
        functools.partial(_out1_kernel, d_model=d, n_heads=C_HEADS),
        grid=(nb, seq // tm),
        in_specs=[tile_in(o_f.shape[2]), tile_in(o_b.shape[2]), tile_in(g_r.shape[2]), _full(gn),
                  tile_in(o_d.shape[2]), _full(wor), _full(wod), tile_in(d), mod(2), _full(norm2), mod(3), mod(4)],
        out_specs=[tile_out, tile_out],
        out_shape=[jax.ShapeDtypeStruct((nb, seq, d), F32), jax.ShapeDtypeStruct((nb, seq, d), BF16)],
        compiler_params=_cparams("arbitrary", "arbitrary"),
        name="out_proj1",
    )(o_f, o_b, g_r, gn, o_d, wor, wod, x_all, mods, norm2, mods, mods)


def kernel(x, c, ctx, c_ctx, ada_w, ada_b, norm1_w, norm2_w, ab_w_in, ab_w_o, a_q_norm, a_k_norm, a_sink,
           b_q_lora_norm, b_kv_lora_norm, b_w_uq, b_w_ukv, b_q_norm, b_k_norm, cd_w_in, cd_w_o, c_decay_fwd,
           c_decay_bwd, c_gn_w, d_q_norm, d_k_norm, peer_w_q, peer_keys, peer_u, peer_v):
    nb, seq, d = x.shape
    ctx_len = ctx.shape[1]
    total = ctx_len + seq
    assert ctx_len == TOK_TILE and seq % TOK_TILE == 0 and seq % GRID_W == 0

    rows = -(-(nb + 1) // SUBLANE) * SUBLANE
    c_rows = jnp.concatenate([c, c_ctx[None, :], jnp.zeros((rows - nb - 1, d), c.dtype)], axis=0).astype(F32)
    mods = _ada(c_rows, ada_w, ada_b).reshape(-1, 1, d)

    tabs_a, half_a = _rope_tables(ctx_len, seq, 0, HEAD_DIM)
    tabs_b, half_b = _rope_tables(ctx_len, seq, B_NOPE, B_ROPE)
    n1 = norm1_w.astype(F32).reshape(-1, 1, d)
    n2 = norm2_w.astype(F32).reshape(-1, 1, d)

    x_all = jnp.concatenate([ctx, x], axis=1).astype(F32)
    qa, ka, va, qb, kb, vb = _proj0(x_all, mods, rows, n1[0], ab_w_in[0], b_w_uq[0], b_w_ukv[0], a_q_norm[0],
                                    a_k_norm[0], b_q_lora_norm[0], b_kv_lora_norm[0], b_q_norm[0], b_k_norm[0],
                                    tabs_a, tabs_b, half_a, half_b)
    o_a = _attn_window(qa, ka, va, a_sink[0], scale=HEAD_DIM ** -0.5, ctx_len=ctx_len)
    o_b = _attn_dense(qb, kb, vb, scale=B_QK ** -0.5, ctx_len=ctx_len)
    x_all, h2 = _out0(o_a, o_b, ab_w_o[0], x_all, mods, rows, n2[0])
    h2 = h2.reshape(nb * total, d)
    sel = _peer_select(h2, peer_w_q[0], peer_keys[0])
    f = _peer_apply(h2, sel, peer_u[0], peer_v[0]).reshape(nb, total, d)

    x_all, qr, kr, vr, gr, qd, kd, vd = _proj1(x_all, f, mods, rows, n1[1], cd_w_in[0], d_q_norm[0], d_k_norm[0],
                                               tabs_a, half_a)
    lg = jnp.stack([jax.nn.log_sigmoid(c_decay_fwd[0].astype(F32)), jax.nn.log_sigmoid(c_decay_bwd[0].astype(F32))])
    o_f, o_bw = _retention(qr, kr, vr, lg, ctx_len)
    o_d = _attn_dense(qd, kd, vd, scale=HEAD_DIM ** -0.5, ctx_len=ctx_len)
    x_lat, h2 = _out1(o_f, o_bw, gr, c_gn_w[0], o_d, cd_w_o[0], x_all, mods, rows, n2[1], ctx_len)
    h2 = h2.reshape(nb * seq, d)
    sel = _peer_select(h2, peer_w_q[1], peer_keys[1])
    out = _peer_apply(h2, sel, peer_u[1], peer_v[1], x=x_lat.reshape(nb * seq, d), mods=mods,
                      mod_index=lambda b: (rows + b) * 6 + 5, tokens_per_batch=seq)
    return out.reshape(nb, seq, d).astype(x.dtype)
```

```python
import functools

import numpy as np
import jax
import jax.numpy as jnp
from jax import lax
from jax.experimental import pallas as pl
from jax.experimental.pallas import tpu as pltpu

F32 = jnp.float32
BF16 = jnp.bfloat16

GRID_W = 64
ROPE_THETA = 10000.0
EPS = 1e-6
NEG_INF = -1e30
HEAD_DIM = 64
A_Q_HEADS, A_KV_HEADS, A_WINDOW = 8, 2, 128
B_HEADS, B_NOPE, B_ROPE, B_V, B_Q_RANK, B_KV_RANK = 8, 64, 32, 64, 256, 256
B_QK = B_NOPE + B_ROPE
C_HEADS, C_DK, C_DV = 4, 64, 128
D_Q_HEADS, D_KV_HEADS = 8, 2
PEER_HEADS, PEER_N_KEYS, PEER_D_KEY, PEER_TOPK = 8, 128, 256, 16

LANE = 128
SUBLANE = 8
BF16_ROWS = 16
VMEM_LIMIT = 56 * 1024 * 1024

TOK_TILE = 256
CHUNK = 128
PEER_SEL_TILE = 256
PEER_TOK_TILE = 512
PEER_EXP_TILE = 1024
GATE_COLS = 256


def _cparams(*sem):
    return pltpu.CompilerParams(dimension_semantics=sem, vmem_limit_bytes=VMEM_LIMIT)


def _full(arr):
    nd = arr.ndim
    return pl.BlockSpec(arr.shape, lambda *_: (0,) * nd)


def _pad_cols(w, n_heads, d):
    lead = w.shape[:-1]
    w = w.reshape(lead + (n_heads, d))
    w = jnp.pad(w, [(0, 0)] * len(lead) + [(0, 0), (0, LANE - d)])
    return w.reshape(lead + (n_heads * LANE,))


def _pad_rows(w, n_heads, d):
    n = w.shape[-1]
    w = w.reshape(n_heads, d, n)
    w = jnp.pad(w, [(0, 0), (0, LANE - d), (0, 0)])
    return w.reshape(n_heads * LANE, n)


def _pad_gain(g, d):
    return jnp.pad(g.astype(F32), (0, LANE - d)).reshape(1, LANE)


def _rope_tables(ctx_len, seq, lane_off, d_rot):
    blk = d_rot // 2
    half = blk // 2
    freqs = ROPE_THETA ** (-np.arange(half, dtype=np.float64) / half)
    pos = np.arange(seq)
    total = ctx_len + seq
    cos = np.ones((total, LANE), np.float64)
    sup = np.zeros((total, LANE), np.float64)
    sdn = np.zeros((total, LANE), np.float64)
    for axis, p in enumerate((pos // GRID_W, pos % GRID_W)):
        ang = p[:, None].astype(np.float64) * freqs[None, :]
        c, s = np.cos(ang), np.sin(ang)
        base = lane_off + axis * blk
        cos[ctx_len:, base:base + half] = c
        cos[ctx_len:, base + half:base + blk] = c
        sdn[ctx_len:, base:base + half] = -s
        sup[ctx_len:, base + half:base + blk] = s
    return (jnp.asarray(cos, F32), jnp.asarray(sup, F32), jnp.asarray(sdn, F32)), half


def _rms_rows(x, true_dim):
    return x * lax.rsqrt(jnp.sum(x * x, axis=-1, keepdims=True) * (1.0 / true_dim) + EPS)


def _rope(y, cos, sup, sdn, half):
    return y * cos + pltpu.roll(y, half, 1) * sup + pltpu.roll(y, LANE - half, 1) * sdn


def _ada_kernel(c_ref, w_ref, b_ref, o_ref):
    c = c_ref[...]
    s = c * jax.nn.sigmoid(c)
    o_ref[0] = jnp.dot(s.astype(BF16), w_ref[0].astype(BF16), preferred_element_type=F32) + b_ref[0]


def _ada(c_rows, ada_w, ada_b):
    depth, d, n = ada_w.shape
    rows = c_rows.shape[0]
    tn = 1536
    return pl.pallas_call(
        _ada_kernel,
        grid=(depth, n // tn),
        in_specs=[pl.BlockSpec((rows, d), lambda l, j: (0, 0)),
                  pl.BlockSpec((1, d, tn), lambda l, j: (l, 0, j)),
                  pl.BlockSpec((1, 1, tn), lambda l, j: (l, 0, j))],
        out_specs=pl.BlockSpec((1, rows, tn), lambda l, j: (l, 0, j)),
        out_shape=jax.ShapeDtypeStruct((depth, rows, n), F32),
        compiler_params=_cparams("arbitrary", "arbitrary"),
        name="ada_mod",
    )(c_rows, ada_w, ada_b.reshape(depth, 1, n))


def _mod_spec(layer, chunk, rows, nb, d, tile_axis=1):
    def imap(*ids):
        b, t = ids[0], ids[tile_axis]
        r = jnp.where(t == 0, nb, b)
        return ((layer * rows + r) * 6 + chunk, 0, 0)
    return pl.BlockSpec((1, 1, d), imap)


def _modulate(x, n_ref, sc_ref, sh_ref, d):
    return _rms_rows(x, d) * n_ref[...] * (1.0 + sc_ref[0]) + sh_ref[0]


def _proj0_kernel(x_ref, n1_ref, sh_ref, sc_ref, w_ref, wuq_ref, wuk_ref, wuv_ref,
                  aqn_ref, akn_ref, bqln_ref, bkvln_ref, bqn_ref, bkn_ref,
                  ca_ref, ua_ref, da_ref, cb_ref, ub_ref, db_ref,
                  qa_ref, ka_ref, va_ref, qb_ref, kb_ref, vb_ref, *, d_model, half_a, half_b, offs):
    h = _modulate(x_ref[0], n1_ref, sc_ref, sh_ref, d_model).astype(BF16)
    ca, ua, da = ca_ref[...], ua_ref[...], da_ref[...]
    cb, ub, db = cb_ref[...], ub_ref[...], db_ref[...]
    o_qa, o_ka, o_va, o_cq, o_ckv, o_kr, o_end = offs

    z = jnp.dot(h, w_ref[:, o_qa:o_ka], preferred_element_type=F32)
    for i in range(A_Q_HEADS):
        y = _rms_rows(z[:, i * LANE:(i + 1) * LANE], HEAD_DIM) * aqn_ref[...]
        qa_ref[0, :, i * LANE:(i + 1) * LANE] = _rope(y, ca, ua, da, half_a).astype(BF16)
    z = jnp.dot(h, w_ref[:, o_ka:o_va], preferred_element_type=F32)
    for i in range(A_KV_HEADS):
        y = _rms_rows(z[:, i * LANE:(i + 1) * LANE], HEAD_DIM) * akn_ref[...]
        ka_ref[0, :, i * LANE:(i + 1) * LANE] = _rope(y, ca, ua, da, half_a).astype(BF16)
    va_ref[0] = jnp.dot(h, w_ref[:, o_va:o_cq], preferred_element_type=F32).astype(BF16)

    cq = jnp.dot(h, w_ref[:, o_cq:o_ckv], preferred_element_type=F32)
    cq = (_rms_rows(cq, B_Q_RANK) * bqln_ref[...]).astype(BF16)
    z = jnp.dot(cq, wuq_ref[...], preferred_element_type=F32)
    for i in range(B_HEADS):
        y = _rms_rows(z[:, i * LANE:(i + 1) * LANE], B_QK) * bqn_ref[...]
        qb_ref[0, :, i * LANE:(i + 1) * LANE] = _rope(y, cb, ub, db, half_b).astype(BF16)

    ckv = jnp.dot(h, w_ref[:, o_ckv:o_kr], preferred_element_type=F32)
    ckv = (_rms_rows(ckv, B_KV_RANK) * bkvln_ref[...]).astype(BF16)
    kr = jnp.dot(h, w_ref[:, o_kr:o_end], preferred_element_type=F32)
    z = jnp.dot(ckv, wuk_ref[...], preferred_element_type=F32)
    for i in range(B_HEADS):
        y = _rms_rows(z[:, i * LANE:(i + 1) * LANE] + kr, B_QK) * bkn_ref[...]
        kb_ref[0, :, i * LANE:(i + 1) * LANE] = _rope(y, cb, ub, db, half_b).astype(BF16)
    vb_ref[0] = jnp.dot(ckv, wuv_ref[...], preferred_element_type=F32).astype(BF16)


def _proj0(x_all, mods, rows, norm1, w_in, b_wuq, b_wukv, a_qn, a_kn, b_qln, b_kvln, b_qn, b_kn, tabs_a, tabs_b,
           half_a, half_b):
    nb, total, d = x_all.shape
    tm = TOK_TILE
    wq, wk, wv, wcq, wckv, wkr = jnp.split(
        w_in, np.cumsum([A_Q_HEADS * HEAD_DIM, A_KV_HEADS * HEAD_DIM, A_KV_HEADS * HEAD_DIM, B_Q_RANK, B_KV_RANK])
        .tolist(), axis=1)
    kr_pad = jnp.pad(wkr, ((0, 0), (B_NOPE, LANE - B_QK)))
    parts = [_pad_cols(wq, A_Q_HEADS, HEAD_DIM), _pad_cols(wk, A_KV_HEADS, HEAD_DIM),
             _pad_cols(wv, A_KV_HEADS, HEAD_DIM), wcq, wckv, kr_pad]
    offs = tuple(int(o) for o in np.cumsum([0] + [p.shape[1] for p in parts]))
    w_all = jnp.concatenate(parts, axis=1).astype(BF16)
    wuq = _pad_cols(b_wuq, B_HEADS, B_QK).astype(BF16)
    wukv = b_wukv.reshape(B_KV_RANK, B_HEADS, B_NOPE + B_V)
    wuk = _pad_cols(wukv[..., :B_NOPE].reshape(B_KV_RANK, -1), B_HEADS, B_NOPE).astype(BF16)
    wuv = _pad_cols(wukv[..., B_NOPE:].reshape(B_KV_RANK, -1), B_HEADS, B_V).astype(BF16)
    consts = [w_all, wuq, wuk, wuv, _pad_gain(a_qn, HEAD_DIM), _pad_gain(a_kn, HEAD_DIM),
              b_qln.astype(F32).reshape(1, -1), b_kvln.astype(F32).reshape(1, -1),
              _pad_gain(b_qn, B_QK), _pad_gain(b_kn, B_QK)]
    tab_spec = pl.BlockSpec((tm, LANE), lambda b, t: (t, 0))
    wide = lambda nh: pl.BlockSpec((1, tm, nh * LANE), lambda b, t: (b, t, 0))
    shp = lambda nh: jax.ShapeDtypeStruct((nb, total, nh * LANE), BF16)
    return pl.pallas_call(
        functools.partial(_proj0_kernel, d_model=d, half_a=half_a, half_b=half_b, offs=offs),
        grid=(nb, total // tm),
        in_specs=[pl.BlockSpec((1, tm, d), lambda b, t: (b, t, 0)), _full(norm1),
                  _mod_spec(0, 0, rows, nb, d), _mod_spec(0, 1, rows, nb, d)]
                 + [_full(c) for c in consts] + [tab_spec] * 6,
        out_specs=[wide(A_Q_HEADS), wide(A_KV_HEADS), wide(A_KV_HEADS), wide(B_HEADS), wide(B_HEADS), wide(B_HEADS)],
        out_shape=[shp(A_Q_HEADS), shp(A_KV_HEADS), shp(A_KV_HEADS), shp(B_HEADS), shp(B_HEADS), shp(B_HEADS)],
        compiler_params=_cparams("arbitrary", "arbitrary"),
        name="proj0",
    )(x_all, norm1, mods, mods, *consts, *tabs_a, *tabs_b)


def _attn_dense_kernel(q_ref, k_ref, v_ref, o_ref, *, scale, ctx_len, tq):
    qt = pl.program_id(2)
    s = lax.dot_general(q_ref[0], k_ref[0], (((1,), (1,)), ((), ())), preferred_element_type=F32) * scale
    kpos = lax.broadcasted_iota(jnp.int32, s.shape, 1)
    s = jnp.where(jnp.logical_or(kpos < ctx_len, qt * tq >= ctx_len), s, NEG_INF)
    m = jnp.max(s, axis=-1, keepdims=True)
    p = jnp.exp(s - m)
    den = jnp.sum(p, axis=-1, keepdims=True)
    o = jnp.dot(p.astype(BF16), v_ref[0], preferred_element_type=F32)
    o_ref[0] = (o / den).astype(BF16)


def _attn_dense(q, k, v, *, scale, ctx_len):
    nb, total, qw = q.shape
    hq, hkv = qw // LANE, k.shape[2] // LANE
    grp = hq // hkv
    tq = TOK_TILE
    assert ctx_len == tq
    return pl.pallas_call(
        functools.partial(_attn_dense_kernel, scale=scale, ctx_len=ctx_len, tq=tq),
        grid=(nb, hq, total // tq),
        in_specs=[pl.BlockSpec((1, tq, LANE), lambda b, h, t: (b, t, h)),
                  pl.BlockSpec((1, total, LANE), lambda b, h, t: (b, 0, h // grp)),
                  pl.BlockSpec((1, total, LANE), lambda b, h, t: (b, 0, h // grp))],
        out_specs=pl.BlockSpec((1, tq, LANE), lambda b, h, t: (b, t, h)),
        out_shape=jax.ShapeDtypeStruct(q.shape, BF16),
        compiler_params=_cparams("arbitrary", "arbitrary", "arbitrary"),
        name="attn_dense",
    )(q, k, v)


def _attn_window_kernel(q_ref, k_ref, v_ref, sink_ref, o_ref, *, scale, ctx_len, tq, total, window):
    qt = pl.program_id(2)
    q = q_ref[0]
    slab = 3 * tq
    start = pl.multiple_of(jnp.clip((qt - 1) * tq, ctx_len, total - slab), tq)
    nt = (((1,), (1,)), ((), ()))
    s_c = lax.dot_general(q, k_ref[0, 0:ctx_len, :], nt, preferred_element_type=F32) * scale
    s_l = lax.dot_general(q, k_ref[0, pl.ds(start, slab), :], nt, preferred_element_type=F32) * scale
    qpos = qt * tq + lax.broadcasted_iota(jnp.int32, s_l.shape, 0)
    kpos = start + lax.broadcasted_iota(jnp.int32, s_l.shape, 1)
    ok = jnp.logical_and(jnp.abs(qpos - kpos) <= window, qt * tq >= ctx_len)
    s_l = jnp.where(ok, s_l, NEG_INF)
    sink = sink_ref[0][:, 0:1]
    m = jnp.maximum(jnp.maximum(jnp.max(s_c, axis=-1, keepdims=True), jnp.max(s_l, axis=-1, keepdims=True)), sink)
    p_c = jnp.exp(s_c - m)
    p_l = jnp.exp(s_l - m)
    den = jnp.sum(p_c, axis=-1, keepdims=True) + jnp.sum(p_l, axis=-1, keepdims=True) + jnp.exp(sink - m)
    o = (jnp.dot(p_c.astype(BF16), v_ref[0, 0:ctx_len, :], preferred_element_type=F32)
         + jnp.dot(p_l.astype(BF16), v_ref[0, pl.ds(start, slab), :], preferred_element_type=F32))
    o_ref[0] = (o / den).astype(BF16)


def _attn_window(q, k, v, sink, *, scale, ctx_len):
    nb, total, qw = q.shape
    hq, hkv = qw // LANE, k.shape[2] // LANE
    grp = hq // hkv
    tq = CHUNK
    sink_rows = jnp.broadcast_to(sink.astype(F32).reshape(hq, 1, 1), (hq, 1, LANE))
    return pl.pallas_call(
        functools.partial(_attn_window_kernel, scale=scale, ctx_len=ctx_len, tq=tq, total=total, window=A_WINDOW),
        grid=(nb, hq, total // tq),
        in_specs=[pl.BlockSpec((1, tq, LANE), lambda b, h, t: (b, t, h)),
                  pl.BlockSpec((1, total, LANE), lambda b, h, t: (b, 0, h // grp)),
                  pl.BlockSpec((1, total, LANE), lambda b, h, t: (b, 0, h // grp)),
                  pl.BlockSpec((1, 1, LANE), lambda b, h, t: (h, 0, 0))],
        out_specs=pl.BlockSpec((1, tq, LANE), lambda b, h, t: (b, t, h)),
        out_shape=jax.ShapeDtypeStruct(q.shape, BF16),
        compiler_params=_cparams("arbitrary", "arbitrary", "arbitrary"),
        name="attn_window",
    )(q, k, v, sink_rows)


def _out0_kernel(oa_ref, ob_ref, woa_ref, wob_ref, x_ref, g1_ref, n2_ref, sh2_ref, sc2_ref, xn_ref, h2_ref, *, d_model):
    y = (jnp.dot(oa_ref[0], woa_ref[...], preferred_element_type=F32)
         + jnp.dot(ob_ref[0], wob_ref[...], preferred_element_type=F32))
    xn = x_ref[0] + g1_ref[0] * y
    xn_ref[0] = xn
    h2_ref[0] = _modulate(xn, n2_ref, sc2_ref, sh2_ref, d_model).astype(BF16)


def _out0(oa, ob, w_o, x_all, mods, rows, norm2):
    nb, total, d = x_all.shape
    tm = TOK_TILE
    woa = _pad_rows(w_o[:A_Q_HEADS * HEAD_DIM], A_Q_HEADS, HEAD_DIM).astype(BF16)
    wob = _pad_rows(w_o[A_Q_HEADS * HEAD_DIM:], B_HEADS, B_V).astype(BF16)
    tile = lambda w: pl.BlockSpec((1, tm, w), lambda b, t: (b, t, 0))
    return pl.pallas_call(
        functools.partial(_out0_kernel, d_model=d),
        grid=(nb, total // tm),
        in_specs=[tile(oa.shape[2]), tile(ob.shape[2]), _full(woa), _full(wob), tile(d),
                  _mod_spec(0, 2, rows, nb, d), _full(norm2), _mod_spec(0, 3, rows, nb, d),
                  _mod_spec(0, 4, rows, nb, d)],
        out_specs=[tile(d), tile(d)],
        out_shape=[jax.ShapeDtypeStruct((nb, total, d), F32), jax.ShapeDtypeStruct((nb, total, d), BF16)],
        compiler_params=_cparams("arbitrary", "arbitrary"),
        name="out_proj0",
    )(oa, ob, woa, wob, x_all, mods, norm2, mods, mods)


def _top_rows(sc, rowf, k):
    n = sc.shape[0]
    vals, idxs = [], []
    work = sc
    for _ in range(k):
        m = jnp.max(work, axis=0, keepdims=True)
        idx = jnp.min(jnp.where(work == m, rowf, float(n)), axis=0, keepdims=True)
        vals.append(m)
        idxs.append(idx)
        work = jnp.where(rowf == idx, -jnp.inf, work)
    return vals, idxs


def _peer_select_kernel(h_ref, wq_ref, keys_ref, cc_ref, e0_ref, rb_ref, e1_ref, q_sc, *, n_heads, topk):
    nk = PEER_N_KEYS
    ts = h_ref.shape[0]
    q_sc[...] = lax.dot_general(wq_ref[...], h_ref[...], (((1,), (1,)), ((), ())), preferred_element_type=F32)
    rowf = lax.broadcasted_iota(jnp.int32, (nk, ts), 0).astype(F32)
    row16 = lax.broadcasted_iota(jnp.int32, (topk, ts), 0).astype(F32)
    slab_rows = [topk] + [SUBLANE] * (topk - 1)
    n_cand = sum(slab_rows)
    ci = lax.broadcasted_iota(jnp.int32, (n_cand, ts), 0)
    rest = ci - topk
    flat = jnp.where(ci < topk, ci, (1 + (rest >> 3)) * topk + (rest & 7)).astype(F32)

    def head_body(hd, carry):
        tops = []
        for p in range(2):
            hp = hd * 2 + p
            qhp = q_sc[pl.ds(pl.multiple_of(hp * nk, nk), nk), :].astype(BF16)
            sc = jnp.dot(keys_ref[hp], qhp, preferred_element_type=F32)
            vals, idxs = _top_rows(sc, rowf, topk)
            tops.append((sc, vals, idxs))
        (sc0, v0, i0), (sc1, v1, i1) = tops
        s1 = jnp.zeros((topk, ts), F32)
        for k in range(topk):
            s1 = jnp.where(row16 == float(k), v1[k], s1)
        cand = jnp.concatenate([v0[k1] + s1[0:slab_rows[k1], :] for k1 in range(topk)], axis=0)
        cnt = jnp.zeros((topk, ts), F32)
        zsum = jnp.zeros((1, ts), F32)
        best0 = None
        work = cand
        for k in range(topk):
            m = jnp.max(work, axis=0, keepdims=True)
            idx = jnp.min(jnp.where(work == m, flat, 1e9), axis=0, keepdims=True)
            work = jnp.where(flat == idx, -jnp.inf, work)
            best0 = m if best0 is None else best0
            zsum = zsum + jnp.exp(m - best0)
            cnt = cnt + jnp.where(row16 == jnp.floor(idx * (1.0 / topk)), 1.0, 0.0)
        cc = jnp.zeros((nk, ts), F32)
        rb = jnp.full((nk, ts), 99.0, F32)
        for k in range(topk):
            ck = jnp.sum(jnp.where(row16 == float(k), cnt, 0.0), axis=0, keepdims=True)
            cc = jnp.where(rowf == i0[k], ck, cc)
            rb = jnp.where(rowf == i1[k], float(k), rb)
        cc_ref[hd] = cc
        rb_ref[hd] = rb.astype(BF16)
        e0_ref[hd] = jnp.exp(sc0 - v0[0])
        e1_ref[hd] = (jnp.exp(sc1 - v1[0]) / zsum).astype(BF16)
        return carry

    lax.fori_loop(0, n_heads, head_body, 0)


def _peer_select(h2, w_q, keys):
    t, d = h2.shape
    ts = PEER_SEL_TILE
    nh, nk = PEER_HEADS, PEER_N_KEYS
    wq_t = w_q.T.astype(BF16)
    keys2 = keys.reshape(nh * 2, nk, PEER_D_KEY // 2).astype(BF16)
    row_out = jax.ShapeDtypeStruct((nh, nk, t), F32)
    col_out = jax.ShapeDtypeStruct((nh, nk, t), BF16)
    ospec = pl.BlockSpec((nh, nk, ts), lambda i: (0, 0, i))
    return pl.pallas_call(
        functools.partial(_peer_select_kernel, n_heads=nh, topk=PEER_TOPK),
        grid=(t // ts,),
        in_specs=[pl.BlockSpec((ts, d), lambda i: (i, 0)), _full(wq_t), _full(keys2)],
        out_specs=[ospec] * 4,
        out_shape=[row_out, row_out, col_out, col_out],
        scratch_shapes=[pltpu.VMEM((wq_t.shape[0], ts), F32)],
        compiler_params=_cparams("arbitrary"),
        name="peer_select",
    )(h2, wq_t, keys2)


def _peer_apply_kernel(*refs, n_heads, final):
    if final:
        (h_ref, u_ref, vt_ref, cc_ref, e0_ref, rb_ref, e1_ref, x_ref, g_ref, o_ref,
         acc_ref, g_sc, p_sc, ht_sc) = refs
    else:
        h_ref, u_ref, vt_ref, cc_ref, e0_ref, rb_ref, e1_ref, o_ref, acc_ref, g_sc, p_sc, ht_sc = refs
    et = pl.program_id(1)
    nk = PEER_N_KEYS
    rows_per_tile = cc_ref.shape[1]
    tt = h_ref.shape[0]

    @pl.when(jnp.logical_and(pl.program_id(0) == 0, et == 0))
    def _():
        g_sc[...] = jnp.zeros_like(g_sc)
        acc_ref[...] = jnp.zeros_like(acc_ref)

    @pl.when(et == 0)
    def _():
        ht_sc[...] = h_ref[...].astype(F32).T.astype(BF16)

    at = jnp.dot(u_ref[...], ht_sc[...], preferred_element_type=F32)

    zero = jnp.zeros((), BF16)
    prev = (et + 1) % 2
    for ii in range(rows_per_tile):
        for c0 in range(0, tt, GATE_COLS):
            cols = slice(c0, c0 + GATE_COLS)
            ccr = [jnp.broadcast_to(cc_ref[hd, ii:ii + 1, cols], (BF16_ROWS, GATE_COLS)).astype(BF16)
                   for hd in range(n_heads)]
            e0r = [jnp.broadcast_to(e0_ref[hd, ii:ii + 1, cols], (BF16_ROWS, GATE_COLS)).astype(BF16)
                   for hd in range(n_heads)]
            for s0 in range(0, nk, BF16_ROWS):
                rws = slice(s0, s0 + BF16_ROWS)
                w = None
                for hd in range(n_heads):
                    term = jnp.where(rb_ref[hd, rws, cols] < ccr[hd], e1_ref[hd, rws, cols], zero) * e0r[hd]
                    w = term if w is None else w + term
                r0 = ii * nk + s0
                p_sc[r0:r0 + BF16_ROWS, cols] = w * g_sc[prev, r0:r0 + BF16_ROWS, cols]
    acc_ref[...] += jnp.dot(vt_ref[...], p_sc[...], preferred_element_type=F32)
    g_sc[et % 2] = jax.nn.gelu(at).astype(BF16)

    @pl.when(et == 0)
    def _():
        acc_ref[...] = jnp.zeros_like(acc_ref)

    @pl.when(et == pl.num_programs(1) - 1)
    def _():
        f = acc_ref[...].T
        if final:
            o_ref[...] = x_ref[...] + g_ref[0] * f
        else:
            o_ref[...] = f


def _peer_apply(h2, sel, u_tab, v_tab, x=None, mods=None, mod_index=None, tokens_per_batch=None):
    t, d = h2.shape
    cc, e0, rb, e1 = sel
    nh, nk = PEER_HEADS, PEER_N_KEYS
    tt, te = PEER_TOK_TILE, PEER_EXP_TILE
    ti = te // nk
    n_et = u_tab.shape[0] // te
    u = u_tab.astype(BF16)
    vt = v_tab.T.astype(BF16)
    final = x is not None
    cur = lambda e: jnp.minimum(e, n_et - 1)
    prv = lambda e: jnp.maximum(e - 1, 0)
    row_spec = pl.BlockSpec((nh, ti, tt), lambda i, e: (0, prv(e), i))
    col_spec = pl.BlockSpec((nh, nk, tt), lambda i, e: (0, 0, i))
    in_specs = [pl.BlockSpec((tt, d), lambda i, e: (i, 0)),
                pl.BlockSpec((te, d), lambda i, e: (cur(e), 0)),
                pl.BlockSpec((d, te), lambda i, e: (0, prv(e))),
                row_spec, row_spec, col_spec, col_spec]
    args = [h2, u, vt, cc, e0, rb, e1]
    if final:
        per = tokens_per_batch // tt
        in_specs += [pl.BlockSpec((tt, d), lambda i, e: (i, 0)),
                     pl.BlockSpec((1, 1, d), lambda i, e: (mod_index(i // per), 0, 0))]
        args += [x, mods]
    return pl.pallas_call(
        functools.partial(_peer_apply_kernel, n_heads=nh, final=final),
        grid=(t // tt, n_et + 1),
        in_specs=in_specs,
        out_specs=pl.BlockSpec((tt, d), lambda i, e: (i, 0)),
        out_shape=jax.ShapeDtypeStruct((t, d), F32),
        scratch_shapes=[pltpu.VMEM((d, tt), F32), pltpu.VMEM((2, te, tt), BF16), pltpu.VMEM((te, tt), BF16),
                        pltpu.VMEM((d, tt), BF16)],
        compiler_params=_cparams("arbitrary", "arbitrary"),
        name="peer_apply_final" if final else "peer_apply",
    )(*args)


def _proj1_kernel(x_ref, f_ref, g2_ref, n1_ref, sh_ref, sc_ref, w_ref, dqn_ref, dkn_ref, ca_ref, ua_ref, da_ref,
                  xn_ref, qr_ref, kr_ref, vr_ref, gr_ref, qd_ref, kd_ref, vd_ref, *, d_model, half_a, offs):
    xn = x_ref[0] + g2_ref[0] * f_ref[0]
    xn_ref[0] = xn
    h = _modulate(xn, n1_ref, sc_ref, sh_ref, d_model).astype(BF16)
    ca, ua, da = ca_ref[...], ua_ref[...], da_ref[...]
    o_qr, o_kr, o_vr, o_gr, o_qd, o_kd, o_vd, o_end = offs

    z = jnp.dot(h, w_ref[:, o_qr:o_kr], preferred_element_type=F32)
    for i in range(C_HEADS):
        qr_ref[0, :, i * LANE:(i + 1) * LANE] = _rope(z[:, i * LANE:(i + 1) * LANE], ca, ua, da, half_a).astype(BF16)
    z = jnp.dot(h, w_ref[:, o_kr:o_vr], preferred_element_type=F32) * (C_DK ** -0.5)
    for i in range(C_HEADS):
        kr_ref[0, :, i * LANE:(i + 1) * LANE] = _rope(z[:, i * LANE:(i + 1) * LANE], ca, ua, da, half_a).astype(BF16)
    vr_ref[0] = jnp.dot(h, w_ref[:, o_vr:o_gr], preferred_element_type=F32).astype(BF16)
    gr_ref[0] = jnp.dot(h, w_ref[:, o_gr:o_qd], preferred_element_type=F32).astype(BF16)
    z = jnp.dot(h, w_ref[:, o_qd:o_kd], preferred_element_type=F32)
    for i in range(D_Q_HEADS):
        y = _rms_rows(z[:, i * LANE:(i + 1) * LANE], HEAD_DIM) * dqn_ref[...]
        qd_ref[0, :, i * LANE:(i + 1) * LANE] = _rope(y, ca, ua, da, half_a).astype(BF16)
    z = jnp.dot(h, w_ref[:, o_kd:o_vd], preferred_element_type=F32)
    for i in range(D_KV_HEADS):
        y = _rms_rows(z[:, i * LANE:(i + 1) * LANE], HEAD_DIM) * dkn_ref[...]
        kd_ref[0, :, i * LANE:(i + 1) * LANE] = _rope(y, ca, ua, da, half_a).astype(BF16)
    vd_ref[0] = jnp.dot(h, w_ref[:, o_vd:o_end], preferred_element_type=F32).astype(BF16)


def _proj1(x_all, f_all, mods, rows, norm1, w_in, d_qn, d_kn, tabs_a, half_a):
    nb, total, d = x_all.shape
    tm = TOK_TILE
    cqk, cv = C_HEADS * C_DK, C_HEADS * C_DV
    wqr, wkr, wvr, wgr, wqd, wkd, wvd = jnp.split(
        w_in, np.cumsum([cqk, cqk, cv, cv, D_Q_HEADS * HEAD_DIM, D_KV_HEADS * HEAD_DIM]).tolist(), axis=1)
    parts = [_pad_cols(wqr, C_HEADS, C_DK), _pad_cols(wkr, C_HEADS, C_DK), wvr, wgr,
             _pad_cols(wqd, D_Q_HEADS, HEAD_DIM), _pad_cols(wkd, D_KV_HEADS, HEAD_DIM),
             _pad_cols(wvd, D_KV_HEADS, HEAD_DIM)]
    offs = tuple(int(o) for o in np.cumsum([0] + [p.shape[1] for p in parts]))
    w_all = jnp.concatenate(parts, axis=1).astype(BF16)
    consts = [w_all, _pad_gain(d_qn, HEAD_DIM), _pad_gain(d_kn, HEAD_DIM)]
    tab_spec = pl.BlockSpec((tm, LANE), lambda b, t: (t, 0))
    tile = lambda w: pl.BlockSpec((1, tm, w), lambda b, t: (b, t, 0))
    widths = [p.shape[1] for p in parts]
    return pl.pallas_call(
        functools.partial(_proj1_kernel, d_model=d, half_a=half_a, offs=offs),
        grid=(nb, total // tm),
        in_specs=[tile(d), tile(d), _mod_spec(0, 5, rows, nb, d), _full(norm1),
                  _mod_spec(1, 0, rows, nb, d), _mod_spec(1, 1, rows, nb, d)]
                 + [_full(c) for c in consts] + [tab_spec] * 3,
        out_specs=[tile(d)] + [tile(w) for w in widths],
        out_shape=[jax.ShapeDtypeStruct((nb, total, d), F32)]
                  + [jax.ShapeDtypeStruct((nb, total, w), BF16) for w in widths],
        compiler_params=_cparams("arbitrary", "arbitrary"),
        name="proj1",
    )(x_all, f_all, mods, norm1, mods, mods, *consts, *tabs_a)


def _retention_kernel(lg_ref, qf_ref, kf_ref, vf_ref, qb_ref, kb_ref, vb_ref, of_ref, ob_ref, st_ref, *, n_heads):
    step = pl.program_id(1)

    @pl.when(step == 0)
    def _():
        st_ref[...] = jnp.zeros_like(st_ref)

    c = CHUNK
    ri = lax.broadcasted_iota(jnp.int32, (c, LANE), 0).astype(F32)
    diff = ri - lax.broadcasted_iota(jnp.int32, (c, LANE), 1).astype(F32)
    nt = (((1,), (1,)), ((), ()))
    for d, (q_ref, k_ref, v_ref, o_ref) in enumerate(((qf_ref, kf_ref, vf_ref, of_ref),
                                                      (qb_ref, kb_ref, vb_ref, ob_ref))):
        for hd in range(n_heads):
            lg = lg_ref[d, hd]
            sl = slice(hd * LANE, (hd + 1) * LANE)
            q, k, v = q_ref[0, :, sl], k_ref[0, :, sl], v_ref[0, :, sl]
            if d == 0:
                dec = jnp.where(diff >= 0, jnp.exp(lg * jnp.maximum(diff, 0.0)), 0.0)
                q_dec = jnp.exp(lg * (ri + 1.0))
                k_dec = jnp.exp(lg * (c - 1.0 - ri))
            else:
                dec = jnp.where(diff <= 0, jnp.exp(lg * jnp.maximum(-diff, 0.0)), 0.0)
                q_dec = jnp.exp(lg * (c - ri))
                k_dec = jnp.exp(lg * ri)
            s = lax.dot_general(q, k, nt, preferred_element_type=F32) * dec
            inner = jnp.dot(s.astype(BF16), v, preferred_element_type=F32)
            st = st_ref[d, hd]
            cross = jnp.dot(q, st.astype(BF16), preferred_element_type=F32) * q_dec
            o_ref[0, :, sl] = inner + cross
            kd_t = (k.astype(F32) * k_dec).T.astype(BF16)
            st_ref[d, hd] = st * jnp.exp(lg * c) + jnp.dot(kd_t, v, preferred_element_type=F32)


def _retention(qr, kr, vr, lg, ctx_len):
    nb, total, w = qr.shape
    nh = w // LANE
    c = CHUNK
    nc, nctx = total // c, ctx_len // c
    fwd = pl.BlockSpec((1, c, w), lambda b, s: (b, s, 0))

    def bmap(b, s):
        return (b, jnp.where(s < nctx, nctx - 1 - s, nc - 1 - (s - nctx)), 0)

    bwd = pl.BlockSpec((1, c, w), bmap)
    out = jax.ShapeDtypeStruct((nb, total, w), F32)
    return pl.pallas_call(
        functools.partial(_retention_kernel, n_heads=nh),
        grid=(nb, nc),
        in_specs=[pl.BlockSpec(memory_space=pltpu.SMEM), fwd, fwd, fwd, bwd, bwd, bwd],
        out_specs=[fwd, bwd],
        out_shape=[out, out],
        scratch_shapes=[pltpu.VMEM((2, nh, LANE, LANE), F32)],
        compiler_params=_cparams("arbitrary", "arbitrary"),
        name="retention",
    )(lg, qr, kr, vr, qr, kr, vr)


def _out1_kernel(of_ref, ob_ref, gr_ref, gn_ref, od_ref, wor_ref, wod_ref, x_ref, g1_ref, n2_ref, sh2_ref, sc2_ref,
                 xn_ref, h2_ref, *, d_model, n_heads):
    o = of_ref[0] + ob_ref[0]
    g = gr_ref[0].astype(F32)
    gate = g * jax.nn.sigmoid(g)
    gn = gn_ref[...]
    ys = []
    for hd in range(n_heads):
        sl = slice(hd * LANE, (hd + 1) * LANE)
        oh = o[:, sl]
        mu = jnp.mean(oh, axis=-1, keepdims=True)
        var = jnp.mean(jnp.square(oh - mu), axis=-1, keepdims=True)
        ys.append((gate[:, sl] * ((oh - mu) * lax.rsqrt(var + EPS) * gn[:, sl])).astype(BF16))
    y_ret = jnp.concatenate(ys, axis=1)
    y = (jnp.dot(y_ret, wor_ref[...], preferred_element_type=F32)
         + jnp.dot(od_ref[0], wod_ref[...], preferred_element_type=F32))
    xn = x_ref[0] + g1_ref[0] * y
    xn_ref[0] = xn
    h2_ref[0] = _modulate(xn, n2_ref, sc2_ref, sh2_ref, d_model).astype(BF16)


def _out1(o_f, o_b, g_r, gn_w, o_d, w_o, x_all, mods, rows, norm2, ctx_len):
    nb, total, d = x_all.shape
    tm = TOK_TILE
    skip = ctx_len // tm
    seq = total - ctx_len
    wor = w_o[:C_HEADS * C_DV].astype(BF16)
    wod = _pad_rows(w_o[C_HEADS * C_DV:], D_Q_HEADS, HEAD_DIM).astype(BF16)
    gn = gn_w.astype(F32).reshape(1, -1)
    tile_in = lambda w: pl.BlockSpec((1, tm, w), lambda b, t: (b, t + skip, 0))
    tile_out = pl.BlockSpec((1, tm, d), lambda b, t: (b, t, 0))
    mod = lambda chunk: pl.BlockSpec((1, 1, d), lambda b, t: ((rows + b) * 6 + chunk, 0, 0))
    return pl.pallas_call(
        functools.partial(_out1_kernel, d_model=d, n_heads=C_HEADS),
        grid=(nb, seq // tm),
        in_specs=[tile_in(o_f.shape[2]), tile_in(o_b.shape[2]), tile_in(g_r.shape[2]), _full(gn),
                  tile_in(o_d.shape[2]), _full(wor), _full(wod), tile_in(d), mod(2), _full(norm2), mod(3), mod(4)],
        out_specs=[tile_out, tile_out],
        out_shape=[jax.ShapeDtypeStruct((nb, seq, d), F32), jax.ShapeDtypeStruct((nb, seq, d), BF16)],
        compiler_params=_cparams("arbitrary", "arbitrary"),
        name="out_proj1",
    )(o_f, o_b, g_r, gn, o_d, wor, wod, x_all, mods, norm2, mods, mods)


def kernel(x, c, ctx, c_ctx, ada_w, ada_b, norm1_w, norm2_w, ab_w_in, ab_w_o, a_q_norm, a_k_norm, a_sink,
           b_q_lora_norm, b_kv_lora_norm, b_w_uq, b_w_ukv, b_q_norm, b_k_norm, cd_w_in, cd_w_o, c_decay_fwd,
           c_decay_bwd, c_gn_w, d_q_norm, d_k_norm, peer_w_q, peer_keys, peer_u, peer_v):
    nb, seq, d = x.shape
    ctx_len = ctx.shape[1]
    total = ctx_len + seq
    assert ctx_len == TOK_TILE and seq % TOK_TILE == 0 and seq % GRID_W == 0

    rows = -(-(nb + 1) // SUBLANE) * SUBLANE
    c_rows = jnp.concatenate([c, c_ctx[None, :], jnp.zeros((rows - nb - 1, d), c.dtype)], axis=0).astype(F32)
    mods = _ada(c_rows, ada_w, ada_b).reshape(-1, 1, d)

    tabs_a, half_a = _rope_tables(ctx_len, seq, 0, HEAD_DIM)
    tabs_b, half_b = _rope_tables(ctx_len, seq, B_NOPE, B_ROPE)
    n1 = norm1_w.astype(F32).reshape(-1, 1, d)
    n2 = norm2_w.astype(F32).reshape(-1, 1, d)

    x_all = jnp.concatenate([ctx, x], axis=1).astype(F32)
    qa, ka, va, qb, kb, vb = _proj0(x_all, mods, rows, n1[0], ab_w_in[0], b_w_uq[0], b_w_ukv[0], a_q_norm[0],
                                    a_k_norm[0], b_q_lora_norm[0], b_kv_lora_norm[0], b_q_norm[0], b_k_norm[0],
                                    tabs_a, tabs_b, half_a, half_b)
    o_a = _attn_window(qa, ka, va, a_sink[0], scale=HEAD_DIM ** -0.5, ctx_len=ctx_len)
    o_b = _attn_dense(qb, kb, vb, scale=B_QK ** -0.5, ctx_len=ctx_len)
    x_all, h2 = _out0(o_a, o_b, ab_w_o[0], x_all, mods, rows, n2[0])
    h2 = h2.reshape(nb * total, d)
    sel = _peer_select(h2, peer_w_q[0], peer_keys[0])
    f = _peer_apply(h2, sel, peer_u[0], peer_v[0]).reshape(nb, total, d)

    x_all, qr, kr, vr, gr, qd, kd, vd = _proj1(x_all, f, mods, rows, n1[1], cd_w_in[0], d_q_norm[0], d_k_norm[0],
                                               tabs_a, half_a)
    lg = jnp.stack([jax.nn.log_sigmoid(c_decay_fwd[0].astype(F32)), jax.nn.log_sigmoid(c_decay_bwd[0].astype(F32))])
    o_f, o_bw = _retention(qr, kr, vr, lg, ctx_len)
    o_d = _attn_dense(qd, kd, vd, scale=HEAD_DIM ** -0.5, ctx_len=ctx_len)
    x_lat, h2 = _out1(o_f, o_bw, gr, c_gn_w[0], o_d, cd_w_o[0], x_all, mods, rows, n2[1], ctx_len)
    h2 = h2.reshape(nb * seq, d)
    sel = _peer_select(h2, peer_w_q[1], peer_keys[1])
    out = _peer_apply(h2, sel, peer_u[1], peer_v[1], x=x_lat.reshape(nb * seq, d), mods=mods,
                      mod_index=lambda b: (rows + b) * 6 + 5, tokens_per_batch=seq)
    return out.reshape(nb, seq, d).astype(x.dtype)
```

```python
import functools

import numpy as np
import jax
import jax.numpy as jnp
from jax import lax
from jax.experimental import pallas as pl
from jax.experimental.pallas import tpu as pltpu

F32 = jnp.float32
BF16 = jnp.bfloat16

GRID_W = 64
ROPE_THETA = 10000.0
EPS = 1e-6
NEG_INF = -1e30
HEAD_DIM = 64
A_Q_HEADS, A_KV_HEADS, A_WINDOW = 8, 2, 128
B_HEADS, B_NOPE, B_ROPE, B_V, B_Q_RANK, B_KV_RANK = 8, 64, 32, 64, 256, 256
B_QK = B_NOPE + B_ROPE
C_HEADS, C_DK, C_DV = 4, 64, 128
D_Q_HEADS, D_KV_HEADS = 8, 2
PEER_HEADS, PEER_N_KEYS, PEER_D_KEY, PEER_TOPK = 8, 128, 256, 16

LANE = 128
SUBLANE = 8
BF16_ROWS = 16
VMEM_LIMIT = 56 * 1024 * 1024

TOK_TILE = 256
CHUNK = 128
PEER_SEL_TILE = 256
PEER_TOK_TILE = 1024
PEER_EXP_TILE = 1024
GATE_COLS = 256


def _cparams(*sem):
    return pltpu.CompilerParams(dimension_semantics=sem, vmem_limit_bytes=VMEM_LIMIT)


def _full(arr):
    nd = arr.ndim
    return pl.BlockSpec(arr.shape, lambda *_: (0,) * nd)


def _pad_cols(w, n_heads, d):
    lead = w.shape[:-1]
    w = w.reshape(lead + (n_heads, d))
    w = jnp.pad(w, [(0, 0)] * len(lead) + [(0, 0), (0, LANE - d)])
    return w.reshape(lead + (n_heads * LANE,))


def _pad_rows(w, n_heads, d):
    n = w.shape[-1]
    w = w.reshape(n_heads, d, n)
    w = jnp.pad(w, [(0, 0), (0, LANE - d), (0, 0)])
    return w.reshape(n_heads * LANE, n)


def _pad_gain(g, d):
    return jnp.pad(g.astype(F32), (0, LANE - d)).reshape(1, LANE)


def _rope_tables(ctx_len, seq, lane_off, d_rot):
    blk = d_rot // 2
    half = blk // 2
    freqs = ROPE_THETA ** (-np.arange(half, dtype=np.float64) / half)
    pos = np.arange(seq)
    total = ctx_len + seq
    cos = np.ones((total, LANE), np.float64)
    sup = np.zeros((total, LANE), np.float64)
    sdn = np.zeros((total, LANE), np.float64)
    for axis, p in enumerate((pos // GRID_W, pos % GRID_W)):
        ang = p[:, None].astype(np.float64) * freqs[None, :]
        c, s = np.cos(ang), np.sin(ang)
        base = lane_off + axis * blk
        cos[ctx_len:, base:base + half] = c
        cos[ctx_len:, base + half:base + blk] = c
        sdn[ctx_len:, base:base + half] = -s
        sup[ctx_len:, base + half:base + blk] = s
    return (jnp.asarray(cos, F32), jnp.asarray(sup, F32), jnp.asarray(sdn, F32)), half


def _rms_rows(x, true_dim):
    return x * lax.rsqrt(jnp.sum(x * x, axis=-1, keepdims=True) * (1.0 / true_dim) + EPS)


def _rope(y, cos, sup, sdn, half):
    return y * cos + pltpu.roll(y, half, 1) * sup + pltpu.roll(y, LANE - half, 1) * sdn


def _ada_kernel(c_ref, w_ref, b_ref, o_ref):
    c = c_ref[...]
    s = c * jax.nn.sigmoid(c)
    o_ref[0] = jnp.dot(s.astype(BF16), w_ref[0].astype(BF16), preferred_element_type=F32) + b_ref[0]


def _ada(c_rows, ada_w, ada_b):
    depth, d, n = ada_w.shape
    rows = c_rows.shape[0]
    tn = 1536
    return pl.pallas_call(
        _ada_kernel,
        grid=(depth, n // tn),
        in_specs=[pl.BlockSpec((rows, d), lambda l, j: (0, 0)),
                  pl.BlockSpec((1, d, tn), lambda l, j: (l, 0, j)),
                  pl.BlockSpec((1, 1, tn), lambda l, j: (l, 0, j))],
        out_specs=pl.BlockSpec((1, rows, tn), lambda l, j: (l, 0, j)),
        out_shape=jax.ShapeDtypeStruct((depth, rows, n), F32),
        compiler_params=_cparams("arbitrary", "arbitrary"),
        name="ada_mod",
    )(c_rows, ada_w, ada_b.reshape(depth, 1, n))


def _mod_spec(layer, chunk, rows, nb, d, tile_axis=1):
    def imap(*ids):
        b, t = ids[0], ids[tile_axis]
        r = jnp.where(t == 0, nb, b)
        return ((layer * rows + r) * 6 + chunk, 0, 0)
    return pl.BlockSpec((1, 1, d), imap)


def _modulate(x, n_ref, sc_ref, sh_ref, d):
    return _rms_rows(x, d) * n_ref[...] * (1.0 + sc_ref[0]) + sh_ref[0]


def _proj0_kernel(x_ref, n1_ref, sh_ref, sc_ref, w_ref, wuq_ref, wuk_ref, wuv_ref,
                  aqn_ref, akn_ref, bqln_ref, bkvln_ref, bqn_ref, bkn_ref,
                  ca_ref, ua_ref, da_ref, cb_ref, ub_ref, db_ref,
                  qa_ref, ka_ref, va_ref, qb_ref, kb_ref, vb_ref, *, d_model, half_a, half_b, offs):
    h = _modulate(x_ref[0], n1_ref, sc_ref, sh_ref, d_model).astype(BF16)
    ca, ua, da = ca_ref[...], ua_ref[...], da_ref[...]
    cb, ub, db = cb_ref[...], ub_ref[...], db_ref[...]
    o_qa, o_ka, o_va, o_cq, o_ckv, o_kr, o_end = offs

    z = jnp.dot(h, w_ref[:, o_qa:o_ka], preferred_element_type=F32)
    for i in range(A_Q_HEADS):
        y = _rms_rows(z[:, i * LANE:(i + 1) * LANE], HEAD_DIM) * aqn_ref[...]
        qa_ref[0, :, i * LANE:(i + 1) * LANE] = (_rope(y, ca, ua, da, half_a) * HEAD_DIM ** -0.5).astype(BF16)
    z = jnp.dot(h, w_ref[:, o_ka:o_va], preferred_element_type=F32)
    for i in range(A_KV_HEADS):
        y = _rms_rows(z[:, i * LANE:(i + 1) * LANE], HEAD_DIM) * akn_ref[...]
        ka_ref[0, :, i * LANE:(i + 1) * LANE] = _rope(y, ca, ua, da, half_a).astype(BF16)
    va_ref[0] = jnp.dot(h, w_ref[:, o_va:o_cq], preferred_element_type=F32).astype(BF16)

    cq = jnp.dot(h, w_ref[:, o_cq:o_ckv], preferred_element_type=F32)
    cq = (_rms_rows(cq, B_Q_RANK) * bqln_ref[...]).astype(BF16)
    z = jnp.dot(cq, wuq_ref[...], preferred_element_type=F32)
    for i in range(B_HEADS):
        y = _rms_rows(z[:, i * LANE:(i + 1) * LANE], B_QK) * bqn_ref[...]
        qb_ref[0, :, i * LANE:(i + 1) * LANE] = (_rope(y, cb, ub, db, half_b) * B_QK ** -0.5).astype(BF16)

    ckv = jnp.dot(h, w_ref[:, o_ckv:o_kr], preferred_element_type=F32)
    ckv = (_rms_rows(ckv, B_KV_RANK) * bkvln_ref[...]).astype(BF16)
    kr = jnp.dot(h, w_ref[:, o_kr:o_end], preferred_element_type=F32)
    z = jnp.dot(ckv, wuk_ref[...], preferred_element_type=F32)
    for i in range(B_HEADS):
        y = _rms_rows(z[:, i * LANE:(i + 1) * LANE] + kr, B_QK) * bkn_ref[...]
        kb_ref[0, :, i * LANE:(i + 1) * LANE] = _rope(y, cb, ub, db, half_b).astype(BF16)
    vb_ref[0] = jnp.dot(ckv, wuv_ref[...], preferred_element_type=F32).astype(BF16)


def _proj0(x_all, mods, rows, norm1, w_in, b_wuq, b_wukv, a_qn, a_kn, b_qln, b_kvln, b_qn, b_kn, tabs_a, tabs_b,
           half_a, half_b):
    nb, total, d = x_all.shape
    tm = TOK_TILE
    wq, wk, wv, wcq, wckv, wkr = jnp.split(
        w_in, np.cumsum([A_Q_HEADS * HEAD_DIM, A_KV_HEADS * HEAD_DIM, A_KV_HEADS * HEAD_DIM, B_Q_RANK, B_KV_RANK])
        .tolist(), axis=1)
    kr_pad = jnp.pad(wkr, ((0, 0), (B_NOPE, LANE - B_QK)))
    parts = [_pad_cols(wq, A_Q_HEADS, HEAD_DIM), _pad_cols(wk, A_KV_HEADS, HEAD_DIM),
             _pad_cols(wv, A_KV_HEADS, HEAD_DIM), wcq, wckv, kr_pad]
    offs = tuple(int(o) for o in np.cumsum([0] + [p.shape[1] for p in parts]))
    w_all = jnp.concatenate(parts, axis=1).astype(BF16)
    wuq = _pad_cols(b_wuq, B_HEADS, B_QK).astype(BF16)
    wukv = b_wukv.reshape(B_KV_RANK, B_HEADS, B_NOPE + B_V)
    wuk = _pad_cols(wukv[..., :B_NOPE].reshape(B_KV_RANK, -1), B_HEADS, B_NOPE).astype(BF16)
    wuv = _pad_cols(wukv[..., B_NOPE:].reshape(B_KV_RANK, -1), B_HEADS, B_V).astype(BF16)
    consts = [w_all, wuq, wuk, wuv, _pad_gain(a_qn, HEAD_DIM), _pad_gain(a_kn, HEAD_DIM),
              b_qln.astype(F32).reshape(1, -1), b_kvln.astype(F32).reshape(1, -1),
              _pad_gain(b_qn, B_QK), _pad_gain(b_kn, B_QK)]
    tab_spec = pl.BlockSpec((tm, LANE), lambda b, t: (t, 0))
    wide = lambda nh: pl.BlockSpec((1, tm, nh * LANE), lambda b, t: (b, t, 0))
    shp = lambda nh: jax.ShapeDtypeStruct((nb, total, nh * LANE), BF16)
    return pl.pallas_call(
        functools.partial(_proj0_kernel, d_model=d, half_a=half_a, half_b=half_b, offs=offs),
        grid=(nb, total // tm),
        in_specs=[pl.BlockSpec((1, tm, d), lambda b, t: (b, t, 0)), _full(norm1),
                  _mod_spec(0, 0, rows, nb, d), _mod_spec(0, 1, rows, nb, d)]
                 + [_full(c) for c in consts] + [tab_spec] * 6,
        out_specs=[wide(A_Q_HEADS), wide(A_KV_HEADS), wide(A_KV_HEADS), wide(B_HEADS), wide(B_HEADS), wide(B_HEADS)],
        out_shape=[shp(A_Q_HEADS), shp(A_KV_HEADS), shp(A_KV_HEADS), shp(B_HEADS), shp(B_HEADS), shp(B_HEADS)],
        compiler_params=_cparams("arbitrary", "arbitrary"),
        name="proj0",
    )(x_all, norm1, mods, mods, *consts, *tabs_a, *tabs_b)


_NT = (((1,), (1,)), ((), ()))


def _stack_heads(q_ref, grp):
    if grp == 1:
        return q_ref[0]
    return jnp.concatenate([q_ref[0, :, g * LANE:(g + 1) * LANE] for g in range(grp)], axis=0)


def _softmax_av(scores, values, sink=None):
    m = None
    for s in scores:
        ms = jnp.max(s, axis=-1, keepdims=True)
        m = ms if m is None else jnp.maximum(m, ms)
    if sink is not None:
        m = jnp.maximum(m, sink)
    den, o = None, None
    for s, v in zip(scores, values):
        p = jnp.exp(s - m)
        ds = jnp.sum(p, axis=-1, keepdims=True)
        os_ = jnp.dot(p.astype(BF16), v, preferred_element_type=F32)
        den = ds if den is None else den + ds
        o = os_ if o is None else o + os_
    if sink is not None:
        den = den + jnp.exp(sink - m)
    return o / den


def _unstack_store(o, o_ref, grp, tq):
    for g in range(grp):
        o_ref[0, :, g * LANE:(g + 1) * LANE] = o[g * tq:(g + 1) * tq].astype(BF16)


def _attn_dense_kernel(q_ref, k_ref, v_ref, o_ref, *, grp, ctx_len, tq):
    qt = pl.program_id(2)
    q = _stack_heads(q_ref, grp)

    @pl.when(qt * tq < ctx_len)
    def _():
        s = lax.dot_general(q, k_ref[0, 0:ctx_len, :], _NT, preferred_element_type=F32)
        _unstack_store(_softmax_av([s], [v_ref[0, 0:ctx_len, :]]), o_ref, grp, tq)

    @pl.when(qt * tq >= ctx_len)
    def _():
        s = lax.dot_general(q, k_ref[0], _NT, preferred_element_type=F32)
        _unstack_store(_softmax_av([s], [v_ref[0]]), o_ref, grp, tq)


def _attn_dense(q, k, v, *, ctx_len):
    nb, total, qw = q.shape
    hq, hkv = qw // LANE, k.shape[2] // LANE
    grp = hq // hkv
    tq = CHUNK if grp > 1 else TOK_TILE
    assert ctx_len % tq == 0
    return pl.pallas_call(
        functools.partial(_attn_dense_kernel, grp=grp, ctx_len=ctx_len, tq=tq),
        grid=(nb, hkv, total // tq),
        in_specs=[pl.BlockSpec((1, tq, grp * LANE), lambda b, h, t: (b, t, h)),
                  pl.BlockSpec((1, total, LANE), lambda b, h, t: (b, 0, h)),
                  pl.BlockSpec((1, total, LANE), lambda b, h, t: (b, 0, h))],
        out_specs=pl.BlockSpec((1, tq, grp * LANE), lambda b, h, t: (b, t, h)),
        out_shape=jax.ShapeDtypeStruct(q.shape, BF16),
        compiler_params=_cparams("arbitrary", "arbitrary", "arbitrary"),
        name="attn_dense",
    )(q, k, v)


def _attn_window_kernel(q_ref, k_ref, v_ref, sink_ref, o_ref, *, grp, ctx_len, tq, total, window):
    qt = pl.program_id(2)
    q = _stack_heads(q_ref, grp)
    sink = jnp.concatenate([jnp.broadcast_to(sink_ref[g][:, 0:1], (tq, 1)) for g in range(grp)], axis=0)
    kc, vc = k_ref[0, 0:ctx_len, :], v_ref[0, 0:ctx_len, :]

    @pl.when(qt * tq < ctx_len)
    def _():
        s_c = lax.dot_general(q, kc, _NT, preferred_element_type=F32)
        _unstack_store(_softmax_av([s_c], [vc], sink), o_ref, grp, tq)

    @pl.when(qt * tq >= ctx_len)
    def _():
        slab = 3 * tq
        start = pl.multiple_of(jnp.clip((qt - 1) * tq, ctx_len, total - slab), tq)
        s_c = lax.dot_general(q, kc, _NT, preferred_element_type=F32)
        s_l = lax.dot_general(q, k_ref[0, pl.ds(start, slab), :], _NT, preferred_element_type=F32)
        qpos = qt * tq + lax.broadcasted_iota(jnp.int32, (tq, slab), 0)
        kpos = start + lax.broadcasted_iota(jnp.int32, (tq, slab), 1)
        bias = jnp.where(jnp.abs(qpos - kpos) <= window, 0.0, NEG_INF)
        s_l = s_l + jnp.concatenate([bias] * grp, axis=0)
        _unstack_store(_softmax_av([s_c, s_l], [vc, v_ref[0, pl.ds(start, slab), :]], sink), o_ref, grp, tq)


def _attn_window(q, k, v, sink, *, ctx_len):
    nb, total, qw = q.shape
    hq, hkv = qw // LANE, k.shape[2] // LANE
    grp = hq // hkv
    tq = CHUNK
    sink_rows = jnp.broadcast_to(sink.astype(F32).reshape(hq, 1, 1), (hq, 1, LANE))
    return pl.pallas_call(
        functools.partial(_attn_window_kernel, grp=grp, ctx_len=ctx_len, tq=tq, total=total, window=A_WINDOW),
        grid=(nb, hkv, total // tq),
        in_specs=[pl.BlockSpec((1, tq, grp * LANE), lambda b, h, t: (b, t, h)),
                  pl.BlockSpec((1, total, LANE), lambda b, h, t: (b, 0, h)),
                  pl.BlockSpec((1, total, LANE), lambda b, h, t: (b, 0, h)),
                  pl.BlockSpec((grp, 1, LANE), lambda b, h, t: (h, 0, 0))],
        out_specs=pl.BlockSpec((1, tq, grp * LANE), lambda b, h, t: (b, t, h)),
        out_shape=jax.ShapeDtypeStruct(q.shape, BF16),
        compiler_params=_cparams("arbitrary", "arbitrary", "arbitrary"),
        name="attn_window",
    )(q, k, v, sink_rows)


def _out0_kernel(oa_ref, ob_ref, woa_ref, wob_ref, x_ref, g1_ref, n2_ref, sh2_ref, sc2_ref, xn_ref, h2_ref, *, d_model):
    y = (jnp.dot(oa_ref[0], woa_ref[...], preferred_element_type=F32)
         + jnp.dot(ob_ref[0], wob_ref[...], preferred_element_type=F32))
    xn = x_ref[0] + g1_ref[0] * y
    xn_ref[0] = xn
    h2_ref[0] = _modulate(xn, n2_ref, sc2_ref, sh2_ref, d_model).astype(BF16)


def _out0(oa, ob, w_o, x_all, mods, rows, norm2):
    nb, total, d = x_all.shape
    tm = TOK_TILE
    woa = _pad_rows(w_o[:A_Q_HEADS * HEAD_DIM], A_Q_HEADS, HEAD_DIM).astype(BF16)
    wob = _pad_rows(w_o[A_Q_HEADS * HEAD_DIM:], B_HEADS, B_V).astype(BF16)
    tile = lambda w: pl.BlockSpec((1, tm, w), lambda b, t: (b, t, 0))
    return pl.pallas_call(
        functools.partial(_out0_kernel, d_model=d),
        grid=(nb, total // tm),
        in_specs=[tile(oa.shape[2]), tile(ob.shape[2]), _full(woa), _full(wob), tile(d),
                  _mod_spec(0, 2, rows, nb, d), _full(norm2), _mod_spec(0, 3, rows, nb, d),
                  _mod_spec(0, 4, rows, nb, d)],
        out_specs=[tile(d), tile(d)],
        out_shape=[jax.ShapeDtypeStruct((nb, total, d), F32), jax.ShapeDtypeStruct((nb, total, d), BF16)],
        compiler_params=_cparams("arbitrary", "arbitrary"),
        name="out_proj0",
    )(oa, ob, woa, wob, x_all, mods, norm2, mods, mods)


def _top_rows(sc, rowf, k):
    n = sc.shape[0]
    vals, idxs = [], []
    work = sc
    for _ in range(k):
        m = jnp.max(work, axis=0, keepdims=True)
        idx = jnp.min(jnp.where(work == m, rowf, float(n)), axis=0, keepdims=True)
        vals.append(m)
        idxs.append(idx)
        work = jnp.where(rowf == idx, -jnp.inf, work)
    return vals, idxs


def _peer_select_kernel(h_ref, wq_ref, keys_ref, cc_ref, e0_ref, rb_ref, e1_ref, q_sc, *, n_heads, topk):
    nk = PEER_N_KEYS
    ts = h_ref.shape[0]
    q_sc[...] = lax.dot_general(wq_ref[...], h_ref[...], (((1,), (1,)), ((), ())), preferred_element_type=F32)
    rowf = lax.broadcasted_iota(jnp.int32, (nk, ts), 0).astype(F32)
    row16 = lax.broadcasted_iota(jnp.int32, (topk, ts), 0).astype(F32)
    slab_rows = [topk] + [SUBLANE] * (topk - 1)
    n_cand = sum(slab_rows)
    ci = lax.broadcasted_iota(jnp.int32, (n_cand, ts), 0)
    rest = ci - topk
    flat = jnp.where(ci < topk, ci, (1 + (rest >> 3)) * topk + (rest & 7)).astype(F32)

    def head_body(hd, carry):
        tops = []
        for p in range(2):
            hp = hd * 2 + p
            qhp = q_sc[pl.ds(pl.multiple_of(hp * nk, nk), nk), :].astype(BF16)
            sc = jnp.dot(keys_ref[hp], qhp, preferred_element_type=F32)
            vals, idxs = _top_rows(sc, rowf, topk)
            tops.append((sc, vals, idxs))
        (sc0, v0, i0), (sc1, v1, i1) = tops
        s1 = jnp.zeros((topk, ts), F32)
        for k in range(topk):
            s1 = jnp.where(row16 == float(k), v1[k], s1)
        cand = jnp.concatenate([v0[k1] + s1[0:slab_rows[k1], :] for k1 in range(topk)], axis=0)
        cnt = jnp.zeros((topk, ts), F32)
        zsum = jnp.zeros((1, ts), F32)
        best0 = None
        work = cand
        for k in range(topk):
            m = jnp.max(work, axis=0, keepdims=True)
            idx = jnp.min(jnp.where(work == m, flat, 1e9), axis=0, keepdims=True)
            work = jnp.where(flat == idx, -jnp.inf, work)
            best0 = m if best0 is None else best0
            zsum = zsum + jnp.exp(m - best0)
            cnt = cnt + jnp.where(row16 == jnp.floor(idx * (1.0 / topk)), 1.0, 0.0)
        cc = jnp.zeros((nk, ts), F32)
        rb = jnp.full((nk, ts), 99.0, F32)
        for k in range(topk):
            ck = jnp.sum(jnp.where(row16 == float(k), cnt, 0.0), axis=0, keepdims=True)
            cc = jnp.where(rowf == i0[k], ck, cc)
            rb = jnp.where(rowf == i1[k], float(k), rb)
        cc_ref[hd] = cc
        rb_ref[hd] = rb.astype(BF16)
        e0_ref[hd] = jnp.exp(sc0 - v0[0])
        e1_ref[hd] = (jnp.exp(sc1 - v1[0]) / zsum).astype(BF16)
        return carry

    lax.fori_loop(0, n_heads, head_body, 0)


def _peer_select(h2, w_q, keys):
    t, d = h2.shape
    ts = PEER_SEL_TILE
    nh, nk = PEER_HEADS, PEER_N_KEYS
    wq_t = w_q.T.astype(BF16)
    keys2 = keys.reshape(nh * 2, nk, PEER_D_KEY // 2).astype(BF16)
    row_out = jax.ShapeDtypeStruct((nh, nk, t), F32)
    col_out = jax.ShapeDtypeStruct((nh, nk, t), BF16)
    ospec = pl.BlockSpec((nh, nk, ts), lambda i: (0, 0, i))
    return pl.pallas_call(
        functools.partial(_peer_select_kernel, n_heads=nh, topk=PEER_TOPK),
        grid=(t // ts,),
        in_specs=[pl.BlockSpec((ts, d), lambda i: (i, 0)), _full(wq_t), _full(keys2)],
        out_specs=[ospec] * 4,
        out_shape=[row_out, row_out, col_out, col_out],
        scratch_shapes=[pltpu.VMEM((wq_t.shape[0], ts), F32)],
        compiler_params=_cparams("arbitrary"),
        name="peer_select",
    )(h2, wq_t, keys2)


def _peer_apply_kernel(*refs, n_heads, final):
    if final:
        (h_ref, u_ref, vt_ref, cc_ref, e0_ref, rb_ref, e1_ref, x_ref, g_ref, o_ref,
         acc_ref, g_sc, p_sc, ht_sc) = refs
    else:
        h_ref, u_ref, vt_ref, cc_ref, e0_ref, rb_ref, e1_ref, o_ref, acc_ref, g_sc, p_sc, ht_sc = refs
    et = pl.program_id(1)
    nk = PEER_N_KEYS
    rows_per_tile = cc_ref.shape[1]
    tt = h_ref.shape[0]

    @pl.when(jnp.logical_and(pl.program_id(0) == 0, et == 0))
    def _():
        g_sc[...] = jnp.zeros_like(g_sc)
        acc_ref[...] = jnp.zeros_like(acc_ref)

    @pl.when(et == 0)
    def _():
        ht_sc[...] = h_ref[...].astype(F32).T.astype(BF16)

    at = jnp.dot(u_ref[...], ht_sc[...], preferred_element_type=F32)

    zero = jnp.zeros((), BF16)
    prev = (et + 1) % 2
    for ii in range(rows_per_tile):
        for c0 in range(0, tt, GATE_COLS):
            cols = slice(c0, c0 + GATE_COLS)
            ccr = [jnp.broadcast_to(cc_ref[hd, ii:ii + 1, cols], (BF16_ROWS, GATE_COLS)).astype(BF16)
                   for hd in range(n_heads)]
            e0r = [jnp.broadcast_to(e0_ref[hd, ii:ii + 1, cols], (BF16_ROWS, GATE_COLS)).astype(BF16)
                   for hd in range(n_heads)]
            for s0 in range(0, nk, BF16_ROWS):
                rws = slice(s0, s0 + BF16_ROWS)
                w = None
                for hd in range(n_heads):
                    term = jnp.where(rb_ref[hd, rws, cols] < ccr[hd], e1_ref[hd, rws, cols], zero) * e0r[hd]
                    w = term if w is None else w + term
                r0 = ii * nk + s0
                p_sc[r0:r0 + BF16_ROWS, cols] = w * g_sc[prev, r0:r0 + BF16_ROWS, cols]
    acc_ref[...] += jnp.dot(vt_ref[0], p_sc[...], preferred_element_type=F32)
    g_sc[et % 2] = jax.nn.gelu(at).astype(BF16)

    @pl.when(et == 0)
    def _():
        acc_ref[...] = jnp.zeros_like(acc_ref)

    @pl.when(et == pl.num_programs(1) - 1)
    def _():
        f = acc_ref[...].T
        if final:
            o_ref[...] = x_ref[...] + g_ref[0] * f
        else:
            o_ref[...] = f


def _peer_apply(h2, sel, u_tab, v_tab, x=None, mods=None, mod_index=None, tokens_per_batch=None):
    t, d = h2.shape
    cc, e0, rb, e1 = sel
    nh, nk = PEER_HEADS, PEER_N_KEYS
    tt, te = PEER_TOK_TILE, PEER_EXP_TILE
    ti = te // nk
    n_et = u_tab.shape[0] // te
    u = u_tab.astype(BF16)
    vt = v_tab.reshape(n_et, te, d).transpose(0, 2, 1).astype(BF16)
    final = x is not None
    cur = lambda e: jnp.minimum(e, n_et - 1)
    prv = lambda e: jnp.maximum(e - 1, 0)
    row_spec = pl.BlockSpec((nh, ti, tt), lambda i, e: (0, prv(e), i))
    col_spec = pl.BlockSpec((nh, nk, tt), lambda i, e: (0, 0, i))
    in_specs = [pl.BlockSpec((tt, d), lambda i, e: (i, 0)),
                pl.BlockSpec((te, d), lambda i, e: (cur(e), 0)),
                pl.BlockSpec((1, d, te), lambda i, e: (prv(e), 0, 0)),
                row_spec, row_spec, col_spec, col_spec]
    args = [h2, u, vt, cc, e0, rb, e1]
    if final:
        per = tokens_per_batch // tt
        in_specs += [pl.BlockSpec((tt, d), lambda i, e: (i, 0)),
                     pl.BlockSpec((1, 1, d), lambda i, e: (mod_index(i // per), 0, 0))]
        args += [x, mods]
    return pl.pallas_call(
        functools.partial(_peer_apply_kernel, n_heads=nh, final=final),
        grid=(t // tt, n_et + 1),
        in_specs=in_specs,
        out_specs=pl.BlockSpec((tt, d), lambda i, e: (i, 0)),
        out_shape=jax.ShapeDtypeStruct((t, d), F32),
        scratch_shapes=[pltpu.VMEM((d, tt), F32), pltpu.VMEM((2, te, tt), BF16), pltpu.VMEM((te, tt), BF16),
                        pltpu.VMEM((d, tt), BF16)],
        compiler_params=_cparams("arbitrary", "arbitrary"),
        name="peer_apply_final" if final else "peer_apply",
    )(*args)


def _proj1_kernel(x_ref, f_ref, g2_ref, n1_ref, sh_ref, sc_ref, w_ref, dqn_ref, dkn_ref, ca_ref, ua_ref, da_ref,
                  xn_ref, qr_ref, kr_ref, vr_ref, gr_ref, qd_ref, kd_ref, vd_ref, *, d_model, half_a, offs):
    xn = x_ref[0] + g2_ref[0] * f_ref[0]
    xn_ref[0] = xn
    h = _modulate(xn, n1_ref, sc_ref, sh_ref, d_model).astype(BF16)
    ca, ua, da = ca_ref[...], ua_ref[...], da_ref[...]
    o_qr, o_kr, o_vr, o_gr, o_qd, o_kd, o_vd, o_end = offs

    z = jnp.dot(h, w_ref[:, o_qr:o_kr], preferred_element_type=F32)
    for i in range(C_HEADS):
        qr_ref[0, :, i * LANE:(i + 1) * LANE] = _rope(z[:, i * LANE:(i + 1) * LANE], ca, ua, da, half_a).astype(BF16)
    z = jnp.dot(h, w_ref[:, o_kr:o_vr], preferred_element_type=F32) * (C_DK ** -0.5)
    for i in range(C_HEADS):
        kr_ref[0, :, i * LANE:(i + 1) * LANE] = _rope(z[:, i * LANE:(i + 1) * LANE], ca, ua, da, half_a).astype(BF16)
    vr_ref[0] = jnp.dot(h, w_ref[:, o_vr:o_gr], preferred_element_type=F32).astype(BF16)
    gr_ref[0] = jnp.dot(h, w_ref[:, o_gr:o_qd], preferred_element_type=F32).astype(BF16)
    z = jnp.dot(h, w_ref[:, o_qd:o_kd], preferred_element_type=F32)
    for i in range(D_Q_HEADS):
        y = _rms_rows(z[:, i * LANE:(i + 1) * LANE], HEAD_DIM) * dqn_ref[...]
        qd_ref[0, :, i * LANE:(i + 1) * LANE] = (_rope(y, ca, ua, da, half_a) * HEAD_DIM ** -0.5).astype(BF16)
    z = jnp.dot(h, w_ref[:, o_kd:o_vd], preferred_element_type=F32)
    for i in range(D_KV_HEADS):
        y = _rms_rows(z[:, i * LANE:(i + 1) * LANE], HEAD_DIM) * dkn_ref[...]
        kd_ref[0, :, i * LANE:(i + 1) * LANE] = _rope(y, ca, ua, da, half_a).astype(BF16)
    vd_ref[0] = jnp.dot(h, w_ref[:, o_vd:o_end], preferred_element_type=F32).astype(BF16)


def _proj1(x_all, f_all, mods, rows, norm1, w_in, d_qn, d_kn, tabs_a, half_a):
    nb, total, d = x_all.shape
    tm = TOK_TILE
    cqk, cv = C_HEADS * C_DK, C_HEADS * C_DV
    wqr, wkr, wvr, wgr, wqd, wkd, wvd = jnp.split(
        w_in, np.cumsum([cqk, cqk, cv, cv, D_Q_HEADS * HEAD_DIM, D_KV_HEADS * HEAD_DIM]).tolist(), axis=1)
    parts = [_pad_cols(wqr, C_HEADS, C_DK), _pad_cols(wkr, C_HEADS, C_DK), wvr, wgr,
             _pad_cols(wqd, D_Q_HEADS, HEAD_DIM), _pad_cols(wkd, D_KV_HEADS, HEAD_DIM),
             _pad_cols(wvd, D_KV_HEADS, HEAD_DIM)]
    offs = tuple(int(o) for o in np.cumsum([0] + [p.shape[1] for p in parts]))
    w_all = jnp.concatenate(parts, axis=1).astype(BF16)
    consts = [w_all, _pad_gain(d_qn, HEAD_DIM), _pad_gain(d_kn, HEAD_DIM)]
    tab_spec = pl.BlockSpec((tm, LANE), lambda b, t: (t, 0))
    tile = lambda w: pl.BlockSpec((1, tm, w), lambda b, t: (b, t, 0))
    widths = [p.shape[1] for p in parts]
    return pl.pallas_call(
        functools.partial(_proj1_kernel, d_model=d, half_a=half_a, offs=offs),
        grid=(nb, total // tm),
        in_specs=[tile(d), tile(d), _mod_spec(0, 5, rows, nb, d), _full(norm1),
                  _mod_spec(1, 0, rows, nb, d), _mod_spec(1, 1, rows, nb, d)]
                 + [_full(c) for c in consts] + [tab_spec] * 3,
        out_specs=[tile(d)] + [tile(w) for w in widths],
        out_shape=[jax.ShapeDtypeStruct((nb, total, d), F32)]
                  + [jax.ShapeDtypeStruct((nb, total, w), BF16) for w in widths],
        compiler_params=_cparams("arbitrary", "arbitrary"),
        name="proj1",
    )(x_all, f_all, mods, norm1, mods, mods, *consts, *tabs_a)


def _retention_kernel(lg_ref, qf_ref, kf_ref, vf_ref, qb_ref, kb_ref, vb_ref, of_ref, ob_ref, st_ref, *, n_heads):
    step = pl.program_id(1)

    @pl.when(step == 0)
    def _():
        st_ref[...] = jnp.zeros_like(st_ref)

    c = CHUNK
    ri = lax.broadcasted_iota(jnp.int32, (c, LANE), 0).astype(F32)
    diff = ri - lax.broadcasted_iota(jnp.int32, (c, LANE), 1).astype(F32)
    nt = (((1,), (1,)), ((), ()))
    for d, (q_ref, k_ref, v_ref, o_ref) in enumerate(((qf_ref, kf_ref, vf_ref, of_ref),
                                                      (qb_ref, kb_ref, vb_ref, ob_ref))):
        for hd in range(n_heads):
            lg = lg_ref[d, hd]
            sl = slice(hd * LANE, (hd + 1) * LANE)
            q, k, v = q_ref[0, :, sl], k_ref[0, :, sl], v_ref[0, :, sl]
            if d == 0:
                dec = jnp.where(diff >= 0, jnp.exp(lg * jnp.maximum(diff, 0.0)), 0.0)
                q_dec = jnp.exp(lg * (ri + 1.0))
                k_dec = jnp.exp(lg * (c - 1.0 - ri))
            else:
                dec = jnp.where(diff <= 0, jnp.exp(lg * jnp.maximum(-diff, 0.0)), 0.0)
                q_dec = jnp.exp(lg * (c - ri))
                k_dec = jnp.exp(lg * ri)
            s = lax.dot_general(q, k, nt, preferred_element_type=F32) * dec
            inner = jnp.dot(s.astype(BF16), v, preferred_element_type=F32)
            st = st_ref[d, hd]
            cross = jnp.dot(q, st.astype(BF16), preferred_element_type=F32) * q_dec
            o_ref[0, :, sl] = inner + cross
            kd_t = (k.astype(F32) * k_dec).T.astype(BF16)
            st_ref[d, hd] = st * jnp.exp(lg * c) + jnp.dot(kd_t, v, preferred_element_type=F32)


def _retention(qr, kr, vr, lg, ctx_len):
    nb, total, w = qr.shape
    nh = w // LANE
    c = CHUNK
    nc, nctx = total // c, ctx_len // c
    fwd = pl.BlockSpec((1, c, w), lambda b, s: (b, s, 0))

    def bmap(b, s):
        return (b, jnp.where(s < nctx, nctx - 1 - s, nc - 1 - (s - nctx)), 0)

    bwd = pl.BlockSpec((1, c, w), bmap)
    out = jax.ShapeDtypeStruct((nb, total, w), F32)
    return pl.pallas_call(
        functools.partial(_retention_kernel, n_heads=nh),
        grid=(nb, nc),
        in_specs=[pl.BlockSpec(memory_space=pltpu.SMEM), fwd, fwd, fwd, bwd, bwd, bwd],
        out_specs=[fwd, bwd],
        out_shape=[out, out],
        scratch_shapes=[pltpu.VMEM((2, nh, LANE, LANE), F32)],
        compiler_params=_cparams("arbitrary", "arbitrary"),
        name="retention",
    )(lg, qr, kr, vr, qr, kr, vr)


def _out1_kernel(of_ref, ob_ref, gr_ref, gn_ref, od_ref, wor_ref, wod_ref, x_ref, g1_ref, n2_ref, sh2_ref, sc2_ref,
                 xn_ref, h2_ref, *, d_model, n_heads):
    o = of_ref[0] + ob_ref[0]
    g = gr_ref[0].astype(F32)
    gate = g * jax.nn.sigmoid(g)
    gn = gn_ref[...]
    ys = []
    for hd in range(n_heads):
        sl = slice(hd * LANE, (hd + 1) * LANE)
        oh = o[:, sl]
        mu = jnp.mean(oh, axis=-1, keepdims=True)
        var = jnp.mean(jnp.square(oh - mu), axis=-1, keepdims=True)
        ys.append((gate[:, sl] * ((oh - mu) * lax.rsqrt(var + EPS) * gn[:, sl])).astype(BF16))
    y_ret = jnp.concatenate(ys, axis=1)
    y = (jnp.dot(y_ret, wor_ref[...], preferred_element_type=F32)
         + jnp.dot(od_ref[0], wod_ref[...], preferred_element_type=F32))
    xn = x_ref[0] + g1_ref[0] * y
    xn_ref[0] = xn
    h2_ref[0] = _modulate(xn, n2_ref, sc2_ref, sh2_ref, d_model).astype(BF16)


def _out1(o_f, o_b, g_r, gn_w, o_d, w_o, x_all, mods, rows, norm2, ctx_len):
    nb, total, d = x_all.shape
    tm = TOK_TILE
    skip = ctx_len // tm
    seq = total - ctx_len
    wor = w_o[:C_HEADS * C_DV].astype(BF16)
    wod = _pad_rows(w_o[C_HEADS * C_DV:], D_Q_HEADS, HEAD_DIM).astype(BF16)
    gn = gn_w.astype(F32).reshape(1, -1)
    tile_in = lambda w: pl.BlockSpec((1, tm, w), lambda b, t: (b, t + skip, 0))
    tile_out = pl.BlockSpec((1, tm, d), lambda b, t: (b, t, 0))
    mod = lambda chunk: pl.BlockSpec((1, 1, d), lambda b, t: ((rows + b) * 6 + chunk, 0, 0))
    return pl.pallas_call(
        functools.partial(_out1_kernel, d_model=d, n_heads=C_HEADS),
        grid=(nb, seq // tm),
        in_specs=[tile_in(o_f.shape[2]), tile_in(o_b.shape[2]), tile_in(g_r.shape[2]), _full(gn),
                  tile_in(o_d.shape[2]), _full(wor), _full(wod), tile_in(d), mod(2), _full(norm2), mod(3), mod(4)],
        out_specs=[tile_out, tile_out],
        out_shape=[jax.ShapeDtypeStruct((nb, seq, d), F32), jax.ShapeDtypeStruct((nb, seq, d), BF16)],
        compiler_params=_cparams("arbitrary", "arbitrary"),
        name="out_proj1",
    )(o_f, o_b, g_r, gn, o_d, wor, wod, x_all, mods, norm2, mods, mods)


def kernel(x, c, ctx, c_ctx, ada_w, ada_b, norm1_w, norm2_w, ab_w_in, ab_w_o, a_q_norm, a_k_norm, a_sink,
           b_q_lora_norm, b_kv_lora_norm, b_w_uq, b_w_ukv, b_q_norm, b_k_norm, cd_w_in, cd_w_o, c_decay_fwd,
           c_decay_bwd, c_gn_w, d_q_norm, d_k_norm, peer_w_q, peer_keys, peer_u, peer_v):
    nb, seq, d = x.shape
    ctx_len = ctx.shape[1]
    total = ctx_len + seq
    assert ctx_len == TOK_TILE and seq % TOK_TILE == 0 and seq % GRID_W == 0

    rows = -(-(nb + 1) // SUBLANE) * SUBLANE
    c_rows = jnp.concatenate([c, c_ctx[None, :], jnp.zeros((rows - nb - 1, d), c.dtype)], axis=0).astype(F32)
    mods = _ada(c_rows, ada_w, ada_b).reshape(-1, 1, d)

    tabs_a, half_a = _rope_tables(ctx_len, seq, 0, HEAD_DIM)
    tabs_b, half_b = _rope_tables(ctx_len, seq, B_NOPE, B_ROPE)
    n1 = norm1_w.astype(F32).reshape(-1, 1, d)
    n2 = norm2_w.astype(F32).reshape(-1, 1, d)

    x_all = jnp.concatenate([ctx, x], axis=1).astype(F32)
    qa, ka, va, qb, kb, vb = _proj0(x_all, mods, rows, n1[0], ab_w_in[0], b_w_uq[0], b_w_ukv[0], a_q_norm[0],
                                    a_k_norm[0], b_q_lora_norm[0], b_kv_lora_norm[0], b_q_norm[0], b_k_norm[0],
                                    tabs_a, tabs_b, half_a, half_b)
    o_a = _attn_window(qa, ka, va, a_sink[0], ctx_len=ctx_len)
    o_b = _attn_dense(qb, kb, vb, ctx_len=ctx_len)
    x_all, h2 = _out0(o_a, o_b, ab_w_o[0], x_all, mods, rows, n2[0])
    h2 = h2.reshape(nb * total, d)
    sel = _peer_select(h2, peer_w_q[0], peer_keys[0])
    f = _peer_apply(h2, sel, peer_u[0], peer_v[0]).reshape(nb, total, d)

    x_all, qr, kr, vr, gr, qd, kd, vd = _proj1(x_all, f, mods, rows, n1[1], cd_w_in[0], d_q_norm[0], d_k_norm[0],
                                               tabs_a, half_a)
    lg = jnp.stack([jax.nn.log_sigmoid(c_decay_fwd[0].astype(F32)), jax.nn.log_sigmoid(c_decay_bwd[0].astype(F32))])
    o_f, o_bw = _retention(qr, kr, vr, lg, ctx_len)
    o_d = _attn_dense(qd, kd, vd, ctx_len=ctx_len)
    x_lat, h2 = _out1(o_f, o_bw, gr, c_gn_w[0], o_d, cd_w_o[0], x_all, mods, rows, n2[1], ctx_len)
    h2 = h2.reshape(nb * seq, d)
    sel = _peer_select(h2, peer_w_q[1], peer_keys[1])
    out = _peer_apply(h2, sel, peer_u[1], peer_v[1], x=x_lat.reshape(nb * seq, d), mods=mods,
                      mod_index=lambda b: (rows + b) * 6 + 5, tokens_per_batch=seq)
    return out.reshape(nb, seq, d).astype(x.dtype)
```

```python
import functools

import numpy as np
import jax
import jax.numpy as jnp
from jax import lax
from jax.experimental import pallas as pl
from jax.experimental.pallas import tpu as pltpu

F32 = jnp.float32
BF16 = jnp.bfloat16

GRID_W = 64
ROPE_THETA = 10000.0
EPS = 1e-6
NEG_INF = -1e30
HEAD_DIM = 64
A_Q_HEADS, A_KV_HEADS, A_WINDOW = 8, 2, 128
B_HEADS, B_NOPE, B_ROPE, B_V, B_Q_RANK, B_KV_RANK = 8, 64, 32, 64, 256, 256
B_QK = B_NOPE + B_ROPE
C_HEADS, C_DK, C_DV = 4, 64, 128
D_Q_HEADS, D_KV_HEADS = 8, 2
PEER_HEADS, PEER_N_KEYS, PEER_D_KEY, PEER_TOPK = 8, 128, 256, 16

LANE = 128
SUBLANE = 8
BF16_ROWS = 16
VMEM_LIMIT = 56 * 1024 * 1024

TOK_TILE = 256
CHUNK = 128
PEER_SEL_TILE = 256
PEER_TOK_TILE = 1024
PEER_EXP_TILE = 1024
GATE_COLS = 256


def _cparams(*sem):
    return pltpu.CompilerParams(dimension_semantics=sem, vmem_limit_bytes=VMEM_LIMIT)


def _full(arr):
    nd = arr.ndim
    return pl.BlockSpec(arr.shape, lambda *_: (0,) * nd)


def _pad_cols(w, n_heads, d):
    lead = w.shape[:-1]
    w = w.reshape(lead + (n_heads, d))
    w = jnp.pad(w, [(0, 0)] * len(lead) + [(0, 0), (0, LANE - d)])
    return w.reshape(lead + (n_heads * LANE,))


def _pad_rows(w, n_heads, d):
    n = w.shape[-1]
    w = w.reshape(n_heads, d, n)
    w = jnp.pad(w, [(0, 0), (0, LANE - d), (0, 0)])
    return w.reshape(n_heads * LANE, n)


def _pad_gain(g, d):
    return jnp.pad(g.astype(F32), (0, LANE - d)).reshape(1, LANE)


def _rope_tables(ctx_len, seq, lane_off, d_rot):
    blk = d_rot // 2
    half = blk // 2
    freqs = ROPE_THETA ** (-np.arange(half, dtype=np.float64) / half)
    pos = np.arange(seq)
    total = ctx_len + seq
    cos = np.ones((total, LANE), np.float64)
    sup = np.zeros((total, LANE), np.float64)
    sdn = np.zeros((total, LANE), np.float64)
    for axis, p in enumerate((pos // GRID_W, pos % GRID_W)):
        ang = p[:, None].astype(np.float64) * freqs[None, :]
        c, s = np.cos(ang), np.sin(ang)
        base = lane_off + axis * blk
        cos[ctx_len:, base:base + half] = c
        cos[ctx_len:, base + half:base + blk] = c
        sdn[ctx_len:, base:base + half] = -s
        sup[ctx_len:, base + half:base + blk] = s
    return (jnp.asarray(cos, F32), jnp.asarray(sup, F32), jnp.asarray(sdn, F32)), half


def _rms_rows(x, true_dim):
    return x * lax.rsqrt(jnp.sum(x * x, axis=-1, keepdims=True) * (1.0 / true_dim) + EPS)


def _rope(y, cos, sup, sdn, half):
    return y * cos + pltpu.roll(y, half, 1) * sup + pltpu.roll(y, LANE - half, 1) * sdn


def _ada_kernel(c_ref, w_ref, b_ref, o_ref):
    c = c_ref[...]
    s = c * jax.nn.sigmoid(c)
    o_ref[0] = jnp.dot(s.astype(BF16), w_ref[0].astype(BF16), preferred_element_type=F32) + b_ref[0]


def _ada(c_rows, ada_w, ada_b):
    depth, d, n = ada_w.shape
    rows = c_rows.shape[0]
    tn = 1536
    return pl.pallas_call(
        _ada_kernel,
        grid=(depth, n // tn),
        in_specs=[pl.BlockSpec((rows, d), lambda l, j: (0, 0)),
                  pl.BlockSpec((1, d, tn), lambda l, j: (l, 0, j)),
                  pl.BlockSpec((1, 1, tn), lambda l, j: (l, 0, j))],
        out_specs=pl.BlockSpec((1, rows, tn), lambda l, j: (l, 0, j)),
        out_shape=jax.ShapeDtypeStruct((depth, rows, n), F32),
        compiler_params=_cparams("arbitrary", "arbitrary"),
        name="ada_mod",
    )(c_rows, ada_w, ada_b.reshape(depth, 1, n))


def _mod_spec(layer, chunk, rows, nb, d, tile_axis=1):
    def imap(*ids):
        b, t = ids[0], ids[tile_axis]
        r = jnp.where(t == 0, nb, b)
        return ((layer * rows + r) * 6 + chunk, 0, 0)
    return pl.BlockSpec((1, 1, d), imap)


def _modulate(x, n_ref, sc_ref, sh_ref, d):
    return _rms_rows(x, d) * n_ref[...] * (1.0 + sc_ref[0]) + sh_ref[0]


def _proj0_kernel(x_ref, n1_ref, sh_ref, sc_ref, w_ref, wuq_ref, wuk_ref, wuv_ref,
                  aqn_ref, akn_ref, bqln_ref, bkvln_ref, bqn_ref, bkn_ref,
                  ca_ref, ua_ref, da_ref, cb_ref, ub_ref, db_ref,
                  qa_ref, ka_ref, va_ref, qb_ref, kb_ref, vb_ref, *, d_model, half_a, half_b, offs):
    h = _modulate(x_ref[0], n1_ref, sc_ref, sh_ref, d_model).astype(BF16)
    ca, ua, da = ca_ref[...], ua_ref[...], da_ref[...]
    cb, ub, db = cb_ref[...], ub_ref[...], db_ref[...]
    o_qa, o_ka, o_va, o_cq, o_ckv, o_kr, o_end = offs

    z = jnp.dot(h, w_ref[:, o_qa:o_ka], preferred_element_type=F32)
    for i in range(A_Q_HEADS):
        y = _rms_rows(z[:, i * LANE:(i + 1) * LANE], HEAD_DIM) * aqn_ref[...]
        qa_ref[0, :, i * LANE:(i + 1) * LANE] = (_rope(y, ca, ua, da, half_a) * HEAD_DIM ** -0.5).astype(BF16)
    z = jnp.dot(h, w_ref[:, o_ka:o_va], preferred_element_type=F32)
    for i in range(A_KV_HEADS):
        y = _rms_rows(z[:, i * LANE:(i + 1) * LANE], HEAD_DIM) * akn_ref[...]
        ka_ref[0, :, i * LANE:(i + 1) * LANE] = _rope(y, ca, ua, da, half_a).astype(BF16)
    va_ref[0] = jnp.dot(h, w_ref[:, o_va:o_cq], preferred_element_type=F32).astype(BF16)

    cq = jnp.dot(h, w_ref[:, o_cq:o_ckv], preferred_element_type=F32)
    cq = (_rms_rows(cq, B_Q_RANK) * bqln_ref[...]).astype(BF16)
    z = jnp.dot(cq, wuq_ref[...], preferred_element_type=F32)
    for i in range(B_HEADS):
        y = _rms_rows(z[:, i * LANE:(i + 1) * LANE], B_QK) * bqn_ref[...]
        qb_ref[0, :, i * LANE:(i + 1) * LANE] = (_rope(y, cb, ub, db, half_b) * B_QK ** -0.5).astype(BF16)

    ckv = jnp.dot(h, w_ref[:, o_ckv:o_kr], preferred_element_type=F32)
    ckv = (_rms_rows(ckv, B_KV_RANK) * bkvln_ref[...]).astype(BF16)
    kr = jnp.dot(h, w_ref[:, o_kr:o_end], preferred_element_type=F32)
    z = jnp.dot(ckv, wuk_ref[...], preferred_element_type=F32)
    for i in range(B_HEADS):
        y = _rms_rows(z[:, i * LANE:(i + 1) * LANE] + kr, B_QK) * bkn_ref[...]
        kb_ref[0, :, i * LANE:(i + 1) * LANE] = _rope(y, cb, ub, db, half_b).astype(BF16)
    vb_ref[0] = jnp.dot(ckv, wuv_ref[...], preferred_element_type=F32).astype(BF16)


def _proj0(x_all, mods, rows, norm1, w_in, b_wuq, b_wukv, a_qn, a_kn, b_qln, b_kvln, b_qn, b_kn, tabs_a, tabs_b,
           half_a, half_b):
    nb, total, d = x_all.shape
    tm = TOK_TILE
    wq, wk, wv, wcq, wckv, wkr = jnp.split(
        w_in, np.cumsum([A_Q_HEADS * HEAD_DIM, A_KV_HEADS * HEAD_DIM, A_KV_HEADS * HEAD_DIM, B_Q_RANK, B_KV_RANK])
        .tolist(), axis=1)
    kr_pad = jnp.pad(wkr, ((0, 0), (B_NOPE, LANE - B_QK)))
    parts = [_pad_cols(wq, A_Q_HEADS, HEAD_DIM), _pad_cols(wk, A_KV_HEADS, HEAD_DIM),
             _pad_cols(wv, A_KV_HEADS, HEAD_DIM), wcq, wckv, kr_pad]
    offs = tuple(int(o) for o in np.cumsum([0] + [p.shape[1] for p in parts]))
    w_all = jnp.concatenate(parts, axis=1).astype(BF16)
    wuq = _pad_cols(b_wuq, B_HEADS, B_QK).astype(BF16)
    wukv = b_wukv.reshape(B_KV_RANK, B_HEADS, B_NOPE + B_V)
    wuk = _pad_cols(wukv[..., :B_NOPE].reshape(B_KV_RANK, -1), B_HEADS, B_NOPE).astype(BF16)
    wuv = _pad_cols(wukv[..., B_NOPE:].reshape(B_KV_RANK, -1), B_HEADS, B_V).astype(BF16)
    consts = [w_all, wuq, wuk, wuv, _pad_gain(a_qn, HEAD_DIM), _pad_gain(a_kn, HEAD_DIM),
              b_qln.astype(F32).reshape(1, -1), b_kvln.astype(F32).reshape(1, -1),
              _pad_gain(b_qn, B_QK), _pad_gain(b_kn, B_QK)]
    tab_spec = pl.BlockSpec((tm, LANE), lambda b, t: (t, 0))
    wide = lambda nh: pl.BlockSpec((1, tm, nh * LANE), lambda b, t: (b, t, 0))
    shp = lambda nh: jax.ShapeDtypeStruct((nb, total, nh * LANE), BF16)
    return pl.pallas_call(
        functools.partial(_proj0_kernel, d_model=d, half_a=half_a, half_b=half_b, offs=offs),
        grid=(nb, total // tm),
        in_specs=[pl.BlockSpec((1, tm, d), lambda b, t: (b, t, 0)), _full(norm1),
                  _mod_spec(0, 0, rows, nb, d), _mod_spec(0, 1, rows, nb, d)]
                 + [_full(c) for c in consts] + [tab_spec] * 6,
        out_specs=[wide(A_Q_HEADS), wide(A_KV_HEADS), wide(A_KV_HEADS), wide(B_HEADS), wide(B_HEADS), wide(B_HEADS)],
        out_shape=[shp(A_Q_HEADS), shp(A_KV_HEADS), shp(A_KV_HEADS), shp(B_HEADS), shp(B_HEADS), shp(B_HEADS)],
        compiler_params=_cparams("arbitrary", "arbitrary"),
        name="proj0",
    )(x_all, norm1, mods, mods, *consts, *tabs_a, *tabs_b)


_NT = (((1,), (1,)), ((), ()))


def _stack_heads(q_ref, grp):
    if grp == 1:
        return q_ref[0]
    return jnp.concatenate([q_ref[0, :, g * LANE:(g + 1) * LANE] for g in range(grp)], axis=0)


def _softmax_av(scores, values, sink=None):
    m = None
    for s in scores:
        ms = jnp.max(s, axis=-1, keepdims=True)
        m = ms if m is None else jnp.maximum(m, ms)
    if sink is not None:
        m = jnp.maximum(m, sink)
    den, o = None, None
    for s, v in zip(scores, values):
        p = jnp.exp(s - m)
        ds = jnp.sum(p, axis=-1, keepdims=True)
        os_ = jnp.dot(p.astype(BF16), v, preferred_element_type=F32)
        den = ds if den is None else den + ds
        o = os_ if o is None else o + os_
    if sink is not None:
        den = den + jnp.exp(sink - m)
    return o / den


def _unstack_store(o, o_ref, grp, tq):
    for g in range(grp):
        o_ref[0, :, g * LANE:(g + 1) * LANE] = o[g * tq:(g + 1) * tq].astype(BF16)


def _attn_dense_kernel(q_ref, k_ref, v_ref, o_ref, *, grp, ctx_len, tq):
    qt = pl.program_id(2)
    q = _stack_heads(q_ref, grp)

    @pl.when(qt * tq < ctx_len)
    def _():
        s = lax.dot_general(q, k_ref[0, 0:ctx_len, :], _NT, preferred_element_type=F32)
        _unstack_store(_softmax_av([s], [v_ref[0, 0:ctx_len, :]]), o_ref, grp, tq)

    @pl.when(qt * tq >= ctx_len)
    def _():
        s = lax.dot_general(q, k_ref[0], _NT, preferred_element_type=F32)
        _unstack_store(_softmax_av([s], [v_ref[0]]), o_ref, grp, tq)


def _attn_dense(q, k, v, *, ctx_len):
    nb, total, qw = q.shape
    hq, hkv = qw // LANE, k.shape[2] // LANE
    grp = hq // hkv
    tq = CHUNK if grp > 1 else TOK_TILE
    assert ctx_len % tq == 0
    return pl.pallas_call(
        functools.partial(_attn_dense_kernel, grp=grp, ctx_len=ctx_len, tq=tq),
        grid=(nb, hkv, total // tq),
        in_specs=[pl.BlockSpec((1, tq, grp * LANE), lambda b, h, t: (b, t, h)),
                  pl.BlockSpec((1, total, LANE), lambda b, h, t: (b, 0, h)),
                  pl.BlockSpec((1, total, LANE), lambda b, h, t: (b, 0, h))],
        out_specs=pl.BlockSpec((1, tq, grp * LANE), lambda b, h, t: (b, t, h)),
        out_shape=jax.ShapeDtypeStruct(q.shape, BF16),
        compiler_params=_cparams("arbitrary", "arbitrary", "arbitrary"),
        name="attn_dense",
    )(q, k, v)


def _attn_window_kernel(q_ref, k_ref, v_ref, sink_ref, o_ref, *, grp, ctx_len, tq, total, window):
    qt = pl.program_id(2)
    q = _stack_heads(q_ref, grp)
    sink = jnp.concatenate([jnp.broadcast_to(sink_ref[g][:, 0:1], (tq, 1)) for g in range(grp)], axis=0)
    kc, vc = k_ref[0, 0:ctx_len, :], v_ref[0, 0:ctx_len, :]

    @pl.when(qt * tq < ctx_len)
    def _():
        s_c = lax.dot_general(q, kc, _NT, preferred_element_type=F32)
        _unstack_store(_softmax_av([s_c], [vc], sink), o_ref, grp, tq)

    @pl.when(qt * tq >= ctx_len)
    def _():
        slab = 3 * tq
        start = pl.multiple_of(jnp.clip((qt - 1) * tq, ctx_len, total - slab), tq)
        s_c = lax.dot_general(q, kc, _NT, preferred_element_type=F32)
        s_l = lax.dot_general(q, k_ref[0, pl.ds(start, slab), :], _NT, preferred_element_type=F32)
        qpos = qt * tq + lax.broadcasted_iota(jnp.int32, (tq, slab), 0)
        kpos = start + lax.broadcasted_iota(jnp.int32, (tq, slab), 1)
        bias = jnp.where(jnp.abs(qpos - kpos) <= window, 0.0, NEG_INF)
        s_l = s_l + jnp.concatenate([bias] * grp, axis=0)
        _unstack_store(_softmax_av([s_c, s_l], [vc, v_ref[0, pl.ds(start, slab), :]], sink), o_ref, grp, tq)


def _attn_window(q, k, v, sink, *, ctx_len):
    nb, total, qw = q.shape
    hq, hkv = qw // LANE, k.shape[2] // LANE
    grp = hq // hkv
    tq = CHUNK
    sink_rows = jnp.broadcast_to(sink.astype(F32).reshape(hq, 1, 1), (hq, 1, LANE))
    return pl.pallas_call(
        functools.partial(_attn_window_kernel, grp=grp, ctx_len=ctx_len, tq=tq, total=total, window=A_WINDOW),
        grid=(nb, hkv, total // tq),
        in_specs=[pl.BlockSpec((1, tq, grp * LANE), lambda b, h, t: (b, t, h)),
                  pl.BlockSpec((1, total, LANE), lambda b, h, t: (b, 0, h)),
                  pl.BlockSpec((1, total, LANE), lambda b, h, t: (b, 0, h)),
                  pl.BlockSpec((grp, 1, LANE), lambda b, h, t: (h, 0, 0))],
        out_specs=pl.BlockSpec((1, tq, grp * LANE), lambda b, h, t: (b, t, h)),
        out_shape=jax.ShapeDtypeStruct(q.shape, BF16),
        compiler_params=_cparams("arbitrary", "arbitrary", "arbitrary"),
        name="attn_window",
    )(q, k, v, sink_rows)


def _out0_kernel(oa_ref, ob_ref, woa_ref, wob_ref, x_ref, g1_ref, n2_ref, sh2_ref, sc2_ref, xn_ref, h2_ref, *, d_model):
    y = (jnp.dot(oa_ref[0], woa_ref[...], preferred_element_type=F32)
         + jnp.dot(ob_ref[0], wob_ref[...], preferred_element_type=F32))
    xn = x_ref[0] + g1_ref[0] * y
    xn_ref[0] = xn
    h2_ref[0] = _modulate(xn, n2_ref, sc2_ref, sh2_ref, d_model).astype(BF16)


def _out0(oa, ob, w_o, x_all, mods, rows, norm2):
    nb, total, d = x_all.shape
    tm = TOK_TILE
    woa = _pad_rows(w_o[:A_Q_HEADS * HEAD_DIM], A_Q_HEADS, HEAD_DIM).astype(BF16)
    wob = _pad_rows(w_o[A_Q_HEADS * HEAD_DIM:], B_HEADS, B_V).astype(BF16)
    tile = lambda w: pl.BlockSpec((1, tm, w), lambda b, t: (b, t, 0))
    return pl.pallas_call(
        functools.partial(_out0_kernel, d_model=d),
        grid=(nb, total // tm),
        in_specs=[tile(oa.shape[2]), tile(ob.shape[2]), _full(woa), _full(wob), tile(d),
                  _mod_spec(0, 2, rows, nb, d), _full(norm2), _mod_spec(0, 3, rows, nb, d),
                  _mod_spec(0, 4, rows, nb, d)],
        out_specs=[tile(d), tile(d)],
        out_shape=[jax.ShapeDtypeStruct((nb, total, d), F32), jax.ShapeDtypeStruct((nb, total, d), BF16)],
        compiler_params=_cparams("arbitrary", "arbitrary"),
        name="out_proj0",
    )(oa, ob, woa, wob, x_all, mods, norm2, mods, mods)


def _top_rows(sc, rowf, k):
    n = sc.shape[0]
    vals, idxs = [], []
    work = sc
    for _ in range(k):
        m = jnp.max(work, axis=0, keepdims=True)
        idx = jnp.min(jnp.where(work == m, rowf, float(n)), axis=0, keepdims=True)
        vals.append(m)
        idxs.append(idx)
        work = jnp.where(rowf == idx, -jnp.inf, work)
    return vals, idxs


def _stack_rows(rows, row16):
    out = jnp.zeros(row16.shape, F32)
    for k, r in enumerate(rows):
        out = jnp.where(row16 == float(k), r, out)
    return out


def _candidates(v0, s1, slab_rows):
    return jnp.concatenate([v0[k1] + s1[0:slab_rows[k1], :] for k1 in range(len(v0))], axis=0)


def _select_exact(sc0, sc1, rowf, row16, flat, slab_rows, topk):
    nk, ts = sc0.shape
    v0, i0 = _top_rows(sc0, rowf, topk)
    v1, i1 = _top_rows(sc1, rowf, topk)
    work = _candidates(v0, _stack_rows(v1, row16), slab_rows)
    cnt = jnp.zeros((topk, ts), F32)
    zsum = jnp.zeros((1, ts), F32)
    best0 = None
    for k in range(topk):
        m = jnp.max(work, axis=0, keepdims=True)
        idx = jnp.min(jnp.where(work == m, flat, 1e9), axis=0, keepdims=True)
        work = jnp.where(flat == idx, -jnp.inf, work)
        best0 = m if best0 is None else best0
        zsum = zsum + jnp.exp(m - best0)
        cnt = cnt + jnp.where(row16 == jnp.floor(idx * (1.0 / topk)), 1.0, 0.0)
    cc = jnp.zeros((nk, ts), F32)
    rb = jnp.full((nk, ts), 99.0, F32)
    for k in range(topk):
        ck = jnp.sum(jnp.where(row16 == float(k), cnt, 0.0), axis=0, keepdims=True)
        cc = jnp.where(rowf == i0[k], ck, cc)
        rb = jnp.where(rowf == i1[k], float(k), rb)
    return cc, rb, zsum


def _select_fast(sc0, sc1, row16, slab_rows, topk):
    nk, ts = sc0.shape
    ninf = -jnp.inf
    count = lambda hit: jnp.sum(jnp.where(hit, 1.0, 0.0), axis=0, keepdims=True)
    work, v0 = sc0, []
    for _ in range(topk):
        m = jnp.max(work, axis=0, keepdims=True)
        v0.append(m)
        work = jnp.where(work == m, ninf, work)
    bad = count(work == ninf) != float(topk)
    work, v1 = sc1, []
    rb = jnp.full((nk, ts), 99.0, F32)
    for k in range(topk):
        m = jnp.max(work, axis=0, keepdims=True)
        v1.append(m)
        hit = work == m
        work = jnp.where(hit, ninf, work)
        rb = jnp.where(hit, float(k), rb)
    bad = jnp.logical_or(bad, count(rb < 99.0) != float(topk))
    work = _candidates(v0, _stack_rows(v1, row16), slab_rows)
    zsum = jnp.zeros((1, ts), F32)
    best0 = None
    for _ in range(topk):
        m = jnp.max(work, axis=0, keepdims=True)
        work = jnp.where(work == m, ninf, work)
        best0 = m if best0 is None else best0
        zsum = zsum + jnp.exp(m - best0)
    chosen = jnp.where(work == ninf, 1.0, 0.0)
    cc = jnp.zeros((nk, ts), F32)
    total = jnp.zeros((1, ts), F32)
    off = 0
    for k1 in range(topk):
        ck = jnp.sum(chosen[off:off + slab_rows[k1], :], axis=0, keepdims=True)
        off += slab_rows[k1]
        total = total + ck
        cc = jnp.where(sc0 == v0[k1], ck, cc)
    bad = jnp.logical_or(bad, total != float(topk))
    return cc, rb, zsum, jnp.max(jnp.where(bad, 1.0, 0.0))


def _peer_select_kernel(h_ref, wq_ref, keys_ref, cc_ref, e0_ref, rb_ref, e1_ref, q_sc, *, n_heads, topk):
    nk = PEER_N_KEYS
    ts = h_ref.shape[0]
    q_sc[...] = lax.dot_general(wq_ref[...], h_ref[...], (((1,), (1,)), ((), ())), preferred_element_type=F32)
    rowf = lax.broadcasted_iota(jnp.int32, (nk, ts), 0).astype(F32)
    row16 = lax.broadcasted_iota(jnp.int32, (topk, ts), 0).astype(F32)
    slab_rows = [topk] + [SUBLANE] * (topk - 1)
    n_cand = sum(slab_rows)
    ci = lax.broadcasted_iota(jnp.int32, (n_cand, ts), 0)
    rest = ci - topk
    flat = jnp.where(ci < topk, ci, (1 + (rest >> 3)) * topk + (rest & 7)).astype(F32)

    def head_body(hd, carry):
        scs = []
        for p in range(2):
            hp = hd * 2 + p
            qhp = q_sc[pl.ds(pl.multiple_of(hp * nk, nk), nk), :].astype(BF16)
            scs.append(jnp.dot(keys_ref[hp], qhp, preferred_element_type=F32))
        sc0, sc1 = scs
        cc, rb, zsum, tie = _select_fast(sc0, sc1, row16, slab_rows, topk)
        cc, rb, zsum = lax.cond(tie > 0.0,
                                lambda: _select_exact(sc0, sc1, rowf, row16, flat, slab_rows, topk),
                                lambda: (cc, rb, zsum))
        cc_ref[hd] = cc
        rb_ref[hd] = rb.astype(BF16)
        e0_ref[hd] = jnp.exp(sc0 - jnp.max(sc0, axis=0, keepdims=True))
        e1_ref[hd] = (jnp.exp(sc1 - jnp.max(sc1, axis=0, keepdims=True)) / zsum).astype(BF16)
        return carry

    lax.fori_loop(0, n_heads, head_body, 0)


def _peer_select(h2, w_q, keys):
    t, d = h2.shape
    ts = PEER_SEL_TILE
    nh, nk = PEER_HEADS, PEER_N_KEYS
    wq_t = w_q.T.astype(BF16)
    keys2 = keys.reshape(nh * 2, nk, PEER_D_KEY // 2).astype(BF16)
    row_out = jax.ShapeDtypeStruct((nh, nk, t), F32)
    col_out = jax.ShapeDtypeStruct((nh, nk, t), BF16)
    ospec = pl.BlockSpec((nh, nk, ts), lambda i: (0, 0, i))
    return pl.pallas_call(
        functools.partial(_peer_select_kernel, n_heads=nh, topk=PEER_TOPK),
        grid=(t // ts,),
        in_specs=[pl.BlockSpec((ts, d), lambda i: (i, 0)), _full(wq_t), _full(keys2)],
        out_specs=[ospec] * 4,
        out_shape=[row_out, row_out, col_out, col_out],
        scratch_shapes=[pltpu.VMEM((wq_t.shape[0], ts), F32)],
        compiler_params=_cparams("arbitrary"),
        name="peer_select",
    )(h2, wq_t, keys2)


def _peer_apply_kernel(*refs, n_heads, final):
    if final:
        (h_ref, u_ref, vt_ref, cc_ref, e0_ref, rb_ref, e1_ref, x_ref, g_ref, o_ref,
         acc_ref, g_sc, p_sc, ht_sc) = refs
    else:
        h_ref, u_ref, vt_ref, cc_ref, e0_ref, rb_ref, e1_ref, o_ref, acc_ref, g_sc, p_sc, ht_sc = refs
    et = pl.program_id(1)
    nk = PEER_N_KEYS
    rows_per_tile = cc_ref.shape[1]
    tt = h_ref.shape[0]

    @pl.when(jnp.logical_and(pl.program_id(0) == 0, et == 0))
    def _():
        g_sc[...] = jnp.zeros_like(g_sc)
        acc_ref[...] = jnp.zeros_like(acc_ref)

    @pl.when(et == 0)
    def _():
        ht_sc[...] = h_ref[...].astype(F32).T.astype(BF16)

    at = jnp.dot(u_ref[...], ht_sc[...], preferred_element_type=F32)

    zero = jnp.zeros((), BF16)
    prev = (et + 1) % 2
    for ii in range(rows_per_tile):
        for c0 in range(0, tt, GATE_COLS):
            cols = slice(c0, c0 + GATE_COLS)
            ccr = [jnp.broadcast_to(cc_ref[hd, ii:ii + 1, cols], (BF16_ROWS, GATE_COLS)).astype(BF16)
                   for hd in range(n_heads)]
            e0r = [jnp.broadcast_to(e0_ref[hd, ii:ii + 1, cols], (BF16_ROWS, GATE_COLS)).astype(BF16)
                   for hd in range(n_heads)]
            for s0 in range(0, nk, BF16_ROWS):
                rws = slice(s0, s0 + BF16_ROWS)
                w = None
                for hd in range(n_heads):
                    term = jnp.where(rb_ref[hd, rws, cols] < ccr[hd], e1_ref[hd, rws, cols], zero) * e0r[hd]
                    w = term if w is None else w + term
                r0 = ii * nk + s0
                p_sc[r0:r0 + BF16_ROWS, cols] = w * g_sc[prev, r0:r0 + BF16_ROWS, cols]
    acc_ref[...] += jnp.dot(vt_ref[0], p_sc[...], preferred_element_type=F32)
    g_sc[et % 2] = jax.nn.gelu(at).astype(BF16)

    @pl.when(et == 0)
    def _():
        acc_ref[...] = jnp.zeros_like(acc_ref)

    @pl.when(et == pl.num_programs(1) - 1)
    def _():
        f = acc_ref[...].T
        if final:
            o_ref[...] = x_ref[...] + g_ref[0] * f
        else:
            o_ref[...] = f


def _peer_apply(h2, sel, u_tab, v_tab, x=None, mods=None, mod_index=None, tokens_per_batch=None):
    t, d = h2.shape
    cc, e0, rb, e1 = sel
    nh, nk = PEER_HEADS, PEER_N_KEYS
    tt, te = PEER_TOK_TILE, PEER_EXP_TILE
    ti = te // nk
    n_et = u_tab.shape[0] // te
    u = u_tab.astype(BF16)
    vt = v_tab.reshape(n_et, te, d).transpose(0, 2, 1).astype(BF16)
    final = x is not None
    cur = lambda e: jnp.minimum(e, n_et - 1)
    prv = lambda e: jnp.maximum(e - 1, 0)
    row_spec = pl.BlockSpec((nh, ti, tt), lambda i, e: (0, prv(e), i))
    col_spec = pl.BlockSpec((nh, nk, tt), lambda i, e: (0, 0, i))
    in_specs = [pl.BlockSpec((tt, d), lambda i, e: (i, 0)),
                pl.BlockSpec((te, d), lambda i, e: (cur(e), 0)),
                pl.BlockSpec((1, d, te), lambda i, e: (prv(e), 0, 0)),
                row_spec, row_spec, col_spec, col_spec]
    args = [h2, u, vt, cc, e0, rb, e1]
    if final:
        per = tokens_per_batch // tt
        in_specs += [pl.BlockSpec((tt, d), lambda i, e: (i, 0)),
                     pl.BlockSpec((1, 1, d), lambda i, e: (mod_index(i // per), 0, 0))]
        args += [x, mods]
    return pl.pallas_call(
        functools.partial(_peer_apply_kernel, n_heads=nh, final=final),
        grid=(t // tt, n_et + 1),
        in_specs=in_specs,
        out_specs=pl.BlockSpec((tt, d), lambda i, e: (i, 0)),
        out_shape=jax.ShapeDtypeStruct((t, d), F32),
        scratch_shapes=[pltpu.VMEM((d, tt), F32), pltpu.VMEM((2, te, tt), BF16), pltpu.VMEM((te, tt), BF16),
                        pltpu.VMEM((d, tt), BF16)],
        compiler_params=_cparams("arbitrary", "arbitrary"),
        name="peer_apply_final" if final else "peer_apply",
    )(*args)


def _proj1_kernel(x_ref, f_ref, g2_ref, n1_ref, sh_ref, sc_ref, w_ref, dqn_ref, dkn_ref, ca_ref, ua_ref, da_ref,
                  xn_ref, qr_ref, kr_ref, vr_ref, gr_ref, qd_ref, kd_ref, vd_ref, *, d_model, half_a, offs):
    xn = x_ref[0] + g2_ref[0] * f_ref[0]
    xn_ref[0] = xn
    h = _modulate(xn, n1_ref, sc_ref, sh_ref, d_model).astype(BF16)
    ca, ua, da = ca_ref[...], ua_ref[...], da_ref[...]
    o_qr, o_kr, o_vr, o_gr, o_qd, o_kd, o_vd, o_end = offs

    z = jnp.dot(h, w_ref[:, o_qr:o_kr], preferred_element_type=F32)
    for i in range(C_HEADS):
        qr_ref[0, :, i * LANE:(i + 1) * LANE] = _rope(z[:, i * LANE:(i + 1) * LANE], ca, ua, da, half_a).astype(BF16)
    z = jnp.dot(h, w_ref[:, o_kr:o_vr], preferred_element_type=F32) * (C_DK ** -0.5)
    for i in range(C_HEADS):
        kr_ref[0, :, i * LANE:(i + 1) * LANE] = _rope(z[:, i * LANE:(i + 1) * LANE], ca, ua, da, half_a).astype(BF16)
    vr_ref[0] = jnp.dot(h, w_ref[:, o_vr:o_gr], preferred_element_type=F32).astype(BF16)
    gr_ref[0] = jnp.dot(h, w_ref[:, o_gr:o_qd], preferred_element_type=F32).astype(BF16)
    z = jnp.dot(h, w_ref[:, o_qd:o_kd], preferred_element_type=F32)
    for i in range(D_Q_HEADS):
        y = _rms_rows(z[:, i * LANE:(i + 1) * LANE], HEAD_DIM) * dqn_ref[...]
        qd_ref[0, :, i * LANE:(i + 1) * LANE] = (_rope(y, ca, ua, da, half_a) * HEAD_DIM ** -0.5).astype(BF16)
    z = jnp.dot(h, w_ref[:, o_kd:o_vd], preferred_element_type=F32)
    for i in range(D_KV_HEADS):
        y = _rms_rows(z[:, i * LANE:(i + 1) * LANE], HEAD_DIM) * dkn_ref[...]
        kd_ref[0, :, i * LANE:(i + 1) * LANE] = _rope(y, ca, ua, da, half_a).astype(BF16)
    vd_ref[0] = jnp.dot(h, w_ref[:, o_vd:o_end], preferred_element_type=F32).astype(BF16)


def _proj1(x_all, f_all, mods, rows, norm1, w_in, d_qn, d_kn, tabs_a, half_a):
    nb, total, d = x_all.shape
    tm = TOK_TILE
    cqk, cv = C_HEADS * C_DK, C_HEADS * C_DV
    wqr, wkr, wvr, wgr, wqd, wkd, wvd = jnp.split(
        w_in, np.cumsum([cqk, cqk, cv, cv, D_Q_HEADS * HEAD_DIM, D_KV_HEADS * HEAD_DIM]).tolist(), axis=1)
    parts = [_pad_cols(wqr, C_HEADS, C_DK), _pad_cols(wkr, C_HEADS, C_DK), wvr, wgr,
             _pad_cols(wqd, D_Q_HEADS, HEAD_DIM), _pad_cols(wkd, D_KV_HEADS, HEAD_DIM),
             _pad_cols(wvd, D_KV_HEADS, HEAD_DIM)]
    offs = tuple(int(o) for o in np.cumsum([0] + [p.shape[1] for p in parts]))
    w_all = jnp.concatenate(parts, axis=1).astype(BF16)
    consts = [w_all, _pad_gain(d_qn, HEAD_DIM), _pad_gain(d_kn, HEAD_DIM)]
    tab_spec = pl.BlockSpec((tm, LANE), lambda b, t: (t, 0))
    tile = lambda w: pl.BlockSpec((1, tm, w), lambda b, t: (b, t, 0))
    widths = [p.shape[1] for p in parts]
    return pl.pallas_call(
        functools.partial(_proj1_kernel, d_model=d, half_a=half_a, offs=offs),
        grid=(nb, total // tm),
        in_specs=[tile(d), tile(d), _mod_spec(0, 5, rows, nb, d), _full(norm1),
                  _mod_spec(1, 0, rows, nb, d), _mod_spec(1, 1, rows, nb, d)]
                 + [_full(c) for c in consts] + [tab_spec] * 3,
        out_specs=[tile(d)] + [tile(w) for w in widths],
        out_shape=[jax.ShapeDtypeStruct((nb, total, d), F32)]
                  + [jax.ShapeDtypeStruct((nb, total, w), BF16) for w in widths],
        compiler_params=_cparams("arbitrary", "arbitrary"),
        name="proj1",
    )(x_all, f_all, mods, norm1, mods, mods, *consts, *tabs_a)


def _retention_kernel(lg_ref, qf_ref, kf_ref, vf_ref, qb_ref, kb_ref, vb_ref, of_ref, ob_ref, st_ref, *, n_heads):
    step = pl.program_id(1)

    @pl.when(step == 0)
    def _():
        st_ref[...] = jnp.zeros_like(st_ref)

    c = CHUNK
    ri = lax.broadcasted_iota(jnp.int32, (c, LANE), 0).astype(F32)
    diff = ri - lax.broadcasted_iota(jnp.int32, (c, LANE), 1).astype(F32)
    nt = (((1,), (1,)), ((), ()))
    for d, (q_ref, k_ref, v_ref, o_ref) in enumerate(((qf_ref, kf_ref, vf_ref, of_ref),
                                                      (qb_ref, kb_ref, vb_ref, ob_ref))):
        for hd in range(n_heads):
            lg = lg_ref[d, hd]
            sl = slice(hd * LANE, (hd + 1) * LANE)
            q, k, v = q_ref[0, :, sl], k_ref[0, :, sl], v_ref[0, :, sl]
            if d == 0:
                dec = jnp.where(diff >= 0, jnp.exp(lg * jnp.maximum(diff, 0.0)), 0.0)
                q_dec = jnp.exp(lg * (ri + 1.0))
                k_dec = jnp.exp(lg * (c - 1.0 - ri))
            else:
                dec = jnp.where(diff <= 0, jnp.exp(lg * jnp.maximum(-diff, 0.0)), 0.0)
                q_dec = jnp.exp(lg * (c - ri))
                k_dec = jnp.exp(lg * ri)
            s = lax.dot_general(q, k, nt, preferred_element_type=F32) * dec
            inner = jnp.dot(s.astype(BF16), v, preferred_element_type=F32)
            st = st_ref[d, hd]
            cross = jnp.dot(q, st.astype(BF16), preferred_element_type=F32) * q_dec
            o_ref[0, :, sl] = inner + cross
            kd_t = (k.astype(F32) * k_dec).T.astype(BF16)
            st_ref[d, hd] = st * jnp.exp(lg * c) + jnp.dot(kd_t, v, preferred_element_type=F32)


def _retention(qr, kr, vr, lg, ctx_len):
    nb, total, w = qr.shape
    nh = w // LANE
    c = CHUNK
    nc, nctx = total // c, ctx_len // c
    fwd = pl.BlockSpec((1, c, w), lambda b, s: (b, s, 0))

    def bmap(b, s):
        return (b, jnp.where(s < nctx, nctx - 1 - s, nc - 1 - (s - nctx)), 0)

    bwd = pl.BlockSpec((1, c, w), bmap)
    out = jax.ShapeDtypeStruct((nb, total, w), F32)
    return pl.pallas_call(
        functools.partial(_retention_kernel, n_heads=nh),
        grid=(nb, nc),
        in_specs=[pl.BlockSpec(memory_space=pltpu.SMEM), fwd, fwd, fwd, bwd, bwd, bwd],
        out_specs=[fwd, bwd],
        out_shape=[out, out],
        scratch_shapes=[pltpu.VMEM((2, nh, LANE, LANE), F32)],
        compiler_params=_cparams("arbitrary", "arbitrary"),
        name="retention",
    )(lg, qr, kr, vr, qr, kr, vr)


def _out1_kernel(of_ref, ob_ref, gr_ref, gn_ref, od_ref, wor_ref, wod_ref, x_ref, g1_ref, n2_ref, sh2_ref, sc2_ref,
                 xn_ref, h2_ref, *, d_model, n_heads):
    o = of_ref[0] + ob_ref[0]
    g = gr_ref[0].astype(F32)
    gate = g * jax.nn.sigmoid(g)
    gn = gn_ref[...]
    ys = []
    for hd in range(n_heads):
        sl = slice(hd * LANE, (hd + 1) * LANE)
        oh = o[:, sl]
        mu = jnp.mean(oh, axis=-1, keepdims=True)
        var = jnp.mean(jnp.square(oh - mu), axis=-1, keepdims=True)
        ys.append((gate[:, sl] * ((oh - mu) * lax.rsqrt(var + EPS) * gn[:, sl])).astype(BF16))
    y_ret = jnp.concatenate(ys, axis=1)
    y = (jnp.dot(y_ret, wor_ref[...], preferred_element_type=F32)
         + jnp.dot(od_ref[0], wod_ref[...], preferred_element_type=F32))
    xn = x_ref[0] + g1_ref[0] * y
    xn_ref[0] = xn
    h2_ref[0] = _modulate(xn, n2_ref, sc2_ref, sh2_ref, d_model).astype(BF16)


def _out1(o_f, o_b, g_r, gn_w, o_d, w_o, x_all, mods, rows, norm2, ctx_len):
    nb, total, d = x_all.shape
    tm = TOK_TILE
    skip = ctx_len // tm
    seq = total - ctx_len
    wor = w_o[:C_HEADS * C_DV].astype(BF16)
    wod = _pad_rows(w_o[C_HEADS * C_DV:], D_Q_HEADS, HEAD_DIM).astype(BF16)
    gn = gn_w.astype(F32).reshape(1, -1)
    tile_in = lambda w: pl.BlockSpec((1, tm, w), lambda b, t: (b, t + skip, 0))
    tile_out = pl.BlockSpec((1, tm, d), lambda b, t: (b, t, 0))
    mod = lambda chunk: pl.BlockSpec((1, 1, d), lambda b, t: ((rows + b) * 6 + chunk, 0, 0))
    return pl.pallas_call(
        functools.partial(_out1_kernel, d_model=d, n_heads=C_HEADS),
        grid=(nb, seq // tm),
        in_specs=[tile_in(o_f.shape[2]), tile_in(o_b.shape[2]), tile_in(g_r.shape[2]), _full(gn),
                  tile_in(o_d.shape[2]), _full(wor), _full(wod), tile_in(d), mod(2), _full(norm2), mod(3), mod(4)],
        out_specs=[tile_out, tile_out],
        out_shape=[jax.ShapeDtypeStruct((nb, seq, d), F32), jax.ShapeDtypeStruct((nb, seq, d), BF16)],
        compiler_params=_cparams("arbitrary", "arbitrary"),
        name="out_proj1",
    )(o_f, o_b, g_r, gn, o_d, wor, wod, x_all, mods, norm2, mods, mods)


def kernel(x, c, ctx, c_ctx, ada_w, ada_b, norm1_w, norm2_w, ab_w_in, ab_w_o, a_q_norm, a_k_norm, a_sink,
           b_q_lora_norm, b_kv_lora_norm, b_w_uq, b_w_ukv, b_q_norm, b_k_norm, cd_w_in, cd_w_o, c_decay_fwd,
           c_decay_bwd, c_gn_w, d_q_norm, d_k_norm, peer_w_q, peer_keys, peer_u, peer_v):
    nb, seq, d = x.shape
    ctx_len = ctx.shape[1]
    total = ctx_len + seq
    assert ctx_len == TOK_TILE and seq % TOK_TILE == 0 and seq % GRID_W == 0

    rows = -(-(nb + 1) // SUBLANE) * SUBLANE
    c_rows = jnp.concatenate([c, c_ctx[None, :], jnp.zeros((rows - nb - 1, d), c.dtype)], axis=0).astype(F32)
    mods = _ada(c_rows, ada_w, ada_b).reshape(-1, 1, d)

    tabs_a, half_a = _rope_tables(ctx_len, seq, 0, HEAD_DIM)
    tabs_b, half_b = _rope_tables(ctx_len, seq, B_NOPE, B_ROPE)
    n1 = norm1_w.astype(F32).reshape(-1, 1, d)
    n2 = norm2_w.astype(F32).reshape(-1, 1, d)

    x_all = jnp.concatenate([ctx, x], axis=1).astype(F32)
    qa, ka, va, qb, kb, vb = _proj0(x_all, mods, rows, n1[0], ab_w_in[0], b_w_uq[0], b_w_ukv[0], a_q_norm[0],
                                    a_k_norm[0], b_q_lora_norm[0], b_kv_lora_norm[0], b_q_norm[0], b_k_norm[0],
                                    tabs_a, tabs_b, half_a, half_b)
    o_a = _attn_window(qa, ka, va, a_sink[0], ctx_len=ctx_len)
    o_b = _attn_dense(qb, kb, vb, ctx_len=ctx_len)
    x_all, h2 = _out0(o_a, o_b, ab_w_o[0], x_all, mods, rows, n2[0])
    h2 = h2.reshape(nb * total, d)
    sel = _peer_select(h2, peer_w_q[0], peer_keys[0])
    f = _peer_apply(h2, sel, peer_u[0], peer_v[0]).reshape(nb, total, d)

    x_all, qr, kr, vr, gr, qd, kd, vd = _proj1(x_all, f, mods, rows, n1[1], cd_w_in[0], d_q_norm[0], d_k_norm[0],
                                               tabs_a, half_a)
    lg = jnp.stack([jax.nn.log_sigmoid(c_decay_fwd[0].astype(F32)), jax.nn.log_sigmoid(c_decay_bwd[0].astype(F32))])
    o_f, o_bw = _retention(qr, kr, vr, lg, ctx_len)
    o_d = _attn_dense(qd, kd, vd, ctx_len=ctx_len)
    x_lat, h2 = _out1(o_f, o_bw, gr, c_gn_w[0], o_d, cd_w_o[0], x_all, mods, rows, n2[1], ctx_len)
    h2 = h2.reshape(nb * seq, d)
    sel = _peer_select(h2, peer_w_q[1], peer_keys[1])
    out = _peer_apply(h2, sel, peer_u[1], peer_v[1], x=x_lat.reshape(nb * seq, d), mods=mods,
                      mod_index=lambda b: (rows + b) * 6 + 5, tokens_per_batch=seq)
    return out.reshape(nb, seq, d).astype(x.dtype)
```

```python
import functools

import numpy as np
import jax
import jax.numpy as jnp
from jax import lax
from jax.experimental import pallas as pl
from jax.experimental.pallas import tpu as pltpu

F32 = jnp.float32
BF16 = jnp.bfloat16

GRID_W = 64
ROPE_THETA = 10000.0
EPS = 1e-6
NEG_INF = -1e30
HEAD_DIM = 64
A_Q_HEADS, A_KV_HEADS, A_WINDOW = 8, 2, 128
B_HEADS, B_NOPE, B_ROPE, B_V, B_Q_RANK, B_KV_RANK = 8, 64, 32, 64, 256, 256
B_QK = B_NOPE + B_ROPE
C_HEADS, C_DK, C_DV = 4, 64, 128
D_Q_HEADS, D_KV_HEADS = 8, 2
PEER_HEADS, PEER_N_KEYS, PEER_D_KEY, PEER_TOPK = 8, 128, 256, 16

LANE = 128
SUBLANE = 8
BF16_ROWS = 16
VMEM_LIMIT = 56 * 1024 * 1024

TOK_TILE = 256
CHUNK = 128
ATTN_KV_PER_STEP = 2
PEER_SEL_TILE = 256
PEER_TOK_TILE = 1024
PEER_EXP_TILE = 1024
GATE_COLS = 256


def _cparams(*sem):
    return pltpu.CompilerParams(dimension_semantics=sem, vmem_limit_bytes=VMEM_LIMIT)


def _full(arr):
    nd = arr.ndim
    return pl.BlockSpec(arr.shape, lambda *_: (0,) * nd)


def _pad_cols(w, n_heads, d):
    lead = w.shape[:-1]
    w = w.reshape(lead + (n_heads, d))
    w = jnp.pad(w, [(0, 0)] * len(lead) + [(0, 0), (0, LANE - d)])
    return w.reshape(lead + (n_heads * LANE,))


def _pad_rows(w, n_heads, d):
    n = w.shape[-1]
    w = w.reshape(n_heads, d, n)
    w = jnp.pad(w, [(0, 0), (0, LANE - d), (0, 0)])
    return w.reshape(n_heads * LANE, n)


def _pad_gain(g, d):
    return jnp.pad(g.astype(F32), (0, LANE - d)).reshape(1, LANE)


def _rope_tables(ctx_len, seq, lane_off, d_rot):
    blk = d_rot // 2
    half = blk // 2
    freqs = ROPE_THETA ** (-np.arange(half, dtype=np.float64) / half)
    pos = np.arange(seq)
    total = ctx_len + seq
    cos = np.ones((total, LANE), np.float64)
    sup = np.zeros((total, LANE), np.float64)
    sdn = np.zeros((total, LANE), np.float64)
    for axis, p in enumerate((pos // GRID_W, pos % GRID_W)):
        ang = p[:, None].astype(np.float64) * freqs[None, :]
        c, s = np.cos(ang), np.sin(ang)
        base = lane_off + axis * blk
        cos[ctx_len:, base:base + half] = c
        cos[ctx_len:, base + half:base + blk] = c
        sdn[ctx_len:, base:base + half] = -s
        sup[ctx_len:, base + half:base + blk] = s
    return (jnp.asarray(cos, F32), jnp.asarray(sup, F32), jnp.asarray(sdn, F32)), half


def _rms_rows(x, true_dim):
    return x * lax.rsqrt(jnp.sum(x * x, axis=-1, keepdims=True) * (1.0 / true_dim) + EPS)


def _rope(y, cos, sup, sdn, half):
    return y * cos + pltpu.roll(y, half, 1) * sup + pltpu.roll(y, LANE - half, 1) * sdn


def _ada_kernel(c_ref, w_ref, b_ref, o_ref):
    c = c_ref[...]
    s = c * jax.nn.sigmoid(c)
    o_ref[0] = jnp.dot(s.astype(BF16), w_ref[0].astype(BF16), preferred_element_type=F32) + b_ref[0]


def _ada(c_rows, ada_w, ada_b):
    depth, d, n = ada_w.shape
    rows = c_rows.shape[0]
    tn = 1536
    return pl.pallas_call(
        _ada_kernel,
        grid=(depth, n // tn),
        in_specs=[pl.BlockSpec((rows, d), lambda l, j: (0, 0)),
                  pl.BlockSpec((1, d, tn), lambda l, j: (l, 0, j)),
                  pl.BlockSpec((1, 1, tn), lambda l, j: (l, 0, j))],
        out_specs=pl.BlockSpec((1, rows, tn), lambda l, j: (l, 0, j)),
        out_shape=jax.ShapeDtypeStruct((depth, rows, n), F32),
        compiler_params=_cparams("arbitrary", "arbitrary"),
        name="ada_mod",
    )(c_rows, ada_w, ada_b.reshape(depth, 1, n))


def _mod_spec(layer, chunk, rows, nb, d, tile_axis=1):
    def imap(*ids):
        b, t = ids[0], ids[tile_axis]
        r = jnp.where(t == 0, nb, b)
        return ((layer * rows + r) * 6 + chunk, 0, 0)
    return pl.BlockSpec((1, 1, d), imap)


def _modulate(x, n_ref, sc_ref, sh_ref, d):
    return _rms_rows(x, d) * n_ref[...] * (1.0 + sc_ref[0]) + sh_ref[0]


def _proj0_kernel(x_ref, n1_ref, sh_ref, sc_ref, w_ref, wuq_ref, wuk_ref, wuv_ref,
                  aqn_ref, akn_ref, bqln_ref, bkvln_ref, bqn_ref, bkn_ref,
                  ca_ref, ua_ref, da_ref, cb_ref, ub_ref, db_ref,
                  qa_ref, ka_ref, va_ref, qb_ref, kb_ref, vb_ref, *, d_model, half_a, half_b, offs):
    h = _modulate(x_ref[0], n1_ref, sc_ref, sh_ref, d_model).astype(BF16)
    ca, ua, da = ca_ref[...], ua_ref[...], da_ref[...]
    cb, ub, db = cb_ref[...], ub_ref[...], db_ref[...]
    o_qa, o_ka, o_va, o_cq, o_ckv, o_kr, o_end = offs

    z = jnp.dot(h, w_ref[:, o_qa:o_ka], preferred_element_type=F32)
    for i in range(A_Q_HEADS):
        y = _rms_rows(z[:, i * LANE:(i + 1) * LANE], HEAD_DIM) * aqn_ref[...]
        qa_ref[0, :, i * LANE:(i + 1) * LANE] = (_rope(y, ca, ua, da, half_a) * HEAD_DIM ** -0.5).astype(BF16)
    z = jnp.dot(h, w_ref[:, o_ka:o_va], preferred_element_type=F32)
    for i in range(A_KV_HEADS):
        y = _rms_rows(z[:, i * LANE:(i + 1) * LANE], HEAD_DIM) * akn_ref[...]
        ka_ref[0, :, i * LANE:(i + 1) * LANE] = _rope(y, ca, ua, da, half_a).astype(BF16)
    va_ref[0] = jnp.dot(h, w_ref[:, o_va:o_cq], preferred_element_type=F32).astype(BF16)

    cq = jnp.dot(h, w_ref[:, o_cq:o_ckv], preferred_element_type=F32)
    cq = (_rms_rows(cq, B_Q_RANK) * bqln_ref[...]).astype(BF16)
    z = jnp.dot(cq, wuq_ref[...], preferred_element_type=F32)
    for i in range(B_HEADS):
        y = _rms_rows(z[:, i * LANE:(i + 1) * LANE], B_QK) * bqn_ref[...]
        qb_ref[0, :, i * LANE:(i + 1) * LANE] = (_rope(y, cb, ub, db, half_b) * B_QK ** -0.5).astype(BF16)

    ckv = jnp.dot(h, w_ref[:, o_ckv:o_kr], preferred_element_type=F32)
    ckv = (_rms_rows(ckv, B_KV_RANK) * bkvln_ref[...]).astype(BF16)
    kr = jnp.dot(h, w_ref[:, o_kr:o_end], preferred_element_type=F32)
    z = jnp.dot(ckv, wuk_ref[...], preferred_element_type=F32)
    for i in range(B_HEADS):
        y = _rms_rows(z[:, i * LANE:(i + 1) * LANE] + kr, B_QK) * bkn_ref[...]
        kb_ref[0, :, i * LANE:(i + 1) * LANE] = _rope(y, cb, ub, db, half_b).astype(BF16)
    vb_ref[0] = jnp.dot(ckv, wuv_ref[...], preferred_element_type=F32).astype(BF16)


def _proj0(x_all, mods, rows, norm1, w_in, b_wuq, b_wukv, a_qn, a_kn, b_qln, b_kvln, b_qn, b_kn, tabs_a, tabs_b,
           half_a, half_b):
    nb, total, d = x_all.shape
    tm = TOK_TILE
    wq, wk, wv, wcq, wckv, wkr = jnp.split(
        w_in, np.cumsum([A_Q_HEADS * HEAD_DIM, A_KV_HEADS * HEAD_DIM, A_KV_HEADS * HEAD_DIM, B_Q_RANK, B_KV_RANK])
        .tolist(), axis=1)
    kr_pad = jnp.pad(wkr, ((0, 0), (B_NOPE, LANE - B_QK)))
    parts = [_pad_cols(wq, A_Q_HEADS, HEAD_DIM), _pad_cols(wk, A_KV_HEADS, HEAD_DIM),
             _pad_cols(wv, A_KV_HEADS, HEAD_DIM), wcq, wckv, kr_pad]
    offs = tuple(int(o) for o in np.cumsum([0] + [p.shape[1] for p in parts]))
    w_all = jnp.concatenate(parts, axis=1).astype(BF16)
    wuq = _pad_cols(b_wuq, B_HEADS, B_QK).astype(BF16)
    wukv = b_wukv.reshape(B_KV_RANK, B_HEADS, B_NOPE + B_V)
    wuk = _pad_cols(wukv[..., :B_NOPE].reshape(B_KV_RANK, -1), B_HEADS, B_NOPE).astype(BF16)
    wuv = _pad_cols(wukv[..., B_NOPE:].reshape(B_KV_RANK, -1), B_HEADS, B_V).astype(BF16)
    consts = [w_all, wuq, wuk, wuv, _pad_gain(a_qn, HEAD_DIM), _pad_gain(a_kn, HEAD_DIM),
              b_qln.astype(F32).reshape(1, -1), b_kvln.astype(F32).reshape(1, -1),
              _pad_gain(b_qn, B_QK), _pad_gain(b_kn, B_QK)]
    tab_spec = pl.BlockSpec((tm, LANE), lambda b, t: (t, 0))
    wide = lambda nh: pl.BlockSpec((1, tm, nh * LANE), lambda b, t: (b, t, 0))
    shp = lambda nh: jax.ShapeDtypeStruct((nb, total, nh * LANE), BF16)
    return pl.pallas_call(
        functools.partial(_proj0_kernel, d_model=d, half_a=half_a, half_b=half_b, offs=offs),
        grid=(nb, total // tm),
        in_specs=[pl.BlockSpec((1, tm, d), lambda b, t: (b, t, 0)), _full(norm1),
                  _mod_spec(0, 0, rows, nb, d), _mod_spec(0, 1, rows, nb, d)]
                 + [_full(c) for c in consts] + [tab_spec] * 6,
        out_specs=[wide(A_Q_HEADS), wide(A_KV_HEADS), wide(A_KV_HEADS), wide(B_HEADS), wide(B_HEADS), wide(B_HEADS)],
        out_shape=[shp(A_Q_HEADS), shp(A_KV_HEADS), shp(A_KV_HEADS), shp(B_HEADS), shp(B_HEADS), shp(B_HEADS)],
        compiler_params=_cparams("arbitrary", "arbitrary"),
        name="proj0",
    )(x_all, norm1, mods, mods, *consts, *tabs_a, *tabs_b)


_NT = (((1,), (1,)), ((), ()))


def _stack_heads(q_ref, grp, j=0):
    h0 = j * grp
    if grp == 1:
        return q_ref[0, :, h0 * LANE:(h0 + 1) * LANE]
    return jnp.concatenate([q_ref[0, :, (h0 + g) * LANE:(h0 + g + 1) * LANE] for g in range(grp)], axis=0)


def _softmax_av(scores, values, sink=None):
    m = None
    for s in scores:
        ms = jnp.max(s, axis=-1, keepdims=True)
        m = ms if m is None else jnp.maximum(m, ms)
    if sink is not None:
        m = jnp.maximum(m, sink)
    den, o = None, None
    for s, v in zip(scores, values):
        p = jnp.exp(s - m)
        ds = jnp.sum(p, axis=-1, keepdims=True)
        os_ = jnp.dot(p.astype(BF16), v, preferred_element_type=F32)
        den = ds if den is None else den + ds
        o = os_ if o is None else o + os_
    if sink is not None:
        den = den + jnp.exp(sink - m)
    return o / den


def _unstack_store(o, o_ref, grp, tq, j=0):
    for g in range(grp):
        h = j * grp + g
        o_ref[0, :, h * LANE:(h + 1) * LANE] = o[g * tq:(g + 1) * tq].astype(BF16)


def _attn_dense_kernel(q_ref, k_ref, v_ref, o_ref, *, grp, kvs, ctx_len, tq):
    qt = pl.program_id(2)

    @pl.when(qt * tq < ctx_len)
    def _():
        for j in range(kvs):
            kj = slice(j * LANE, (j + 1) * LANE)
            s = lax.dot_general(_stack_heads(q_ref, grp, j), k_ref[0, 0:ctx_len, kj], _NT,
                                preferred_element_type=F32)
            _unstack_store(_softmax_av([s], [v_ref[0, 0:ctx_len, kj]]), o_ref, grp, tq, j)

    @pl.when(qt * tq >= ctx_len)
    def _():
        for j in range(kvs):
            kj = slice(j * LANE, (j + 1) * LANE)
            s = lax.dot_general(_stack_heads(q_ref, grp, j), k_ref[0, :, kj], _NT, preferred_element_type=F32)
            _unstack_store(_softmax_av([s], [v_ref[0, :, kj]]), o_ref, grp, tq, j)


def _attn_dense(q, k, v, *, ctx_len):
    nb, total, qw = q.shape
    hq, hkv = qw // LANE, k.shape[2] // LANE
    grp = hq // hkv
    kvs = ATTN_KV_PER_STEP
    tq = CHUNK if grp > 1 else TOK_TILE
    assert ctx_len % tq == 0 and hkv % kvs == 0
    return pl.pallas_call(
        functools.partial(_attn_dense_kernel, grp=grp, kvs=kvs, ctx_len=ctx_len, tq=tq),
        grid=(nb, hkv // kvs, total // tq),
        in_specs=[pl.BlockSpec((1, tq, kvs * grp * LANE), lambda b, h, t: (b, t, h)),
                  pl.BlockSpec((1, total, kvs * LANE), lambda b, h, t: (b, 0, h)),
                  pl.BlockSpec((1, total, kvs * LANE), lambda b, h, t: (b, 0, h))],
        out_specs=pl.BlockSpec((1, tq, kvs * grp * LANE), lambda b, h, t: (b, t, h)),
        out_shape=jax.ShapeDtypeStruct(q.shape, BF16),
        compiler_params=_cparams("arbitrary", "arbitrary", "arbitrary"),
        name="attn_dense",
    )(q, k, v)


def _attn_window_kernel(q_ref, k_ref, v_ref, sink_ref, o_ref, *, grp, kvs, ctx_len, tq, total, window):
    qt = pl.program_id(2)

    def sink_col(j):
        return jnp.concatenate([jnp.broadcast_to(sink_ref[j * grp + g][:, 0:1], (tq, 1)) for g in range(grp)], axis=0)

    @pl.when(qt * tq < ctx_len)
    def _():
        for j in range(kvs):
            kj = slice(j * LANE, (j + 1) * LANE)
            s_c = lax.dot_general(_stack_heads(q_ref, grp, j), k_ref[0, 0:ctx_len, kj], _NT,
                                  preferred_element_type=F32)
            _unstack_store(_softmax_av([s_c], [v_ref[0, 0:ctx_len, kj]], sink_col(j)), o_ref, grp, tq, j)

    @pl.when(qt * tq >= ctx_len)
    def _():
        slab = 3 * tq
        start = pl.multiple_of(jnp.clip((qt - 1) * tq, ctx_len, total - slab), tq)
        qpos = qt * tq + lax.broadcasted_iota(jnp.int32, (tq, slab), 0)
        kpos = start + lax.broadcasted_iota(jnp.int32, (tq, slab), 1)
        bias = jnp.where(jnp.abs(qpos - kpos) <= window, 0.0, NEG_INF)
        bias = jnp.concatenate([bias] * grp, axis=0)
        for j in range(kvs):
            kj = slice(j * LANE, (j + 1) * LANE)
            q = _stack_heads(q_ref, grp, j)
            s_c = lax.dot_general(q, k_ref[0, 0:ctx_len, kj], _NT, preferred_element_type=F32)
            s_l = lax.dot_general(q, k_ref[0, pl.ds(start, slab), kj], _NT, preferred_element_type=F32) + bias
            o = _softmax_av([s_c, s_l], [v_ref[0, 0:ctx_len, kj], v_ref[0, pl.ds(start, slab), kj]], sink_col(j))
            _unstack_store(o, o_ref, grp, tq, j)


def _attn_window(q, k, v, sink, *, ctx_len):
    nb, total, qw = q.shape
    hq, hkv = qw // LANE, k.shape[2] // LANE
    grp = hq // hkv
    kvs = ATTN_KV_PER_STEP
    tq = CHUNK
    assert hkv % kvs == 0
    sink_rows = jnp.broadcast_to(sink.astype(F32).reshape(hq, 1, 1), (hq, 1, LANE))
    return pl.pallas_call(
        functools.partial(_attn_window_kernel, grp=grp, kvs=kvs, ctx_len=ctx_len, tq=tq, total=total,
                          window=A_WINDOW),
        grid=(nb, hkv // kvs, total // tq),
        in_specs=[pl.BlockSpec((1, tq, kvs * grp * LANE), lambda b, h, t: (b, t, h)),
                  pl.BlockSpec((1, total, kvs * LANE), lambda b, h, t: (b, 0, h)),
                  pl.BlockSpec((1, total, kvs * LANE), lambda b, h, t: (b, 0, h)),
                  pl.BlockSpec((kvs * grp, 1, LANE), lambda b, h, t: (h, 0, 0))],
        out_specs=pl.BlockSpec((1, tq, kvs * grp * LANE), lambda b, h, t: (b, t, h)),
        out_shape=jax.ShapeDtypeStruct(q.shape, BF16),
        compiler_params=_cparams("arbitrary", "arbitrary", "arbitrary"),
        name="attn_window",
    )(q, k, v, sink_rows)


def _out0_kernel(oa_ref, ob_ref, woa_ref, wob_ref, x_ref, g1_ref, n2_ref, sh2_ref, sc2_ref, xn_ref, h2_ref, *, d_model):
    y = (jnp.dot(oa_ref[0], woa_ref[...], preferred_element_type=F32)
         + jnp.dot(ob_ref[0], wob_ref[...], preferred_element_type=F32))
    xn = x_ref[0] + g1_ref[0] * y
    xn_ref[0] = xn
    h2_ref[0] = _modulate(xn, n2_ref, sc2_ref, sh2_ref, d_model).astype(BF16)


def _out0(oa, ob, w_o, x_all, mods, rows, norm2):
    nb, total, d = x_all.shape
    tm = TOK_TILE
    woa = _pad_rows(w_o[:A_Q_HEADS * HEAD_DIM], A_Q_HEADS, HEAD_DIM).astype(BF16)
    wob = _pad_rows(w_o[A_Q_HEADS * HEAD_DIM:], B_HEADS, B_V).astype(BF16)
    tile = lambda w: pl.BlockSpec((1, tm, w), lambda b, t: (b, t, 0))
    return pl.pallas_call(
        functools.partial(_out0_kernel, d_model=d),
        grid=(nb, total // tm),
        in_specs=[tile(oa.shape[2]), tile(ob.shape[2]), _full(woa), _full(wob), tile(d),
                  _mod_spec(0, 2, rows, nb, d), _full(norm2), _mod_spec(0, 3, rows, nb, d),
                  _mod_spec(0, 4, rows, nb, d)],
        out_specs=[tile(d), tile(d)],
        out_shape=[jax.ShapeDtypeStruct((nb, total, d), F32), jax.ShapeDtypeStruct((nb, total, d), BF16)],
        compiler_params=_cparams("arbitrary", "arbitrary"),
        name="out_proj0",
    )(oa, ob, woa, wob, x_all, mods, norm2, mods, mods)


def _top_rows(sc, rowf, k):
    n = sc.shape[0]
    vals, idxs = [], []
    work = sc
    for _ in range(k):
        m = jnp.max(work, axis=0, keepdims=True)
        idx = jnp.min(jnp.where(work == m, rowf, float(n)), axis=0, keepdims=True)
        vals.append(m)
        idxs.append(idx)
        work = jnp.where(rowf == idx, -jnp.inf, work)
    return vals, idxs


def _stack_rows(rows, row16):
    out = jnp.zeros(row16.shape, F32)
    for k, r in enumerate(rows):
        out = jnp.where(row16 == float(k), r, out)
    return out


def _candidates(v0, s1, slab_rows):
    return jnp.concatenate([v0[k1] + s1[0:slab_rows[k1], :] for k1 in range(len(v0))], axis=0)


def _select_exact(sc0, sc1, rowf, row16, flat, slab_rows, topk):
    nk, ts = sc0.shape
    v0, i0 = _top_rows(sc0, rowf, topk)
    v1, i1 = _top_rows(sc1, rowf, topk)
    work = _candidates(v0, _stack_rows(v1, row16), slab_rows)
    cnt = jnp.zeros((topk, ts), F32)
    zsum = jnp.zeros((1, ts), F32)
    best0 = None
    for k in range(topk):
        m = jnp.max(work, axis=0, keepdims=True)
        idx = jnp.min(jnp.where(work == m, flat, 1e9), axis=0, keepdims=True)
        work = jnp.where(flat == idx, -jnp.inf, work)
        best0 = m if best0 is None else best0
        zsum = zsum + jnp.exp(m - best0)
        cnt = cnt + jnp.where(row16 == jnp.floor(idx * (1.0 / topk)), 1.0, 0.0)
    cc = jnp.zeros((nk, ts), F32)
    rb = jnp.full((nk, ts), 99.0, F32)
    for k in range(topk):
        ck = jnp.sum(jnp.where(row16 == float(k), cnt, 0.0), axis=0, keepdims=True)
        cc = jnp.where(rowf == i0[k], ck, cc)
        rb = jnp.where(rowf == i1[k], float(k), rb)
    return cc, rb, zsum


def _select_fast(sc0, sc1, row16, slab_rows, topk):
    nk, ts = sc0.shape
    ninf = -jnp.inf
    count = lambda hit: jnp.sum(jnp.where(hit, 1.0, 0.0), axis=0, keepdims=True)
    work, v0 = sc0, []
    for _ in range(topk):
        m = jnp.max(work, axis=0, keepdims=True)
        v0.append(m)
        work = jnp.where(work == m, ninf, work)
    bad = count(work == ninf) != float(topk)
    work, v1 = sc1, []
    rb = jnp.full((nk, ts), 99.0, F32)
    for k in range(topk):
        m = jnp.max(work, axis=0, keepdims=True)
        v1.append(m)
        hit = work == m
        work = jnp.where(hit, ninf, work)
        rb = jnp.where(hit, float(k), rb)
    bad = jnp.logical_or(bad, count(rb < 99.0) != float(topk))
    work = _candidates(v0, _stack_rows(v1, row16), slab_rows)
    zsum = jnp.zeros((1, ts), F32)
    best0 = None
    for _ in range(topk):
        m = jnp.max(work, axis=0, keepdims=True)
        work = jnp.where(work == m, ninf, work)
        best0 = m if best0 is None else best0
        zsum = zsum + jnp.exp(m - best0)
    chosen = jnp.where(work == ninf, 1.0, 0.0)
    cc = jnp.zeros((nk, ts), F32)
    total = jnp.zeros((1, ts), F32)
    off = 0
    for k1 in range(topk):
        ck = jnp.sum(chosen[off:off + slab_rows[k1], :], axis=0, keepdims=True)
        off += slab_rows[k1]
        total = total + ck
        cc = jnp.where(sc0 == v0[k1], ck, cc)
    bad = jnp.logical_or(bad, total != float(topk))
    return cc, rb, zsum, jnp.max(jnp.where(bad, 1.0, 0.0))


def _peer_select_kernel(h_ref, wq_ref, keys_ref, cc_ref, e0_ref, rb_ref, e1_ref, q_sc, *, n_heads, topk):
    nk = PEER_N_KEYS
    ts = h_ref.shape[0]
    q_sc[...] = lax.dot_general(wq_ref[...], h_ref[...], (((1,), (1,)), ((), ())), preferred_element_type=F32)
    rowf = lax.broadcasted_iota(jnp.int32, (nk, ts), 0).astype(F32)
    row16 = lax.broadcasted_iota(jnp.int32, (topk, ts), 0).astype(F32)
    slab_rows = [topk] + [SUBLANE] * (topk - 1)
    n_cand = sum(slab_rows)
    ci = lax.broadcasted_iota(jnp.int32, (n_cand, ts), 0)
    rest = ci - topk
    flat = jnp.where(ci < topk, ci, (1 + (rest >> 3)) * topk + (rest & 7)).astype(F32)

    def head_body(hd, carry):
        scs = []
        for p in range(2):
            hp = hd * 2 + p
            qhp = q_sc[pl.ds(pl.multiple_of(hp * nk, nk), nk), :].astype(BF16)
            scs.append(jnp.dot(keys_ref[hp], qhp, preferred_element_type=F32))
        sc0, sc1 = scs
        cc, rb, zsum, tie = _select_fast(sc0, sc1, row16, slab_rows, topk)
        cc, rb, zsum = lax.cond(tie > 0.0,
                                lambda: _select_exact(sc0, sc1, rowf, row16, flat, slab_rows, topk),
                                lambda: (cc, rb, zsum))
        cc_ref[hd] = cc
        rb_ref[hd] = rb.astype(BF16)
        e0_ref[hd] = jnp.exp(sc0 - jnp.max(sc0, axis=0, keepdims=True))
        e1_ref[hd] = (jnp.exp(sc1 - jnp.max(sc1, axis=0, keepdims=True)) / zsum).astype(BF16)
        return carry

    lax.fori_loop(0, n_heads, head_body, 0)


def _peer_select(h2, w_q, keys):
    t, d = h2.shape
    ts = PEER_SEL_TILE
    nh, nk = PEER_HEADS, PEER_N_KEYS
    wq_t = w_q.T.astype(BF16)
    keys2 = keys.reshape(nh * 2, nk, PEER_D_KEY // 2).astype(BF16)
    row_out = jax.ShapeDtypeStruct((nh, nk, t), F32)
    col_out = jax.ShapeDtypeStruct((nh, nk, t), BF16)
    ospec = pl.BlockSpec((nh, nk, ts), lambda i: (0, 0, i))
    return pl.pallas_call(
        functools.partial(_peer_select_kernel, n_heads=nh, topk=PEER_TOPK),
        grid=(t // ts,),
        in_specs=[pl.BlockSpec((ts, d), lambda i: (i, 0)), _full(wq_t), _full(keys2)],
        out_specs=[ospec] * 4,
        out_shape=[row_out, row_out, col_out, col_out],
        scratch_shapes=[pltpu.VMEM((wq_t.shape[0], ts), F32)],
        compiler_params=_cparams("arbitrary"),
        name="peer_select",
    )(h2, wq_t, keys2)


def _peer_apply_kernel(*refs, n_heads, final):
    if final:
        (h_ref, u_ref, vt_ref, cc_ref, e0_ref, rb_ref, e1_ref, x_ref, g_ref, o_ref,
         acc_ref, g_sc, p_sc, ht_sc) = refs
    else:
        h_ref, u_ref, vt_ref, cc_ref, e0_ref, rb_ref, e1_ref, o_ref, acc_ref, g_sc, p_sc, ht_sc = refs
    et = pl.program_id(1)
    nk = PEER_N_KEYS
    rows_per_tile = cc_ref.shape[1]
    tt = h_ref.shape[0]

    @pl.when(jnp.logical_and(pl.program_id(0) == 0, et == 0))
    def _():
        g_sc[...] = jnp.zeros_like(g_sc)
        acc_ref[...] = jnp.zeros_like(acc_ref)

    @pl.when(et == 0)
    def _():
        ht_sc[...] = h_ref[...].astype(F32).T.astype(BF16)

    at = jnp.dot(u_ref[...], ht_sc[...], preferred_element_type=F32)

    zero = jnp.zeros((), BF16)
    prev = (et + 1) % 2
    for ii in range(rows_per_tile):
        for c0 in range(0, tt, GATE_COLS):
            cols = slice(c0, c0 + GATE_COLS)
            ccr = [jnp.broadcast_to(cc_ref[hd, ii:ii + 1, cols], (BF16_ROWS, GATE_COLS)).astype(BF16)
                   for hd in range(n_heads)]
            e0r = [jnp.broadcast_to(e0_ref[hd, ii:ii + 1, cols], (BF16_ROWS, GATE_COLS)).astype(BF16)
                   for hd in range(n_heads)]
            for s0 in range(0, nk, BF16_ROWS):
                rws = slice(s0, s0 + BF16_ROWS)
                w = None
                for hd in range(n_heads):
                    term = jnp.where(rb_ref[hd, rws, cols] < ccr[hd], e1_ref[hd, rws, cols], zero) * e0r[hd]
                    w = term if w is None else w + term
                r0 = ii * nk + s0
                p_sc[r0:r0 + BF16_ROWS, cols] = w * g_sc[prev, r0:r0 + BF16_ROWS, cols]
    acc_ref[...] += jnp.dot(vt_ref[0], p_sc[...], preferred_element_type=F32)
    g_sc[et % 2] = jax.nn.gelu(at).astype(BF16)

    @pl.when(et == 0)
    def _():
        acc_ref[...] = jnp.zeros_like(acc_ref)

    @pl.when(et == pl.num_programs(1) - 1)
    def _():
        f = acc_ref[...].T
        if final:
            o_ref[...] = x_ref[...] + g_ref[0] * f
        else:
            o_ref[...] = f


def _peer_apply(h2, sel, u_tab, v_tab, x=None, mods=None, mod_index=None, tokens_per_batch=None):
    t, d = h2.shape
    cc, e0, rb, e1 = sel
    nh, nk = PEER_HEADS, PEER_N_KEYS
    tt, te = PEER_TOK_TILE, PEER_EXP_TILE
    ti = te // nk
    n_et = u_tab.shape[0] // te
    u = u_tab.astype(BF16)
    vt = v_tab.reshape(n_et, te, d).transpose(0, 2, 1).astype(BF16)
    final = x is not None
    cur = lambda e: jnp.minimum(e, n_et - 1)
    prv = lambda e: jnp.maximum(e - 1, 0)
    row_spec = pl.BlockSpec((nh, ti, tt), lambda i, e: (0, prv(e), i))
    col_spec = pl.BlockSpec((nh, nk, tt), lambda i, e: (0, 0, i))
    in_specs = [pl.BlockSpec((tt, d), lambda i, e: (i, 0)),
                pl.BlockSpec((te, d), lambda i, e: (cur(e), 0)),
                pl.BlockSpec((1, d, te), lambda i, e: (prv(e), 0, 0)),
                row_spec, row_spec, col_spec, col_spec]
    args = [h2, u, vt, cc, e0, rb, e1]
    if final:
        per = tokens_per_batch // tt
        in_specs += [pl.BlockSpec((tt, d), lambda i, e: (i, 0)),
                     pl.BlockSpec((1, 1, d), lambda i, e: (mod_index(i // per), 0, 0))]
        args += [x, mods]
    return pl.pallas_call(
        functools.partial(_peer_apply_kernel, n_heads=nh, final=final),
        grid=(t // tt, n_et + 1),
        in_specs=in_specs,
        out_specs=pl.BlockSpec((tt, d), lambda i, e: (i, 0)),
        out_shape=jax.ShapeDtypeStruct((t, d), F32),
        scratch_shapes=[pltpu.VMEM((d, tt), F32), pltpu.VMEM((2, te, tt), BF16), pltpu.VMEM((te, tt), BF16),
                        pltpu.VMEM((d, tt), BF16)],
        compiler_params=_cparams("arbitrary", "arbitrary"),
        name="peer_apply_final" if final else "peer_apply",
    )(*args)


def _proj1_kernel(x_ref, f_ref, g2_ref, n1_ref, sh_ref, sc_ref, w_ref, dqn_ref, dkn_ref, ca_ref, ua_ref, da_ref,
                  xn_ref, qr_ref, kr_ref, vr_ref, gr_ref, qd_ref, kd_ref, vd_ref, *, d_model, half_a, offs):
    xn = x_ref[0] + g2_ref[0] * f_ref[0]
    xn_ref[0] = xn
    h = _modulate(xn, n1_ref, sc_ref, sh_ref, d_model).astype(BF16)
    ca, ua, da = ca_ref[...], ua_ref[...], da_ref[...]
    o_qr, o_kr, o_vr, o_gr, o_qd, o_kd, o_vd, o_end = offs

    z = jnp.dot(h, w_ref[:, o_qr:o_kr], preferred_element_type=F32)
    for i in range(C_HEADS):
        qr_ref[0, :, i * LANE:(i + 1) * LANE] = _rope(z[:, i * LANE:(i + 1) * LANE], ca, ua, da, half_a).astype(BF16)
    z = jnp.dot(h, w_ref[:, o_kr:o_vr], preferred_element_type=F32) * (C_DK ** -0.5)
    for i in range(C_HEADS):
        kr_ref[0, :, i * LANE:(i + 1) * LANE] = _rope(z[:, i * LANE:(i + 1) * LANE], ca, ua, da, half_a).astype(BF16)
    vr_ref[0] = jnp.dot(h, w_ref[:, o_vr:o_gr], preferred_element_type=F32).astype(BF16)
    gr_ref[0] = jnp.dot(h, w_ref[:, o_gr:o_qd], preferred_element_type=F32).astype(BF16)
    z = jnp.dot(h, w_ref[:, o_qd:o_kd], preferred_element_type=F32)
    for i in range(D_Q_HEADS):
        y = _rms_rows(z[:, i * LANE:(i + 1) * LANE], HEAD_DIM) * dqn_ref[...]
        qd_ref[0, :, i * LANE:(i + 1) * LANE] = (_rope(y, ca, ua, da, half_a) * HEAD_DIM ** -0.5).astype(BF16)
    z = jnp.dot(h, w_ref[:, o_kd:o_vd], preferred_element_type=F32)
    for i in range(D_KV_HEADS):
        y = _rms_rows(z[:, i * LANE:(i + 1) * LANE], HEAD_DIM) * dkn_ref[...]
        kd_ref[0, :, i * LANE:(i + 1) * LANE] = _rope(y, ca, ua, da, half_a).astype(BF16)
    vd_ref[0] = jnp.dot(h, w_ref[:, o_vd:o_end], preferred_element_type=F32).astype(BF16)


def _proj1(x_all, f_all, mods, rows, norm1, w_in, d_qn, d_kn, tabs_a, half_a):
    nb, total, d = x_all.shape
    tm = TOK_TILE
    cqk, cv = C_HEADS * C_DK, C_HEADS * C_DV
    wqr, wkr, wvr, wgr, wqd, wkd, wvd = jnp.split(
        w_in, np.cumsum([cqk, cqk, cv, cv, D_Q_HEADS * HEAD_DIM, D_KV_HEADS * HEAD_DIM]).tolist(), axis=1)
    parts = [_pad_cols(wqr, C_HEADS, C_DK), _pad_cols(wkr, C_HEADS, C_DK), wvr, wgr,
             _pad_cols(wqd, D_Q_HEADS, HEAD_DIM), _pad_cols(wkd, D_KV_HEADS, HEAD_DIM),
             _pad_cols(wvd, D_KV_HEADS, HEAD_DIM)]
    offs = tuple(int(o) for o in np.cumsum([0] + [p.shape[1] for p in parts]))
    w_all = jnp.concatenate(parts, axis=1).astype(BF16)
    consts = [w_all, _pad_gain(d_qn, HEAD_DIM), _pad_gain(d_kn, HEAD_DIM)]
    tab_spec = pl.BlockSpec((tm, LANE), lambda b, t: (t, 0))
    tile = lambda w: pl.BlockSpec((1, tm, w), lambda b, t: (b, t, 0))
    widths = [p.shape[1] for p in parts]
    return pl.pallas_call(
        functools.partial(_proj1_kernel, d_model=d, half_a=half_a, offs=offs),
        grid=(nb, total // tm),
        in_specs=[tile(d), tile(d), _mod_spec(0, 5, rows, nb, d), _full(norm1),
                  _mod_spec(1, 0, rows, nb, d), _mod_spec(1, 1, rows, nb, d)]
                 + [_full(c) for c in consts] + [tab_spec] * 3,
        out_specs=[tile(d)] + [tile(w) for w in widths],
        out_shape=[jax.ShapeDtypeStruct((nb, total, d), F32)]
                  + [jax.ShapeDtypeStruct((nb, total, w), BF16) for w in widths],
        compiler_params=_cparams("arbitrary", "arbitrary"),
        name="proj1",
    )(x_all, f_all, mods, norm1, mods, mods, *consts, *tabs_a)


def _retention_kernel(lg_ref, qf_ref, kf_ref, vf_ref, qb_ref, kb_ref, vb_ref, of_ref, ob_ref, st_ref, *, n_heads):
    step = pl.program_id(1)

    @pl.when(step == 0)
    def _():
        st_ref[...] = jnp.zeros_like(st_ref)

    c = CHUNK
    ri = lax.broadcasted_iota(jnp.int32, (c, LANE), 0).astype(F32)
    diff = ri - lax.broadcasted_iota(jnp.int32, (c, LANE), 1).astype(F32)
    nt = (((1,), (1,)), ((), ()))
    for d, (q_ref, k_ref, v_ref, o_ref) in enumerate(((qf_ref, kf_ref, vf_ref, of_ref),
                                                      (qb_ref, kb_ref, vb_ref, ob_ref))):
        for hd in range(n_heads):
            lg = lg_ref[d, hd]
            sl = slice(hd * LANE, (hd + 1) * LANE)
            q, k, v = q_ref[0, :, sl], k_ref[0, :, sl], v_ref[0, :, sl]
            if d == 0:
                dec = jnp.where(diff >= 0, jnp.exp(lg * jnp.maximum(diff, 0.0)), 0.0)
                q_dec = jnp.exp(lg * (ri + 1.0))
                k_dec = jnp.exp(lg * (c - 1.0 - ri))
            else:
                dec = jnp.where(diff <= 0, jnp.exp(lg * jnp.maximum(-diff, 0.0)), 0.0)
                q_dec = jnp.exp(lg * (c - ri))
                k_dec = jnp.exp(lg * ri)
            s = lax.dot_general(q, k, nt, preferred_element_type=F32) * dec
            inner = jnp.dot(s.astype(BF16), v, preferred_element_type=F32)
            st = st_ref[d, hd]
            cross = jnp.dot(q, st.astype(BF16), preferred_element_type=F32) * q_dec
            o_ref[0, :, sl] = inner + cross
            kd_t = (k.astype(F32) * k_dec).T.astype(BF16)
            st_ref[d, hd] = st * jnp.exp(lg * c) + jnp.dot(kd_t, v, preferred_element_type=F32)


def _retention(qr, kr, vr, lg, ctx_len):
    nb, total, w = qr.shape
    nh = w // LANE
    c = CHUNK
    nc, nctx = total // c, ctx_len // c
    fwd = pl.BlockSpec((1, c, w), lambda b, s: (b, s, 0))

    def bmap(b, s):
        return (b, jnp.where(s < nctx, nctx - 1 - s, nc - 1 - (s - nctx)), 0)

    bwd = pl.BlockSpec((1, c, w), bmap)
    out = jax.ShapeDtypeStruct((nb, total, w), F32)
    return pl.pallas_call(
        functools.partial(_retention_kernel, n_heads=nh),
        grid=(nb, nc),
        in_specs=[pl.BlockSpec(memory_space=pltpu.SMEM), fwd, fwd, fwd, bwd, bwd, bwd],
        out_specs=[fwd, bwd],
        out_shape=[out, out],
        scratch_shapes=[pltpu.VMEM((2, nh, LANE, LANE), F32)],
        compiler_params=_cparams("arbitrary", "arbitrary"),
        name="retention",
    )(lg, qr, kr, vr, qr, kr, vr)


def _out1_kernel(of_ref, ob_ref, gr_ref, gn_ref, od_ref, wor_ref, wod_ref, x_ref, g1_ref, n2_ref, sh2_ref, sc2_ref,
                 xn_ref, h2_ref, *, d_model, n_heads):
    o = of_ref[0] + ob_ref[0]
    g = gr_ref[0].astype(F32)
    gate = g * jax.nn.sigmoid(g)
    gn = gn_ref[...]
    ys = []
    for hd in range(n_heads):
        sl = slice(hd * LANE, (hd + 1) * LANE)
        oh = o[:, sl]
        mu = jnp.mean(oh, axis=-1, keepdims=True)
        var = jnp.mean(jnp.square(oh - mu), axis=-1, keepdims=True)
        ys.append((gate[:, sl] * ((oh - mu) * lax.rsqrt(var + EPS) * gn[:, sl])).astype(BF16))
    y_ret = jnp.concatenate(ys, axis=1)
    y = (jnp.dot(y_ret, wor_ref[...], preferred_element_type=F32)
         + jnp.dot(od_ref[0], wod_ref[...], preferred_element_type=F32))
    xn = x_ref[0] + g1_ref[0] * y
    xn_ref[0] = xn
    h2_ref[0] = _modulate(xn, n2_ref, sc2_ref, sh2_ref, d_model).astype(BF16)


def _out1(o_f, o_b, g_r, gn_w, o_d, w_o, x_all, mods, rows, norm2, ctx_len):
    nb, total, d = x_all.shape
    tm = TOK_TILE
    skip = ctx_len // tm
    seq = total - ctx_len
    wor = w_o[:C_HEADS * C_DV].astype(BF16)
    wod = _pad_rows(w_o[C_HEADS * C_DV:], D_Q_HEADS, HEAD_DIM).astype(BF16)
    gn = gn_w.astype(F32).reshape(1, -1)
    tile_in = lambda w: pl.BlockSpec((1, tm, w), lambda b, t: (b, t + skip, 0))
    tile_out = pl.BlockSpec((1, tm, d), lambda b, t: (b, t, 0))
    mod = lambda chunk: pl.BlockSpec((1, 1, d), lambda b, t: ((rows + b) * 6 + chunk, 0, 0))
    return pl.pallas_call(
        functools.partial(_out1_kernel, d_model=d, n_heads=C_HEADS),
        grid=(nb, seq // tm),
        in_specs=[tile_in(o_f.shape[2]), tile_in(o_b.shape[2]), tile_in(g_r.shape[2]), _full(gn),
                  tile_in(o_d.shape[2]), _full(wor), _full(wod), tile_in(d), mod(2), _full(norm2), mod(3), mod(4)],
        out_specs=[tile_out, tile_out],
        out_shape=[jax.ShapeDtypeStruct((nb, seq, d), F32), jax.ShapeDtypeStruct((nb, seq, d), BF16)],
        compiler_params=_cparams("arbitrary", "arbitrary"),
        name="out_proj1",
    )(o_f, o_b, g_r, gn, o_d, wor, wod, x_all, mods, norm2, mods, mods)


def kernel(x, c, ctx, c_ctx, ada_w, ada_b, norm1_w, norm2_w, ab_w_in, ab_w_o, a_q_norm, a_k_norm, a_sink,
           b_q_lora_norm, b_kv_lora_norm, b_w_uq, b_w_ukv, b_q_norm, b_k_norm, cd_w_in, cd_w_o, c_decay_fwd,
           c_decay_bwd, c_gn_w, d_q_norm, d_k_norm, peer_w_q, peer_keys, peer_u, peer_v):
    nb, seq, d = x.shape
    ctx_len = ctx.shape[1]
    total = ctx_len + seq
    assert ctx_len == TOK_TILE and seq % TOK_TILE == 0 and seq % GRID_W == 0

    rows = -(-(nb + 1) // SUBLANE) * SUBLANE
    c_rows = jnp.concatenate([c, c_ctx[None, :], jnp.zeros((rows - nb - 1, d), c.dtype)], axis=0).astype(F32)
    mods = _ada(c_rows, ada_w, ada_b).reshape(-1, 1, d)

    tabs_a, half_a = _rope_tables(ctx_len, seq, 0, HEAD_DIM)
    tabs_b, half_b = _rope_tables(ctx_len, seq, B_NOPE, B_ROPE)
    n1 = norm1_w.astype(F32).reshape(-1, 1, d)
    n2 = norm2_w.astype(F32).reshape(-1, 1, d)

    x_all = jnp.concatenate([ctx, x], axis=1).astype(F32)
    qa, ka, va, qb, kb, vb = _proj0(x_all, mods, rows, n1[0], ab_w_in[0], b_w_uq[0], b_w_ukv[0], a_q_norm[0],
                                    a_k_norm[0], b_q_lora_norm[0], b_kv_lora_norm[0], b_q_norm[0], b_k_norm[0],
                                    tabs_a, tabs_b, half_a, half_b)
    o_a = _attn_window(qa, ka, va, a_sink[0], ctx_len=ctx_len)
    o_b = _attn_dense(qb, kb, vb, ctx_len=ctx_len)
    x_all, h2 = _out0(o_a, o_b, ab_w_o[0], x_all, mods, rows, n2[0])
    h2 = h2.reshape(nb * total, d)
    sel = _peer_select(h2, peer_w_q[0], peer_keys[0])
    f = _peer_apply(h2, sel, peer_u[0], peer_v[0]).reshape(nb, total, d)

    x_all, qr, kr, vr, gr, qd, kd, vd = _proj1(x_all, f, mods, rows, n1[1], cd_w_in[0], d_q_norm[0], d_k_norm[0],
                                               tabs_a, half_a)
    lg = jnp.stack([jax.nn.log_sigmoid(c_decay_fwd[0].astype(F32)), jax.nn.log_sigmoid(c_decay_bwd[0].astype(F32))])
    o_f, o_bw = _retention(qr, kr, vr, lg, ctx_len)
    o_d = _attn_dense(qd, kd, vd, ctx_len=ctx_len)
    x_lat, h2 = _out1(o_f, o_bw, gr, c_gn_w[0], o_d, cd_w_o[0], x_all, mods, rows, n2[1], ctx_len)
    h2 = h2.reshape(nb * seq, d)
    sel = _peer_select(h2, peer_w_q[1], peer_keys[1])
    out = _peer_apply(h2, sel, peer_u[1], peer_v[1], x=x_lat.reshape(nb * seq, d), mods=mods,
                      mod_index=lambda b: (rows + b) * 6 + 5, tokens_per_batch=seq)
    return out.reshape(nb, seq, d).astype(x.dtype)
```

```python
import functools

import numpy as np
import jax
import jax.numpy as jnp
from jax import lax
from jax.experimental import pallas as pl
from jax.experimental.pallas import tpu as pltpu

F32 = jnp.float32
BF16 = jnp.bfloat16

GRID_W = 64
ROPE_THETA = 10000.0
EPS = 1e-6
NEG_INF = -1e30
HEAD_DIM = 64
A_Q_HEADS, A_KV_HEADS, A_WINDOW = 8, 2, 128
B_HEADS, B_NOPE, B_ROPE, B_V, B_Q_RANK, B_KV_RANK = 8, 64, 32, 64, 256, 256
B_QK = B_NOPE + B_ROPE
C_HEADS, C_DK, C_DV = 4, 64, 128
D_Q_HEADS, D_KV_HEADS = 8, 2
PEER_HEADS, PEER_N_KEYS, PEER_D_KEY, PEER_TOPK = 8, 128, 256, 16

LANE = 128
SUBLANE = 8
BF16_ROWS = 16
VMEM_LIMIT = 56 * 1024 * 1024

TOK_TILE = 256
CHUNK = 128
ATTN_KV_PER_STEP = 2
ATTN_CHAINS_PER_STEP = 4
PEER_SEL_TILE = 256
SELECT_HEADS = 2
PEER_TOK_TILE = 1024
PEER_EXP_TILE = 1024
GATE_COLS = 256


def _cparams(*sem):
    return pltpu.CompilerParams(dimension_semantics=sem, vmem_limit_bytes=VMEM_LIMIT)


def _full(arr):
    nd = arr.ndim
    return pl.BlockSpec(arr.shape, lambda *_: (0,) * nd)


def _pad_cols(w, n_heads, d):
    lead = w.shape[:-1]
    w = w.reshape(lead + (n_heads, d))
    w = jnp.pad(w, [(0, 0)] * len(lead) + [(0, 0), (0, LANE - d)])
    return w.reshape(lead + (n_heads * LANE,))


def _pad_rows(w, n_heads, d):
    n = w.shape[-1]
    w = w.reshape(n_heads, d, n)
    w = jnp.pad(w, [(0, 0), (0, LANE - d), (0, 0)])
    return w.reshape(n_heads * LANE, n)


def _pad_gain(g, d):
    return jnp.pad(g.astype(F32), (0, LANE - d)).reshape(1, LANE)


def _rope_tables(ctx_len, seq, lane_off, d_rot):
    blk = d_rot // 2
    half = blk // 2
    freqs = ROPE_THETA ** (-np.arange(half, dtype=np.float64) / half)
    pos = np.arange(seq)
    total = ctx_len + seq
    cos = np.ones((total, LANE), np.float64)
    sup = np.zeros((total, LANE), np.float64)
    sdn = np.zeros((total, LANE), np.float64)
    for axis, p in enumerate((pos // GRID_W, pos % GRID_W)):
        ang = p[:, None].astype(np.float64) * freqs[None, :]
        c, s = np.cos(ang), np.sin(ang)
        base = lane_off + axis * blk
        cos[ctx_len:, base:base + half] = c
        cos[ctx_len:, base + half:base + blk] = c
        sdn[ctx_len:, base:base + half] = -s
        sup[ctx_len:, base + half:base + blk] = s
    return (jnp.asarray(cos, F32), jnp.asarray(sup, F32), jnp.asarray(sdn, F32)), half


def _rms_rows(x, true_dim):
    return x * lax.rsqrt(jnp.sum(x * x, axis=-1, keepdims=True) * (1.0 / true_dim) + EPS)


def _rope(y, cos, sup, sdn, half):
    return y * cos + pltpu.roll(y, half, 1) * sup + pltpu.roll(y, LANE - half, 1) * sdn


def _ada_kernel(c_ref, w_ref, b_ref, o_ref):
    c = c_ref[...]
    s = c * jax.nn.sigmoid(c)
    o_ref[0] = jnp.dot(s.astype(BF16), w_ref[0].astype(BF16), preferred_element_type=F32) + b_ref[0]


def _ada(c_rows, ada_w, ada_b):
    depth, d, n = ada_w.shape
    rows = c_rows.shape[0]
    tn = 1536
    return pl.pallas_call(
        _ada_kernel,
        grid=(depth, n // tn),
        in_specs=[pl.BlockSpec((rows, d), lambda l, j: (0, 0)),
                  pl.BlockSpec((1, d, tn), lambda l, j: (l, 0, j)),
                  pl.BlockSpec((1, 1, tn), lambda l, j: (l, 0, j))],
        out_specs=pl.BlockSpec((1, rows, tn), lambda l, j: (l, 0, j)),
        out_shape=jax.ShapeDtypeStruct((depth, rows, n), F32),
        compiler_params=_cparams("arbitrary", "arbitrary"),
        name="ada_mod",
    )(c_rows, ada_w, ada_b.reshape(depth, 1, n))


def _mod_spec(layer, chunk, rows, nb, d, tile_axis=1):
    def imap(*ids):
        b, t = ids[0], ids[tile_axis]
        r = jnp.where(t == 0, nb, b)
        return ((layer * rows + r) * 6 + chunk, 0, 0)
    return pl.BlockSpec((1, 1, d), imap)


def _modulate(x, n_ref, sc_ref, sh_ref, d):
    return _rms_rows(x, d) * n_ref[...] * (1.0 + sc_ref[0]) + sh_ref[0]


def _proj0_kernel(x_ref, n1_ref, sh_ref, sc_ref, w_ref, wuq_ref, wuk_ref, wuv_ref,
                  aqn_ref, akn_ref, bqln_ref, bkvln_ref, bqn_ref, bkn_ref,
                  ca_ref, ua_ref, da_ref, cb_ref, ub_ref, db_ref,
                  qa_ref, ka_ref, va_ref, qb_ref, kb_ref, vb_ref, *, d_model, half_a, half_b, offs):
    h = _modulate(x_ref[0], n1_ref, sc_ref, sh_ref, d_model).astype(BF16)
    ca, ua, da = ca_ref[...], ua_ref[...], da_ref[...]
    cb, ub, db = cb_ref[...], ub_ref[...], db_ref[...]
    o_qa, o_ka, o_va, o_cq, o_ckv, o_kr, o_end = offs

    z = jnp.dot(h, w_ref[:, o_qa:o_ka], preferred_element_type=F32)
    for i in range(A_Q_HEADS):
        y = _rms_rows(z[:, i * LANE:(i + 1) * LANE], HEAD_DIM) * aqn_ref[...]
        qa_ref[0, :, i * LANE:(i + 1) * LANE] = (_rope(y, ca, ua, da, half_a) * HEAD_DIM ** -0.5).astype(BF16)
    z = jnp.dot(h, w_ref[:, o_ka:o_va], preferred_element_type=F32)
    for i in range(A_KV_HEADS):
        y = _rms_rows(z[:, i * LANE:(i + 1) * LANE], HEAD_DIM) * akn_ref[...]
        ka_ref[0, :, i * LANE:(i + 1) * LANE] = _rope(y, ca, ua, da, half_a).astype(BF16)
    va_ref[0] = jnp.dot(h, w_ref[:, o_va:o_cq], preferred_element_type=F32).astype(BF16)

    cq = jnp.dot(h, w_ref[:, o_cq:o_ckv], preferred_element_type=F32)
    cq = (_rms_rows(cq, B_Q_RANK) * bqln_ref[...]).astype(BF16)
    z = jnp.dot(cq, wuq_ref[...], preferred_element_type=F32)
    for i in range(B_HEADS):
        y = _rms_rows(z[:, i * LANE:(i + 1) * LANE], B_QK) * bqn_ref[...]
        qb_ref[0, :, i * LANE:(i + 1) * LANE] = (_rope(y, cb, ub, db, half_b) * B_QK ** -0.5).astype(BF16)

    ckv = jnp.dot(h, w_ref[:, o_ckv:o_kr], preferred_element_type=F32)
    ckv = (_rms_rows(ckv, B_KV_RANK) * bkvln_ref[...]).astype(BF16)
    kr = jnp.dot(h, w_ref[:, o_kr:o_end], preferred_element_type=F32)
    z = jnp.dot(ckv, wuk_ref[...], preferred_element_type=F32)
    for i in range(B_HEADS):
        y = _rms_rows(z[:, i * LANE:(i + 1) * LANE] + kr, B_QK) * bkn_ref[...]
        kb_ref[0, :, i * LANE:(i + 1) * LANE] = _rope(y, cb, ub, db, half_b).astype(BF16)
    vb_ref[0] = jnp.dot(ckv, wuv_ref[...], preferred_element_type=F32).astype(BF16)


def _proj0(x_all, mods, rows, norm1, w_in, b_wuq, b_wukv, a_qn, a_kn, b_qln, b_kvln, b_qn, b_kn, tabs_a, tabs_b,
           half_a, half_b):
    nb, total, d = x_all.shape
    tm = TOK_TILE
    wq, wk, wv, wcq, wckv, wkr = jnp.split(
        w_in, np.cumsum([A_Q_HEADS * HEAD_DIM, A_KV_HEADS * HEAD_DIM, A_KV_HEADS * HEAD_DIM, B_Q_RANK, B_KV_RANK])
        .tolist(), axis=1)
    kr_pad = jnp.pad(wkr, ((0, 0), (B_NOPE, LANE - B_QK)))
    parts = [_pad_cols(wq, A_Q_HEADS, HEAD_DIM), _pad_cols(wk, A_KV_HEADS, HEAD_DIM),
             _pad_cols(wv, A_KV_HEADS, HEAD_DIM), wcq, wckv, kr_pad]
    offs = tuple(int(o) for o in np.cumsum([0] + [p.shape[1] for p in parts]))
    w_all = jnp.concatenate(parts, axis=1).astype(BF16)
    wuq = _pad_cols(b_wuq, B_HEADS, B_QK).astype(BF16)
    wukv = b_wukv.reshape(B_KV_RANK, B_HEADS, B_NOPE + B_V)
    wuk = _pad_cols(wukv[..., :B_NOPE].reshape(B_KV_RANK, -1), B_HEADS, B_NOPE).astype(BF16)
    wuv = _pad_cols(wukv[..., B_NOPE:].reshape(B_KV_RANK, -1), B_HEADS, B_V).astype(BF16)
    consts = [w_all, wuq, wuk, wuv, _pad_gain(a_qn, HEAD_DIM), _pad_gain(a_kn, HEAD_DIM),
              b_qln.astype(F32).reshape(1, -1), b_kvln.astype(F32).reshape(1, -1),
              _pad_gain(b_qn, B_QK), _pad_gain(b_kn, B_QK)]
    tab_spec = pl.BlockSpec((tm, LANE), lambda b, t: (t, 0))
    wide = lambda nh: pl.BlockSpec((1, tm, nh * LANE), lambda b, t: (b, t, 0))
    shp = lambda nh: jax.ShapeDtypeStruct((nb, total, nh * LANE), BF16)
    return pl.pallas_call(
        functools.partial(_proj0_kernel, d_model=d, half_a=half_a, half_b=half_b, offs=offs),
        grid=(nb, total // tm),
        in_specs=[pl.BlockSpec((1, tm, d), lambda b, t: (b, t, 0)), _full(norm1),
                  _mod_spec(0, 0, rows, nb, d), _mod_spec(0, 1, rows, nb, d)]
                 + [_full(c) for c in consts] + [tab_spec] * 6,
        out_specs=[wide(A_Q_HEADS), wide(A_KV_HEADS), wide(A_KV_HEADS), wide(B_HEADS), wide(B_HEADS), wide(B_HEADS)],
        out_shape=[shp(A_Q_HEADS), shp(A_KV_HEADS), shp(A_KV_HEADS), shp(B_HEADS), shp(B_HEADS), shp(B_HEADS)],
        compiler_params=_cparams("arbitrary", "arbitrary"),
        name="proj0",
    )(x_all, norm1, mods, mods, *consts, *tabs_a, *tabs_b)


_NT = (((1,), (1,)), ((), ()))


def _stack_heads(q_ref, grp, j=0):
    h0 = j * grp
    if grp == 1:
        return q_ref[0, :, h0 * LANE:(h0 + 1) * LANE]
    return jnp.concatenate([q_ref[0, :, (h0 + g) * LANE:(h0 + g + 1) * LANE] for g in range(grp)], axis=0)


def _softmax_av(scores, values, sink=None):
    m = None
    for s in scores:
        ms = jnp.max(s, axis=-1, keepdims=True)
        m = ms if m is None else jnp.maximum(m, ms)
    if sink is not None:
        m = jnp.maximum(m, sink)
    den, o = None, None
    for s, v in zip(scores, values):
        p = jnp.exp(s - m)
        ds = jnp.sum(p, axis=-1, keepdims=True)
        os_ = jnp.dot(p.astype(BF16), v, preferred_element_type=F32)
        den = ds if den is None else den + ds
        o = os_ if o is None else o + os_
    if sink is not None:
        den = den + jnp.exp(sink - m)
    return o / den


def _unstack_store(o, o_ref, grp, tq, j=0):
    for g in range(grp):
        h = j * grp + g
        o_ref[0, :, h * LANE:(h + 1) * LANE] = o[g * tq:(g + 1) * tq].astype(BF16)


def _attn_dense_kernel(q_ref, k_ref, v_ref, o_ref, *, grp, kvs, ctx_len, tq):
    qt = pl.program_id(2)

    @pl.when(qt * tq < ctx_len)
    def _():
        for j in range(kvs):
            kj = slice(j * LANE, (j + 1) * LANE)
            s = lax.dot_general(_stack_heads(q_ref, grp, j), k_ref[0, 0:ctx_len, kj], _NT,
                                preferred_element_type=F32)
            _unstack_store(_softmax_av([s], [v_ref[0, 0:ctx_len, kj]]), o_ref, grp, tq, j)

    @pl.when(qt * tq >= ctx_len)
    def _():
        for j in range(kvs):
            kj = slice(j * LANE, (j + 1) * LANE)
            s = lax.dot_general(_stack_heads(q_ref, grp, j), k_ref[0, :, kj], _NT, preferred_element_type=F32)
            _unstack_store(_softmax_av([s], [v_ref[0, :, kj]]), o_ref, grp, tq, j)


def _attn_dense(q, k, v, *, ctx_len):
    nb, total, qw = q.shape
    hq, hkv = qw // LANE, k.shape[2] // LANE
    grp = hq // hkv
    kvs = min(ATTN_CHAINS_PER_STEP if grp == 1 else ATTN_KV_PER_STEP, hkv)
    tq = CHUNK if grp > 1 else TOK_TILE
    assert ctx_len % tq == 0 and hkv % kvs == 0
    return pl.pallas_call(
        functools.partial(_attn_dense_kernel, grp=grp, kvs=kvs, ctx_len=ctx_len, tq=tq),
        grid=(nb, hkv // kvs, total // tq),
        in_specs=[pl.BlockSpec((1, tq, kvs * grp * LANE), lambda b, h, t: (b, t, h)),
                  pl.BlockSpec((1, total, kvs * LANE), lambda b, h, t: (b, 0, h)),
                  pl.BlockSpec((1, total, kvs * LANE), lambda b, h, t: (b, 0, h))],
        out_specs=pl.BlockSpec((1, tq, kvs * grp * LANE), lambda b, h, t: (b, t, h)),
        out_shape=jax.ShapeDtypeStruct(q.shape, BF16),
        compiler_params=_cparams("arbitrary", "arbitrary", "arbitrary"),
        name="attn_dense",
    )(q, k, v)


def _attn_window_kernel(q_ref, k_ref, v_ref, sink_ref, o_ref, *, grp, kvs, ctx_len, tq, total, window):
    qt = pl.program_id(2)

    def sink_col(j):
        return jnp.concatenate([jnp.broadcast_to(sink_ref[j * grp + g][:, 0:1], (tq, 1)) for g in range(grp)], axis=0)

    @pl.when(qt * tq < ctx_len)
    def _():
        for j in range(kvs):
            kj = slice(j * LANE, (j + 1) * LANE)
            s_c = lax.dot_general(_stack_heads(q_ref, grp, j), k_ref[0, 0:ctx_len, kj], _NT,
                                  preferred_element_type=F32)
            _unstack_store(_softmax_av([s_c], [v_ref[0, 0:ctx_len, kj]], sink_col(j)), o_ref, grp, tq, j)

    @pl.when(qt * tq >= ctx_len)
    def _():
        slab = 3 * tq
        start = pl.multiple_of(jnp.clip((qt - 1) * tq, ctx_len, total - slab), tq)
        qpos = qt * tq + lax.broadcasted_iota(jnp.int32, (tq, slab), 0)
        kpos = start + lax.broadcasted_iota(jnp.int32, (tq, slab), 1)
        bias = jnp.where(jnp.abs(qpos - kpos) <= window, 0.0, NEG_INF)
        bias = jnp.concatenate([bias] * grp, axis=0)
        for j in range(kvs):
            kj = slice(j * LANE, (j + 1) * LANE)
            q = _stack_heads(q_ref, grp, j)
            s_c = lax.dot_general(q, k_ref[0, 0:ctx_len, kj], _NT, preferred_element_type=F32)
            s_l = lax.dot_general(q, k_ref[0, pl.ds(start, slab), kj], _NT, preferred_element_type=F32) + bias
            o = _softmax_av([s_c, s_l], [v_ref[0, 0:ctx_len, kj], v_ref[0, pl.ds(start, slab), kj]], sink_col(j))
            _unstack_store(o, o_ref, grp, tq, j)


def _attn_window(q, k, v, sink, *, ctx_len):
    nb, total, qw = q.shape
    hq, hkv = qw // LANE, k.shape[2] // LANE
    grp = hq // hkv
    kvs = ATTN_KV_PER_STEP
    tq = CHUNK
    assert hkv % kvs == 0
    sink_rows = jnp.broadcast_to(sink.astype(F32).reshape(hq, 1, 1), (hq, 1, LANE))
    return pl.pallas_call(
        functools.partial(_attn_window_kernel, grp=grp, kvs=kvs, ctx_len=ctx_len, tq=tq, total=total,
                          window=A_WINDOW),
        grid=(nb, hkv // kvs, total // tq),
        in_specs=[pl.BlockSpec((1, tq, kvs * grp * LANE), lambda b, h, t: (b, t, h)),
                  pl.BlockSpec((1, total, kvs * LANE), lambda b, h, t: (b, 0, h)),
                  pl.BlockSpec((1, total, kvs * LANE), lambda b, h, t: (b, 0, h)),
                  pl.BlockSpec((kvs * grp, 1, LANE), lambda b, h, t: (h, 0, 0))],
        out_specs=pl.BlockSpec((1, tq, kvs * grp * LANE), lambda b, h, t: (b, t, h)),
        out_shape=jax.ShapeDtypeStruct(q.shape, BF16),
        compiler_params=_cparams("arbitrary", "arbitrary", "arbitrary"),
        name="attn_window",
    )(q, k, v, sink_rows)


def _out0_kernel(oa_ref, ob_ref, woa_ref, wob_ref, x_ref, g1_ref, n2_ref, sh2_ref, sc2_ref, xn_ref, h2_ref, *, d_model):
    y = (jnp.dot(oa_ref[0], woa_ref[...], preferred_element_type=F32)
         + jnp.dot(ob_ref[0], wob_ref[...], preferred_element_type=F32))
    xn = x_ref[0] + g1_ref[0] * y
    xn_ref[0] = xn
    h2_ref[0] = _modulate(xn, n2_ref, sc2_ref, sh2_ref, d_model).astype(BF16)


def _out0(oa, ob, w_o, x_all, mods, rows, norm2):
    nb, total, d = x_all.shape
    tm = TOK_TILE
    woa = _pad_rows(w_o[:A_Q_HEADS * HEAD_DIM], A_Q_HEADS, HEAD_DIM).astype(BF16)
    wob = _pad_rows(w_o[A_Q_HEADS * HEAD_DIM:], B_HEADS, B_V).astype(BF16)
    tile = lambda w: pl.BlockSpec((1, tm, w), lambda b, t: (b, t, 0))
    return pl.pallas_call(
        functools.partial(_out0_kernel, d_model=d),
        grid=(nb, total // tm),
        in_specs=[tile(oa.shape[2]), tile(ob.shape[2]), _full(woa), _full(wob), tile(d),
                  _mod_spec(0, 2, rows, nb, d), _full(norm2), _mod_spec(0, 3, rows, nb, d),
                  _mod_spec(0, 4, rows, nb, d)],
        out_specs=[tile(d), tile(d)],
        out_shape=[jax.ShapeDtypeStruct((nb, total, d), F32), jax.ShapeDtypeStruct((nb, total, d), BF16)],
        compiler_params=_cparams("arbitrary", "arbitrary"),
        name="out_proj0",
    )(oa, ob, woa, wob, x_all, mods, norm2, mods, mods)


def _top_rows(sc, rowf, k):
    n = sc.shape[0]
    vals, idxs = [], []
    work = sc
    for _ in range(k):
        m = jnp.max(work, axis=0, keepdims=True)
        idx = jnp.min(jnp.where(work == m, rowf, float(n)), axis=0, keepdims=True)
        vals.append(m)
        idxs.append(idx)
        work = jnp.where(rowf == idx, -jnp.inf, work)
    return vals, idxs


def _stack_rows(rows, row16):
    out = jnp.zeros(row16.shape, F32)
    for k, r in enumerate(rows):
        out = jnp.where(row16 == float(k), r, out)
    return out


def _candidates(v0, s1, slab_rows):
    return jnp.concatenate([v0[k1] + s1[0:slab_rows[k1], :] for k1 in range(len(v0))], axis=0)


def _select_exact(sc0, sc1, rowf, row16, flat, slab_rows, topk):
    nk, ts = sc0.shape
    v0, i0 = _top_rows(sc0, rowf, topk)
    v1, i1 = _top_rows(sc1, rowf, topk)
    work = _candidates(v0, _stack_rows(v1, row16), slab_rows)
    cnt = jnp.zeros((topk, ts), F32)
    zsum = jnp.zeros((1, ts), F32)
    best0 = None
    for k in range(topk):
        m = jnp.max(work, axis=0, keepdims=True)
        idx = jnp.min(jnp.where(work == m, flat, 1e9), axis=0, keepdims=True)
        work = jnp.where(flat == idx, -jnp.inf, work)
        best0 = m if best0 is None else best0
        zsum = zsum + jnp.exp(m - best0)
        cnt = cnt + jnp.where(row16 == jnp.floor(idx * (1.0 / topk)), 1.0, 0.0)
    cc = jnp.zeros((nk, ts), F32)
    rb = jnp.full((nk, ts), 99.0, F32)
    for k in range(topk):
        ck = jnp.sum(jnp.where(row16 == float(k), cnt, 0.0), axis=0, keepdims=True)
        cc = jnp.where(rowf == i0[k], ck, cc)
        rb = jnp.where(rowf == i1[k], float(k), rb)
    return cc, rb, zsum


def _select_fast(sc0, sc1, row16, slab_rows, topk):
    nk, ts = sc0.shape
    ninf = -jnp.inf
    count = lambda hit: jnp.sum(jnp.where(hit, 1.0, 0.0), axis=0, keepdims=True)
    work, v0 = sc0, []
    for _ in range(topk):
        m = jnp.max(work, axis=0, keepdims=True)
        v0.append(m)
        work = jnp.where(work == m, ninf, work)
    bad = count(work == ninf) != float(topk)
    work, v1 = sc1, []
    rb = jnp.full((nk, ts), 99.0, F32)
    for k in range(topk):
        m = jnp.max(work, axis=0, keepdims=True)
        v1.append(m)
        hit = work == m
        work = jnp.where(hit, ninf, work)
        rb = jnp.where(hit, float(k), rb)
    bad = jnp.logical_or(bad, count(rb < 99.0) != float(topk))
    work = _candidates(v0, _stack_rows(v1, row16), slab_rows)
    zsum = jnp.zeros((1, ts), F32)
    best0 = None
    for _ in range(topk):
        m = jnp.max(work, axis=0, keepdims=True)
        work = jnp.where(work == m, ninf, work)
        best0 = m if best0 is None else best0
        zsum = zsum + jnp.exp(m - best0)
    chosen = jnp.where(work == ninf, 1.0, 0.0)
    cc = jnp.zeros((nk, ts), F32)
    total = jnp.zeros((1, ts), F32)
    off = 0
    for k1 in range(topk):
        ck = jnp.sum(chosen[off:off + slab_rows[k1], :], axis=0, keepdims=True)
        off += slab_rows[k1]
        total = total + ck
        cc = jnp.where(sc0 == v0[k1], ck, cc)
    bad = jnp.logical_or(bad, total != float(topk))
    return cc, rb, zsum, jnp.max(jnp.where(bad, 1.0, 0.0))


def _peer_select_kernel(h_ref, wq_ref, keys_ref, cc_ref, e0_ref, rb_ref, e1_ref, q_sc, *, n_heads, topk):
    nk = PEER_N_KEYS
    ts = h_ref.shape[0]
    q_sc[...] = lax.dot_general(wq_ref[...], h_ref[...], (((1,), (1,)), ((), ())), preferred_element_type=F32)
    rowf = lax.broadcasted_iota(jnp.int32, (nk, ts), 0).astype(F32)
    row16 = lax.broadcasted_iota(jnp.int32, (topk, ts), 0).astype(F32)
    slab_rows = [topk] + [SUBLANE] * (topk - 1)
    n_cand = sum(slab_rows)
    ci = lax.broadcasted_iota(jnp.int32, (n_cand, ts), 0)
    rest = ci - topk
    flat = jnp.where(ci < topk, ci, (1 + (rest >> 3)) * topk + (rest & 7)).astype(F32)

    def scores(hp):
        qhp = q_sc[pl.ds(pl.multiple_of(hp * nk, nk), nk), :].astype(BF16)
        return jnp.dot(keys_ref[hp], qhp, preferred_element_type=F32)

    def heads_body(it, carry):
        heads = [it * SELECT_HEADS + u for u in range(SELECT_HEADS)]
        scs = [(scores(hd * 2), scores(hd * 2 + 1)) for hd in heads]
        fast = [_select_fast(sc0, sc1, row16, slab_rows, topk) for sc0, sc1 in scs]
        for hd, (sc0, sc1), (cc, rb, zsum, tie) in zip(heads, scs, fast):
            cc, rb, zsum = lax.cond(tie > 0.0,
                                    lambda: _select_exact(sc0, sc1, rowf, row16, flat, slab_rows, topk),
                                    lambda: (cc, rb, zsum))
            cc_ref[hd] = cc
            rb_ref[hd] = rb.astype(BF16)
            e0_ref[hd] = jnp.exp(sc0 - jnp.max(sc0, axis=0, keepdims=True))
            e1_ref[hd] = (jnp.exp(sc1 - jnp.max(sc1, axis=0, keepdims=True)) / zsum).astype(BF16)
        return carry

    lax.fori_loop(0, n_heads // SELECT_HEADS, heads_body, 0)


def _peer_select(h2, w_q, keys):
    t, d = h2.shape
    ts = PEER_SEL_TILE
    nh, nk = PEER_HEADS, PEER_N_KEYS
    wq_t = w_q.T.astype(BF16)
    keys2 = keys.reshape(nh * 2, nk, PEER_D_KEY // 2).astype(BF16)
    row_out = jax.ShapeDtypeStruct((nh, nk, t), F32)
    col_out = jax.ShapeDtypeStruct((nh, nk, t), BF16)
    ospec = pl.BlockSpec((nh, nk, ts), lambda i: (0, 0, i))
    return pl.pallas_call(
        functools.partial(_peer_select_kernel, n_heads=nh, topk=PEER_TOPK),
        grid=(t // ts,),
        in_specs=[pl.BlockSpec((ts, d), lambda i: (i, 0)), _full(wq_t), _full(keys2)],
        out_specs=[ospec] * 4,
        out_shape=[row_out, row_out, col_out, col_out],
        scratch_shapes=[pltpu.VMEM((wq_t.shape[0], ts), F32)],
        compiler_params=_cparams("arbitrary"),
        name="peer_select",
    )(h2, wq_t, keys2)


def _peer_apply_kernel(*refs, n_heads, final):
    if final:
        (h_ref, u_ref, vt_ref, cc_ref, e0_ref, rb_ref, e1_ref, x_ref, g_ref, o_ref,
         acc_ref, g_sc, p_sc, ht_sc) = refs
    else:
        h_ref, u_ref, vt_ref, cc_ref, e0_ref, rb_ref, e1_ref, o_ref, acc_ref, g_sc, p_sc, ht_sc = refs
    et = pl.program_id(1)
    nk = PEER_N_KEYS
    rows_per_tile = cc_ref.shape[1]
    tt = h_ref.shape[0]

    @pl.when(jnp.logical_and(pl.program_id(0) == 0, et == 0))
    def _():
        g_sc[...] = jnp.zeros_like(g_sc)
        acc_ref[...] = jnp.zeros_like(acc_ref)

    @pl.when(et == 0)
    def _():
        ht_sc[...] = h_ref[...].astype(F32).T.astype(BF16)

    at = jnp.dot(u_ref[...], ht_sc[...], preferred_element_type=F32)

    zero = jnp.zeros((), BF16)
    prev = (et + 1) % 2
    for ii in range(rows_per_tile):
        for c0 in range(0, tt, GATE_COLS):
            cols = slice(c0, c0 + GATE_COLS)
            ccr = [jnp.broadcast_to(cc_ref[hd, ii:ii + 1, cols], (BF16_ROWS, GATE_COLS)).astype(BF16)
                   for hd in range(n_heads)]
            e0r = [jnp.broadcast_to(e0_ref[hd, ii:ii + 1, cols], (BF16_ROWS, GATE_COLS)).astype(BF16)
                   for hd in range(n_heads)]
            for s0 in range(0, nk, BF16_ROWS):
                rws = slice(s0, s0 + BF16_ROWS)
                w = None
                for hd in range(n_heads):
                    term = jnp.where(rb_ref[hd, rws, cols] < ccr[hd], e1_ref[hd, rws, cols], zero) * e0r[hd]
                    w = term if w is None else w + term
                r0 = ii * nk + s0
                p_sc[r0:r0 + BF16_ROWS, cols] = w * g_sc[prev, r0:r0 + BF16_ROWS, cols]
    acc_ref[...] += jnp.dot(vt_ref[0], p_sc[...], preferred_element_type=F32)
    g_sc[et % 2] = jax.nn.gelu(at.astype(BF16))

    @pl.when(et == 0)
    def _():
        acc_ref[...] = jnp.zeros_like(acc_ref)

    @pl.when(et == pl.num_programs(1) - 1)
    def _():
        f = acc_ref[...].T
        if final:
            o_ref[...] = x_ref[...] + g_ref[0] * f
        else:
            o_ref[...] = f


def _peer_apply(h2, sel, u_tab, v_tab, x=None, mods=None, mod_index=None, tokens_per_batch=None):
    t, d = h2.shape
    cc, e0, rb, e1 = sel
    nh, nk = PEER_HEADS, PEER_N_KEYS
    tt, te = PEER_TOK_TILE, PEER_EXP_TILE
    ti = te // nk
    n_et = u_tab.shape[0] // te
    u = u_tab.astype(BF16)
    vt = v_tab.reshape(n_et, te, d).transpose(0, 2, 1).astype(BF16)
    final = x is not None
    cur = lambda e: jnp.minimum(e, n_et - 1)
    prv = lambda e: jnp.maximum(e - 1, 0)
    row_spec = pl.BlockSpec((nh, ti, tt), lambda i, e: (0, prv(e), i))
    col_spec = pl.BlockSpec((nh, nk, tt), lambda i, e: (0, 0, i))
    in_specs = [pl.BlockSpec((tt, d), lambda i, e: (i, 0)),
                pl.BlockSpec((te, d), lambda i, e: (cur(e), 0)),
                pl.BlockSpec((1, d, te), lambda i, e: (prv(e), 0, 0)),
                row_spec, row_spec, col_spec, col_spec]
    args = [h2, u, vt, cc, e0, rb, e1]
    if final:
        per = tokens_per_batch // tt
        in_specs += [pl.BlockSpec((tt, d), lambda i, e: (i, 0)),
                     pl.BlockSpec((1, 1, d), lambda i, e: (mod_index(i // per), 0, 0))]
        args += [x, mods]
    return pl.pallas_call(
        functools.partial(_peer_apply_kernel, n_heads=nh, final=final),
        grid=(t // tt, n_et + 1),
        in_specs=in_specs,
        out_specs=pl.BlockSpec((tt, d), lambda i, e: (i, 0)),
        out_shape=jax.ShapeDtypeStruct((t, d), F32),
        scratch_shapes=[pltpu.VMEM((d, tt), F32), pltpu.VMEM((2, te, tt), BF16), pltpu.VMEM((te, tt), BF16),
                        pltpu.VMEM((d, tt), BF16)],
        compiler_params=_cparams("arbitrary", "arbitrary"),
        name="peer_apply_final" if final else "peer_apply",
    )(*args)


def _proj1_kernel(x_ref, f_ref, g2_ref, n1_ref, sh_ref, sc_ref, w_ref, dqn_ref, dkn_ref, ca_ref, ua_ref, da_ref,
                  xn_ref, qr_ref, kr_ref, vr_ref, gr_ref, qd_ref, kd_ref, vd_ref, *, d_model, half_a, offs):
    xn = x_ref[0] + g2_ref[0] * f_ref[0]
    xn_ref[0] = xn
    h = _modulate(xn, n1_ref, sc_ref, sh_ref, d_model).astype(BF16)
    ca, ua, da = ca_ref[...], ua_ref[...], da_ref[...]
    o_qr, o_kr, o_vr, o_gr, o_qd, o_kd, o_vd, o_end = offs

    z = jnp.dot(h, w_ref[:, o_qr:o_kr], preferred_element_type=F32)
    for i in range(C_HEADS):
        qr_ref[0, :, i * LANE:(i + 1) * LANE] = _rope(z[:, i * LANE:(i + 1) * LANE], ca, ua, da, half_a).astype(BF16)
    z = jnp.dot(h, w_ref[:, o_kr:o_vr], preferred_element_type=F32) * (C_DK ** -0.5)
    for i in range(C_HEADS):
        kr_ref[0, :, i * LANE:(i + 1) * LANE] = _rope(z[:, i * LANE:(i + 1) * LANE], ca, ua, da, half_a).astype(BF16)
    vr_ref[0] = jnp.dot(h, w_ref[:, o_vr:o_gr], preferred_element_type=F32).astype(BF16)
    gr_ref[0] = jnp.dot(h, w_ref[:, o_gr:o_qd], preferred_element_type=F32).astype(BF16)
    z = jnp.dot(h, w_ref[:, o_qd:o_kd], preferred_element_type=F32)
    for i in range(D_Q_HEADS):
        y = _rms_rows(z[:, i * LANE:(i + 1) * LANE], HEAD_DIM) * dqn_ref[...]
        qd_ref[0, :, i * LANE:(i + 1) * LANE] = (_rope(y, ca, ua, da, half_a) * HEAD_DIM ** -0.5).astype(BF16)
    z = jnp.dot(h, w_ref[:, o_kd:o_vd], preferred_element_type=F32)
    for i in range(D_KV_HEADS):
        y = _rms_rows(z[:, i * LANE:(i + 1) * LANE], HEAD_DIM) * dkn_ref[...]
        kd_ref[0, :, i * LANE:(i + 1) * LANE] = _rope(y, ca, ua, da, half_a).astype(BF16)
    vd_ref[0] = jnp.dot(h, w_ref[:, o_vd:o_end], preferred_element_type=F32).astype(BF16)


def _proj1(x_all, f_all, mods, rows, norm1, w_in, d_qn, d_kn, tabs_a, half_a):
    nb, total, d = x_all.shape
    tm = TOK_TILE
    cqk, cv = C_HEADS * C_DK, C_HEADS * C_DV
    wqr, wkr, wvr, wgr, wqd, wkd, wvd = jnp.split(
        w_in, np.cumsum([cqk, cqk, cv, cv, D_Q_HEADS * HEAD_DIM, D_KV_HEADS * HEAD_DIM]).tolist(), axis=1)
    parts = [_pad_cols(wqr, C_HEADS, C_DK), _pad_cols(wkr, C_HEADS, C_DK), wvr, wgr,
             _pad_cols(wqd, D_Q_HEADS, HEAD_DIM), _pad_cols(wkd, D_KV_HEADS, HEAD_DIM),
             _pad_cols(wvd, D_KV_HEADS, HEAD_DIM)]
    offs = tuple(int(o) for o in np.cumsum([0] + [p.shape[1] for p in parts]))
    w_all = jnp.concatenate(parts, axis=1).astype(BF16)
    consts = [w_all, _pad_gain(d_qn, HEAD_DIM), _pad_gain(d_kn, HEAD_DIM)]
    tab_spec = pl.BlockSpec((tm, LANE), lambda b, t: (t, 0))
    tile = lambda w: pl.BlockSpec((1, tm, w), lambda b, t: (b, t, 0))
    widths = [p.shape[1] for p in parts]
    return pl.pallas_call(
        functools.partial(_proj1_kernel, d_model=d, half_a=half_a, offs=offs),
        grid=(nb, total // tm),
        in_specs=[tile(d), tile(d), _mod_spec(0, 5, rows, nb, d), _full(norm1),
                  _mod_spec(1, 0, rows, nb, d), _mod_spec(1, 1, rows, nb, d)]
                 + [_full(c) for c in consts] + [tab_spec] * 3,
        out_specs=[tile(d)] + [tile(w) for w in widths],
        out_shape=[jax.ShapeDtypeStruct((nb, total, d), F32)]
                  + [jax.ShapeDtypeStruct((nb, total, w), BF16) for w in widths],
        compiler_params=_cparams("arbitrary", "arbitrary"),
        name="proj1",
    )(x_all, f_all, mods, norm1, mods, mods, *consts, *tabs_a)


def _retention_kernel(lg_ref, qf_ref, kf_ref, vf_ref, qb_ref, kb_ref, vb_ref, of_ref, ob_ref, st_ref, *, n_heads):
    step = pl.program_id(1)

    @pl.when(step == 0)
    def _():
        st_ref[...] = jnp.zeros_like(st_ref)

    c = CHUNK
    ri = lax.broadcasted_iota(jnp.int32, (c, LANE), 0).astype(F32)
    diff = ri - lax.broadcasted_iota(jnp.int32, (c, LANE), 1).astype(F32)
    nt = (((1,), (1,)), ((), ()))
    for d, (q_ref, k_ref, v_ref, o_ref) in enumerate(((qf_ref, kf_ref, vf_ref, of_ref),
                                                      (qb_ref, kb_ref, vb_ref, ob_ref))):
        for hd in range(n_heads):
            lg = lg_ref[d, hd]
            sl = slice(hd * LANE, (hd + 1) * LANE)
            q, k, v = q_ref[0, :, sl], k_ref[0, :, sl], v_ref[0, :, sl]
            if d == 0:
                dec = jnp.where(diff >= 0, jnp.exp(lg * jnp.maximum(diff, 0.0)), 0.0)
                q_dec = jnp.exp(lg * (ri + 1.0))
                k_dec = jnp.exp(lg * (c - 1.0 - ri))
            else:
                dec = jnp.where(diff <= 0, jnp.exp(lg * jnp.maximum(-diff, 0.0)), 0.0)
                q_dec = jnp.exp(lg * (c - ri))
                k_dec = jnp.exp(lg * ri)
            s = lax.dot_general(q, k, nt, preferred_element_type=F32) * dec
            inner = jnp.dot(s.astype(BF16), v, preferred_element_type=F32)
            st = st_ref[d, hd]
            cross = jnp.dot(q, st.astype(BF16), preferred_element_type=F32) * q_dec
            o_ref[0, :, sl] = inner + cross
            kd_t = (k.astype(F32) * k_dec).T.astype(BF16)
            st_ref[d, hd] = st * jnp.exp(lg * c) + jnp.dot(kd_t, v, preferred_element_type=F32)


def _retention(qr, kr, vr, lg, ctx_len):
    nb, total, w = qr.shape
    nh = w // LANE
    c = CHUNK
    nc, nctx = total // c, ctx_len // c
    fwd = pl.BlockSpec((1, c, w), lambda b, s: (b, s, 0))

    def bmap(b, s):
        return (b, jnp.where(s < nctx, nctx - 1 - s, nc - 1 - (s - nctx)), 0)

    bwd = pl.BlockSpec((1, c, w), bmap)
    out = jax.ShapeDtypeStruct((nb, total, w), F32)
    return pl.pallas_call(
        functools.partial(_retention_kernel, n_heads=nh),
        grid=(nb, nc),
        in_specs=[pl.BlockSpec(memory_space=pltpu.SMEM), fwd, fwd, fwd, bwd, bwd, bwd],
        out_specs=[fwd, bwd],
        out_shape=[out, out],
        scratch_shapes=[pltpu.VMEM((2, nh, LANE, LANE), F32)],
        compiler_params=_cparams("arbitrary", "arbitrary"),
        name="retention",
    )(lg, qr, kr, vr, qr, kr, vr)


def _out1_kernel(of_ref, ob_ref, gr_ref, gn_ref, od_ref, wor_ref, wod_ref, x_ref, g1_ref, n2_ref, sh2_ref, sc2_ref,
                 xn_ref, h2_ref, *, d_model, n_heads):
    o = of_ref[0] + ob_ref[0]
    g = gr_ref[0].astype(F32)
    gate = g * jax.nn.sigmoid(g)
    gn = gn_ref[...]
    ys = []
    for hd in range(n_heads):
        sl = slice(hd * LANE, (hd + 1) * LANE)
        oh = o[:, sl]
        mu = jnp.mean(oh, axis=-1, keepdims=True)
        var = jnp.mean(jnp.square(oh - mu), axis=-1, keepdims=True)
        ys.append((gate[:, sl] * ((oh - mu) * lax.rsqrt(var + EPS) * gn[:, sl])).astype(BF16))
    y_ret = jnp.concatenate(ys, axis=1)
    y = (jnp.dot(y_ret, wor_ref[...], preferred_element_type=F32)
         + jnp.dot(od_ref[0], wod_ref[...], preferred_element_type=F32))
    xn = x_ref[0] + g1_ref[0] * y
    xn_ref[0] = xn
    h2_ref[0] = _modulate(xn, n2_ref, sc2_ref, sh2_ref, d_model).astype(BF16)


def _out1(o_f, o_b, g_r, gn_w, o_d, w_o, x_all, mods, rows, norm2, ctx_len):
    nb, total, d = x_all.shape
    tm = TOK_TILE
    skip = ctx_len // tm
    seq = total - ctx_len
    wor = w_o[:C_HEADS * C_DV].astype(BF16)
    wod = _pad_rows(w_o[C_HEADS * C_DV:], D_Q_HEADS, HEAD_DIM).astype(BF16)
    gn = gn_w.astype(F32).reshape(1, -1)
    tile_in = lambda w: pl.BlockSpec((1, tm, w), lambda b, t: (b, t + skip, 0))
    tile_out = pl.BlockSpec((1, tm, d), lambda b, t: (b, t, 0))
    mod = lambda chunk: pl.BlockSpec((1, 1, d), lambda b, t: ((rows + b) * 6 + chunk, 0, 0))
    return pl.pallas_call(
        functools.partial(_out1_kernel, d_model=d, n_heads=C_HEADS),
        grid=(nb, seq // tm),
        in_specs=[tile_in(o_f.shape[2]), tile_in(o_b.shape[2]), tile_in(g_r.shape[2]), _full(gn),
                  tile_in(o_d.shape[2]), _full(wor), _full(wod), tile_in(d), mod(2), _full(norm2), mod(3), mod(4)],
        out_specs=[tile_out, tile_out],
        out_shape=[jax.ShapeDtypeStruct((nb, seq, d), F32), jax.ShapeDtypeStruct((nb, seq, d), BF16)],
        compiler_params=_cparams("arbitrary", "arbitrary"),
        name="out_proj1",
    )(o_f, o_b, g_r, gn, o_d, wor, wod, x_all, mods, norm2, mods, mods)


def kernel(x, c, ctx, c_ctx, ada_w, ada_b, norm1_w, norm2_w, ab_w_in, ab_w_o, a_q_norm, a_k_norm, a_sink,
           b_q_lora_norm, b_kv_lora_norm, b_w_uq, b_w_ukv, b_q_norm, b_k_norm, cd_w_in, cd_w_o, c_decay_fwd,
           c_decay_bwd, c_gn_w, d_q_norm, d_k_norm, peer_w_q, peer_keys, peer_u, peer_v):
    nb, seq, d = x.shape
    ctx_len = ctx.shape[1]
    total = ctx_len + seq
    assert ctx_len == TOK_TILE and seq % TOK_TILE == 0 and seq % GRID_W == 0

    rows = -(-(nb + 1) // SUBLANE) * SUBLANE
    c_rows = jnp.concatenate([c, c_ctx[None, :], jnp.zeros((rows - nb - 1, d), c.dtype)], axis=0).astype(F32)
    mods = _ada(c_rows, ada_w, ada_b).reshape(-1, 1, d)

    tabs_a, half_a = _rope_tables(ctx_len, seq, 0, HEAD_DIM)
    tabs_b, half_b = _rope_tables(ctx_len, seq, B_NOPE, B_ROPE)
    n1 = norm1_w.astype(F32).reshape(-1, 1, d)
    n2 = norm2_w.astype(F32).reshape(-1, 1, d)

    x_all = jnp.concatenate([ctx, x], axis=1).astype(F32)
    qa, ka, va, qb, kb, vb = _proj0(x_all, mods, rows, n1[0], ab_w_in[0], b_w_uq[0], b_w_ukv[0], a_q_norm[0],
                                    a_k_norm[0], b_q_lora_norm[0], b_kv_lora_norm[0], b_q_norm[0], b_k_norm[0],
                                    tabs_a, tabs_b, half_a, half_b)
    o_a = _attn_window(qa, ka, va, a_sink[0], ctx_len=ctx_len)
    o_b = _attn_dense(qb, kb, vb, ctx_len=ctx_len)
    x_all, h2 = _out0(o_a, o_b, ab_w_o[0], x_all, mods, rows, n2[0])
    h2 = h2.reshape(nb * total, d)
    sel = _peer_select(h2, peer_w_q[0], peer_keys[0])
    f = _peer_apply(h2, sel, peer_u[0], peer_v[0]).reshape(nb, total, d)

    x_all, qr, kr, vr, gr, qd, kd, vd = _proj1(x_all, f, mods, rows, n1[1], cd_w_in[0], d_q_norm[0], d_k_norm[0],
                                               tabs_a, half_a)
    lg = jnp.stack([jax.nn.log_sigmoid(c_decay_fwd[0].astype(F32)), jax.nn.log_sigmoid(c_decay_bwd[0].astype(F32))])
    o_f, o_bw = _retention(qr, kr, vr, lg, ctx_len)
    o_d = _attn_dense(qd, kd, vd, ctx_len=ctx_len)
    x_lat, h2 = _out1(o_f, o_bw, gr, c_gn_w[0], o_d, cd_w_o[0], x_all, mods, rows, n2[1], ctx_len)
    h2 = h2.reshape(nb * seq, d)
    sel = _peer_select(h2, peer_w_q[1], peer_keys[1])
    out = _peer_apply(h2, sel, peer_u[1], peer_v[1], x=x_lat.reshape(nb * seq, d), mods=mods,
                      mod_index=lambda b: (rows + b) * 6 + 5, tokens_per_batch=seq)
    return out.reshape(nb, seq, d).astype(x.dtype)
```

```python
import functools

import numpy as np
import jax
import jax.numpy as jnp
from jax import lax
from jax.experimental import pallas as pl
from jax.experimental.pallas import tpu as pltpu

F32 = jnp.float32
BF16 = jnp.bfloat16

GRID_W = 64
ROPE_THETA = 10000.0
EPS = 1e-6
NEG_INF = -1e30
HEAD_DIM = 64
A_Q_HEADS, A_KV_HEADS, A_WINDOW = 8, 2, 128
B_HEADS, B_NOPE, B_ROPE, B_V, B_Q_RANK, B_KV_RANK = 8, 64, 32, 64, 256, 256
B_QK = B_NOPE + B_ROPE
C_HEADS, C_DK, C_DV = 4, 64, 128
D_Q_HEADS, D_KV_HEADS = 8, 2
PEER_HEADS, PEER_N_KEYS, PEER_D_KEY, PEER_TOPK = 8, 128, 256, 16

LANE = 128
SUBLANE = 8
BF16_ROWS = 16
ONES_LANE = LANE - 1
VMEM_LIMIT = 56 * 1024 * 1024

TOK_TILE = 256
CHUNK = 128
ATTN_KV_PER_STEP = 2
ATTN_CHAINS_PER_STEP = 4
PEER_SEL_TILE = 256
SELECT_HEADS = 2
PEER_TOK_TILE = 1024
PEER_EXP_TILE = 1024
GATE_COLS = 256


def _cparams(*sem):
    return pltpu.CompilerParams(dimension_semantics=sem, vmem_limit_bytes=VMEM_LIMIT)


def _full(arr):
    nd = arr.ndim
    return pl.BlockSpec(arr.shape, lambda *_: (0,) * nd)


def _pad_cols(w, n_heads, d):
    lead = w.shape[:-1]
    w = w.reshape(lead + (n_heads, d))
    w = jnp.pad(w, [(0, 0)] * len(lead) + [(0, 0), (0, LANE - d)])
    return w.reshape(lead + (n_heads * LANE,))


def _pad_rows(w, n_heads, d):
    n = w.shape[-1]
    w = w.reshape(n_heads, d, n)
    w = jnp.pad(w, [(0, 0), (0, LANE - d), (0, 0)])
    return w.reshape(n_heads * LANE, n)


def _pad_gain(g, d):
    return jnp.pad(g.astype(F32), (0, LANE - d)).reshape(1, LANE)


def _rope_tables(ctx_len, seq, lane_off, d_rot):
    blk = d_rot // 2
    half = blk // 2
    freqs = ROPE_THETA ** (-np.arange(half, dtype=np.float64) / half)
    pos = np.arange(seq)
    total = ctx_len + seq
    cos = np.ones((total, LANE), np.float64)
    sup = np.zeros((total, LANE), np.float64)
    sdn = np.zeros((total, LANE), np.float64)
    for axis, p in enumerate((pos // GRID_W, pos % GRID_W)):
        ang = p[:, None].astype(np.float64) * freqs[None, :]
        c, s = np.cos(ang), np.sin(ang)
        base = lane_off + axis * blk
        cos[ctx_len:, base:base + half] = c
        cos[ctx_len:, base + half:base + blk] = c
        sdn[ctx_len:, base:base + half] = -s
        sup[ctx_len:, base + half:base + blk] = s
    return (jnp.asarray(cos, F32), jnp.asarray(sup, F32), jnp.asarray(sdn, F32)), half


def _rms_rows(x, true_dim):
    return x * lax.rsqrt(jnp.sum(x * x, axis=-1, keepdims=True) * (1.0 / true_dim) + EPS)


def _rope(y, cos, sup, sdn, half):
    return y * cos + pltpu.roll(y, half, 1) * sup + pltpu.roll(y, LANE - half, 1) * sdn


def _ada_kernel(c_ref, w_ref, b_ref, o_ref):
    c = c_ref[...]
    s = c * jax.nn.sigmoid(c)
    o_ref[0] = jnp.dot(s.astype(BF16), w_ref[0].astype(BF16), preferred_element_type=F32) + b_ref[0]


def _ada(c_rows, ada_w, ada_b):
    depth, d, n = ada_w.shape
    rows = c_rows.shape[0]
    tn = 1536
    return pl.pallas_call(
        _ada_kernel,
        grid=(depth, n // tn),
        in_specs=[pl.BlockSpec((rows, d), lambda l, j: (0, 0)),
                  pl.BlockSpec((1, d, tn), lambda l, j: (l, 0, j)),
                  pl.BlockSpec((1, 1, tn), lambda l, j: (l, 0, j))],
        out_specs=pl.BlockSpec((1, rows, tn), lambda l, j: (l, 0, j)),
        out_shape=jax.ShapeDtypeStruct((depth, rows, n), F32),
        compiler_params=_cparams("arbitrary", "arbitrary"),
        name="ada_mod",
    )(c_rows, ada_w, ada_b.reshape(depth, 1, n))


def _mod_spec(layer, chunk, rows, nb, d, tile_axis=1):
    def imap(*ids):
        b, t = ids[0], ids[tile_axis]
        r = jnp.where(t == 0, nb, b)
        return ((layer * rows + r) * 6 + chunk, 0, 0)
    return pl.BlockSpec((1, 1, d), imap)


def _modulate(x, n_ref, sc_ref, sh_ref, d):
    return _rms_rows(x, d) * n_ref[...] * (1.0 + sc_ref[0]) + sh_ref[0]


def _proj0_kernel(x_ref, n1_ref, sh_ref, sc_ref, w_ref, wuq_ref, wuk_ref, wuv_ref,
                  aqn_ref, akn_ref, bqln_ref, bkvln_ref, bqn_ref, bkn_ref,
                  ca_ref, ua_ref, da_ref, cb_ref, ub_ref, db_ref,
                  qa_ref, ka_ref, va_ref, qb_ref, kb_ref, vb_ref, *, d_model, half_a, half_b, offs):
    h = _modulate(x_ref[0], n1_ref, sc_ref, sh_ref, d_model).astype(BF16)
    ca, ua, da = ca_ref[...], ua_ref[...], da_ref[...]
    cb, ub, db = cb_ref[...], ub_ref[...], db_ref[...]
    o_qa, o_ka, o_va, o_cq, o_ckv, o_kr, o_end = offs

    z = jnp.dot(h, w_ref[:, o_qa:o_ka], preferred_element_type=F32)
    for i in range(A_Q_HEADS):
        y = _rms_rows(z[:, i * LANE:(i + 1) * LANE], HEAD_DIM) * aqn_ref[...]
        qa_ref[0, :, i * LANE:(i + 1) * LANE] = (_rope(y, ca, ua, da, half_a) * HEAD_DIM ** -0.5).astype(BF16)
    z = jnp.dot(h, w_ref[:, o_ka:o_va], preferred_element_type=F32)
    for i in range(A_KV_HEADS):
        y = _rms_rows(z[:, i * LANE:(i + 1) * LANE], HEAD_DIM) * akn_ref[...]
        ka_ref[0, :, i * LANE:(i + 1) * LANE] = _rope(y, ca, ua, da, half_a).astype(BF16)
    va = jnp.dot(h, w_ref[:, o_va:o_cq], preferred_element_type=F32)
    va_ref[0] = _with_ones_lane(va, A_KV_HEADS).astype(BF16)

    cq = jnp.dot(h, w_ref[:, o_cq:o_ckv], preferred_element_type=F32)
    cq = (_rms_rows(cq, B_Q_RANK) * bqln_ref[...]).astype(BF16)
    z = jnp.dot(cq, wuq_ref[...], preferred_element_type=F32)
    for i in range(B_HEADS):
        y = _rms_rows(z[:, i * LANE:(i + 1) * LANE], B_QK) * bqn_ref[...]
        qb_ref[0, :, i * LANE:(i + 1) * LANE] = (_rope(y, cb, ub, db, half_b) * B_QK ** -0.5).astype(BF16)

    ckv = jnp.dot(h, w_ref[:, o_ckv:o_kr], preferred_element_type=F32)
    ckv = (_rms_rows(ckv, B_KV_RANK) * bkvln_ref[...]).astype(BF16)
    kr = jnp.dot(h, w_ref[:, o_kr:o_end], preferred_element_type=F32)
    z = jnp.dot(ckv, wuk_ref[...], preferred_element_type=F32)
    for i in range(B_HEADS):
        y = _rms_rows(z[:, i * LANE:(i + 1) * LANE] + kr, B_QK) * bkn_ref[...]
        kb_ref[0, :, i * LANE:(i + 1) * LANE] = _rope(y, cb, ub, db, half_b).astype(BF16)
    vb = jnp.dot(ckv, wuv_ref[...], preferred_element_type=F32)
    vb_ref[0] = _with_ones_lane(vb, B_HEADS).astype(BF16)


def _proj0(x_all, mods, rows, norm1, w_in, b_wuq, b_wukv, a_qn, a_kn, b_qln, b_kvln, b_qn, b_kn, tabs_a, tabs_b,
           half_a, half_b):
    nb, total, d = x_all.shape
    tm = TOK_TILE
    wq, wk, wv, wcq, wckv, wkr = jnp.split(
        w_in, np.cumsum([A_Q_HEADS * HEAD_DIM, A_KV_HEADS * HEAD_DIM, A_KV_HEADS * HEAD_DIM, B_Q_RANK, B_KV_RANK])
        .tolist(), axis=1)
    kr_pad = jnp.pad(wkr, ((0, 0), (B_NOPE, LANE - B_QK)))
    parts = [_pad_cols(wq, A_Q_HEADS, HEAD_DIM), _pad_cols(wk, A_KV_HEADS, HEAD_DIM),
             _pad_cols(wv, A_KV_HEADS, HEAD_DIM), wcq, wckv, kr_pad]
    offs = tuple(int(o) for o in np.cumsum([0] + [p.shape[1] for p in parts]))
    w_all = jnp.concatenate(parts, axis=1).astype(BF16)
    wuq = _pad_cols(b_wuq, B_HEADS, B_QK).astype(BF16)
    wukv = b_wukv.reshape(B_KV_RANK, B_HEADS, B_NOPE + B_V)
    wuk = _pad_cols(wukv[..., :B_NOPE].reshape(B_KV_RANK, -1), B_HEADS, B_NOPE).astype(BF16)
    wuv = _pad_cols(wukv[..., B_NOPE:].reshape(B_KV_RANK, -1), B_HEADS, B_V).astype(BF16)
    consts = [w_all, wuq, wuk, wuv, _pad_gain(a_qn, HEAD_DIM), _pad_gain(a_kn, HEAD_DIM),
              b_qln.astype(F32).reshape(1, -1), b_kvln.astype(F32).reshape(1, -1),
              _pad_gain(b_qn, B_QK), _pad_gain(b_kn, B_QK)]
    tab_spec = pl.BlockSpec((tm, LANE), lambda b, t: (t, 0))
    wide = lambda nh: pl.BlockSpec((1, tm, nh * LANE), lambda b, t: (b, t, 0))
    shp = lambda nh: jax.ShapeDtypeStruct((nb, total, nh * LANE), BF16)
    return pl.pallas_call(
        functools.partial(_proj0_kernel, d_model=d, half_a=half_a, half_b=half_b, offs=offs),
        grid=(nb, total // tm),
        in_specs=[pl.BlockSpec((1, tm, d), lambda b, t: (b, t, 0)), _full(norm1),
                  _mod_spec(0, 0, rows, nb, d), _mod_spec(0, 1, rows, nb, d)]
                 + [_full(c) for c in consts] + [tab_spec] * 6,
        out_specs=[wide(A_Q_HEADS), wide(A_KV_HEADS), wide(A_KV_HEADS), wide(B_HEADS), wide(B_HEADS), wide(B_HEADS)],
        out_shape=[shp(A_Q_HEADS), shp(A_KV_HEADS), shp(A_KV_HEADS), shp(B_HEADS), shp(B_HEADS), shp(B_HEADS)],
        compiler_params=_cparams("arbitrary", "arbitrary"),
        name="proj0",
    )(x_all, norm1, mods, mods, *consts, *tabs_a, *tabs_b)


_NT = (((1,), (1,)), ((), ()))


def _stack_heads(q_ref, grp, j=0):
    h0 = j * grp
    if grp == 1:
        return q_ref[0, :, h0 * LANE:(h0 + 1) * LANE]
    return jnp.concatenate([q_ref[0, :, (h0 + g) * LANE:(h0 + g + 1) * LANE] for g in range(grp)], axis=0)


def _softmax_av(scores, values, sink=None):
    m = None
    for s in scores:
        ms = jnp.max(s, axis=-1, keepdims=True)
        m = ms if m is None else jnp.maximum(m, ms)
    if sink is not None:
        m = jnp.maximum(m, sink)
    o = None
    for s, v in zip(scores, values):
        os_ = jnp.dot(jnp.exp(s - m).astype(BF16), v, preferred_element_type=F32)
        o = os_ if o is None else o + os_
    den = o[:, ONES_LANE:ONES_LANE + 1]
    if sink is not None:
        den = den + jnp.exp(sink - m)
    return o / den


def _with_ones_lane(v, n_heads):
    lane = lax.broadcasted_iota(jnp.int32, (1, n_heads * LANE), 1) % LANE
    return v + jnp.where(lane == ONES_LANE, 1.0, 0.0)


def _unstack_store(o, o_ref, grp, tq, j=0):
    for g in range(grp):
        h = j * grp + g
        o_ref[0, :, h * LANE:(h + 1) * LANE] = o[g * tq:(g + 1) * tq].astype(BF16)


def _attn_dense_kernel(q_ref, k_ref, v_ref, o_ref, *, grp, kvs, ctx_len, tq):
    qt = pl.program_id(2)

    @pl.when(qt * tq < ctx_len)
    def _():
        for j in range(kvs):
            kj = slice(j * LANE, (j + 1) * LANE)
            s = lax.dot_general(_stack_heads(q_ref, grp, j), k_ref[0, 0:ctx_len, kj], _NT,
                                preferred_element_type=F32)
            _unstack_store(_softmax_av([s], [v_ref[0, 0:ctx_len, kj]]), o_ref, grp, tq, j)

    @pl.when(qt * tq >= ctx_len)
    def _():
        for j in range(kvs):
            kj = slice(j * LANE, (j + 1) * LANE)
            s = lax.dot_general(_stack_heads(q_ref, grp, j), k_ref[0, :, kj], _NT, preferred_element_type=F32)
            _unstack_store(_softmax_av([s], [v_ref[0, :, kj]]), o_ref, grp, tq, j)


def _attn_dense(q, k, v, *, ctx_len):
    nb, total, qw = q.shape
    hq, hkv = qw // LANE, k.shape[2] // LANE
    grp = hq // hkv
    kvs = min(ATTN_CHAINS_PER_STEP if grp == 1 else ATTN_KV_PER_STEP, hkv)
    tq = CHUNK if grp > 1 else TOK_TILE
    assert ctx_len % tq == 0 and hkv % kvs == 0
    return pl.pallas_call(
        functools.partial(_attn_dense_kernel, grp=grp, kvs=kvs, ctx_len=ctx_len, tq=tq),
        grid=(nb, hkv // kvs, total // tq),
        in_specs=[pl.BlockSpec((1, tq, kvs * grp * LANE), lambda b, h, t: (b, t, h)),
                  pl.BlockSpec((1, total, kvs * LANE), lambda b, h, t: (b, 0, h)),
                  pl.BlockSpec((1, total, kvs * LANE), lambda b, h, t: (b, 0, h))],
        out_specs=pl.BlockSpec((1, tq, kvs * grp * LANE), lambda b, h, t: (b, t, h)),
        out_shape=jax.ShapeDtypeStruct(q.shape, BF16),
        compiler_params=_cparams("arbitrary", "arbitrary", "arbitrary"),
        name="attn_dense",
    )(q, k, v)


def _attn_window_kernel(q_ref, k_ref, v_ref, sink_ref, o_ref, *, grp, kvs, ctx_len, tq, total, window):
    qt = pl.program_id(2)

    def sink_col(j):
        return jnp.concatenate([jnp.broadcast_to(sink_ref[j * grp + g][:, 0:1], (tq, 1)) for g in range(grp)], axis=0)

    @pl.when(qt * tq < ctx_len)
    def _():
        for j in range(kvs):
            kj = slice(j * LANE, (j + 1) * LANE)
            s_c = lax.dot_general(_stack_heads(q_ref, grp, j), k_ref[0, 0:ctx_len, kj], _NT,
                                  preferred_element_type=F32)
            _unstack_store(_softmax_av([s_c], [v_ref[0, 0:ctx_len, kj]], sink_col(j)), o_ref, grp, tq, j)

    @pl.when(qt * tq >= ctx_len)
    def _():
        slab = 3 * tq
        start = pl.multiple_of(jnp.clip((qt - 1) * tq, ctx_len, total - slab), tq)
        qpos = qt * tq + lax.broadcasted_iota(jnp.int32, (tq, slab), 0)
        kpos = start + lax.broadcasted_iota(jnp.int32, (tq, slab), 1)
        bias = jnp.where(jnp.abs(qpos - kpos) <= window, 0.0, NEG_INF)
        bias = jnp.concatenate([bias] * grp, axis=0)
        for j in range(kvs):
            kj = slice(j * LANE, (j + 1) * LANE)
            q = _stack_heads(q_ref, grp, j)
            s_c = lax.dot_general(q, k_ref[0, 0:ctx_len, kj], _NT, preferred_element_type=F32)
            s_l = lax.dot_general(q, k_ref[0, pl.ds(start, slab), kj], _NT, preferred_element_type=F32) + bias
            o = _softmax_av([s_c, s_l], [v_ref[0, 0:ctx_len, kj], v_ref[0, pl.ds(start, slab), kj]], sink_col(j))
            _unstack_store(o, o_ref, grp, tq, j)


def _attn_window(q, k, v, sink, *, ctx_len):
    nb, total, qw = q.shape
    hq, hkv = qw // LANE, k.shape[2] // LANE
    grp = hq // hkv
    kvs = ATTN_KV_PER_STEP
    tq = CHUNK
    assert hkv % kvs == 0
    sink_rows = jnp.broadcast_to(sink.astype(F32).reshape(hq, 1, 1), (hq, 1, LANE))
    return pl.pallas_call(
        functools.partial(_attn_window_kernel, grp=grp, kvs=kvs, ctx_len=ctx_len, tq=tq, total=total,
                          window=A_WINDOW),
        grid=(nb, hkv // kvs, total // tq),
        in_specs=[pl.BlockSpec((1, tq, kvs * grp * LANE), lambda b, h, t: (b, t, h)),
                  pl.BlockSpec((1, total, kvs * LANE), lambda b, h, t: (b, 0, h)),
                  pl.BlockSpec((1, total, kvs * LANE), lambda b, h, t: (b, 0, h)),
                  pl.BlockSpec((kvs * grp, 1, LANE), lambda b, h, t: (h, 0, 0))],
        out_specs=pl.BlockSpec((1, tq, kvs * grp * LANE), lambda b, h, t: (b, t, h)),
        out_shape=jax.ShapeDtypeStruct(q.shape, BF16),
        compiler_params=_cparams("arbitrary", "arbitrary", "arbitrary"),
        name="attn_window",
    )(q, k, v, sink_rows)


def _out0_kernel(oa_ref, ob_ref, woa_ref, wob_ref, x_ref, g1_ref, n2_ref, sh2_ref, sc2_ref, xn_ref, h2_ref, *, d_model):
    y = (jnp.dot(oa_ref[0], woa_ref[...], preferred_element_type=F32)
         + jnp.dot(ob_ref[0], wob_ref[...], preferred_element_type=F32))
    xn = x_ref[0] + g1_ref[0] * y
    xn_ref[0] = xn
    h2_ref[0] = _modulate(xn, n2_ref, sc2_ref, sh2_ref, d_model).astype(BF16)


def _out0(oa, ob, w_o, x_all, mods, rows, norm2):
    nb, total, d = x_all.shape
    tm = TOK_TILE
    woa = _pad_rows(w_o[:A_Q_HEADS * HEAD_DIM], A_Q_HEADS, HEAD_DIM).astype(BF16)
    wob = _pad_rows(w_o[A_Q_HEADS * HEAD_DIM:], B_HEADS, B_V).astype(BF16)
    tile = lambda w: pl.BlockSpec((1, tm, w), lambda b, t: (b, t, 0))
    return pl.pallas_call(
        functools.partial(_out0_kernel, d_model=d),
        grid=(nb, total // tm),
        in_specs=[tile(oa.shape[2]), tile(ob.shape[2]), _full(woa), _full(wob), tile(d),
                  _mod_spec(0, 2, rows, nb, d), _full(norm2), _mod_spec(0, 3, rows, nb, d),
                  _mod_spec(0, 4, rows, nb, d)],
        out_specs=[tile(d), tile(d)],
        out_shape=[jax.ShapeDtypeStruct((nb, total, d), F32), jax.ShapeDtypeStruct((nb, total, d), BF16)],
        compiler_params=_cparams("arbitrary", "arbitrary"),
        name="out_proj0",
    )(oa, ob, woa, wob, x_all, mods, norm2, mods, mods)


def _top_rows(sc, rowf, k):
    n = sc.shape[0]
    vals, idxs = [], []
    work = sc
    for _ in range(k):
        m = jnp.max(work, axis=0, keepdims=True)
        idx = jnp.min(jnp.where(work == m, rowf, float(n)), axis=0, keepdims=True)
        vals.append(m)
        idxs.append(idx)
        work = jnp.where(rowf == idx, -jnp.inf, work)
    return vals, idxs


def _stack_rows(rows, row16):
    out = jnp.zeros(row16.shape, F32)
    for k, r in enumerate(rows):
        out = jnp.where(row16 == float(k), r, out)
    return out


def _candidates(v0, s1, slab_rows):
    return jnp.concatenate([v0[k1] + s1[0:slab_rows[k1], :] for k1 in range(len(v0))], axis=0)


def _select_exact(sc0, sc1, rowf, row16, flat, slab_rows, topk):
    nk, ts = sc0.shape
    v0, i0 = _top_rows(sc0, rowf, topk)
    v1, i1 = _top_rows(sc1, rowf, topk)
    work = _candidates(v0, _stack_rows(v1, row16), slab_rows)
    cnt = jnp.zeros((topk, ts), F32)
    zsum = jnp.zeros((1, ts), F32)
    best0 = None
    for k in range(topk):
        m = jnp.max(work, axis=0, keepdims=True)
        idx = jnp.min(jnp.where(work == m, flat, 1e9), axis=0, keepdims=True)
        work = jnp.where(flat == idx, -jnp.inf, work)
        best0 = m if best0 is None else best0
        zsum = zsum + jnp.exp(m - best0)
        cnt = cnt + jnp.where(row16 == jnp.floor(idx * (1.0 / topk)), 1.0, 0.0)
    cc = jnp.zeros((nk, ts), F32)
    rb = jnp.full((nk, ts), 99.0, F32)
    for k in range(topk):
        ck = jnp.sum(jnp.where(row16 == float(k), cnt, 0.0), axis=0, keepdims=True)
        cc = jnp.where(rowf == i0[k], ck, cc)
        rb = jnp.where(rowf == i1[k], float(k), rb)
    return cc, rb, zsum


def _select_fast(sc0, sc1, row16, slab_rows, topk):
    nk, ts = sc0.shape
    ninf = -jnp.inf
    count = lambda hit: jnp.sum(jnp.where(hit, 1.0, 0.0), axis=0, keepdims=True)
    work, v0 = sc0, []
    for _ in range(topk):
        m = jnp.max(work, axis=0, keepdims=True)
        v0.append(m)
        work = jnp.where(work == m, ninf, work)
    bad = count(work == ninf) != float(topk)
    work, v1 = sc1, []
    rb = jnp.full((nk, ts), 99.0, F32)
    for k in range(topk):
        m = jnp.max(work, axis=0, keepdims=True)
        v1.append(m)
        hit = work == m
        work = jnp.where(hit, ninf, work)
        rb = jnp.where(hit, float(k), rb)
    bad = jnp.logical_or(bad, count(rb < 99.0) != float(topk))
    work = _candidates(v0, _stack_rows(v1, row16), slab_rows)
    zsum = jnp.zeros((1, ts), F32)
    best0 = None
    for _ in range(topk):
        m = jnp.max(work, axis=0, keepdims=True)
        work = jnp.where(work == m, ninf, work)
        best0 = m if best0 is None else best0
        zsum = zsum + jnp.exp(m - best0)
    chosen = jnp.where(work == ninf, 1.0, 0.0)
    cc = jnp.zeros((nk, ts), F32)
    total = jnp.zeros((1, ts), F32)
    off = 0
    for k1 in range(topk):
        ck = jnp.sum(chosen[off:off + slab_rows[k1], :], axis=0, keepdims=True)
        off += slab_rows[k1]
        total = total + ck
        cc = jnp.where(sc0 == v0[k1], ck, cc)
    bad = jnp.logical_or(bad, total != float(topk))
    return cc, rb, zsum, jnp.max(jnp.where(bad, 1.0, 0.0))


def _peer_select_kernel(h_ref, wq_ref, keys_ref, cc_ref, e0_ref, rb_ref, e1_ref, q_sc, *, n_heads, topk):
    nk = PEER_N_KEYS
    ts = h_ref.shape[0]
    q_sc[...] = lax.dot_general(wq_ref[...], h_ref[...], (((1,), (1,)), ((), ())), preferred_element_type=F32)
    rowf = lax.broadcasted_iota(jnp.int32, (nk, ts), 0).astype(F32)
    row16 = lax.broadcasted_iota(jnp.int32, (topk, ts), 0).astype(F32)
    slab_rows = [topk] + [SUBLANE] * (topk - 1)
    n_cand = sum(slab_rows)
    ci = lax.broadcasted_iota(jnp.int32, (n_cand, ts), 0)
    rest = ci - topk
    flat = jnp.where(ci < topk, ci, (1 + (rest >> 3)) * topk + (rest & 7)).astype(F32)

    def scores(hp):
        qhp = q_sc[pl.ds(pl.multiple_of(hp * nk, nk), nk), :].astype(BF16)
        return jnp.dot(keys_ref[hp], qhp, preferred_element_type=F32)

    def heads_body(it, carry):
        heads = [it * SELECT_HEADS + u for u in range(SELECT_HEADS)]
        scs = [(scores(hd * 2), scores(hd * 2 + 1)) for hd in heads]
        fast = [_select_fast(sc0, sc1, row16, slab_rows, topk) for sc0, sc1 in scs]
        for hd, (sc0, sc1), (cc, rb, zsum, tie) in zip(heads, scs, fast):
            cc, rb, zsum = lax.cond(tie > 0.0,
                                    lambda: _select_exact(sc0, sc1, rowf, row16, flat, slab_rows, topk),
                                    lambda: (cc, rb, zsum))
            cc_ref[hd] = cc
            rb_ref[hd] = rb.astype(BF16)
            e0_ref[hd] = jnp.exp(sc0 - jnp.max(sc0, axis=0, keepdims=True))
            e1_ref[hd] = (jnp.exp(sc1 - jnp.max(sc1, axis=0, keepdims=True)) / zsum).astype(BF16)
        return carry

    lax.fori_loop(0, n_heads // SELECT_HEADS, heads_body, 0)


def _peer_select(h2, w_q, keys):
    t, d = h2.shape
    ts = PEER_SEL_TILE
    nh, nk = PEER_HEADS, PEER_N_KEYS
    wq_t = w_q.T.astype(BF16)
    keys2 = keys.reshape(nh * 2, nk, PEER_D_KEY // 2).astype(BF16)
    row_out = jax.ShapeDtypeStruct((nh, nk, t), F32)
    col_out = jax.ShapeDtypeStruct((nh, nk, t), BF16)
    ospec = pl.BlockSpec((nh, nk, ts), lambda i: (0, 0, i))
    return pl.pallas_call(
        functools.partial(_peer_select_kernel, n_heads=nh, topk=PEER_TOPK),
        grid=(t // ts,),
        in_specs=[pl.BlockSpec((ts, d), lambda i: (i, 0)), _full(wq_t), _full(keys2)],
        out_specs=[ospec] * 4,
        out_shape=[row_out, row_out, col_out, col_out],
        scratch_shapes=[pltpu.VMEM((wq_t.shape[0], ts), F32)],
        compiler_params=_cparams("arbitrary"),
        name="peer_select",
    )(h2, wq_t, keys2)


def _peer_apply_kernel(*refs, n_heads, final):
    if final:
        (h_ref, u_ref, vt_ref, cc_ref, e0_ref, rb_ref, e1_ref, x_ref, g_ref, o_ref,
         acc_ref, g_sc, p_sc, ht_sc) = refs
    else:
        h_ref, u_ref, vt_ref, cc_ref, e0_ref, rb_ref, e1_ref, o_ref, acc_ref, g_sc, p_sc, ht_sc = refs
    et = pl.program_id(1)
    nk = PEER_N_KEYS
    rows_per_tile = cc_ref.shape[1]
    tt = h_ref.shape[0]

    @pl.when(jnp.logical_and(pl.program_id(0) == 0, et == 0))
    def _():
        g_sc[...] = jnp.zeros_like(g_sc)
        acc_ref[...] = jnp.zeros_like(acc_ref)

    @pl.when(et == 0)
    def _():
        ht_sc[...] = h_ref[...].astype(F32).T.astype(BF16)

    at = jnp.dot(u_ref[...], ht_sc[...], preferred_element_type=F32)

    zero = jnp.zeros((), BF16)
    prev = (et + 1) % 2
    for ii in range(rows_per_tile):
        for c0 in range(0, tt, GATE_COLS):
            cols = slice(c0, c0 + GATE_COLS)
            ccr = [jnp.broadcast_to(cc_ref[hd, ii:ii + 1, cols], (BF16_ROWS, GATE_COLS)).astype(BF16)
                   for hd in range(n_heads)]
            e0r = [jnp.broadcast_to(e0_ref[hd, ii:ii + 1, cols], (BF16_ROWS, GATE_COLS)).astype(BF16)
                   for hd in range(n_heads)]
            for s0 in range(0, nk, BF16_ROWS):
                rws = slice(s0, s0 + BF16_ROWS)
                w = None
                for hd in range(n_heads):
                    term = jnp.where(rb_ref[hd, rws, cols] < ccr[hd], e1_ref[hd, rws, cols], zero) * e0r[hd]
                    w = term if w is None else w + term
                r0 = ii * nk + s0
                p_sc[r0:r0 + BF16_ROWS, cols] = w * g_sc[prev, r0:r0 + BF16_ROWS, cols]
    acc_ref[...] += jnp.dot(vt_ref[0], p_sc[...], preferred_element_type=F32)
    g_sc[et % 2] = jax.nn.gelu(at.astype(BF16))

    @pl.when(et == 0)
    def _():
        acc_ref[...] = jnp.zeros_like(acc_ref)

    @pl.when(et == pl.num_programs(1) - 1)
    def _():
        f = acc_ref[...].T
        if final:
            o_ref[...] = x_ref[...] + g_ref[0] * f
        else:
            o_ref[...] = f


def _peer_apply(h2, sel, u_tab, v_tab, x=None, mods=None, mod_index=None, tokens_per_batch=None):
    t, d = h2.shape
    cc, e0, rb, e1 = sel
    nh, nk = PEER_HEADS, PEER_N_KEYS
    tt, te = PEER_TOK_TILE, PEER_EXP_TILE
    ti = te // nk
    n_et = u_tab.shape[0] // te
    u = u_tab.astype(BF16)
    vt = v_tab.reshape(n_et, te, d).transpose(0, 2, 1).astype(BF16)
    final = x is not None
    cur = lambda e: jnp.minimum(e, n_et - 1)
    prv = lambda e: jnp.maximum(e - 1, 0)
    row_spec = pl.BlockSpec((nh, ti, tt), lambda i, e: (0, prv(e), i))
    col_spec = pl.BlockSpec((nh, nk, tt), lambda i, e: (0, 0, i))
    in_specs = [pl.BlockSpec((tt, d), lambda i, e: (i, 0)),
                pl.BlockSpec((te, d), lambda i, e: (cur(e), 0)),
                pl.BlockSpec((1, d, te), lambda i, e: (prv(e), 0, 0)),
                row_spec, row_spec, col_spec, col_spec]
    args = [h2, u, vt, cc, e0, rb, e1]
    if final:
        per = tokens_per_batch // tt
        in_specs += [pl.BlockSpec((tt, d), lambda i, e: (i, 0)),
                     pl.BlockSpec((1, 1, d), lambda i, e: (mod_index(i // per), 0, 0))]
        args += [x, mods]
    return pl.pallas_call(
        functools.partial(_peer_apply_kernel, n_heads=nh, final=final),
        grid=(t // tt, n_et + 1),
        in_specs=in_specs,
        out_specs=pl.BlockSpec((tt, d), lambda i, e: (i, 0)),
        out_shape=jax.ShapeDtypeStruct((t, d), F32),
        scratch_shapes=[pltpu.VMEM((d, tt), F32), pltpu.VMEM((2, te, tt), BF16), pltpu.VMEM((te, tt), BF16),
                        pltpu.VMEM((d, tt), BF16)],
        compiler_params=_cparams("arbitrary", "arbitrary"),
        name="peer_apply_final" if final else "peer_apply",
    )(*args)


def _proj1_kernel(x_ref, f_ref, g2_ref, n1_ref, sh_ref, sc_ref, w_ref, dqn_ref, dkn_ref, ca_ref, ua_ref, da_ref,
                  xn_ref, qr_ref, kr_ref, vr_ref, gr_ref, qd_ref, kd_ref, vd_ref, *, d_model, half_a, offs):
    xn = x_ref[0] + g2_ref[0] * f_ref[0]
    xn_ref[0] = xn
    h = _modulate(xn, n1_ref, sc_ref, sh_ref, d_model).astype(BF16)
    ca, ua, da = ca_ref[...], ua_ref[...], da_ref[...]
    o_qr, o_kr, o_vr, o_gr, o_qd, o_kd, o_vd, o_end = offs

    z = jnp.dot(h, w_ref[:, o_qr:o_kr], preferred_element_type=F32)
    for i in range(C_HEADS):
        qr_ref[0, :, i * LANE:(i + 1) * LANE] = _rope(z[:, i * LANE:(i + 1) * LANE], ca, ua, da, half_a).astype(BF16)
    z = jnp.dot(h, w_ref[:, o_kr:o_vr], preferred_element_type=F32) * (C_DK ** -0.5)
    for i in range(C_HEADS):
        kr_ref[0, :, i * LANE:(i + 1) * LANE] = _rope(z[:, i * LANE:(i + 1) * LANE], ca, ua, da, half_a).astype(BF16)
    vr_ref[0] = jnp.dot(h, w_ref[:, o_vr:o_gr], preferred_element_type=F32).astype(BF16)
    gr_ref[0] = jnp.dot(h, w_ref[:, o_gr:o_qd], preferred_element_type=F32).astype(BF16)
    z = jnp.dot(h, w_ref[:, o_qd:o_kd], preferred_element_type=F32)
    for i in range(D_Q_HEADS):
        y = _rms_rows(z[:, i * LANE:(i + 1) * LANE], HEAD_DIM) * dqn_ref[...]
        qd_ref[0, :, i * LANE:(i + 1) * LANE] = (_rope(y, ca, ua, da, half_a) * HEAD_DIM ** -0.5).astype(BF16)
    z = jnp.dot(h, w_ref[:, o_kd:o_vd], preferred_element_type=F32)
    for i in range(D_KV_HEADS):
        y = _rms_rows(z[:, i * LANE:(i + 1) * LANE], HEAD_DIM) * dkn_ref[...]
        kd_ref[0, :, i * LANE:(i + 1) * LANE] = _rope(y, ca, ua, da, half_a).astype(BF16)
    vd = jnp.dot(h, w_ref[:, o_vd:o_end], preferred_element_type=F32)
    vd_ref[0] = _with_ones_lane(vd, D_KV_HEADS).astype(BF16)


def _proj1(x_all, f_all, mods, rows, norm1, w_in, d_qn, d_kn, tabs_a, half_a):
    nb, total, d = x_all.shape
    tm = TOK_TILE
    cqk, cv = C_HEADS * C_DK, C_HEADS * C_DV
    wqr, wkr, wvr, wgr, wqd, wkd, wvd = jnp.split(
        w_in, np.cumsum([cqk, cqk, cv, cv, D_Q_HEADS * HEAD_DIM, D_KV_HEADS * HEAD_DIM]).tolist(), axis=1)
    parts = [_pad_cols(wqr, C_HEADS, C_DK), _pad_cols(wkr, C_HEADS, C_DK), wvr, wgr,
             _pad_cols(wqd, D_Q_HEADS, HEAD_DIM), _pad_cols(wkd, D_KV_HEADS, HEAD_DIM),
             _pad_cols(wvd, D_KV_HEADS, HEAD_DIM)]
    offs = tuple(int(o) for o in np.cumsum([0] + [p.shape[1] for p in parts]))
    w_all = jnp.concatenate(parts, axis=1).astype(BF16)
    consts = [w_all, _pad_gain(d_qn, HEAD_DIM), _pad_gain(d_kn, HEAD_DIM)]
    tab_spec = pl.BlockSpec((tm, LANE), lambda b, t: (t, 0))
    tile = lambda w: pl.BlockSpec((1, tm, w), lambda b, t: (b, t, 0))
    widths = [p.shape[1] for p in parts]
    return pl.pallas_call(
        functools.partial(_proj1_kernel, d_model=d, half_a=half_a, offs=offs),
        grid=(nb, total // tm),
        in_specs=[tile(d), tile(d), _mod_spec(0, 5, rows, nb, d), _full(norm1),
                  _mod_spec(1, 0, rows, nb, d), _mod_spec(1, 1, rows, nb, d)]
                 + [_full(c) for c in consts] + [tab_spec] * 3,
        out_specs=[tile(d)] + [tile(w) for w in widths],
        out_shape=[jax.ShapeDtypeStruct((nb, total, d), F32)]
                  + [jax.ShapeDtypeStruct((nb, total, w), BF16) for w in widths],
        compiler_params=_cparams("arbitrary", "arbitrary"),
        name="proj1",
    )(x_all, f_all, mods, norm1, mods, mods, *consts, *tabs_a)


def _retention_kernel(lg_ref, qf_ref, kf_ref, vf_ref, qb_ref, kb_ref, vb_ref, of_ref, ob_ref, st_ref, dec_ref, *,
                      n_heads):
    step = pl.program_id(1)
    c = CHUNK

    @pl.when(step == 0)
    def _():
        st_ref[...] = jnp.zeros_like(st_ref)
        ri = lax.broadcasted_iota(jnp.int32, (c, LANE), 0).astype(F32)
        diff = ri - lax.broadcasted_iota(jnp.int32, (c, LANE), 1).astype(F32)
        for d in range(2):
            for hd in range(n_heads):
                lg = lg_ref[d, hd]
                if d == 0:
                    dec_ref[d, hd, 0] = jnp.where(diff >= 0, jnp.exp(lg * jnp.maximum(diff, 0.0)), 0.0)
                    dec_ref[d, hd, 1] = jnp.exp(lg * (ri + 1.0))
                    dec_ref[d, hd, 2] = jnp.exp(lg * (c - 1.0 - ri))
                else:
                    dec_ref[d, hd, 0] = jnp.where(diff <= 0, jnp.exp(lg * jnp.maximum(-diff, 0.0)), 0.0)
                    dec_ref[d, hd, 1] = jnp.exp(lg * (c - ri))
                    dec_ref[d, hd, 2] = jnp.exp(lg * ri)

    for d, (q_ref, k_ref, v_ref, o_ref) in enumerate(((qf_ref, kf_ref, vf_ref, of_ref),
                                                      (qb_ref, kb_ref, vb_ref, ob_ref))):
        for hd in range(n_heads):
            sl = slice(hd * LANE, (hd + 1) * LANE)
            q, k, v = q_ref[0, :, sl], k_ref[0, :, sl], v_ref[0, :, sl]
            s = lax.dot_general(q, k, _NT, preferred_element_type=F32) * dec_ref[d, hd, 0]
            inner = jnp.dot(s.astype(BF16), v, preferred_element_type=F32)
            st = st_ref[d, hd]
            cross = jnp.dot(q, st.astype(BF16), preferred_element_type=F32) * dec_ref[d, hd, 1]
            o_ref[0, :, sl] = inner + cross
            kd_t = (k.astype(F32) * dec_ref[d, hd, 2]).T.astype(BF16)
            st_ref[d, hd] = st * jnp.exp(lg_ref[d, hd] * c) + jnp.dot(kd_t, v, preferred_element_type=F32)


def _retention(qr, kr, vr, lg, ctx_len):
    nb, total, w = qr.shape
    nh = w // LANE
    c = CHUNK
    nc, nctx = total // c, ctx_len // c
    fwd = pl.BlockSpec((1, c, w), lambda b, s: (b, s, 0))

    def bmap(b, s):
        return (b, jnp.where(s < nctx, nctx - 1 - s, nc - 1 - (s - nctx)), 0)

    bwd = pl.BlockSpec((1, c, w), bmap)
    out = jax.ShapeDtypeStruct((nb, total, w), F32)
    return pl.pallas_call(
        functools.partial(_retention_kernel, n_heads=nh),
        grid=(nb, nc),
        in_specs=[pl.BlockSpec(memory_space=pltpu.SMEM), fwd, fwd, fwd, bwd, bwd, bwd],
        out_specs=[fwd, bwd],
        out_shape=[out, out],
        scratch_shapes=[pltpu.VMEM((2, nh, LANE, LANE), F32), pltpu.VMEM((2, nh, 3, c, LANE), F32)],
        compiler_params=_cparams("arbitrary", "arbitrary"),
        name="retention",
    )(lg, qr, kr, vr, qr, kr, vr)


def _out1_kernel(of_ref, ob_ref, gr_ref, gn_ref, od_ref, wor_ref, wod_ref, x_ref, g1_ref, n2_ref, sh2_ref, sc2_ref,
                 xn_ref, h2_ref, *, d_model, n_heads):
    o = of_ref[0] + ob_ref[0]
    g = gr_ref[0].astype(F32)
    gate = g * jax.nn.sigmoid(g)
    gn = gn_ref[...]
    ys = []
    for hd in range(n_heads):
        sl = slice(hd * LANE, (hd + 1) * LANE)
        oh = o[:, sl]
        mu = jnp.mean(oh, axis=-1, keepdims=True)
        var = jnp.mean(jnp.square(oh - mu), axis=-1, keepdims=True)
        ys.append((gate[:, sl] * ((oh - mu) * lax.rsqrt(var + EPS) * gn[:, sl])).astype(BF16))
    y_ret = jnp.concatenate(ys, axis=1)
    y = (jnp.dot(y_ret, wor_ref[...], preferred_element_type=F32)
         + jnp.dot(od_ref[0], wod_ref[...], preferred_element_type=F32))
    xn = x_ref[0] + g1_ref[0] * y
    xn_ref[0] = xn
    h2_ref[0] = _modulate(xn, n2_ref, sc2_ref, sh2_ref, d_model).astype(BF16)


def _out1(o_f, o_b, g_r, gn_w, o_d, w_o, x_all, mods, rows, norm2, ctx_len):
    nb, total, d = x_all.shape
    tm = TOK_TILE
    skip = ctx_len // tm
    seq = total - ctx_len
    wor = w_o[:C_HEADS * C_DV].astype(BF16)
    wod = _pad_rows(w_o[C_HEADS * C_DV:], D_Q_HEADS, HEAD_DIM).astype(BF16)
    gn = gn_w.astype(F32).reshape(1, -1)
    tile_in = lambda w: pl.BlockSpec((1, tm, w), lambda b, t: (b, t + skip, 0))
    tile_out = pl.BlockSpec((1, tm, d), lambda b, t: (b, t, 0))
    mod = lambda chunk: pl.BlockSpec((1, 1, d), lambda b, t: ((rows + b) * 6 + chunk, 0, 0))
    return pl.pallas_call(
        functools.partial(_out1_kernel, d_model=d, n_heads=C_HEADS),
        grid=(nb, seq // tm),
        in_specs=[tile_in(o_f.shape[2]), tile_in(o_b.shape[2]), tile_in(g_r.shape[2]), _full(gn),
                  tile_in(o_d.shape[2]), _full(wor), _full(wod), tile_in(d), mod(2), _full(norm2), mod(3), mod(4)],
        out_specs=[tile_out, tile_out],
        out_shape=[jax.ShapeDtypeStruct((nb, seq, d), F32), jax.ShapeDtypeStruct((nb, seq, d), BF16)],
        compiler_params=_cparams("arbitrary", "arbitrary"),
        name="out_proj1",
    )(o_f, o_b, g_r, gn, o_d, wor, wod, x_all, mods, norm2, mods, mods)


def kernel(x, c, ctx, c_ctx, ada_w, ada_b, norm1_w, norm2_w, ab_w_in, ab_w_o, a_q_norm, a_k_norm, a_sink,
           b_q_lora_norm, b_kv_lora_norm, b_w_uq, b_w_ukv, b_q_norm, b_k_norm, cd_w_in, cd_w_o, c_decay_fwd,
           c_decay_bwd, c_gn_w, d_q_norm, d_k_norm, peer_w_q, peer_keys, peer_u, peer_v):
    nb, seq, d = x.shape
    ctx_len = ctx.shape[1]
    total = ctx_len + seq
    assert ctx_len == TOK_TILE and seq % TOK_TILE == 0 and seq % GRID_W == 0

    rows = -(-(nb + 1) // SUBLANE) * SUBLANE
    c_rows = jnp.concatenate([c, c_ctx[None, :], jnp.zeros((rows - nb - 1, d), c.dtype)], axis=0).astype(F32)
    mods = _ada(c_rows, ada_w, ada_b).reshape(-1, 1, d)

    tabs_a, half_a = _rope_tables(ctx_len, seq, 0, HEAD_DIM)
    tabs_b, half_b = _rope_tables(ctx_len, seq, B_NOPE, B_ROPE)
    n1 = norm1_w.astype(F32).reshape(-1, 1, d)
    n2 = norm2_w.astype(F32).reshape(-1, 1, d)

    x_all = jnp.concatenate([ctx, x], axis=1).astype(F32)
    qa, ka, va, qb, kb, vb = _proj0(x_all, mods, rows, n1[0], ab_w_in[0], b_w_uq[0], b_w_ukv[0], a_q_norm[0],
                                    a_k_norm[0], b_q_lora_norm[0], b_kv_lora_norm[0], b_q_norm[0], b_k_norm[0],
                                    tabs_a, tabs_b, half_a, half_b)
    o_a = _attn_window(qa, ka, va, a_sink[0], ctx_len=ctx_len)
    o_b = _attn_dense(qb, kb, vb, ctx_len=ctx_len)
    x_all, h2 = _out0(o_a, o_b, ab_w_o[0], x_all, mods, rows, n2[0])
    h2 = h2.reshape(nb * total, d)
    sel = _peer_select(h2, peer_w_q[0], peer_keys[0])
    f = _peer_apply(h2, sel, peer_u[0], peer_v[0]).reshape(nb, total, d)

    x_all, qr, kr, vr, gr, qd, kd, vd = _proj1(x_all, f, mods, rows, n1[1], cd_w_in[0], d_q_norm[0], d_k_norm[0],
                                               tabs_a, half_a)
    lg = jnp.stack([jax.nn.log_sigmoid(c_decay_fwd[0].astype(F32)), jax.nn.log_sigmoid(c_decay_bwd[0].astype(F32))])
    o_f, o_bw = _retention(qr, kr, vr, lg, ctx_len)
    o_d = _attn_dense(qd, kd, vd, ctx_len=ctx_len)
    x_lat, h2 = _out1(o_f, o_bw, gr, c_gn_w[0], o_d, cd_w_o[0], x_all, mods, rows, n2[1], ctx_len)
    h2 = h2.reshape(nb * seq, d)
    sel = _peer_select(h2, peer_w_q[1], peer_keys[1])
    out = _peer_apply(h2, sel, peer_u[1], peer_v[1], x=x_lat.reshape(nb * seq, d), mods=mods,
                      mod_index=lambda b: (rows + b) * 6 + 5, tokens_per_batch=seq)
    return out.reshape(nb, seq, d).astype(x.dtype)
```

```python
import functools

import numpy as np
import jax
import jax.numpy as jnp
from jax import lax
from jax.experimental import pallas as pl
from jax.experimental.pallas import tpu as pltpu

F32 = jnp.float32
BF16 = jnp.bfloat16

GRID_W = 64
ROPE_THETA = 10000.0
EPS = 1e-6
NEG_INF = -1e30
HEAD_DIM = 64
A_Q_HEADS, A_KV_HEADS, A_WINDOW = 8, 2, 128
B_HEADS, B_NOPE, B_ROPE, B_V, B_Q_RANK, B_KV_RANK = 8, 64, 32, 64, 256, 256
B_QK = B_NOPE + B_ROPE
C_HEADS, C_DK, C_DV = 4, 64, 128
D_Q_HEADS, D_KV_HEADS = 8, 2
PEER_HEADS, PEER_N_KEYS, PEER_D_KEY, PEER_TOPK = 8, 128, 256, 16

LANE = 128
SUBLANE = 8
BF16_ROWS = 16
ONES_LANE = LANE - 1
VMEM_LIMIT = 56 * 1024 * 1024

TOK_TILE = 256
CHUNK = 128
ATTN_KV_PER_STEP = 2
ATTN_CHAINS_PER_STEP = 4
PEER_SEL_TILE = 256
SELECT_HEADS = 2
PEER_TOK_TILE = 512
PEER_EXP_TILE = 2048
GATE_COLS = 256


def _cparams(*sem):
    return pltpu.CompilerParams(dimension_semantics=sem, vmem_limit_bytes=VMEM_LIMIT)


def _full(arr):
    nd = arr.ndim
    return pl.BlockSpec(arr.shape, lambda *_: (0,) * nd)


def _pad_cols(w, n_heads, d):
    lead = w.shape[:-1]
    w = w.reshape(lead + (n_heads, d))
    w = jnp.pad(w, [(0, 0)] * len(lead) + [(0, 0), (0, LANE - d)])
    return w.reshape(lead + (n_heads * LANE,))


def _pad_rows(w, n_heads, d):
    n = w.shape[-1]
    w = w.reshape(n_heads, d, n)
    w = jnp.pad(w, [(0, 0), (0, LANE - d), (0, 0)])
    return w.reshape(n_heads * LANE, n)


def _pad_gain(g, d):
    return jnp.pad(g.astype(F32), (0, LANE - d)).reshape(1, LANE)


def _rope_tables(ctx_len, seq, lane_off, d_rot):
    blk = d_rot // 2
    half = blk // 2
    freqs = ROPE_THETA ** (-np.arange(half, dtype=np.float64) / half)
    pos = np.arange(seq)
    total = ctx_len + seq
    cos = np.ones((total, LANE), np.float64)
    sup = np.zeros((total, LANE), np.float64)
    sdn = np.zeros((total, LANE), np.float64)
    for axis, p in enumerate((pos // GRID_W, pos % GRID_W)):
        ang = p[:, None].astype(np.float64) * freqs[None, :]
        c, s = np.cos(ang), np.sin(ang)
        base = lane_off + axis * blk
        cos[ctx_len:, base:base + half] = c
        cos[ctx_len:, base + half:base + blk] = c
        sdn[ctx_len:, base:base + half] = -s
        sup[ctx_len:, base + half:base + blk] = s
    return (jnp.asarray(cos, F32), jnp.asarray(sup, F32), jnp.asarray(sdn, F32)), half


def _rms_rows(x, true_dim):
    return x * lax.rsqrt(jnp.sum(x * x, axis=-1, keepdims=True) * (1.0 / true_dim) + EPS)


def _rope(y, cos, sup, sdn, half):
    return y * cos + pltpu.roll(y, half, 1) * sup + pltpu.roll(y, LANE - half, 1) * sdn


def _ada_kernel(c_ref, w_ref, b_ref, o_ref):
    c = c_ref[...]
    s = c * jax.nn.sigmoid(c)
    o_ref[0] = jnp.dot(s.astype(BF16), w_ref[0].astype(BF16), preferred_element_type=F32) + b_ref[0]


def _ada(c_rows, ada_w, ada_b):
    depth, d, n = ada_w.shape
    rows = c_rows.shape[0]
    tn = 1536
    return pl.pallas_call(
        _ada_kernel,
        grid=(depth, n // tn),
        in_specs=[pl.BlockSpec((rows, d), lambda l, j: (0, 0)),
                  pl.BlockSpec((1, d, tn), lambda l, j: (l, 0, j)),
                  pl.BlockSpec((1, 1, tn), lambda l, j: (l, 0, j))],
        out_specs=pl.BlockSpec((1, rows, tn), lambda l, j: (l, 0, j)),
        out_shape=jax.ShapeDtypeStruct((depth, rows, n), F32),
        compiler_params=_cparams("arbitrary", "arbitrary"),
        name="ada_mod",
    )(c_rows, ada_w, ada_b.reshape(depth, 1, n))


def _mod_spec(layer, chunk, rows, nb, d, tile_axis=1):
    def imap(*ids):
        b, t = ids[0], ids[tile_axis]
        r = jnp.where(t == 0, nb, b)
        return ((layer * rows + r) * 6 + chunk, 0, 0)
    return pl.BlockSpec((1, 1, d), imap)


def _modulate(x, n_ref, sc_ref, sh_ref, d):
    return _rms_rows(x, d) * n_ref[...] * (1.0 + sc_ref[0]) + sh_ref[0]


def _proj0_kernel(x_ref, n1_ref, sh_ref, sc_ref, w_ref, wuq_ref, wuk_ref, wuv_ref,
                  aqn_ref, akn_ref, bqln_ref, bkvln_ref, bqn_ref, bkn_ref,
                  ca_ref, ua_ref, da_ref, cb_ref, ub_ref, db_ref,
                  qa_ref, ka_ref, va_ref, qb_ref, kb_ref, vb_ref, *, d_model, half_a, half_b, offs):
    h = _modulate(x_ref[0], n1_ref, sc_ref, sh_ref, d_model).astype(BF16)
    ca, ua, da = ca_ref[...], ua_ref[...], da_ref[...]
    cb, ub, db = cb_ref[...], ub_ref[...], db_ref[...]
    o_qa, o_ka, o_va, o_cq, o_ckv, o_kr, o_end = offs

    z = jnp.dot(h, w_ref[:, o_qa:o_ka], preferred_element_type=F32)
    for i in range(A_Q_HEADS):
        y = _rms_rows(z[:, i * LANE:(i + 1) * LANE], HEAD_DIM) * aqn_ref[...]
        qa_ref[0, :, i * LANE:(i + 1) * LANE] = (_rope(y, ca, ua, da, half_a) * HEAD_DIM ** -0.5).astype(BF16)
    z = jnp.dot(h, w_ref[:, o_ka:o_va], preferred_element_type=F32)
    for i in range(A_KV_HEADS):
        y = _rms_rows(z[:, i * LANE:(i + 1) * LANE], HEAD_DIM) * akn_ref[...]
        ka_ref[0, :, i * LANE:(i + 1) * LANE] = _rope(y, ca, ua, da, half_a).astype(BF16)
    va = jnp.dot(h, w_ref[:, o_va:o_cq], preferred_element_type=F32)
    va_ref[0] = _with_ones_lane(va, A_KV_HEADS).astype(BF16)

    cq = jnp.dot(h, w_ref[:, o_cq:o_ckv], preferred_element_type=F32)
    cq = (_rms_rows(cq, B_Q_RANK) * bqln_ref[...]).astype(BF16)
    z = jnp.dot(cq, wuq_ref[...], preferred_element_type=F32)
    for i in range(B_HEADS):
        y = _rms_rows(z[:, i * LANE:(i + 1) * LANE], B_QK) * bqn_ref[...]
        qb_ref[0, :, i * LANE:(i + 1) * LANE] = (_rope(y, cb, ub, db, half_b) * B_QK ** -0.5).astype(BF16)

    ckv = jnp.dot(h, w_ref[:, o_ckv:o_kr], preferred_element_type=F32)
    ckv = (_rms_rows(ckv, B_KV_RANK) * bkvln_ref[...]).astype(BF16)
    kr = jnp.dot(h, w_ref[:, o_kr:o_end], preferred_element_type=F32)
    z = jnp.dot(ckv, wuk_ref[...], preferred_element_type=F32)
    for i in range(B_HEADS):
        y = _rms_rows(z[:, i * LANE:(i + 1) * LANE] + kr, B_QK) * bkn_ref[...]
        kb_ref[0, :, i * LANE:(i + 1) * LANE] = _rope(y, cb, ub, db, half_b).astype(BF16)
    vb = jnp.dot(ckv, wuv_ref[...], preferred_element_type=F32)
    vb_ref[0] = _with_ones_lane(vb, B_HEADS).astype(BF16)


def _proj0(x_all, mods, rows, norm1, w_in, b_wuq, b_wukv, a_qn, a_kn, b_qln, b_kvln, b_qn, b_kn, tabs_a, tabs_b,
           half_a, half_b):
    nb, total, d = x_all.shape
    tm = TOK_TILE
    wq, wk, wv, wcq, wckv, wkr = jnp.split(
        w_in, np.cumsum([A_Q_HEADS * HEAD_DIM, A_KV_HEADS * HEAD_DIM, A_KV_HEADS * HEAD_DIM, B_Q_RANK, B_KV_RANK])
        .tolist(), axis=1)
    kr_pad = jnp.pad(wkr, ((0, 0), (B_NOPE, LANE - B_QK)))
    parts = [_pad_cols(wq, A_Q_HEADS, HEAD_DIM), _pad_cols(wk, A_KV_HEADS, HEAD_DIM),
             _pad_cols(wv, A_KV_HEADS, HEAD_DIM), wcq, wckv, kr_pad]
    offs = tuple(int(o) for o in np.cumsum([0] + [p.shape[1] for p in parts]))
    w_all = jnp.concatenate(parts, axis=1).astype(BF16)
    wuq = _pad_cols(b_wuq, B_HEADS, B_QK).astype(BF16)
    wukv = b_wukv.reshape(B_KV_RANK, B_HEADS, B_NOPE + B_V)
    wuk = _pad_cols(wukv[..., :B_NOPE].reshape(B_KV_RANK, -1), B_HEADS, B_NOPE).astype(BF16)
    wuv = _pad_cols(wukv[..., B_NOPE:].reshape(B_KV_RANK, -1), B_HEADS, B_V).astype(BF16)
    consts = [w_all, wuq, wuk, wuv, _pad_gain(a_qn, HEAD_DIM), _pad_gain(a_kn, HEAD_DIM),
              b_qln.astype(F32).reshape(1, -1), b_kvln.astype(F32).reshape(1, -1),
              _pad_gain(b_qn, B_QK), _pad_gain(b_kn, B_QK)]
    tab_spec = pl.BlockSpec((tm, LANE), lambda b, t: (t, 0))
    wide = lambda nh: pl.BlockSpec((1, tm, nh * LANE), lambda b, t: (b, t, 0))
    shp = lambda nh: jax.ShapeDtypeStruct((nb, total, nh * LANE), BF16)
    return pl.pallas_call(
        functools.partial(_proj0_kernel, d_model=d, half_a=half_a, half_b=half_b, offs=offs),
        grid=(nb, total // tm),
        in_specs=[pl.BlockSpec((1, tm, d), lambda b, t: (b, t, 0)), _full(norm1),
                  _mod_spec(0, 0, rows, nb, d), _mod_spec(0, 1, rows, nb, d)]
                 + [_full(c) for c in consts] + [tab_spec] * 6,
        out_specs=[wide(A_Q_HEADS), wide(A_KV_HEADS), wide(A_KV_HEADS), wide(B_HEADS), wide(B_HEADS), wide(B_HEADS)],
        out_shape=[shp(A_Q_HEADS), shp(A_KV_HEADS), shp(A_KV_HEADS), shp(B_HEADS), shp(B_HEADS), shp(B_HEADS)],
        compiler_params=_cparams("arbitrary", "arbitrary"),
        name="proj0",
    )(x_all, norm1, mods, mods, *consts, *tabs_a, *tabs_b)


_NT = (((1,), (1,)), ((), ()))


def _stack_heads(q_ref, grp, j=0):
    h0 = j * grp
    if grp == 1:
        return q_ref[0, :, h0 * LANE:(h0 + 1) * LANE]
    return jnp.concatenate([q_ref[0, :, (h0 + g) * LANE:(h0 + g + 1) * LANE] for g in range(grp)], axis=0)


def _softmax_av(scores, values, sink=None):
    m = None
    for s in scores:
        ms = jnp.max(s, axis=-1, keepdims=True)
        m = ms if m is None else jnp.maximum(m, ms)
    if sink is not None:
        m = jnp.maximum(m, sink)
    o = None
    for s, v in zip(scores, values):
        os_ = jnp.dot(jnp.exp(s - m).astype(BF16), v, preferred_element_type=F32)
        o = os_ if o is None else o + os_
    den = o[:, ONES_LANE:ONES_LANE + 1]
    if sink is not None:
        den = den + jnp.exp(sink - m)
    return o / den


def _with_ones_lane(v, n_heads):
    lane = lax.broadcasted_iota(jnp.int32, (1, n_heads * LANE), 1) % LANE
    return v + jnp.where(lane == ONES_LANE, 1.0, 0.0)


def _unstack_store(o, o_ref, grp, tq, j=0):
    for g in range(grp):
        h = j * grp + g
        o_ref[0, :, h * LANE:(h + 1) * LANE] = o[g * tq:(g + 1) * tq].astype(BF16)


def _attn_dense_kernel(q_ref, k_ref, v_ref, o_ref, *, grp, kvs, ctx_len, tq):
    qt = pl.program_id(2)

    @pl.when(qt * tq < ctx_len)
    def _():
        for j in range(kvs):
            kj = slice(j * LANE, (j + 1) * LANE)
            s = lax.dot_general(_stack_heads(q_ref, grp, j), k_ref[0, 0:ctx_len, kj], _NT,
                                preferred_element_type=F32)
            _unstack_store(_softmax_av([s], [v_ref[0, 0:ctx_len, kj]]), o_ref, grp, tq, j)

    @pl.when(qt * tq >= ctx_len)
    def _():
        for j in range(kvs):
            kj = slice(j * LANE, (j + 1) * LANE)
            s = lax.dot_general(_stack_heads(q_ref, grp, j), k_ref[0, :, kj], _NT, preferred_element_type=F32)
            _unstack_store(_softmax_av([s], [v_ref[0, :, kj]]), o_ref, grp, tq, j)


def _attn_dense(q, k, v, *, ctx_len):
    nb, total, qw = q.shape
    hq, hkv = qw // LANE, k.shape[2] // LANE
    grp = hq // hkv
    kvs = min(ATTN_CHAINS_PER_STEP if grp == 1 else ATTN_KV_PER_STEP, hkv)
    tq = CHUNK if grp > 1 else TOK_TILE
    assert ctx_len % tq == 0 and hkv % kvs == 0
    return pl.pallas_call(
        functools.partial(_attn_dense_kernel, grp=grp, kvs=kvs, ctx_len=ctx_len, tq=tq),
        grid=(nb, hkv // kvs, total // tq),
        in_specs=[pl.BlockSpec((1, tq, kvs * grp * LANE), lambda b, h, t: (b, t, h)),
                  pl.BlockSpec((1, total, kvs * LANE), lambda b, h, t: (b, 0, h)),
                  pl.BlockSpec((1, total, kvs * LANE), lambda b, h, t: (b, 0, h))],
        out_specs=pl.BlockSpec((1, tq, kvs * grp * LANE), lambda b, h, t: (b, t, h)),
        out_shape=jax.ShapeDtypeStruct(q.shape, BF16),
        compiler_params=_cparams("arbitrary", "arbitrary", "arbitrary"),
        name="attn_dense",
    )(q, k, v)


def _attn_window_kernel(q_ref, k_ref, v_ref, sink_ref, o_ref, *, grp, kvs, ctx_len, tq, total, window):
    qt = pl.program_id(2)

    def sink_col(j):
        return jnp.concatenate([jnp.broadcast_to(sink_ref[j * grp + g][:, 0:1], (tq, 1)) for g in range(grp)], axis=0)

    @pl.when(qt * tq < ctx_len)
    def _():
        for j in range(kvs):
            kj = slice(j * LANE, (j + 1) * LANE)
            s_c = lax.dot_general(_stack_heads(q_ref, grp, j), k_ref[0, 0:ctx_len, kj], _NT,
                                  preferred_element_type=F32)
            _unstack_store(_softmax_av([s_c], [v_ref[0, 0:ctx_len, kj]], sink_col(j)), o_ref, grp, tq, j)

    @pl.when(qt * tq >= ctx_len)
    def _():
        slab = 3 * tq
        start = pl.multiple_of(jnp.clip((qt - 1) * tq, ctx_len, total - slab), tq)
        qpos = qt * tq + lax.broadcasted_iota(jnp.int32, (tq, slab), 0)
        kpos = start + lax.broadcasted_iota(jnp.int32, (tq, slab), 1)
        bias = jnp.where(jnp.abs(qpos - kpos) <= window, 0.0, NEG_INF)
        bias = jnp.concatenate([bias] * grp, axis=0)
        for j in range(kvs):
            kj = slice(j * LANE, (j + 1) * LANE)
            q = _stack_heads(q_ref, grp, j)
            s_c = lax.dot_general(q, k_ref[0, 0:ctx_len, kj], _NT, preferred_element_type=F32)
            s_l = lax.dot_general(q, k_ref[0, pl.ds(start, slab), kj], _NT, preferred_element_type=F32) + bias
            o = _softmax_av([s_c, s_l], [v_ref[0, 0:ctx_len, kj], v_ref[0, pl.ds(start, slab), kj]], sink_col(j))
            _unstack_store(o, o_ref, grp, tq, j)


def _attn_window(q, k, v, sink, *, ctx_len):
    nb, total, qw = q.shape
    hq, hkv = qw // LANE, k.shape[2] // LANE
    grp = hq // hkv
    kvs = ATTN_KV_PER_STEP
    tq = CHUNK
    assert hkv % kvs == 0
    sink_rows = jnp.broadcast_to(sink.astype(F32).reshape(hq, 1, 1), (hq, 1, LANE))
    return pl.pallas_call(
        functools.partial(_attn_window_kernel, grp=grp, kvs=kvs, ctx_len=ctx_len, tq=tq, total=total,
                          window=A_WINDOW),
        grid=(nb, hkv // kvs, total // tq),
        in_specs=[pl.BlockSpec((1, tq, kvs * grp * LANE), lambda b, h, t: (b, t, h)),
                  pl.BlockSpec((1, total, kvs * LANE), lambda b, h, t: (b, 0, h)),
                  pl.BlockSpec((1, total, kvs * LANE), lambda b, h, t: (b, 0, h)),
                  pl.BlockSpec((kvs * grp, 1, LANE), lambda b, h, t: (h, 0, 0))],
        out_specs=pl.BlockSpec((1, tq, kvs * grp * LANE), lambda b, h, t: (b, t, h)),
        out_shape=jax.ShapeDtypeStruct(q.shape, BF16),
        compiler_params=_cparams("arbitrary", "arbitrary", "arbitrary"),
        name="attn_window",
    )(q, k, v, sink_rows)


def _out0_kernel(oa_ref, ob_ref, woa_ref, wob_ref, x_ref, g1_ref, n2_ref, sh2_ref, sc2_ref, xn_ref, h2_ref, *, d_model):
    y = (jnp.dot(oa_ref[0], woa_ref[...], preferred_element_type=F32)
         + jnp.dot(ob_ref[0], wob_ref[...], preferred_element_type=F32))
    xn = x_ref[0] + g1_ref[0] * y
    xn_ref[0] = xn
    h2_ref[0] = _modulate(xn, n2_ref, sc2_ref, sh2_ref, d_model).astype(BF16)


def _out0(oa, ob, w_o, x_all, mods, rows, norm2):
    nb, total, d = x_all.shape
    tm = TOK_TILE
    woa = _pad_rows(w_o[:A_Q_HEADS * HEAD_DIM], A_Q_HEADS, HEAD_DIM).astype(BF16)
    wob = _pad_rows(w_o[A_Q_HEADS * HEAD_DIM:], B_HEADS, B_V).astype(BF16)
    tile = lambda w: pl.BlockSpec((1, tm, w), lambda b, t: (b, t, 0))
    return pl.pallas_call(
        functools.partial(_out0_kernel, d_model=d),
        grid=(nb, total // tm),
        in_specs=[tile(oa.shape[2]), tile(ob.shape[2]), _full(woa), _full(wob), tile(d),
                  _mod_spec(0, 2, rows, nb, d), _full(norm2), _mod_spec(0, 3, rows, nb, d),
                  _mod_spec(0, 4, rows, nb, d)],
        out_specs=[tile(d), tile(d)],
        out_shape=[jax.ShapeDtypeStruct((nb, total, d), F32), jax.ShapeDtypeStruct((nb, total, d), BF16)],
        compiler_params=_cparams("arbitrary", "arbitrary"),
        name="out_proj0",
    )(oa, ob, woa, wob, x_all, mods, norm2, mods, mods)


def _top_rows(sc, rowf, k):
    n = sc.shape[0]
    vals, idxs = [], []
    work = sc
    for _ in range(k):
        m = jnp.max(work, axis=0, keepdims=True)
        idx = jnp.min(jnp.where(work == m, rowf, float(n)), axis=0, keepdims=True)
        vals.append(m)
        idxs.append(idx)
        work = jnp.where(rowf == idx, -jnp.inf, work)
    return vals, idxs


def _stack_rows(rows, row16):
    out = jnp.zeros(row16.shape, F32)
    for k, r in enumerate(rows):
        out = jnp.where(row16 == float(k), r, out)
    return out


def _candidates(v0, s1, slab_rows):
    return jnp.concatenate([v0[k1] + s1[0:slab_rows[k1], :] for k1 in range(len(v0))], axis=0)


def _select_exact(sc0, sc1, rowf, row16, flat, slab_rows, topk):
    nk, ts = sc0.shape
    v0, i0 = _top_rows(sc0, rowf, topk)
    v1, i1 = _top_rows(sc1, rowf, topk)
    work = _candidates(v0, _stack_rows(v1, row16), slab_rows)
    cnt = jnp.zeros((topk, ts), F32)
    zsum = jnp.zeros((1, ts), F32)
    best0 = None
    for k in range(topk):
        m = jnp.max(work, axis=0, keepdims=True)
        idx = jnp.min(jnp.where(work == m, flat, 1e9), axis=0, keepdims=True)
        work = jnp.where(flat == idx, -jnp.inf, work)
        best0 = m if best0 is None else best0
        zsum = zsum + jnp.exp(m - best0)
        cnt = cnt + jnp.where(row16 == jnp.floor(idx * (1.0 / topk)), 1.0, 0.0)
    cc = jnp.zeros((nk, ts), F32)
    rb = jnp.full((nk, ts), 99.0, F32)
    for k in range(topk):
        ck = jnp.sum(jnp.where(row16 == float(k), cnt, 0.0), axis=0, keepdims=True)
        cc = jnp.where(rowf == i0[k], ck, cc)
        rb = jnp.where(rowf == i1[k], float(k), rb)
    return cc, rb, zsum


def _select_fast(sc0, sc1, row16, slab_rows, topk):
    nk, ts = sc0.shape
    ninf = -jnp.inf
    count = lambda hit: jnp.sum(jnp.where(hit, 1.0, 0.0), axis=0, keepdims=True)
    work, v0 = sc0, []
    for _ in range(topk):
        m = jnp.max(work, axis=0, keepdims=True)
        v0.append(m)
        work = jnp.where(work == m, ninf, work)
    bad = count(work == ninf) != float(topk)
    work, v1 = sc1, []
    rb = jnp.full((nk, ts), 99.0, F32)
    for k in range(topk):
        m = jnp.max(work, axis=0, keepdims=True)
        v1.append(m)
        hit = work == m
        work = jnp.where(hit, ninf, work)
        rb = jnp.where(hit, float(k), rb)
    bad = jnp.logical_or(bad, count(rb < 99.0) != float(topk))
    work = _candidates(v0, _stack_rows(v1, row16), slab_rows)
    zsum = jnp.zeros((1, ts), F32)
    best0 = None
    for _ in range(topk):
        m = jnp.max(work, axis=0, keepdims=True)
        work = jnp.where(work == m, ninf, work)
        best0 = m if best0 is None else best0
        zsum = zsum + jnp.exp(m - best0)
    chosen = jnp.where(work == ninf, 1.0, 0.0)
    cc = jnp.zeros((nk, ts), F32)
    total = jnp.zeros((1, ts), F32)
    off = 0
    for k1 in range(topk):
        ck = jnp.sum(chosen[off:off + slab_rows[k1], :], axis=0, keepdims=True)
        off += slab_rows[k1]
        total = total + ck
        cc = jnp.where(sc0 == v0[k1], ck, cc)
    bad = jnp.logical_or(bad, total != float(topk))
    return cc, rb, zsum, jnp.max(jnp.where(bad, 1.0, 0.0))


def _peer_select_kernel(h_ref, wq_ref, keys_ref, cc_ref, e0_ref, rb_ref, e1_ref, q_sc, *, n_heads, topk):
    nk = PEER_N_KEYS
    ts = h_ref.shape[0]
    q_sc[...] = lax.dot_general(wq_ref[...], h_ref[...], (((1,), (1,)), ((), ())), preferred_element_type=F32)
    rowf = lax.broadcasted_iota(jnp.int32, (nk, ts), 0).astype(F32)
    row16 = lax.broadcasted_iota(jnp.int32, (topk, ts), 0).astype(F32)
    slab_rows = [topk] + [SUBLANE] * (topk - 1)
    n_cand = sum(slab_rows)
    ci = lax.broadcasted_iota(jnp.int32, (n_cand, ts), 0)
    rest = ci - topk
    flat = jnp.where(ci < topk, ci, (1 + (rest >> 3)) * topk + (rest & 7)).astype(F32)

    def scores(hp):
        qhp = q_sc[pl.ds(pl.multiple_of(hp * nk, nk), nk), :].astype(BF16)
        return jnp.dot(keys_ref[hp], qhp, preferred_element_type=F32)

    def heads_body(it, carry):
        heads = [it * SELECT_HEADS + u for u in range(SELECT_HEADS)]
        scs = [(scores(hd * 2), scores(hd * 2 + 1)) for hd in heads]
        fast = [_select_fast(sc0, sc1, row16, slab_rows, topk) for sc0, sc1 in scs]
        for hd, (sc0, sc1), (cc, rb, zsum, tie) in zip(heads, scs, fast):
            cc, rb, zsum = lax.cond(tie > 0.0,
                                    lambda: _select_exact(sc0, sc1, rowf, row16, flat, slab_rows, topk),
                                    lambda: (cc, rb, zsum))
            cc_ref[hd] = cc
            rb_ref[hd] = rb.astype(BF16)
            e0_ref[hd] = jnp.exp(sc0 - jnp.max(sc0, axis=0, keepdims=True))
            e1_ref[hd] = (jnp.exp(sc1 - jnp.max(sc1, axis=0, keepdims=True)) / zsum).astype(BF16)
        return carry

    lax.fori_loop(0, n_heads // SELECT_HEADS, heads_body, 0)


def _peer_select(h2, w_q, keys):
    t, d = h2.shape
    ts = PEER_SEL_TILE
    nh, nk = PEER_HEADS, PEER_N_KEYS
    wq_t = w_q.T.astype(BF16)
    keys2 = keys.reshape(nh * 2, nk, PEER_D_KEY // 2).astype(BF16)
    row_out = jax.ShapeDtypeStruct((nh, nk, t), F32)
    col_out = jax.ShapeDtypeStruct((nh, nk, t), BF16)
    ospec = pl.BlockSpec((nh, nk, ts), lambda i: (0, 0, i))
    return pl.pallas_call(
        functools.partial(_peer_select_kernel, n_heads=nh, topk=PEER_TOPK),
        grid=(t // ts,),
        in_specs=[pl.BlockSpec((ts, d), lambda i: (i, 0)), _full(wq_t), _full(keys2)],
        out_specs=[ospec] * 4,
        out_shape=[row_out, row_out, col_out, col_out],
        scratch_shapes=[pltpu.VMEM((wq_t.shape[0], ts), F32)],
        compiler_params=_cparams("arbitrary"),
        name="peer_select",
    )(h2, wq_t, keys2)


def _peer_apply_kernel(*refs, n_heads, final):
    if final:
        (h_ref, u_ref, vt_ref, cc_ref, e0_ref, rb_ref, e1_ref, x_ref, g_ref, o_ref,
         acc_ref, g_sc, p_sc, ht_sc) = refs
    else:
        h_ref, u_ref, vt_ref, cc_ref, e0_ref, rb_ref, e1_ref, o_ref, acc_ref, g_sc, p_sc, ht_sc = refs
    et = pl.program_id(1)
    nk = PEER_N_KEYS
    rows_per_tile = cc_ref.shape[1]
    tt = h_ref.shape[0]

    @pl.when(et == 0)
    def _():
        ht_sc[...] = h_ref[...].astype(F32).T.astype(BF16)

    def stage1():
        return jnp.dot(u_ref[...], ht_sc[...], preferred_element_type=F32)

    def stage1_store(at):
        g_sc[et % 2] = jax.nn.gelu(at.astype(BF16))

    def stage2():
        zero = jnp.zeros((), BF16)
        prev = (et + 1) % 2
        for ii in range(rows_per_tile):
            for c0 in range(0, tt, GATE_COLS):
                cols = slice(c0, c0 + GATE_COLS)
                ccr = [jnp.broadcast_to(cc_ref[hd, ii:ii + 1, cols], (BF16_ROWS, GATE_COLS)).astype(BF16)
                       for hd in range(n_heads)]
                e0r = [jnp.broadcast_to(e0_ref[hd, ii:ii + 1, cols], (BF16_ROWS, GATE_COLS)).astype(BF16)
                       for hd in range(n_heads)]
                for s0 in range(0, nk, BF16_ROWS):
                    rws = slice(s0, s0 + BF16_ROWS)
                    w = None
                    for hd in range(n_heads):
                        term = jnp.where(rb_ref[hd, rws, cols] < ccr[hd], e1_ref[hd, rws, cols], zero) * e0r[hd]
                        w = term if w is None else w + term
                    r0 = ii * nk + s0
                    p_sc[r0:r0 + BF16_ROWS, cols] = w * g_sc[prev, r0:r0 + BF16_ROWS, cols]
        acc_ref[...] += jnp.dot(vt_ref[0], p_sc[...], preferred_element_type=F32)

    last = pl.num_programs(1) - 1

    @pl.when(et == 0)
    def _():
        acc_ref[...] = jnp.zeros_like(acc_ref)
        stage1_store(stage1())

    @pl.when(jnp.logical_and(et > 0, et < last))
    def _():
        at = stage1()
        stage2()
        stage1_store(at)

    @pl.when(et == last)
    def _():
        stage2()

    @pl.when(et == last)
    def _():
        f = acc_ref[...].T
        if final:
            o_ref[...] = x_ref[...] + g_ref[0] * f
        else:
            o_ref[...] = f


def _peer_apply(h2, sel, u_tab, v_tab, x=None, mods=None, mod_index=None, tokens_per_batch=None):
    t, d = h2.shape
    cc, e0, rb, e1 = sel
    nh, nk = PEER_HEADS, PEER_N_KEYS
    tt, te = PEER_TOK_TILE, PEER_EXP_TILE
    ti = te // nk
    n_et = u_tab.shape[0] // te
    u = u_tab.astype(BF16)
    vt = v_tab.reshape(n_et, te, d).transpose(0, 2, 1).astype(BF16)
    final = x is not None
    cur = lambda e: jnp.minimum(e, n_et - 1)
    prv = lambda e: jnp.maximum(e - 1, 0)
    row_spec = pl.BlockSpec((nh, ti, tt), lambda i, e: (0, prv(e), i))
    col_spec = pl.BlockSpec((nh, nk, tt), lambda i, e: (0, 0, i))
    in_specs = [pl.BlockSpec((tt, d), lambda i, e: (i, 0)),
                pl.BlockSpec((te, d), lambda i, e: (cur(e), 0)),
                pl.BlockSpec((1, d, te), lambda i, e: (prv(e), 0, 0)),
                row_spec, row_spec, col_spec, col_spec]
    args = [h2, u, vt, cc, e0, rb, e1]
    if final:
        per = tokens_per_batch // tt
        in_specs += [pl.BlockSpec((tt, d), lambda i, e: (i, 0)),
                     pl.BlockSpec((1, 1, d), lambda i, e: (mod_index(i // per), 0, 0))]
        args += [x, mods]
    return pl.pallas_call(
        functools.partial(_peer_apply_kernel, n_heads=nh, final=final),
        grid=(t // tt, n_et + 1),
        in_specs=in_specs,
        out_specs=pl.BlockSpec((tt, d), lambda i, e: (i, 0)),
        out_shape=jax.ShapeDtypeStruct((t, d), F32),
        scratch_shapes=[pltpu.VMEM((d, tt), F32), pltpu.VMEM((2, te, tt), BF16), pltpu.VMEM((te, tt), BF16),
                        pltpu.VMEM((d, tt), BF16)],
        compiler_params=_cparams("arbitrary", "arbitrary"),
        name="peer_apply_final" if final else "peer_apply",
    )(*args)


def _proj1_kernel(x_ref, f_ref, g2_ref, n1_ref, sh_ref, sc_ref, w_ref, dqn_ref, dkn_ref, ca_ref, ua_ref, da_ref,
                  xn_ref, qr_ref, kr_ref, vr_ref, gr_ref, qd_ref, kd_ref, vd_ref, *, d_model, half_a, offs):
    xn = x_ref[0] + g2_ref[0] * f_ref[0]
    xn_ref[0] = xn
    h = _modulate(xn, n1_ref, sc_ref, sh_ref, d_model).astype(BF16)
    ca, ua, da = ca_ref[...], ua_ref[...], da_ref[...]
    o_qr, o_kr, o_vr, o_gr, o_qd, o_kd, o_vd, o_end = offs

    z = jnp.dot(h, w_ref[:, o_qr:o_kr], preferred_element_type=F32)
    for i in range(C_HEADS):
        qr_ref[0, :, i * LANE:(i + 1) * LANE] = _rope(z[:, i * LANE:(i + 1) * LANE], ca, ua, da, half_a).astype(BF16)
    z = jnp.dot(h, w_ref[:, o_kr:o_vr], preferred_element_type=F32) * (C_DK ** -0.5)
    for i in range(C_HEADS):
        kr_ref[0, :, i * LANE:(i + 1) * LANE] = _rope(z[:, i * LANE:(i + 1) * LANE], ca, ua, da, half_a).astype(BF16)
    vr_ref[0] = jnp.dot(h, w_ref[:, o_vr:o_gr], preferred_element_type=F32).astype(BF16)
    gr_ref[0] = jnp.dot(h, w_ref[:, o_gr:o_qd], preferred_element_type=F32).astype(BF16)
    z = jnp.dot(h, w_ref[:, o_qd:o_kd], preferred_element_type=F32)
    for i in range(D_Q_HEADS):
        y = _rms_rows(z[:, i * LANE:(i + 1) * LANE], HEAD_DIM) * dqn_ref[...]
        qd_ref[0, :, i * LANE:(i + 1) * LANE] = (_rope(y, ca, ua, da, half_a) * HEAD_DIM ** -0.5).astype(BF16)
    z = jnp.dot(h, w_ref[:, o_kd:o_vd], preferred_element_type=F32)
    for i in range(D_KV_HEADS):
        y = _rms_rows(z[:, i * LANE:(i + 1) * LANE], HEAD_DIM) * dkn_ref[...]
        kd_ref[0, :, i * LANE:(i + 1) * LANE] = _rope(y, ca, ua, da, half_a).astype(BF16)
    vd = jnp.dot(h, w_ref[:, o_vd:o_end], preferred_element_type=F32)
    vd_ref[0] = _with_ones_lane(vd, D_KV_HEADS).astype(BF16)


def _proj1(x_all, f_all, mods, rows, norm1, w_in, d_qn, d_kn, tabs_a, half_a):
    nb, total, d = x_all.shape
    tm = TOK_TILE
    cqk, cv = C_HEADS * C_DK, C_HEADS * C_DV
    wqr, wkr, wvr, wgr, wqd, wkd, wvd = jnp.split(
        w_in, np.cumsum([cqk, cqk, cv, cv, D_Q_HEADS * HEAD_DIM, D_KV_HEADS * HEAD_DIM]).tolist(), axis=1)
    parts = [_pad_cols(wqr, C_HEADS, C_DK), _pad_cols(wkr, C_HEADS, C_DK), wvr, wgr,
             _pad_cols(wqd, D_Q_HEADS, HEAD_DIM), _pad_cols(wkd, D_KV_HEADS, HEAD_DIM),
             _pad_cols(wvd, D_KV_HEADS, HEAD_DIM)]
    offs = tuple(int(o) for o in np.cumsum([0] + [p.shape[1] for p in parts]))
    w_all = jnp.concatenate(parts, axis=1).astype(BF16)
    consts = [w_all, _pad_gain(d_qn, HEAD_DIM), _pad_gain(d_kn, HEAD_DIM)]
    tab_spec = pl.BlockSpec((tm, LANE), lambda b, t: (t, 0))
    tile = lambda w: pl.BlockSpec((1, tm, w), lambda b, t: (b, t, 0))
    widths = [p.shape[1] for p in parts]
    return pl.pallas_call(
        functools.partial(_proj1_kernel, d_model=d, half_a=half_a, offs=offs),
        grid=(nb, total // tm),
        in_specs=[tile(d), tile(d), _mod_spec(0, 5, rows, nb, d), _full(norm1),
                  _mod_spec(1, 0, rows, nb, d), _mod_spec(1, 1, rows, nb, d)]
                 + [_full(c) for c in consts] + [tab_spec] * 3,
        out_specs=[tile(d)] + [tile(w) for w in widths],
        out_shape=[jax.ShapeDtypeStruct((nb, total, d), F32)]
                  + [jax.ShapeDtypeStruct((nb, total, w), BF16) for w in widths],
        compiler_params=_cparams("arbitrary", "arbitrary"),
        name="proj1",
    )(x_all, f_all, mods, norm1, mods, mods, *consts, *tabs_a)


def _retention_kernel(lg_ref, qf_ref, kf_ref, vf_ref, qb_ref, kb_ref, vb_ref, of_ref, ob_ref, st_ref, dec_ref, *,
                      n_heads):
    step = pl.program_id(1)
    c = CHUNK

    @pl.when(step == 0)
    def _():
        st_ref[...] = jnp.zeros_like(st_ref)
        ri = lax.broadcasted_iota(jnp.int32, (c, LANE), 0).astype(F32)
        diff = ri - lax.broadcasted_iota(jnp.int32, (c, LANE), 1).astype(F32)
        for d in range(2):
            for hd in range(n_heads):
                lg = lg_ref[d, hd]
                if d == 0:
                    dec_ref[d, hd, 0] = jnp.where(diff >= 0, jnp.exp(lg * jnp.maximum(diff, 0.0)), 0.0)
                    dec_ref[d, hd, 1] = jnp.exp(lg * (ri + 1.0))
                    dec_ref[d, hd, 2] = jnp.exp(lg * (c - 1.0 - ri))
                else:
                    dec_ref[d, hd, 0] = jnp.where(diff <= 0, jnp.exp(lg * jnp.maximum(-diff, 0.0)), 0.0)
                    dec_ref[d, hd, 1] = jnp.exp(lg * (c - ri))
                    dec_ref[d, hd, 2] = jnp.exp(lg * ri)

    for d, (q_ref, k_ref, v_ref, o_ref) in enumerate(((qf_ref, kf_ref, vf_ref, of_ref),
                                                      (qb_ref, kb_ref, vb_ref, ob_ref))):
        for hd in range(n_heads):
            sl = slice(hd * LANE, (hd + 1) * LANE)
            q, k, v = q_ref[0, :, sl], k_ref[0, :, sl], v_ref[0, :, sl]
            s = lax.dot_general(q, k, _NT, preferred_element_type=F32) * dec_ref[d, hd, 0]
            inner = jnp.dot(s.astype(BF16), v, preferred_element_type=F32)
            st = st_ref[d, hd]
            cross = jnp.dot(q, st.astype(BF16), preferred_element_type=F32) * dec_ref[d, hd, 1]
            o_ref[0, :, sl] = inner + cross
            kd_t = (k.astype(F32) * dec_ref[d, hd, 2]).T.astype(BF16)
            st_ref[d, hd] = st * jnp.exp(lg_ref[d, hd] * c) + jnp.dot(kd_t, v, preferred_element_type=F32)


def _retention(qr, kr, vr, lg, ctx_len):
    nb, total, w = qr.shape
    nh = w // LANE
    c = CHUNK
    nc, nctx = total // c, ctx_len // c
    fwd = pl.BlockSpec((1, c, w), lambda b, s: (b, s, 0))

    def bmap(b, s):
        return (b, jnp.where(s < nctx, nctx - 1 - s, nc - 1 - (s - nctx)), 0)

    bwd = pl.BlockSpec((1, c, w), bmap)
    out = jax.ShapeDtypeStruct((nb, total, w), F32)
    return pl.pallas_call(
        functools.partial(_retention_kernel, n_heads=nh),
        grid=(nb, nc),
        in_specs=[pl.BlockSpec(memory_space=pltpu.SMEM), fwd, fwd, fwd, bwd, bwd, bwd],
        out_specs=[fwd, bwd],
        out_shape=[out, out],
        scratch_shapes=[pltpu.VMEM((2, nh, LANE, LANE), F32), pltpu.VMEM((2, nh, 3, c, LANE), F32)],
        compiler_params=_cparams("arbitrary", "arbitrary"),
        name="retention",
    )(lg, qr, kr, vr, qr, kr, vr)


def _out1_kernel(of_ref, ob_ref, gr_ref, gn_ref, od_ref, wor_ref, wod_ref, x_ref, g1_ref, n2_ref, sh2_ref, sc2_ref,
                 xn_ref, h2_ref, *, d_model, n_heads):
    o = of_ref[0] + ob_ref[0]
    g = gr_ref[0].astype(F32)
    gate = g * jax.nn.sigmoid(g)
    gn = gn_ref[...]
    ys = []
    for hd in range(n_heads):
        sl = slice(hd * LANE, (hd + 1) * LANE)
        oh = o[:, sl]
        mu = jnp.mean(oh, axis=-1, keepdims=True)
        var = jnp.mean(jnp.square(oh - mu), axis=-1, keepdims=True)
        ys.append((gate[:, sl] * ((oh - mu) * lax.rsqrt(var + EPS) * gn[:, sl])).astype(BF16))
    y_ret = jnp.concatenate(ys, axis=1)
    y = (jnp.dot(y_ret, wor_ref[...], preferred_element_type=F32)
         + jnp.dot(od_ref[0], wod_ref[...], preferred_element_type=F32))
    xn = x_ref[0] + g1_ref[0] * y
    xn_ref[0] = xn
    h2_ref[0] = _modulate(xn, n2_ref, sc2_ref, sh2_ref, d_model).astype(BF16)


def _out1(o_f, o_b, g_r, gn_w, o_d, w_o, x_all, mods, rows, norm2, ctx_len):
    nb, total, d = x_all.shape
    tm = TOK_TILE
    skip = ctx_len // tm
    seq = total - ctx_len
    wor = w_o[:C_HEADS * C_DV].astype(BF16)
    wod = _pad_rows(w_o[C_HEADS * C_DV:], D_Q_HEADS, HEAD_DIM).astype(BF16)
    gn = gn_w.astype(F32).reshape(1, -1)
    tile_in = lambda w: pl.BlockSpec((1, tm, w), lambda b, t: (b, t + skip, 0))
    tile_out = pl.BlockSpec((1, tm, d), lambda b, t: (b, t, 0))
    mod = lambda chunk: pl.BlockSpec((1, 1, d), lambda b, t: ((rows + b) * 6 + chunk, 0, 0))
    return pl.pallas_call(
        functools.partial(_out1_kernel, d_model=d, n_heads=C_HEADS),
        grid=(nb, seq // tm),
        in_specs=[tile_in(o_f.shape[2]), tile_in(o_b.shape[2]), tile_in(g_r.shape[2]), _full(gn),
                  tile_in(o_d.shape[2]), _full(wor), _full(wod), tile_in(d), mod(2), _full(norm2), mod(3), mod(4)],
        out_specs=[tile_out, tile_out],
        out_shape=[jax.ShapeDtypeStruct((nb, seq, d), F32), jax.ShapeDtypeStruct((nb, seq, d), BF16)],
        compiler_params=_cparams("arbitrary", "arbitrary"),
        name="out_proj1",
    )(o_f, o_b, g_r, gn, o_d, wor, wod, x_all, mods, norm2, mods, mods)


def kernel(x, c, ctx, c_ctx, ada_w, ada_b, norm1_w, norm2_w, ab_w_in, ab_w_o, a_q_norm, a_k_norm, a_sink,
           b_q_lora_norm, b_kv_lora_norm, b_w_uq, b_w_ukv, b_q_norm, b_k_norm, cd_w_in, cd_w_o, c_decay_fwd,
           c_decay_bwd, c_gn_w, d_q_norm, d_k_norm, peer_w_q, peer_keys, peer_u, peer_v):
    nb, seq, d = x.shape
    ctx_len = ctx.shape[1]
    total = ctx_len + seq
    assert ctx_len == TOK_TILE and seq % TOK_TILE == 0 and seq % GRID_W == 0

    rows = -(-(nb + 1) // SUBLANE) * SUBLANE
    c_rows = jnp.concatenate([c, c_ctx[None, :], jnp.zeros((rows - nb - 1, d), c.dtype)], axis=0).astype(F32)
    mods = _ada(c_rows, ada_w, ada_b).reshape(-1, 1, d)

    tabs_a, half_a = _rope_tables(ctx_len, seq, 0, HEAD_DIM)
    tabs_b, half_b = _rope_tables(ctx_len, seq, B_NOPE, B_ROPE)
    n1 = norm1_w.astype(F32).reshape(-1, 1, d)
    n2 = norm2_w.astype(F32).reshape(-1, 1, d)

    x_all = jnp.concatenate([ctx, x], axis=1).astype(F32)
    qa, ka, va, qb, kb, vb = _proj0(x_all, mods, rows, n1[0], ab_w_in[0], b_w_uq[0], b_w_ukv[0], a_q_norm[0],
                                    a_k_norm[0], b_q_lora_norm[0], b_kv_lora_norm[0], b_q_norm[0], b_k_norm[0],
                                    tabs_a, tabs_b, half_a, half_b)
    o_a = _attn_window(qa, ka, va, a_sink[0], ctx_len=ctx_len)
    o_b = _attn_dense(qb, kb, vb, ctx_len=ctx_len)
    x_all, h2 = _out0(o_a, o_b, ab_w_o[0], x_all, mods, rows, n2[0])
    h2 = h2.reshape(nb * total, d)
    sel = _peer_select(h2, peer_w_q[0], peer_keys[0])
    f = _peer_apply(h2, sel, peer_u[0], peer_v[0]).reshape(nb, total, d)

    x_all, qr, kr, vr, gr, qd, kd, vd = _proj1(x_all, f, mods, rows, n1[1], cd_w_in[0], d_q_norm[0], d_k_norm[0],
                                               tabs_a, half_a)
    lg = jnp.stack([jax.nn.log_sigmoid(c_decay_fwd[0].astype(F32)), jax.nn.log_sigmoid(c_decay_bwd[0].astype(F32))])
    o_f, o_bw = _retention(qr, kr, vr, lg, ctx_len)
    o_d = _attn_dense(qd, kd, vd, ctx_len=ctx_len)
    x_lat, h2 = _out1(o_f, o_bw, gr, c_gn_w[0], o_d, cd_w_o[0], x_all, mods, rows, n2[1], ctx_len)
    h2 = h2.reshape(nb * seq, d)
    sel = _peer_select(h2, peer_w_q[1], peer_keys[1])
    out = _peer_apply(h2, sel, peer_u[1], peer_v[1], x=x_lat.reshape(nb * seq, d), mods=mods,
                      mod_index=lambda b: (rows + b) * 6 + 5, tokens_per_batch=seq)
    return out.reshape(nb, seq, d).astype(x.dtype)
```

```python
import functools

import numpy as np
import jax
import jax.numpy as jnp
from jax import lax
from jax.experimental import pallas as pl
from jax.experimental.pallas import tpu as pltpu

F32 = jnp.float32
BF16 = jnp.bfloat16

GRID_W = 64
ROPE_THETA = 10000.0
EPS = 1e-6
NEG_INF = -1e30
HEAD_DIM = 64
A_Q_HEADS, A_KV_HEADS, A_WINDOW = 8, 2, 128
B_HEADS, B_NOPE, B_ROPE, B_V, B_Q_RANK, B_KV_RANK = 8, 64, 32, 64, 256, 256
B_QK = B_NOPE + B_ROPE
C_HEADS, C_DK, C_DV = 4, 64, 128
D_Q_HEADS, D_KV_HEADS = 8, 2
PEER_HEADS, PEER_N_KEYS, PEER_D_KEY, PEER_TOPK = 8, 128, 256, 16

LANE = 128
SUBLANE = 8
BF16_ROWS = 16
ONES_LANE = LANE - 1
VMEM_LIMIT = 56 * 1024 * 1024

TOK_TILE = 256
CHUNK = 128
ATTN_KV_PER_STEP = 2
ATTN_CHAINS_PER_STEP = 4
PEER_SEL_TILE = 256
SELECT_HEADS = 2
PEER_TOK_TILE = 512
PEER_EXP_TILE = 2048


def _cparams(*sem):
    return pltpu.CompilerParams(dimension_semantics=sem, vmem_limit_bytes=VMEM_LIMIT)


def _full(arr):
    nd = arr.ndim
    return pl.BlockSpec(arr.shape, lambda *_: (0,) * nd)


def _pad_cols(w, n_heads, d):
    lead = w.shape[:-1]
    w = w.reshape(lead + (n_heads, d))
    w = jnp.pad(w, [(0, 0)] * len(lead) + [(0, 0), (0, LANE - d)])
    return w.reshape(lead + (n_heads * LANE,))


def _pad_rows(w, n_heads, d):
    n = w.shape[-1]
    w = w.reshape(n_heads, d, n)
    w = jnp.pad(w, [(0, 0), (0, LANE - d), (0, 0)])
    return w.reshape(n_heads * LANE, n)


def _pad_gain(g, d):
    return jnp.pad(g.astype(F32), (0, LANE - d)).reshape(1, LANE)


def _rope_tables(ctx_len, seq, lane_off, d_rot):
    blk = d_rot // 2
    half = blk // 2
    freqs = ROPE_THETA ** (-np.arange(half, dtype=np.float64) / half)
    pos = np.arange(seq)
    total = ctx_len + seq
    cos = np.ones((total, LANE), np.float64)
    sup = np.zeros((total, LANE), np.float64)
    sdn = np.zeros((total, LANE), np.float64)
    for axis, p in enumerate((pos // GRID_W, pos % GRID_W)):
        ang = p[:, None].astype(np.float64) * freqs[None, :]
        c, s = np.cos(ang), np.sin(ang)
        base = lane_off + axis * blk
        cos[ctx_len:, base:base + half] = c
        cos[ctx_len:, base + half:base + blk] = c
        sdn[ctx_len:, base:base + half] = -s
        sup[ctx_len:, base + half:base + blk] = s
    return (jnp.asarray(cos, F32), jnp.asarray(sup, F32), jnp.asarray(sdn, F32)), half


def _rms_rows(x, true_dim):
    return x * lax.rsqrt(jnp.sum(x * x, axis=-1, keepdims=True) * (1.0 / true_dim) + EPS)


def _rope(y, cos, sup, sdn, half):
    return y * cos + pltpu.roll(y, half, 1) * sup + pltpu.roll(y, LANE - half, 1) * sdn


def _ada_kernel(c_ref, w_ref, b_ref, o_ref):
    c = c_ref[...]
    s = c * jax.nn.sigmoid(c)
    o_ref[0] = jnp.dot(s.astype(BF16), w_ref[0].astype(BF16), preferred_element_type=F32) + b_ref[0]


def _ada(c_rows, ada_w, ada_b):
    depth, d, n = ada_w.shape
    rows = c_rows.shape[0]
    tn = 1536
    return pl.pallas_call(
        _ada_kernel,
        grid=(depth, n // tn),
        in_specs=[pl.BlockSpec((rows, d), lambda l, j: (0, 0)),
                  pl.BlockSpec((1, d, tn), lambda l, j: (l, 0, j)),
                  pl.BlockSpec((1, 1, tn), lambda l, j: (l, 0, j))],
        out_specs=pl.BlockSpec((1, rows, tn), lambda l, j: (l, 0, j)),
        out_shape=jax.ShapeDtypeStruct((depth, rows, n), F32),
        compiler_params=_cparams("arbitrary", "arbitrary"),
        name="ada_mod",
    )(c_rows, ada_w, ada_b.reshape(depth, 1, n))


def _mod_spec(layer, chunk, rows, nb, d, tile_axis=1):
    def imap(*ids):
        b, t = ids[0], ids[tile_axis]
        r = jnp.where(t == 0, nb, b)
        return ((layer * rows + r) * 6 + chunk, 0, 0)
    return pl.BlockSpec((1, 1, d), imap)


def _modulate(x, n_ref, sc_ref, sh_ref, d):
    return _rms_rows(x, d) * n_ref[...] * (1.0 + sc_ref[0]) + sh_ref[0]


def _proj0_kernel(x_ref, n1_ref, sh_ref, sc_ref, w_ref, wuq_ref, wuk_ref, wuv_ref,
                  aqn_ref, akn_ref, bqln_ref, bkvln_ref, bqn_ref, bkn_ref,
                  ca_ref, ua_ref, da_ref, cb_ref, ub_ref, db_ref,
                  qa_ref, ka_ref, va_ref, qb_ref, kb_ref, vb_ref, *, d_model, half_a, half_b, offs):
    h = _modulate(x_ref[0], n1_ref, sc_ref, sh_ref, d_model).astype(BF16)
    ca, ua, da = ca_ref[...], ua_ref[...], da_ref[...]
    cb, ub, db = cb_ref[...], ub_ref[...], db_ref[...]
    o_qa, o_ka, o_va, o_cq, o_ckv, o_kr, o_end = offs

    z = jnp.dot(h, w_ref[:, o_qa:o_ka], preferred_element_type=F32)
    for i in range(A_Q_HEADS):
        y = _rms_rows(z[:, i * LANE:(i + 1) * LANE], HEAD_DIM) * aqn_ref[...]
        qa_ref[0, :, i * LANE:(i + 1) * LANE] = (_rope(y, ca, ua, da, half_a) * HEAD_DIM ** -0.5).astype(BF16)
    z = jnp.dot(h, w_ref[:, o_ka:o_va], preferred_element_type=F32)
    for i in range(A_KV_HEADS):
        y = _rms_rows(z[:, i * LANE:(i + 1) * LANE], HEAD_DIM) * akn_ref[...]
        ka_ref[0, :, i * LANE:(i + 1) * LANE] = _rope(y, ca, ua, da, half_a).astype(BF16)
    va = jnp.dot(h, w_ref[:, o_va:o_cq], preferred_element_type=F32)
    va_ref[0] = _with_ones_lane(va, A_KV_HEADS).astype(BF16)

    cq = jnp.dot(h, w_ref[:, o_cq:o_ckv], preferred_element_type=F32)
    cq = (_rms_rows(cq, B_Q_RANK) * bqln_ref[...]).astype(BF16)
    z = jnp.dot(cq, wuq_ref[...], preferred_element_type=F32)
    for i in range(B_HEADS):
        y = _rms_rows(z[:, i * LANE:(i + 1) * LANE], B_QK) * bqn_ref[...]
        qb_ref[0, :, i * LANE:(i + 1) * LANE] = (_rope(y, cb, ub, db, half_b) * B_QK ** -0.5).astype(BF16)

    ckv = jnp.dot(h, w_ref[:, o_ckv:o_kr], preferred_element_type=F32)
    ckv = (_rms_rows(ckv, B_KV_RANK) * bkvln_ref[...]).astype(BF16)
    kr = jnp.dot(h, w_ref[:, o_kr:o_end], preferred_element_type=F32)
    z = jnp.dot(ckv, wuk_ref[...], preferred_element_type=F32)
    for i in range(B_HEADS):
        y = _rms_rows(z[:, i * LANE:(i + 1) * LANE] + kr, B_QK) * bkn_ref[...]
        kb_ref[0, :, i * LANE:(i + 1) * LANE] = _rope(y, cb, ub, db, half_b).astype(BF16)
    vb = jnp.dot(ckv, wuv_ref[...], preferred_element_type=F32)
    vb_ref[0] = _with_ones_lane(vb, B_HEADS).astype(BF16)


def _proj0(x_all, mods, rows, norm1, w_in, b_wuq, b_wukv, a_qn, a_kn, b_qln, b_kvln, b_qn, b_kn, tabs_a, tabs_b,
           half_a, half_b):
    nb, total, d = x_all.shape
    tm = TOK_TILE
    wq, wk, wv, wcq, wckv, wkr = jnp.split(
        w_in, np.cumsum([A_Q_HEADS * HEAD_DIM, A_KV_HEADS * HEAD_DIM, A_KV_HEADS * HEAD_DIM, B_Q_RANK, B_KV_RANK])
        .tolist(), axis=1)
    kr_pad = jnp.pad(wkr, ((0, 0), (B_NOPE, LANE - B_QK)))
    parts = [_pad_cols(wq, A_Q_HEADS, HEAD_DIM), _pad_cols(wk, A_KV_HEADS, HEAD_DIM),
             _pad_cols(wv, A_KV_HEADS, HEAD_DIM), wcq, wckv, kr_pad]
    offs = tuple(int(o) for o in np.cumsum([0] + [p.shape[1] for p in parts]))
    w_all = jnp.concatenate(parts, axis=1).astype(BF16)
    wuq = _pad_cols(b_wuq, B_HEADS, B_QK).astype(BF16)
    wukv = b_wukv.reshape(B_KV_RANK, B_HEADS, B_NOPE + B_V)
    wuk = _pad_cols(wukv[..., :B_NOPE].reshape(B_KV_RANK, -1), B_HEADS, B_NOPE).astype(BF16)
    wuv = _pad_cols(wukv[..., B_NOPE:].reshape(B_KV_RANK, -1), B_HEADS, B_V).astype(BF16)
    consts = [w_all, wuq, wuk, wuv, _pad_gain(a_qn, HEAD_DIM), _pad_gain(a_kn, HEAD_DIM),
              b_qln.astype(F32).reshape(1, -1), b_kvln.astype(F32).reshape(1, -1),
              _pad_gain(b_qn, B_QK), _pad_gain(b_kn, B_QK)]
    tab_spec = pl.BlockSpec((tm, LANE), lambda b, t: (t, 0))
    wide = lambda nh: pl.BlockSpec((1, tm, nh * LANE), lambda b, t: (b, t, 0))
    shp = lambda nh: jax.ShapeDtypeStruct((nb, total, nh * LANE), BF16)
    return pl.pallas_call(
        functools.partial(_proj0_kernel, d_model=d, half_a=half_a, half_b=half_b, offs=offs),
        grid=(nb, total // tm),
        in_specs=[pl.BlockSpec((1, tm, d), lambda b, t: (b, t, 0)), _full(norm1),
                  _mod_spec(0, 0, rows, nb, d), _mod_spec(0, 1, rows, nb, d)]
                 + [_full(c) for c in consts] + [tab_spec] * 6,
        out_specs=[wide(A_Q_HEADS), wide(A_KV_HEADS), wide(A_KV_HEADS), wide(B_HEADS), wide(B_HEADS), wide(B_HEADS)],
        out_shape=[shp(A_Q_HEADS), shp(A_KV_HEADS), shp(A_KV_HEADS), shp(B_HEADS), shp(B_HEADS), shp(B_HEADS)],
        compiler_params=_cparams("arbitrary", "arbitrary"),
        name="proj0",
    )(x_all, norm1, mods, mods, *consts, *tabs_a, *tabs_b)


_NT = (((1,), (1,)), ((), ()))


def _stack_heads(q_ref, grp, j=0):
    h0 = j * grp
    if grp == 1:
        return q_ref[0, :, h0 * LANE:(h0 + 1) * LANE]
    return jnp.concatenate([q_ref[0, :, (h0 + g) * LANE:(h0 + g + 1) * LANE] for g in range(grp)], axis=0)


def _softmax_av(scores, values, sink=None):
    m = None
    for s in scores:
        ms = jnp.max(s, axis=-1, keepdims=True)
        m = ms if m is None else jnp.maximum(m, ms)
    if sink is not None:
        m = jnp.maximum(m, sink)
    o = None
    for s, v in zip(scores, values):
        os_ = jnp.dot(jnp.exp(s - m).astype(BF16), v, preferred_element_type=F32)
        o = os_ if o is None else o + os_
    den = o[:, ONES_LANE:ONES_LANE + 1]
    if sink is not None:
        den = den + jnp.exp(sink - m)
    return o / den


def _with_ones_lane(v, n_heads):
    lane = lax.broadcasted_iota(jnp.int32, (1, n_heads * LANE), 1) % LANE
    return v + jnp.where(lane == ONES_LANE, 1.0, 0.0)


def _unstack_store(o, o_ref, grp, tq, j=0):
    for g in range(grp):
        h = j * grp + g
        o_ref[0, :, h * LANE:(h + 1) * LANE] = o[g * tq:(g + 1) * tq].astype(BF16)


def _attn_dense_kernel(q_ref, k_ref, v_ref, o_ref, *, grp, kvs, ctx_len, tq):
    qt = pl.program_id(2)

    @pl.when(qt * tq < ctx_len)
    def _():
        for j in range(kvs):
            kj = slice(j * LANE, (j + 1) * LANE)
            s = lax.dot_general(_stack_heads(q_ref, grp, j), k_ref[0, 0:ctx_len, kj], _NT,
                                preferred_element_type=F32)
            _unstack_store(_softmax_av([s], [v_ref[0, 0:ctx_len, kj]]), o_ref, grp, tq, j)

    @pl.when(qt * tq >= ctx_len)
    def _():
        for j in range(kvs):
            kj = slice(j * LANE, (j + 1) * LANE)
            s = lax.dot_general(_stack_heads(q_ref, grp, j), k_ref[0, :, kj], _NT, preferred_element_type=F32)
            _unstack_store(_softmax_av([s], [v_ref[0, :, kj]]), o_ref, grp, tq, j)


def _attn_dense(q, k, v, *, ctx_len):
    nb, total, qw = q.shape
    hq, hkv = qw // LANE, k.shape[2] // LANE
    grp = hq // hkv
    kvs = min(ATTN_CHAINS_PER_STEP if grp == 1 else ATTN_KV_PER_STEP, hkv)
    tq = CHUNK if grp > 1 else TOK_TILE
    assert ctx_len % tq == 0 and hkv % kvs == 0
    return pl.pallas_call(
        functools.partial(_attn_dense_kernel, grp=grp, kvs=kvs, ctx_len=ctx_len, tq=tq),
        grid=(nb, hkv // kvs, total // tq),
        in_specs=[pl.BlockSpec((1, tq, kvs * grp * LANE), lambda b, h, t: (b, t, h)),
                  pl.BlockSpec((1, total, kvs * LANE), lambda b, h, t: (b, 0, h)),
                  pl.BlockSpec((1, total, kvs * LANE), lambda b, h, t: (b, 0, h))],
        out_specs=pl.BlockSpec((1, tq, kvs * grp * LANE), lambda b, h, t: (b, t, h)),
        out_shape=jax.ShapeDtypeStruct(q.shape, BF16),
        compiler_params=_cparams("arbitrary", "arbitrary", "arbitrary"),
        name="attn_dense",
    )(q, k, v)


def _attn_window_kernel(q_ref, k_ref, v_ref, sink_ref, o_ref, *, grp, kvs, ctx_len, tq, total, window):
    qt = pl.program_id(2)

    def sink_col(j):
        return jnp.concatenate([jnp.broadcast_to(sink_ref[j * grp + g][:, 0:1], (tq, 1)) for g in range(grp)], axis=0)

    @pl.when(qt * tq < ctx_len)
    def _():
        for j in range(kvs):
            kj = slice(j * LANE, (j + 1) * LANE)
            s_c = lax.dot_general(_stack_heads(q_ref, grp, j), k_ref[0, 0:ctx_len, kj], _NT,
                                  preferred_element_type=F32)
            _unstack_store(_softmax_av([s_c], [v_ref[0, 0:ctx_len, kj]], sink_col(j)), o_ref, grp, tq, j)

    @pl.when(qt * tq >= ctx_len)
    def _():
        slab = 3 * tq
        start = pl.multiple_of(jnp.clip((qt - 1) * tq, ctx_len, total - slab), tq)
        qpos = qt * tq + lax.broadcasted_iota(jnp.int32, (tq, slab), 0)
        kpos = start + lax.broadcasted_iota(jnp.int32, (tq, slab), 1)
        bias = jnp.where(jnp.abs(qpos - kpos) <= window, 0.0, NEG_INF)
        bias = jnp.concatenate([bias] * grp, axis=0)
        for j in range(kvs):
            kj = slice(j * LANE, (j + 1) * LANE)
            q = _stack_heads(q_ref, grp, j)
            s_c = lax.dot_general(q, k_ref[0, 0:ctx_len, kj], _NT, preferred_element_type=F32)
            s_l = lax.dot_general(q, k_ref[0, pl.ds(start, slab), kj], _NT, preferred_element_type=F32) + bias
            o = _softmax_av([s_c, s_l], [v_ref[0, 0:ctx_len, kj], v_ref[0, pl.ds(start, slab), kj]], sink_col(j))
            _unstack_store(o, o_ref, grp, tq, j)


def _attn_window(q, k, v, sink, *, ctx_len):
    nb, total, qw = q.shape
    hq, hkv = qw // LANE, k.shape[2] // LANE
    grp = hq // hkv
    kvs = ATTN_KV_PER_STEP
    tq = CHUNK
    assert hkv % kvs == 0
    sink_rows = jnp.broadcast_to(sink.astype(F32).reshape(hq, 1, 1), (hq, 1, LANE))
    return pl.pallas_call(
        functools.partial(_attn_window_kernel, grp=grp, kvs=kvs, ctx_len=ctx_len, tq=tq, total=total,
                          window=A_WINDOW),
        grid=(nb, hkv // kvs, total // tq),
        in_specs=[pl.BlockSpec((1, tq, kvs * grp * LANE), lambda b, h, t: (b, t, h)),
                  pl.BlockSpec((1, total, kvs * LANE), lambda b, h, t: (b, 0, h)),
                  pl.BlockSpec((1, total, kvs * LANE), lambda b, h, t: (b, 0, h)),
                  pl.BlockSpec((kvs * grp, 1, LANE), lambda b, h, t: (h, 0, 0))],
        out_specs=pl.BlockSpec((1, tq, kvs * grp * LANE), lambda b, h, t: (b, t, h)),
        out_shape=jax.ShapeDtypeStruct(q.shape, BF16),
        compiler_params=_cparams("arbitrary", "arbitrary", "arbitrary"),
        name="attn_window",
    )(q, k, v, sink_rows)


def _out0_kernel(oa_ref, ob_ref, woa_ref, wob_ref, x_ref, g1_ref, n2_ref, sh2_ref, sc2_ref, xn_ref, h2_ref, *, d_model):
    y = (jnp.dot(oa_ref[0], woa_ref[...], preferred_element_type=F32)
         + jnp.dot(ob_ref[0], wob_ref[...], preferred_element_type=F32))
    xn = x_ref[0] + g1_ref[0] * y
    xn_ref[0] = xn
    h2_ref[0] = _modulate(xn, n2_ref, sc2_ref, sh2_ref, d_model).astype(BF16)


def _out0(oa, ob, w_o, x_all, mods, rows, norm2):
    nb, total, d = x_all.shape
    tm = TOK_TILE
    woa = _pad_rows(w_o[:A_Q_HEADS * HEAD_DIM], A_Q_HEADS, HEAD_DIM).astype(BF16)
    wob = _pad_rows(w_o[A_Q_HEADS * HEAD_DIM:], B_HEADS, B_V).astype(BF16)
    tile = lambda w: pl.BlockSpec((1, tm, w), lambda b, t: (b, t, 0))
    return pl.pallas_call(
        functools.partial(_out0_kernel, d_model=d),
        grid=(nb, total // tm),
        in_specs=[tile(oa.shape[2]), tile(ob.shape[2]), _full(woa), _full(wob), tile(d),
                  _mod_spec(0, 2, rows, nb, d), _full(norm2), _mod_spec(0, 3, rows, nb, d),
                  _mod_spec(0, 4, rows, nb, d)],
        out_specs=[tile(d), tile(d)],
        out_shape=[jax.ShapeDtypeStruct((nb, total, d), F32), jax.ShapeDtypeStruct((nb, total, d), BF16)],
        compiler_params=_cparams("arbitrary", "arbitrary"),
        name="out_proj0",
    )(oa, ob, woa, wob, x_all, mods, norm2, mods, mods)


def _top_rows(sc, rowf, k):
    n = sc.shape[0]
    vals, idxs = [], []
    work = sc
    for _ in range(k):
        m = jnp.max(work, axis=0, keepdims=True)
        idx = jnp.min(jnp.where(work == m, rowf, float(n)), axis=0, keepdims=True)
        vals.append(m)
        idxs.append(idx)
        work = jnp.where(rowf == idx, -jnp.inf, work)
    return vals, idxs


def _stack_rows(rows, row16):
    out = jnp.zeros(row16.shape, F32)
    for k, r in enumerate(rows):
        out = jnp.where(row16 == float(k), r, out)
    return out


def _candidates(v0, s1, slab_rows):
    return jnp.concatenate([v0[k1] + s1[0:slab_rows[k1], :] for k1 in range(len(v0))], axis=0)


def _select_exact(sc0, sc1, rowf, row16, flat, slab_rows, topk):
    nk, ts = sc0.shape
    v0, i0 = _top_rows(sc0, rowf, topk)
    v1, i1 = _top_rows(sc1, rowf, topk)
    work = _candidates(v0, _stack_rows(v1, row16), slab_rows)
    cnt = jnp.zeros((topk, ts), F32)
    zsum = jnp.zeros((1, ts), F32)
    best0 = None
    for k in range(topk):
        m = jnp.max(work, axis=0, keepdims=True)
        idx = jnp.min(jnp.where(work == m, flat, 1e9), axis=0, keepdims=True)
        work = jnp.where(flat == idx, -jnp.inf, work)
        best0 = m if best0 is None else best0
        zsum = zsum + jnp.exp(m - best0)
        cnt = cnt + jnp.where(row16 == jnp.floor(idx * (1.0 / topk)), 1.0, 0.0)
    cc = jnp.zeros((nk, ts), F32)
    rb = jnp.full((nk, ts), 99.0, F32)
    for k in range(topk):
        ck = jnp.sum(jnp.where(row16 == float(k), cnt, 0.0), axis=0, keepdims=True)
        cc = jnp.where(rowf == i0[k], ck, cc)
        rb = jnp.where(rowf == i1[k], float(k), rb)
    return cc, rb, zsum


def _select_fast(sc0, sc1, row16, slab_rows, topk):
    nk, ts = sc0.shape
    ninf = -jnp.inf
    count = lambda hit: jnp.sum(jnp.where(hit, 1.0, 0.0), axis=0, keepdims=True)
    work, v0 = sc0, []
    for _ in range(topk):
        m = jnp.max(work, axis=0, keepdims=True)
        v0.append(m)
        work = jnp.where(work == m, ninf, work)
    bad = count(work == ninf) != float(topk)
    work, v1 = sc1, []
    rb = jnp.full((nk, ts), 99.0, F32)
    for k in range(topk):
        m = jnp.max(work, axis=0, keepdims=True)
        v1.append(m)
        hit = work == m
        work = jnp.where(hit, ninf, work)
        rb = jnp.where(hit, float(k), rb)
    bad = jnp.logical_or(bad, count(rb < 99.0) != float(topk))
    work = _candidates(v0, _stack_rows(v1, row16), slab_rows)
    zsum = jnp.zeros((1, ts), F32)
    best0 = None
    for _ in range(topk):
        m = jnp.max(work, axis=0, keepdims=True)
        work = jnp.where(work == m, ninf, work)
        best0 = m if best0 is None else best0
        zsum = zsum + jnp.exp(m - best0)
    chosen = jnp.where(work == ninf, 1.0, 0.0)
    cc = jnp.zeros((nk, ts), F32)
    total = jnp.zeros((1, ts), F32)
    off = 0
    for k1 in range(topk):
        ck = jnp.sum(chosen[off:off + slab_rows[k1], :], axis=0, keepdims=True)
        off += slab_rows[k1]
        total = total + ck
        cc = jnp.where(sc0 == v0[k1], ck, cc)
    bad = jnp.logical_or(bad, total != float(topk))
    return cc, rb, zsum, jnp.max(jnp.where(bad, 1.0, 0.0))


def _peer_select_kernel(h_ref, wq_ref, keys_ref, cc_ref, e0_ref, rb_ref, e1_ref, q_sc, *, n_heads, topk):
    nk = PEER_N_KEYS
    ts = h_ref.shape[0]
    q_sc[...] = lax.dot_general(wq_ref[...], h_ref[...], (((1,), (1,)), ((), ())), preferred_element_type=F32)
    rowf = lax.broadcasted_iota(jnp.int32, (nk, ts), 0).astype(F32)
    row16 = lax.broadcasted_iota(jnp.int32, (topk, ts), 0).astype(F32)
    slab_rows = [topk] + [SUBLANE] * (topk - 1)
    n_cand = sum(slab_rows)
    ci = lax.broadcasted_iota(jnp.int32, (n_cand, ts), 0)
    rest = ci - topk
    flat = jnp.where(ci < topk, ci, (1 + (rest >> 3)) * topk + (rest & 7)).astype(F32)

    def scores(hp):
        qhp = q_sc[pl.ds(pl.multiple_of(hp * nk, nk), nk), :].astype(BF16)
        return jnp.dot(keys_ref[hp], qhp, preferred_element_type=F32)

    def heads_body(it, carry):
        heads = [it * SELECT_HEADS + u for u in range(SELECT_HEADS)]
        scs = [(scores(hd * 2), scores(hd * 2 + 1)) for hd in heads]
        fast = [_select_fast(sc0, sc1, row16, slab_rows, topk) for sc0, sc1 in scs]
        for hd, (sc0, sc1), (cc, rb, zsum, tie) in zip(heads, scs, fast):
            cc, rb, zsum = lax.cond(tie > 0.0,
                                    lambda: _select_exact(sc0, sc1, rowf, row16, flat, slab_rows, topk),
                                    lambda: (cc, rb, zsum))
            cc_ref[0, hd] = cc
            rb_ref[0, hd] = rb.astype(BF16)
            e0_ref[0, hd] = jnp.exp(sc0 - jnp.max(sc0, axis=0, keepdims=True))
            e1_ref[0, hd] = (jnp.exp(sc1 - jnp.max(sc1, axis=0, keepdims=True)) / zsum).astype(BF16)
        return carry

    lax.fori_loop(0, n_heads // SELECT_HEADS, heads_body, 0)


def _peer_select(h2, w_q, keys):
    t, d = h2.shape
    ts = PEER_SEL_TILE
    nh, nk = PEER_HEADS, PEER_N_KEYS
    wq_t = w_q.T.astype(BF16)
    keys2 = keys.reshape(nh * 2, nk, PEER_D_KEY // 2).astype(BF16)
    row_out = jax.ShapeDtypeStruct((t // ts, nh, nk, ts), F32)
    col_out = jax.ShapeDtypeStruct((t // ts, nh, nk, ts), BF16)
    ospec = pl.BlockSpec((1, nh, nk, ts), lambda i: (i, 0, 0, 0))
    return pl.pallas_call(
        functools.partial(_peer_select_kernel, n_heads=nh, topk=PEER_TOPK),
        grid=(t // ts,),
        in_specs=[pl.BlockSpec((ts, d), lambda i: (i, 0)), _full(wq_t), _full(keys2)],
        out_specs=[ospec] * 4,
        out_shape=[row_out, row_out, col_out, col_out],
        scratch_shapes=[pltpu.VMEM((wq_t.shape[0], ts), F32)],
        compiler_params=_cparams("arbitrary"),
        name="peer_select",
    )(h2, wq_t, keys2)


def _peer_apply_kernel(*refs, n_heads, final):
    if final:
        (h_ref, u_ref, vt_ref, cc_ref, e0_ref, rb_ref, e1_ref, x_ref, g_ref, o_ref,
         acc_ref, g_sc, p_sc, ht_sc) = refs
    else:
        h_ref, u_ref, vt_ref, cc_ref, e0_ref, rb_ref, e1_ref, o_ref, acc_ref, g_sc, p_sc, ht_sc = refs
    et, hh = pl.program_id(1), pl.program_id(2)
    nk = PEER_N_KEYS
    rows_per_tile = cc_ref.shape[2]
    tw = cc_ref.shape[3]
    tok = pl.ds(pl.multiple_of(hh * tw, tw), tw)

    @pl.when(et == 0)
    def _():
        ht_sc[hh] = h_ref[tok, :].astype(F32).T.astype(BF16)

    def stage1():
        return jnp.dot(u_ref[...], ht_sc[hh], preferred_element_type=F32)

    def stage1_store(at):
        g_sc[hh * 2 + et % 2] = jax.nn.gelu(at.astype(BF16))

    def stage2():
        zero = jnp.zeros((), BF16)
        prev = hh * 2 + (et + 1) % 2
        for ii in range(rows_per_tile):
            ccr = [jnp.broadcast_to(cc_ref[hh, hd, ii:ii + 1, :], (BF16_ROWS, tw)).astype(BF16)
                   for hd in range(n_heads)]
            e0r = [jnp.broadcast_to(e0_ref[hh, hd, ii:ii + 1, :], (BF16_ROWS, tw)).astype(BF16)
                   for hd in range(n_heads)]
            for s0 in range(0, nk, BF16_ROWS):
                rws = slice(s0, s0 + BF16_ROWS)
                w = None
                for hd in range(n_heads):
                    term = jnp.where(rb_ref[hh, hd, rws, :] < ccr[hd], e1_ref[hh, hd, rws, :], zero) * e0r[hd]
                    w = term if w is None else w + term
                r0 = ii * nk + s0
                p_sc[r0:r0 + BF16_ROWS, :] = w * g_sc[prev, r0:r0 + BF16_ROWS, :]
        acc_ref[hh] += jnp.dot(vt_ref[0], p_sc[...], preferred_element_type=F32)

    last = pl.num_programs(1) - 1

    @pl.when(et == 0)
    def _():
        acc_ref[hh] = jnp.zeros(acc_ref.shape[1:], F32)
        stage1_store(stage1())

    @pl.when(jnp.logical_and(et > 0, et < last))
    def _():
        at = stage1()
        stage2()
        stage1_store(at)

    @pl.when(et == last)
    def _():
        stage2()

    @pl.when(et == last)
    def _():
        f = acc_ref[hh].T
        if final:
            o_ref[tok, :] = x_ref[tok, :] + g_ref[0] * f
        else:
            o_ref[tok, :] = f


def _peer_apply(h2, sel, u_tab, v_tab, x=None, mods=None, mod_index=None, tokens_per_batch=None):
    t, d = h2.shape
    cc, e0, rb, e1 = sel
    nh, nk = PEER_HEADS, PEER_N_KEYS
    tt, te = PEER_TOK_TILE, PEER_EXP_TILE
    ti = te // nk
    n_et = u_tab.shape[0] // te
    u = u_tab.astype(BF16)
    vt = v_tab.reshape(n_et, te, d).transpose(0, 2, 1).astype(BF16)
    final = x is not None
    tw = cc.shape[3]
    halves = tt // tw
    assert tt % tw == 0 and tw == PEER_SEL_TILE
    cur = lambda e: jnp.minimum(e, n_et - 1)
    prv = lambda e: jnp.maximum(e - 1, 0)
    row_spec = pl.BlockSpec((halves, nh, ti, tw), lambda i, e, hh: (i, 0, prv(e), 0))
    col_spec = pl.BlockSpec((halves, nh, nk, tw), lambda i, e, hh: (i, 0, 0, 0))
    in_specs = [pl.BlockSpec((tt, d), lambda i, e, hh: (i, 0)),
                pl.BlockSpec((te, d), lambda i, e, hh: (cur(e), 0)),
                pl.BlockSpec((1, d, te), lambda i, e, hh: (prv(e), 0, 0)),
                row_spec, row_spec, col_spec, col_spec]
    args = [h2, u, vt, cc, e0, rb, e1]
    if final:
        per = tokens_per_batch // tt
        in_specs += [pl.BlockSpec((tt, d), lambda i, e, hh: (i, 0)),
                     pl.BlockSpec((1, 1, d), lambda i, e, hh: (mod_index(i // per), 0, 0))]
        args += [x, mods]
    return pl.pallas_call(
        functools.partial(_peer_apply_kernel, n_heads=nh, final=final),
        grid=(t // tt, n_et + 1, halves),
        in_specs=in_specs,
        out_specs=pl.BlockSpec((tt, d), lambda i, e, hh: (i, 0)),
        out_shape=jax.ShapeDtypeStruct((t, d), F32),
        scratch_shapes=[pltpu.VMEM((halves, d, tw), F32), pltpu.VMEM((halves * 2, te, tw), BF16),
                        pltpu.VMEM((te, tw), BF16), pltpu.VMEM((halves, d, tw), BF16)],
        compiler_params=_cparams("arbitrary", "arbitrary", "arbitrary"),
        name="peer_apply_final" if final else "peer_apply",
    )(*args)


def _proj1_kernel(x_ref, f_ref, g2_ref, n1_ref, sh_ref, sc_ref, w_ref, dqn_ref, dkn_ref, ca_ref, ua_ref, da_ref,
                  xn_ref, qr_ref, kr_ref, vr_ref, gr_ref, qd_ref, kd_ref, vd_ref, *, d_model, half_a, offs):
    xn = x_ref[0] + g2_ref[0] * f_ref[0]
    xn_ref[0] = xn
    h = _modulate(xn, n1_ref, sc_ref, sh_ref, d_model).astype(BF16)
    ca, ua, da = ca_ref[...], ua_ref[...], da_ref[...]
    o_qr, o_kr, o_vr, o_gr, o_qd, o_kd, o_vd, o_end = offs

    z = jnp.dot(h, w_ref[:, o_qr:o_kr], preferred_element_type=F32)
    for i in range(C_HEADS):
        qr_ref[0, :, i * LANE:(i + 1) * LANE] = _rope(z[:, i * LANE:(i + 1) * LANE], ca, ua, da, half_a).astype(BF16)
    z = jnp.dot(h, w_ref[:, o_kr:o_vr], preferred_element_type=F32) * (C_DK ** -0.5)
    for i in range(C_HEADS):
        kr_ref[0, :, i * LANE:(i + 1) * LANE] = _rope(z[:, i * LANE:(i + 1) * LANE], ca, ua, da, half_a).astype(BF16)
    vr_ref[0] = jnp.dot(h, w_ref[:, o_vr:o_gr], preferred_element_type=F32).astype(BF16)
    gr_ref[0] = jnp.dot(h, w_ref[:, o_gr:o_qd], preferred_element_type=F32).astype(BF16)
    z = jnp.dot(h, w_ref[:, o_qd:o_kd], preferred_element_type=F32)
    for i in range(D_Q_HEADS):
        y = _rms_rows(z[:, i * LANE:(i + 1) * LANE], HEAD_DIM) * dqn_ref[...]
        qd_ref[0, :, i * LANE:(i + 1) * LANE] = (_rope(y, ca, ua, da, half_a) * HEAD_DIM ** -0.5).astype(BF16)
    z = jnp.dot(h, w_ref[:, o_kd:o_vd], preferred_element_type=F32)
    for i in range(D_KV_HEADS):
        y = _rms_rows(z[:, i * LANE:(i + 1) * LANE], HEAD_DIM) * dkn_ref[...]
        kd_ref[0, :, i * LANE:(i + 1) * LANE] = _rope(y, ca, ua, da, half_a).astype(BF16)
    vd = jnp.dot(h, w_ref[:, o_vd:o_end], preferred_element_type=F32)
    vd_ref[0] = _with_ones_lane(vd, D_KV_HEADS).astype(BF16)


def _proj1(x_all, f_all, mods, rows, norm1, w_in, d_qn, d_kn, tabs_a, half_a):
    nb, total, d = x_all.shape
    tm = TOK_TILE
    cqk, cv = C_HEADS * C_DK, C_HEADS * C_DV
    wqr, wkr, wvr, wgr, wqd, wkd, wvd = jnp.split(
        w_in, np.cumsum([cqk, cqk, cv, cv, D_Q_HEADS * HEAD_DIM, D_KV_HEADS * HEAD_DIM]).tolist(), axis=1)
    parts = [_pad_cols(wqr, C_HEADS, C_DK), _pad_cols(wkr, C_HEADS, C_DK), wvr, wgr,
             _pad_cols(wqd, D_Q_HEADS, HEAD_DIM), _pad_cols(wkd, D_KV_HEADS, HEAD_DIM),
             _pad_cols(wvd, D_KV_HEADS, HEAD_DIM)]
    offs = tuple(int(o) for o in np.cumsum([0] + [p.shape[1] for p in parts]))
    w_all = jnp.concatenate(parts, axis=1).astype(BF16)
    consts = [w_all, _pad_gain(d_qn, HEAD_DIM), _pad_gain(d_kn, HEAD_DIM)]
    tab_spec = pl.BlockSpec((tm, LANE), lambda b, t: (t, 0))
    tile = lambda w: pl.BlockSpec((1, tm, w), lambda b, t: (b, t, 0))
    widths = [p.shape[1] for p in parts]
    return pl.pallas_call(
        functools.partial(_proj1_kernel, d_model=d, half_a=half_a, offs=offs),
        grid=(nb, total // tm),
        in_specs=[tile(d), tile(d), _mod_spec(0, 5, rows, nb, d), _full(norm1),
                  _mod_spec(1, 0, rows, nb, d), _mod_spec(1, 1, rows, nb, d)]
                 + [_full(c) for c in consts] + [tab_spec] * 3,
        out_specs=[tile(d)] + [tile(w) for w in widths],
        out_shape=[jax.ShapeDtypeStruct((nb, total, d), F32)]
                  + [jax.ShapeDtypeStruct((nb, total, w), BF16) for w in widths],
        compiler_params=_cparams("arbitrary", "arbitrary"),
        name="proj1",
    )(x_all, f_all, mods, norm1, mods, mods, *consts, *tabs_a)


def _retention_kernel(lg_ref, qf_ref, kf_ref, vf_ref, qb_ref, kb_ref, vb_ref, of_ref, ob_ref, st_ref, dec_ref, *,
                      n_heads):
    step = pl.program_id(1)
    c = CHUNK

    @pl.when(step == 0)
    def _():
        st_ref[...] = jnp.zeros_like(st_ref)
        ri = lax.broadcasted_iota(jnp.int32, (c, LANE), 0).astype(F32)
        diff = ri - lax.broadcasted_iota(jnp.int32, (c, LANE), 1).astype(F32)
        for d in range(2):
            for hd in range(n_heads):
                lg = lg_ref[d, hd]
                if d == 0:
                    dec_ref[d, hd, 0] = jnp.where(diff >= 0, jnp.exp(lg * jnp.maximum(diff, 0.0)), 0.0)
                    dec_ref[d, hd, 1] = jnp.exp(lg * (ri + 1.0))
                    dec_ref[d, hd, 2] = jnp.exp(lg * (c - 1.0 - ri))
                else:
                    dec_ref[d, hd, 0] = jnp.where(diff <= 0, jnp.exp(lg * jnp.maximum(-diff, 0.0)), 0.0)
                    dec_ref[d, hd, 1] = jnp.exp(lg * (c - ri))
                    dec_ref[d, hd, 2] = jnp.exp(lg * ri)

    for d, (q_ref, k_ref, v_ref, o_ref) in enumerate(((qf_ref, kf_ref, vf_ref, of_ref),
                                                      (qb_ref, kb_ref, vb_ref, ob_ref))):
        for hd in range(n_heads):
            sl = slice(hd * LANE, (hd + 1) * LANE)
            q, k, v = q_ref[0, :, sl], k_ref[0, :, sl], v_ref[0, :, sl]
            s = lax.dot_general(q, k, _NT, preferred_element_type=F32) * dec_ref[d, hd, 0]
            inner = jnp.dot(s.astype(BF16), v, preferred_element_type=F32)
            st = st_ref[d, hd]
            cross = jnp.dot(q, st.astype(BF16), preferred_element_type=F32) * dec_ref[d, hd, 1]
            o_ref[0, :, sl] = inner + cross
            kd_t = (k.astype(F32) * dec_ref[d, hd, 2]).T.astype(BF16)
            st_ref[d, hd] = st * jnp.exp(lg_ref[d, hd] * c) + jnp.dot(kd_t, v, preferred_element_type=F32)


def _retention(qr, kr, vr, lg, ctx_len):
    nb, total, w = qr.shape
    nh = w // LANE
    c = CHUNK
    nc, nctx = total // c, ctx_len // c
    fwd = pl.BlockSpec((1, c, w), lambda b, s: (b, s, 0))

    def bmap(b, s):
        return (b, jnp.where(s < nctx, nctx - 1 - s, nc - 1 - (s - nctx)), 0)

    bwd = pl.BlockSpec((1, c, w), bmap)
    out = jax.ShapeDtypeStruct((nb, total, w), F32)
    return pl.pallas_call(
        functools.partial(_retention_kernel, n_heads=nh),
        grid=(nb, nc),
        in_specs=[pl.BlockSpec(memory_space=pltpu.SMEM), fwd, fwd, fwd, bwd, bwd, bwd],
        out_specs=[fwd, bwd],
        out_shape=[out, out],
        scratch_shapes=[pltpu.VMEM((2, nh, LANE, LANE), F32), pltpu.VMEM((2, nh, 3, c, LANE), F32)],
        compiler_params=_cparams("arbitrary", "arbitrary"),
        name="retention",
    )(lg, qr, kr, vr, qr, kr, vr)


def _out1_kernel(of_ref, ob_ref, gr_ref, gn_ref, od_ref, wor_ref, wod_ref, x_ref, g1_ref, n2_ref, sh2_ref, sc2_ref,
                 xn_ref, h2_ref, *, d_model, n_heads):
    o = of_ref[0] + ob_ref[0]
    g = gr_ref[0].astype(F32)
    gate = g * jax.nn.sigmoid(g)
    gn = gn_ref[...]
    ys = []
    for hd in range(n_heads):
        sl = slice(hd * LANE, (hd + 1) * LANE)
        oh = o[:, sl]
        mu = jnp.mean(oh, axis=-1, keepdims=True)
        var = jnp.mean(jnp.square(oh - mu), axis=-1, keepdims=True)
        ys.append((gate[:, sl] * ((oh - mu) * lax.rsqrt(var + EPS) * gn[:, sl])).astype(BF16))
    y_ret = jnp.concatenate(ys, axis=1)
    y = (jnp.dot(y_ret, wor_ref[...], preferred_element_type=F32)
         + jnp.dot(od_ref[0], wod_ref[...], preferred_element_type=F32))
    xn = x_ref[0] + g1_ref[0] * y
    xn_ref[0] = xn
    h2_ref[0] = _modulate(xn, n2_ref, sc2_ref, sh2_ref, d_model).astype(BF16)


def _out1(o_f, o_b, g_r, gn_w, o_d, w_o, x_all, mods, rows, norm2, ctx_len):
    nb, total, d = x_all.shape
    tm = TOK_TILE
    skip = ctx_len // tm
    seq = total - ctx_len
    wor = w_o[:C_HEADS * C_DV].astype(BF16)
    wod = _pad_rows(w_o[C_HEADS * C_DV:], D_Q_HEADS, HEAD_DIM).astype(BF16)
    gn = gn_w.astype(F32).reshape(1, -1)
    tile_in = lambda w: pl.BlockSpec((1, tm, w), lambda b, t: (b, t + skip, 0))
    tile_out = pl.BlockSpec((1, tm, d), lambda b, t: (b, t, 0))
    mod = lambda chunk: pl.BlockSpec((1, 1, d), lambda b, t: ((rows + b) * 6 + chunk, 0, 0))
    return pl.pallas_call(
        functools.partial(_out1_kernel, d_model=d, n_heads=C_HEADS),
        grid=(nb, seq // tm),
        in_specs=[tile_in(o_f.shape[2]), tile_in(o_b.shape[2]), tile_in(g_r.shape[2]), _full(gn),
                  tile_in(o_d.shape[2]), _full(wor), _full(wod), tile_in(d), mod(2), _full(norm2), mod(3), mod(4)],
        out_specs=[tile_out, tile_out],
        out_shape=[jax.ShapeDtypeStruct((nb, seq, d), F32), jax.ShapeDtypeStruct((nb, seq, d), BF16)],
        compiler_params=_cparams("arbitrary", "arbitrary"),
        name="out_proj1",
    )(o_f, o_b, g_r, gn, o_d, wor, wod, x_all, mods, norm2, mods, mods)


def kernel(x, c, ctx, c_ctx, ada_w, ada_b, norm1_w, norm2_w, ab_w_in, ab_w_o, a_q_norm, a_k_norm, a_sink,
           b_q_lora_norm, b_kv_lora_norm, b_w_uq, b_w_ukv, b_q_norm, b_k_norm, cd_w_in, cd_w_o, c_decay_fwd,
           c_decay_bwd, c_gn_w, d_q_norm, d_k_norm, peer_w_q, peer_keys, peer_u, peer_v):
    nb, seq, d = x.shape
    ctx_len = ctx.shape[1]
    total = ctx_len + seq
    assert ctx_len == TOK_TILE and seq % TOK_TILE == 0 and seq % GRID_W == 0

    rows = -(-(nb + 1) // SUBLANE) * SUBLANE
    c_rows = jnp.concatenate([c, c_ctx[None, :], jnp.zeros((rows - nb - 1, d), c.dtype)], axis=0).astype(F32)
    mods = _ada(c_rows, ada_w, ada_b).reshape(-1, 1, d)

    tabs_a, half_a = _rope_tables(ctx_len, seq, 0, HEAD_DIM)
    tabs_b, half_b = _rope_tables(ctx_len, seq, B_NOPE, B_ROPE)
    n1 = norm1_w.astype(F32).reshape(-1, 1, d)
    n2 = norm2_w.astype(F32).reshape(-1, 1, d)

    x_all = jnp.concatenate([ctx, x], axis=1).astype(F32)
    qa, ka, va, qb, kb, vb = _proj0(x_all, mods, rows, n1[0], ab_w_in[0], b_w_uq[0], b_w_ukv[0], a_q_norm[0],
                                    a_k_norm[0], b_q_lora_norm[0], b_kv_lora_norm[0], b_q_norm[0], b_k_norm[0],
                                    tabs_a, tabs_b, half_a, half_b)
    o_a = _attn_window(qa, ka, va, a_sink[0], ctx_len=ctx_len)
    o_b = _attn_dense(qb, kb, vb, ctx_len=ctx_len)
    x_all, h2 = _out0(o_a, o_b, ab_w_o[0], x_all, mods, rows, n2[0])
    h2 = h2.reshape(nb * total, d)
    sel = _peer_select(h2, peer_w_q[0], peer_keys[0])
    f = _peer_apply(h2, sel, peer_u[0], peer_v[0]).reshape(nb, total, d)

    x_all, qr, kr, vr, gr, qd, kd, vd = _proj1(x_all, f, mods, rows, n1[1], cd_w_in[0], d_q_norm[0], d_k_norm[0],
                                               tabs_a, half_a)
    lg = jnp.stack([jax.nn.log_sigmoid(c_decay_fwd[0].astype(F32)), jax.nn.log_sigmoid(c_decay_bwd[0].astype(F32))])
    o_f, o_bw = _retention(qr, kr, vr, lg, ctx_len)
    o_d = _attn_dense(qd, kd, vd, ctx_len=ctx_len)
    x_lat, h2 = _out1(o_f, o_bw, gr, c_gn_w[0], o_d, cd_w_o[0], x_all, mods, rows, n2[1], ctx_len)
    h2 = h2.reshape(nb * seq, d)
    sel = _peer_select(h2, peer_w_q[1], peer_keys[1])
    out = _peer_apply(h2, sel, peer_u[1], peer_v[1], x=x_lat.reshape(nb * seq, d), mods=mods,
                      mod_index=lambda b: (rows + b) * 6 + 5, tokens_per_batch=seq)
    return out.reshape(nb, seq, d).astype(x.dtype)
```

```python
import functools

import numpy as np
import jax
import jax.numpy as jnp
from jax import lax
from jax.experimental import pallas as pl
from jax.experimental.pallas import tpu as pltpu

F32 = jnp.float32
BF16 = jnp.bfloat16

GRID_W = 64
ROPE_THETA = 10000.0
EPS = 1e-6
NEG_INF = -1e30
HEAD_DIM = 64
A_Q_HEADS, A_KV_HEADS, A_WINDOW = 8, 2, 128
B_HEADS, B_NOPE, B_ROPE, B_V, B_Q_RANK, B_KV_RANK = 8, 64, 32, 64, 256, 256
B_QK = B_NOPE + B_ROPE
C_HEADS, C_DK, C_DV = 4, 64, 128
D_Q_HEADS, D_KV_HEADS = 8, 2
PEER_HEADS, PEER_N_KEYS, PEER_D_KEY, PEER_TOPK = 8, 128, 256, 16

LANE = 128
SUBLANE = 8
BF16_ROWS = 16
ONES_LANE = LANE - 1
VMEM_LIMIT = 56 * 1024 * 1024

TOK_TILE = 256
CHUNK = 128
ATTN_KV_PER_STEP = 2
ATTN_CHAINS_PER_STEP = 4
PEER_SEL_TILE = 256
SELECT_HEADS = 2
PEER_TOK_TILE = 512
PEER_EXP_TILE = 2048


def _cparams(*sem):
    return pltpu.CompilerParams(dimension_semantics=sem, vmem_limit_bytes=VMEM_LIMIT)


def _full(arr):
    nd = arr.ndim
    return pl.BlockSpec(arr.shape, lambda *_: (0,) * nd)


def _pad_cols(w, n_heads, d):
    lead = w.shape[:-1]
    w = w.reshape(lead + (n_heads, d))
    w = jnp.pad(w, [(0, 0)] * len(lead) + [(0, 0), (0, LANE - d)])
    return w.reshape(lead + (n_heads * LANE,))


def _pad_rows(w, n_heads, d):
    n = w.shape[-1]
    w = w.reshape(n_heads, d, n)
    w = jnp.pad(w, [(0, 0), (0, LANE - d), (0, 0)])
    return w.reshape(n_heads * LANE, n)


def _pad_gain(g, d):
    return jnp.pad(g.astype(F32), (0, LANE - d)).reshape(1, LANE)


def _rope_tables(ctx_len, seq, lane_off, d_rot):
    blk = d_rot // 2
    half = blk // 2
    freqs = ROPE_THETA ** (-np.arange(half, dtype=np.float64) / half)
    pos = np.arange(seq)
    total = ctx_len + seq
    cos = np.ones((total, LANE), np.float64)
    sup = np.zeros((total, LANE), np.float64)
    sdn = np.zeros((total, LANE), np.float64)
    for axis, p in enumerate((pos // GRID_W, pos % GRID_W)):
        ang = p[:, None].astype(np.float64) * freqs[None, :]
        c, s = np.cos(ang), np.sin(ang)
        base = lane_off + axis * blk
        cos[ctx_len:, base:base + half] = c
        cos[ctx_len:, base + half:base + blk] = c
        sdn[ctx_len:, base:base + half] = -s
        sup[ctx_len:, base + half:base + blk] = s
    return (jnp.asarray(cos, F32), jnp.asarray(sup, F32), jnp.asarray(sdn, F32)), half


def _rms_rows(x, true_dim):
    return x * lax.rsqrt(jnp.sum(x * x, axis=-1, keepdims=True) * (1.0 / true_dim) + EPS)


def _rope(y, cos, sup, sdn, half):
    return y * cos + pltpu.roll(y, half, 1) * sup + pltpu.roll(y, LANE - half, 1) * sdn


def _ada_kernel(c_ref, w_ref, b_ref, o_ref):
    c = c_ref[...]
    s = c * jax.nn.sigmoid(c)
    o_ref[0] = jnp.dot(s.astype(BF16), w_ref[0].astype(BF16), preferred_element_type=F32) + b_ref[0]


def _ada(c_rows, ada_w, ada_b):
    depth, d, n = ada_w.shape
    rows = c_rows.shape[0]
    tn = 1536
    return pl.pallas_call(
        _ada_kernel,
        grid=(depth, n // tn),
        in_specs=[pl.BlockSpec((rows, d), lambda l, j: (0, 0)),
                  pl.BlockSpec((1, d, tn), lambda l, j: (l, 0, j)),
                  pl.BlockSpec((1, 1, tn), lambda l, j: (l, 0, j))],
        out_specs=pl.BlockSpec((1, rows, tn), lambda l, j: (l, 0, j)),
        out_shape=jax.ShapeDtypeStruct((depth, rows, n), F32),
        compiler_params=_cparams("arbitrary", "arbitrary"),
        name="ada_mod",
    )(c_rows, ada_w, ada_b.reshape(depth, 1, n))


def _mod_spec(layer, chunk, rows, nb, d, tile_axis=1):
    def imap(*ids):
        b, t = ids[0], ids[tile_axis]
        r = jnp.where(t == 0, nb, b)
        return ((layer * rows + r) * 6 + chunk, 0, 0)
    return pl.BlockSpec((1, 1, d), imap)


def _modulate(x, n_ref, sc_ref, sh_ref, d):
    return _rms_rows(x, d) * n_ref[...] * (1.0 + sc_ref[0]) + sh_ref[0]


def _proj0_kernel(x_ref, n1_ref, sh_ref, sc_ref, w_ref, wuq_ref, wuk_ref, wuv_ref,
                  aqn_ref, akn_ref, bqln_ref, bkvln_ref, bqn_ref, bkn_ref,
                  ca_ref, ua_ref, da_ref, cb_ref, ub_ref, db_ref,
                  qa_ref, ka_ref, va_ref, qb_ref, kb_ref, vb_ref, *, d_model, half_a, half_b, offs):
    h = _modulate(x_ref[0], n1_ref, sc_ref, sh_ref, d_model).astype(BF16)
    ca, ua, da = ca_ref[...], ua_ref[...], da_ref[...]
    cb, ub, db = cb_ref[...], ub_ref[...], db_ref[...]
    o_qa, o_ka, o_va, o_cq, o_ckv, o_kr, o_end = offs

    z = jnp.dot(h, w_ref[:, o_qa:o_ka], preferred_element_type=F32)
    for i in range(A_Q_HEADS):
        y = _rms_rows(z[:, i * LANE:(i + 1) * LANE], HEAD_DIM) * aqn_ref[...]
        qa_ref[0, :, i * LANE:(i + 1) * LANE] = (_rope(y, ca, ua, da, half_a) * HEAD_DIM ** -0.5).astype(BF16)
    z = jnp.dot(h, w_ref[:, o_ka:o_va], preferred_element_type=F32)
    for i in range(A_KV_HEADS):
        y = _rms_rows(z[:, i * LANE:(i + 1) * LANE], HEAD_DIM) * akn_ref[...]
        ka_ref[0, :, i * LANE:(i + 1) * LANE] = _rope(y, ca, ua, da, half_a).astype(BF16)
    va = jnp.dot(h, w_ref[:, o_va:o_cq], preferred_element_type=F32)
    va_ref[0] = _with_ones_lane(va, A_KV_HEADS).astype(BF16)

    cq = jnp.dot(h, w_ref[:, o_cq:o_ckv], preferred_element_type=F32)
    cq = (_rms_rows(cq, B_Q_RANK) * bqln_ref[...]).astype(BF16)
    z = jnp.dot(cq, wuq_ref[...], preferred_element_type=F32)
    for i in range(B_HEADS):
        y = _rms_rows(z[:, i * LANE:(i + 1) * LANE], B_QK) * bqn_ref[...]
        qb_ref[0, :, i * LANE:(i + 1) * LANE] = (_rope(y, cb, ub, db, half_b) * B_QK ** -0.5).astype(BF16)

    ckv = jnp.dot(h, w_ref[:, o_ckv:o_kr], preferred_element_type=F32)
    ckv = (_rms_rows(ckv, B_KV_RANK) * bkvln_ref[...]).astype(BF16)
    kr = jnp.dot(h, w_ref[:, o_kr:o_end], preferred_element_type=F32)
    z = jnp.dot(ckv, wuk_ref[...], preferred_element_type=F32)
    for i in range(B_HEADS):
        y = _rms_rows(z[:, i * LANE:(i + 1) * LANE] + kr, B_QK) * bkn_ref[...]
        kb_ref[0, :, i * LANE:(i + 1) * LANE] = _rope(y, cb, ub, db, half_b).astype(BF16)
    vb = jnp.dot(ckv, wuv_ref[...], preferred_element_type=F32)
    vb_ref[0] = _with_ones_lane(vb, B_HEADS).astype(BF16)


def _proj0(x_all, mods, rows, norm1, w_in, b_wuq, b_wukv, a_qn, a_kn, b_qln, b_kvln, b_qn, b_kn, tabs_a, tabs_b,
           half_a, half_b):
    nb, total, d = x_all.shape
    tm = TOK_TILE
    wq, wk, wv, wcq, wckv, wkr = jnp.split(
        w_in, np.cumsum([A_Q_HEADS * HEAD_DIM, A_KV_HEADS * HEAD_DIM, A_KV_HEADS * HEAD_DIM, B_Q_RANK, B_KV_RANK])
        .tolist(), axis=1)
    kr_pad = jnp.pad(wkr, ((0, 0), (B_NOPE, LANE - B_QK)))
    parts = [_pad_cols(wq, A_Q_HEADS, HEAD_DIM), _pad_cols(wk, A_KV_HEADS, HEAD_DIM),
             _pad_cols(wv, A_KV_HEADS, HEAD_DIM), wcq, wckv, kr_pad]
    offs = tuple(int(o) for o in np.cumsum([0] + [p.shape[1] for p in parts]))
    w_all = jnp.concatenate(parts, axis=1).astype(BF16)
    wuq = _pad_cols(b_wuq, B_HEADS, B_QK).astype(BF16)
    wukv = b_wukv.reshape(B_KV_RANK, B_HEADS, B_NOPE + B_V)
    wuk = _pad_cols(wukv[..., :B_NOPE].reshape(B_KV_RANK, -1), B_HEADS, B_NOPE).astype(BF16)
    wuv = _pad_cols(wukv[..., B_NOPE:].reshape(B_KV_RANK, -1), B_HEADS, B_V).astype(BF16)
    consts = [w_all, wuq, wuk, wuv, _pad_gain(a_qn, HEAD_DIM), _pad_gain(a_kn, HEAD_DIM),
              b_qln.astype(F32).reshape(1, -1), b_kvln.astype(F32).reshape(1, -1),
              _pad_gain(b_qn, B_QK), _pad_gain(b_kn, B_QK)]
    tab_spec = pl.BlockSpec((tm, LANE), lambda b, t: (t, 0))
    wide = lambda nh: pl.BlockSpec((1, tm, nh * LANE), lambda b, t: (b, t, 0))
    shp = lambda nh: jax.ShapeDtypeStruct((nb, total, nh * LANE), BF16)
    return pl.pallas_call(
        functools.partial(_proj0_kernel, d_model=d, half_a=half_a, half_b=half_b, offs=offs),
        grid=(nb, total // tm),
        in_specs=[pl.BlockSpec((1, tm, d), lambda b, t: (b, t, 0)), _full(norm1),
                  _mod_spec(0, 0, rows, nb, d), _mod_spec(0, 1, rows, nb, d)]
                 + [_full(c) for c in consts] + [tab_spec] * 6,
        out_specs=[wide(A_Q_HEADS), wide(A_KV_HEADS), wide(A_KV_HEADS), wide(B_HEADS), wide(B_HEADS), wide(B_HEADS)],
        out_shape=[shp(A_Q_HEADS), shp(A_KV_HEADS), shp(A_KV_HEADS), shp(B_HEADS), shp(B_HEADS), shp(B_HEADS)],
        compiler_params=_cparams("arbitrary", "arbitrary"),
        name="proj0",
    )(x_all, norm1, mods, mods, *consts, *tabs_a, *tabs_b)


_NT = (((1,), (1,)), ((), ()))


def _stack_heads(q_ref, grp, j=0):
    h0 = j * grp
    if grp == 1:
        return q_ref[0, :, h0 * LANE:(h0 + 1) * LANE]
    return jnp.concatenate([q_ref[0, :, (h0 + g) * LANE:(h0 + g + 1) * LANE] for g in range(grp)], axis=0)


def _softmax_av(scores, values, sink=None):
    m = None
    for s in scores:
        ms = jnp.max(s, axis=-1, keepdims=True)
        m = ms if m is None else jnp.maximum(m, ms)
    if sink is not None:
        m = jnp.maximum(m, sink)
    o = None
    for s, v in zip(scores, values):
        os_ = jnp.dot(jnp.exp(s - m).astype(BF16), v, preferred_element_type=F32)
        o = os_ if o is None else o + os_
    den = o[:, ONES_LANE:ONES_LANE + 1]
    if sink is not None:
        den = den + jnp.exp(sink - m)
    return o / den


def _with_ones_lane(v, n_heads):
    lane = lax.broadcasted_iota(jnp.int32, (1, n_heads * LANE), 1) % LANE
    return v + jnp.where(lane == ONES_LANE, 1.0, 0.0)


def _unstack_store(o, o_ref, grp, tq, j=0):
    for g in range(grp):
        h = j * grp + g
        o_ref[0, :, h * LANE:(h + 1) * LANE] = o[g * tq:(g + 1) * tq].astype(BF16)


def _attn_dense_kernel(q_ref, k_ref, v_ref, o_ref, *, grp, kvs, ctx_len, tq):
    qt = pl.program_id(2)

    @pl.when(qt * tq < ctx_len)
    def _():
        for j in range(kvs):
            kj = slice(j * LANE, (j + 1) * LANE)
            s = lax.dot_general(_stack_heads(q_ref, grp, j), k_ref[0, 0:ctx_len, kj], _NT,
                                preferred_element_type=F32)
            _unstack_store(_softmax_av([s], [v_ref[0, 0:ctx_len, kj]]), o_ref, grp, tq, j)

    @pl.when(qt * tq >= ctx_len)
    def _():
        for j in range(kvs):
            kj = slice(j * LANE, (j + 1) * LANE)
            s = lax.dot_general(_stack_heads(q_ref, grp, j), k_ref[0, :, kj], _NT, preferred_element_type=F32)
            _unstack_store(_softmax_av([s], [v_ref[0, :, kj]]), o_ref, grp, tq, j)


def _attn_dense(q, k, v, *, ctx_len):
    nb, total, qw = q.shape
    hq, hkv = qw // LANE, k.shape[2] // LANE
    grp = hq // hkv
    kvs = min(ATTN_CHAINS_PER_STEP if grp == 1 else ATTN_KV_PER_STEP, hkv)
    tq = CHUNK if grp > 1 else TOK_TILE
    assert ctx_len % tq == 0 and hkv % kvs == 0
    return pl.pallas_call(
        functools.partial(_attn_dense_kernel, grp=grp, kvs=kvs, ctx_len=ctx_len, tq=tq),
        grid=(nb, hkv // kvs, total // tq),
        in_specs=[pl.BlockSpec((1, tq, kvs * grp * LANE), lambda b, h, t: (b, t, h)),
                  pl.BlockSpec((1, total, kvs * LANE), lambda b, h, t: (b, 0, h)),
                  pl.BlockSpec((1, total, kvs * LANE), lambda b, h, t: (b, 0, h))],
        out_specs=pl.BlockSpec((1, tq, kvs * grp * LANE), lambda b, h, t: (b, t, h)),
        out_shape=jax.ShapeDtypeStruct(q.shape, BF16),
        compiler_params=_cparams("arbitrary", "arbitrary", "arbitrary"),
        name="attn_dense",
    )(q, k, v)


def _attn_window_kernel(q_ref, k_ref, v_ref, sink_ref, o_ref, *, grp, kvs, ctx_len, tq, total, window):
    qt = pl.program_id(2)

    def sink_col(j):
        return jnp.concatenate([jnp.broadcast_to(sink_ref[j * grp + g][:, 0:1], (tq, 1)) for g in range(grp)], axis=0)

    @pl.when(qt * tq < ctx_len)
    def _():
        for j in range(kvs):
            kj = slice(j * LANE, (j + 1) * LANE)
            s_c = lax.dot_general(_stack_heads(q_ref, grp, j), k_ref[0, 0:ctx_len, kj], _NT,
                                  preferred_element_type=F32)
            _unstack_store(_softmax_av([s_c], [v_ref[0, 0:ctx_len, kj]], sink_col(j)), o_ref, grp, tq, j)

    @pl.when(qt * tq >= ctx_len)
    def _():
        slab = 3 * tq
        start = pl.multiple_of(jnp.clip((qt - 1) * tq, ctx_len, total - slab), tq)
        qpos = qt * tq + lax.broadcasted_iota(jnp.int32, (tq, slab), 0)
        kpos = start + lax.broadcasted_iota(jnp.int32, (tq, slab), 1)
        bias = jnp.where(jnp.abs(qpos - kpos) <= window, 0.0, NEG_INF)
        bias = jnp.concatenate([bias] * grp, axis=0)
        for j in range(kvs):
            kj = slice(j * LANE, (j + 1) * LANE)
            q = _stack_heads(q_ref, grp, j)
            s_c = lax.dot_general(q, k_ref[0, 0:ctx_len, kj], _NT, preferred_element_type=F32)
            s_l = lax.dot_general(q, k_ref[0, pl.ds(start, slab), kj], _NT, preferred_element_type=F32) + bias
            o = _softmax_av([s_c, s_l], [v_ref[0, 0:ctx_len, kj], v_ref[0, pl.ds(start, slab), kj]], sink_col(j))
            _unstack_store(o, o_ref, grp, tq, j)


def _attn_window(q, k, v, sink, *, ctx_len):
    nb, total, qw = q.shape
    hq, hkv = qw // LANE, k.shape[2] // LANE
    grp = hq // hkv
    kvs = ATTN_KV_PER_STEP
    tq = CHUNK
    assert hkv % kvs == 0
    sink_rows = jnp.broadcast_to(sink.astype(F32).reshape(hq, 1, 1), (hq, 1, LANE))
    return pl.pallas_call(
        functools.partial(_attn_window_kernel, grp=grp, kvs=kvs, ctx_len=ctx_len, tq=tq, total=total,
                          window=A_WINDOW),
        grid=(nb, hkv // kvs, total // tq),
        in_specs=[pl.BlockSpec((1, tq, kvs * grp * LANE), lambda b, h, t: (b, t, h)),
                  pl.BlockSpec((1, total, kvs * LANE), lambda b, h, t: (b, 0, h)),
                  pl.BlockSpec((1, total, kvs * LANE), lambda b, h, t: (b, 0, h)),
                  pl.BlockSpec((kvs * grp, 1, LANE), lambda b, h, t: (h, 0, 0))],
        out_specs=pl.BlockSpec((1, tq, kvs * grp * LANE), lambda b, h, t: (b, t, h)),
        out_shape=jax.ShapeDtypeStruct(q.shape, BF16),
        compiler_params=_cparams("arbitrary", "arbitrary", "arbitrary"),
        name="attn_window",
    )(q, k, v, sink_rows)


def _out0_kernel(oa_ref, ob_ref, woa_ref, wob_ref, x_ref, g1_ref, n2_ref, sh2_ref, sc2_ref, xn_ref, h2_ref, *, d_model):
    y = (jnp.dot(oa_ref[0], woa_ref[...], preferred_element_type=F32)
         + jnp.dot(ob_ref[0], wob_ref[...], preferred_element_type=F32))
    xn = x_ref[0] + g1_ref[0] * y
    xn_ref[0] = xn
    h2_ref[0] = _modulate(xn, n2_ref, sc2_ref, sh2_ref, d_model).astype(BF16)


def _out0(oa, ob, w_o, x_all, mods, rows, norm2):
    nb, total, d = x_all.shape
    tm = TOK_TILE
    woa = _pad_rows(w_o[:A_Q_HEADS * HEAD_DIM], A_Q_HEADS, HEAD_DIM).astype(BF16)
    wob = _pad_rows(w_o[A_Q_HEADS * HEAD_DIM:], B_HEADS, B_V).astype(BF16)
    tile = lambda w: pl.BlockSpec((1, tm, w), lambda b, t: (b, t, 0))
    return pl.pallas_call(
        functools.partial(_out0_kernel, d_model=d),
        grid=(nb, total // tm),
        in_specs=[tile(oa.shape[2]), tile(ob.shape[2]), _full(woa), _full(wob), tile(d),
                  _mod_spec(0, 2, rows, nb, d), _full(norm2), _mod_spec(0, 3, rows, nb, d),
                  _mod_spec(0, 4, rows, nb, d)],
        out_specs=[tile(d), tile(d)],
        out_shape=[jax.ShapeDtypeStruct((nb, total, d), F32), jax.ShapeDtypeStruct((nb, total, d), BF16)],
        compiler_params=_cparams("arbitrary", "arbitrary"),
        name="out_proj0",
    )(oa, ob, woa, wob, x_all, mods, norm2, mods, mods)


def _top_rows(sc, rowf, k):
    n = sc.shape[0]
    vals, idxs = [], []
    work = sc
    for _ in range(k):
        m = jnp.max(work, axis=0, keepdims=True)
        idx = jnp.min(jnp.where(work == m, rowf, float(n)), axis=0, keepdims=True)
        vals.append(m)
        idxs.append(idx)
        work = jnp.where(rowf == idx, -jnp.inf, work)
    return vals, idxs


def _stack_rows(rows, row16):
    out = jnp.zeros(row16.shape, F32)
    for k, r in enumerate(rows):
        out = jnp.where(row16 == float(k), r, out)
    return out


def _candidates(v0, s1, slab_rows):
    return jnp.concatenate([v0[k1] + s1[0:slab_rows[k1], :] for k1 in range(len(v0))], axis=0)


def _select_exact(sc0, sc1, rowf, row16, flat, slab_rows, topk):
    nk, ts = sc0.shape
    v0, i0 = _top_rows(sc0, rowf, topk)
    v1, i1 = _top_rows(sc1, rowf, topk)
    work = _candidates(v0, _stack_rows(v1, row16), slab_rows)
    cnt = jnp.zeros((topk, ts), F32)
    zsum = jnp.zeros((1, ts), F32)
    best0 = None
    for k in range(topk):
        m = jnp.max(work, axis=0, keepdims=True)
        idx = jnp.min(jnp.where(work == m, flat, 1e9), axis=0, keepdims=True)
        work = jnp.where(flat == idx, -jnp.inf, work)
        best0 = m if best0 is None else best0
        zsum = zsum + jnp.exp(m - best0)
        cnt = cnt + jnp.where(row16 == jnp.floor(idx * (1.0 / topk)), 1.0, 0.0)
    cc = jnp.zeros((nk, ts), F32)
    rb = jnp.full((nk, ts), 99.0, F32)
    for k in range(topk):
        ck = jnp.sum(jnp.where(row16 == float(k), cnt, 0.0), axis=0, keepdims=True)
        cc = jnp.where(rowf == i0[k], ck, cc)
        rb = jnp.where(rowf == i1[k], float(k), rb)
    return cc, rb, zsum


def _select_fast(sc0, sc1, row16, slab_rows, topk):
    nk, ts = sc0.shape
    ninf = -jnp.inf
    count = lambda hit: jnp.sum(jnp.where(hit, 1.0, 0.0), axis=0, keepdims=True)
    work, v0 = sc0, []
    for _ in range(topk):
        m = jnp.max(work, axis=0, keepdims=True)
        v0.append(m)
        work = jnp.where(work == m, ninf, work)
    bad = count(work == ninf) != float(topk)
    work, v1 = sc1, []
    rb = jnp.full((nk, ts), 99.0, F32)
    for k in range(topk):
        m = jnp.max(work, axis=0, keepdims=True)
        v1.append(m)
        hit = work == m
        work = jnp.where(hit, ninf, work)
        rb = jnp.where(hit, float(k), rb)
    bad = jnp.logical_or(bad, count(rb < 99.0) != float(topk))
    work = _candidates(v0, _stack_rows(v1, row16), slab_rows)
    zsum = jnp.zeros((1, ts), F32)
    best0 = None
    for _ in range(topk):
        m = jnp.max(work, axis=0, keepdims=True)
        work = jnp.where(work == m, ninf, work)
        best0 = m if best0 is None else best0
        zsum = zsum + jnp.exp(m - best0)
    chosen = jnp.where(work == ninf, 1.0, 0.0)
    cc = jnp.zeros((nk, ts), F32)
    total = jnp.zeros((1, ts), F32)
    off = 0
    for k1 in range(topk):
        ck = jnp.sum(chosen[off:off + slab_rows[k1], :], axis=0, keepdims=True)
        off += slab_rows[k1]
        total = total + ck
        cc = jnp.where(sc0 == v0[k1], ck, cc)
    bad = jnp.logical_or(bad, total != float(topk))
    return cc, rb, zsum, jnp.max(jnp.where(bad, 1.0, 0.0))


def _peer_select_kernel(h_ref, wq_ref, keys_ref, cc_ref, e0_ref, rb_ref, e1_ref, q_sc, *, n_heads, topk):
    nk = PEER_N_KEYS
    ts = h_ref.shape[0]
    q_sc[...] = lax.dot_general(wq_ref[...], h_ref[...], (((1,), (1,)), ((), ())), preferred_element_type=F32)
    rowf = lax.broadcasted_iota(jnp.int32, (nk, ts), 0).astype(F32)
    row16 = lax.broadcasted_iota(jnp.int32, (topk, ts), 0).astype(F32)
    slab_rows = [topk] + [SUBLANE] * (topk - 1)
    n_cand = sum(slab_rows)
    ci = lax.broadcasted_iota(jnp.int32, (n_cand, ts), 0)
    rest = ci - topk
    flat = jnp.where(ci < topk, ci, (1 + (rest >> 3)) * topk + (rest & 7)).astype(F32)

    def scores(hp):
        qhp = q_sc[pl.ds(pl.multiple_of(hp * nk, nk), nk), :].astype(BF16)
        return jnp.dot(keys_ref[hp], qhp, preferred_element_type=F32)

    def heads_body(it, carry):
        heads = [it * SELECT_HEADS + u for u in range(SELECT_HEADS)]
        scs = [(scores(hd * 2), scores(hd * 2 + 1)) for hd in heads]
        fast = [_select_fast(sc0, sc1, row16, slab_rows, topk) for sc0, sc1 in scs]
        for hd, (sc0, sc1), (cc, rb, zsum, tie) in zip(heads, scs, fast):
            cc, rb, zsum = lax.cond(tie > 0.0,
                                    lambda: _select_exact(sc0, sc1, rowf, row16, flat, slab_rows, topk),
                                    lambda: (cc, rb, zsum))
            cc_ref[0, hd] = cc
            rb_ref[0, hd] = rb.astype(BF16)
            e0_ref[0, hd] = jnp.exp(sc0 - jnp.max(sc0, axis=0, keepdims=True))
            e1_ref[0, hd] = (jnp.exp(sc1 - jnp.max(sc1, axis=0, keepdims=True)) / zsum).astype(BF16)
        return carry

    lax.fori_loop(0, n_heads // SELECT_HEADS, heads_body, 0)


def _peer_select(h2, w_q, keys):
    t, d = h2.shape
    ts = PEER_SEL_TILE
    nh, nk = PEER_HEADS, PEER_N_KEYS
    wq_t = w_q.T.astype(BF16)
    keys2 = keys.reshape(nh * 2, nk, PEER_D_KEY // 2).astype(BF16)
    row_out = jax.ShapeDtypeStruct((t // ts, nh, nk, ts), F32)
    col_out = jax.ShapeDtypeStruct((t // ts, nh, nk, ts), BF16)
    ospec = pl.BlockSpec((1, nh, nk, ts), lambda i: (i, 0, 0, 0))
    return pl.pallas_call(
        functools.partial(_peer_select_kernel, n_heads=nh, topk=PEER_TOPK),
        grid=(t // ts,),
        in_specs=[pl.BlockSpec((ts, d), lambda i: (i, 0)), _full(wq_t), _full(keys2)],
        out_specs=[ospec] * 4,
        out_shape=[row_out, row_out, col_out, col_out],
        scratch_shapes=[pltpu.VMEM((wq_t.shape[0], ts), F32)],
        compiler_params=_cparams("arbitrary"),
        name="peer_select",
    )(h2, wq_t, keys2)


def _peer_apply_kernel(*refs, n_heads, final):
    if final:
        (h_ref, u_hbm, vt_hbm, cc_ref, e0_ref, rb_ref, e1_ref, x_ref, g_ref, o_ref,
         acc_ref, g_sc, p_sc, ht_sc, u_buf, vt_buf, sem) = refs
    else:
        (h_ref, u_hbm, vt_hbm, cc_ref, e0_ref, rb_ref, e1_ref, o_ref,
         acc_ref, g_sc, p_sc, ht_sc, u_buf, vt_buf, sem) = refs
    tp, et, hh = pl.program_id(0), pl.program_id(1), pl.program_id(2)
    nk = PEER_N_KEYS
    rows_per_tile = cc_ref.shape[2]
    tw = cc_ref.shape[3]
    tok = pl.ds(pl.multiple_of(hh * tw, tw), tw)
    te = u_buf.shape[1]
    n_et = pl.num_programs(1) - 1
    last = n_et

    def u_copy(tile):
        return pltpu.make_async_copy(u_hbm.at[pl.ds(pl.multiple_of(tile * te, te), te), :], u_buf.at[tile % 2],
                                     sem.at[0, tile % 2])

    def vt_copy(tile):
        return pltpu.make_async_copy(vt_hbm.at[tile], vt_buf.at[tile % 2], sem.at[1, tile % 2])

    @pl.when(hh == 0)
    def _():
        @pl.when(jnp.logical_and(tp == 0, et == 0))
        def _():
            u_copy(0).start()

        @pl.when(et < n_et)
        def _():
            u_copy(et).wait()

        @pl.when(et >= 1)
        def _():
            vt_copy(et - 1).wait()

        @pl.when(et + 1 < n_et)
        def _():
            u_copy(et + 1).start()

        @pl.when(et < n_et)
        def _():
            vt_copy(et).start()

        @pl.when(jnp.logical_and(et == n_et, tp + 1 < pl.num_programs(0)))
        def _():
            u_copy(0).start()

    @pl.when(et == 0)
    def _():
        ht_sc[hh] = h_ref[tok, :].astype(F32).T.astype(BF16)

    def stage1():
        return jnp.dot(u_buf[et % 2], ht_sc[hh], preferred_element_type=F32)

    def stage1_store(at):
        g_sc[hh * 2 + et % 2] = jax.nn.gelu(at.astype(BF16))

    def stage2():
        zero = jnp.zeros((), BF16)
        prev = hh * 2 + (et + 1) % 2
        for ii in range(rows_per_tile):
            ccr = [jnp.broadcast_to(cc_ref[hh, hd, ii:ii + 1, :], (BF16_ROWS, tw)).astype(BF16)
                   for hd in range(n_heads)]
            e0r = [jnp.broadcast_to(e0_ref[hh, hd, ii:ii + 1, :], (BF16_ROWS, tw)).astype(BF16)
                   for hd in range(n_heads)]
            for s0 in range(0, nk, BF16_ROWS):
                rws = slice(s0, s0 + BF16_ROWS)
                w = None
                for hd in range(n_heads):
                    term = jnp.where(rb_ref[hh, hd, rws, :] < ccr[hd], e1_ref[hh, hd, rws, :], zero) * e0r[hd]
                    w = term if w is None else w + term
                r0 = ii * nk + s0
                p_sc[r0:r0 + BF16_ROWS, :] = w * g_sc[prev, r0:r0 + BF16_ROWS, :]
        acc_ref[hh] += jnp.dot(vt_buf[(et + 1) % 2], p_sc[...], preferred_element_type=F32)

    @pl.when(et == 0)
    def _():
        acc_ref[hh] = jnp.zeros(acc_ref.shape[1:], F32)
        stage1_store(stage1())

    @pl.when(jnp.logical_and(et > 0, et < last))
    def _():
        at = stage1()
        stage2()
        stage1_store(at)

    @pl.when(et == last)
    def _():
        stage2()

    @pl.when(et == last)
    def _():
        f = acc_ref[hh].T
        if final:
            o_ref[tok, :] = x_ref[tok, :] + g_ref[0] * f
        else:
            o_ref[tok, :] = f


def _peer_apply(h2, sel, u_tab, v_tab, x=None, mods=None, mod_index=None, tokens_per_batch=None):
    t, d = h2.shape
    cc, e0, rb, e1 = sel
    nh, nk = PEER_HEADS, PEER_N_KEYS
    tt, te = PEER_TOK_TILE, PEER_EXP_TILE
    ti = te // nk
    n_et = u_tab.shape[0] // te
    u = u_tab.astype(BF16)
    vt = v_tab.reshape(n_et, te, d).transpose(0, 2, 1).astype(BF16)
    final = x is not None
    tw = cc.shape[3]
    halves = tt // tw
    assert tt % tw == 0 and tw == PEER_SEL_TILE
    prv = lambda e: jnp.maximum(e - 1, 0)
    row_spec = pl.BlockSpec((halves, nh, ti, tw), lambda i, e, hh: (i, 0, prv(e), 0))
    col_spec = pl.BlockSpec((halves, nh, nk, tw), lambda i, e, hh: (i, 0, 0, 0))
    in_specs = [pl.BlockSpec((tt, d), lambda i, e, hh: (i, 0)),
                pl.BlockSpec(memory_space=pl.ANY), pl.BlockSpec(memory_space=pl.ANY),
                row_spec, row_spec, col_spec, col_spec]
    args = [h2, u, vt, cc, e0, rb, e1]
    if final:
        per = tokens_per_batch // tt
        in_specs += [pl.BlockSpec((tt, d), lambda i, e, hh: (i, 0)),
                     pl.BlockSpec((1, 1, d), lambda i, e, hh: (mod_index(i // per), 0, 0))]
        args += [x, mods]
    return pl.pallas_call(
        functools.partial(_peer_apply_kernel, n_heads=nh, final=final),
        grid=(t // tt, n_et + 1, halves),
        in_specs=in_specs,
        out_specs=pl.BlockSpec((tt, d), lambda i, e, hh: (i, 0)),
        out_shape=jax.ShapeDtypeStruct((t, d), F32),
        scratch_shapes=[pltpu.VMEM((halves, d, tw), F32), pltpu.VMEM((halves * 2, te, tw), BF16),
                        pltpu.VMEM((te, tw), BF16), pltpu.VMEM((halves, d, tw), BF16),
                        pltpu.VMEM((2, te, d), BF16), pltpu.VMEM((2, d, te), BF16),
                        pltpu.SemaphoreType.DMA((2, 2))],
        compiler_params=_cparams("arbitrary", "arbitrary", "arbitrary"),
        name="peer_apply_final" if final else "peer_apply",
    )(*args)


def _proj1_kernel(x_ref, f_ref, g2_ref, n1_ref, sh_ref, sc_ref, w_ref, dqn_ref, dkn_ref, ca_ref, ua_ref, da_ref,
                  xn_ref, qr_ref, kr_ref, vr_ref, gr_ref, qd_ref, kd_ref, vd_ref, *, d_model, half_a, offs):
    xn = x_ref[0] + g2_ref[0] * f_ref[0]
    xn_ref[0] = xn
    h = _modulate(xn, n1_ref, sc_ref, sh_ref, d_model).astype(BF16)
    ca, ua, da = ca_ref[...], ua_ref[...], da_ref[...]
    o_qr, o_kr, o_vr, o_gr, o_qd, o_kd, o_vd, o_end = offs

    z = jnp.dot(h, w_ref[:, o_qr:o_kr], preferred_element_type=F32)
    for i in range(C_HEADS):
        qr_ref[0, :, i * LANE:(i + 1) * LANE] = _rope(z[:, i * LANE:(i + 1) * LANE], ca, ua, da, half_a).astype(BF16)
    z = jnp.dot(h, w_ref[:, o_kr:o_vr], preferred_element_type=F32) * (C_DK ** -0.5)
    for i in range(C_HEADS):
        kr_ref[0, :, i * LANE:(i + 1) * LANE] = _rope(z[:, i * LANE:(i + 1) * LANE], ca, ua, da, half_a).astype(BF16)
    vr_ref[0] = jnp.dot(h, w_ref[:, o_vr:o_gr], preferred_element_type=F32).astype(BF16)
    gr_ref[0] = jnp.dot(h, w_ref[:, o_gr:o_qd], preferred_element_type=F32).astype(BF16)
    z = jnp.dot(h, w_ref[:, o_qd:o_kd], preferred_element_type=F32)
    for i in range(D_Q_HEADS):
        y = _rms_rows(z[:, i * LANE:(i + 1) * LANE], HEAD_DIM) * dqn_ref[...]
        qd_ref[0, :, i * LANE:(i + 1) * LANE] = (_rope(y, ca, ua, da, half_a) * HEAD_DIM ** -0.5).astype(BF16)
    z = jnp.dot(h, w_ref[:, o_kd:o_vd], preferred_element_type=F32)
    for i in range(D_KV_HEADS):
        y = _rms_rows(z[:, i * LANE:(i + 1) * LANE], HEAD_DIM) * dkn_ref[...]
        kd_ref[0, :, i * LANE:(i + 1) * LANE] = _rope(y, ca, ua, da, half_a).astype(BF16)
    vd = jnp.dot(h, w_ref[:, o_vd:o_end], preferred_element_type=F32)
    vd_ref[0] = _with_ones_lane(vd, D_KV_HEADS).astype(BF16)


def _proj1(x_all, f_all, mods, rows, norm1, w_in, d_qn, d_kn, tabs_a, half_a):
    nb, total, d = x_all.shape
    tm = TOK_TILE
    cqk, cv = C_HEADS * C_DK, C_HEADS * C_DV
    wqr, wkr, wvr, wgr, wqd, wkd, wvd = jnp.split(
        w_in, np.cumsum([cqk, cqk, cv, cv, D_Q_HEADS * HEAD_DIM, D_KV_HEADS * HEAD_DIM]).tolist(), axis=1)
    parts = [_pad_cols(wqr, C_HEADS, C_DK), _pad_cols(wkr, C_HEADS, C_DK), wvr, wgr,
             _pad_cols(wqd, D_Q_HEADS, HEAD_DIM), _pad_cols(wkd, D_KV_HEADS, HEAD_DIM),
             _pad_cols(wvd, D_KV_HEADS, HEAD_DIM)]
    offs = tuple(int(o) for o in np.cumsum([0] + [p.shape[1] for p in parts]))
    w_all = jnp.concatenate(parts, axis=1).astype(BF16)
    consts = [w_all, _pad_gain(d_qn, HEAD_DIM), _pad_gain(d_kn, HEAD_DIM)]
    tab_spec = pl.BlockSpec((tm, LANE), lambda b, t: (t, 0))
    tile = lambda w: pl.BlockSpec((1, tm, w), lambda b, t: (b, t, 0))
    widths = [p.shape[1] for p in parts]
    return pl.pallas_call(
        functools.partial(_proj1_kernel, d_model=d, half_a=half_a, offs=offs),
        grid=(nb, total // tm),
        in_specs=[tile(d), tile(d), _mod_spec(0, 5, rows, nb, d), _full(norm1),
                  _mod_spec(1, 0, rows, nb, d), _mod_spec(1, 1, rows, nb, d)]
                 + [_full(c) for c in consts] + [tab_spec] * 3,
        out_specs=[tile(d)] + [tile(w) for w in widths],
        out_shape=[jax.ShapeDtypeStruct((nb, total, d), F32)]
                  + [jax.ShapeDtypeStruct((nb, total, w), BF16) for w in widths],
        compiler_params=_cparams("arbitrary", "arbitrary"),
        name="proj1",
    )(x_all, f_all, mods, norm1, mods, mods, *consts, *tabs_a)


def _retention_kernel(lg_ref, qf_ref, kf_ref, vf_ref, qb_ref, kb_ref, vb_ref, of_ref, ob_ref, st_ref, dec_ref, *,
                      n_heads):
    step = pl.program_id(1)
    c = CHUNK

    @pl.when(step == 0)
    def _():
        st_ref[...] = jnp.zeros_like(st_ref)
        ri = lax.broadcasted_iota(jnp.int32, (c, LANE), 0).astype(F32)
        diff = ri - lax.broadcasted_iota(jnp.int32, (c, LANE), 1).astype(F32)
        for d in range(2):
            for hd in range(n_heads):
                lg = lg_ref[d, hd]
                if d == 0:
                    dec_ref[d, hd, 0] = jnp.where(diff >= 0, jnp.exp(lg * jnp.maximum(diff, 0.0)), 0.0)
                    dec_ref[d, hd, 1] = jnp.exp(lg * (ri + 1.0))
                    dec_ref[d, hd, 2] = jnp.exp(lg * (c - 1.0 - ri))
                else:
                    dec_ref[d, hd, 0] = jnp.where(diff <= 0, jnp.exp(lg * jnp.maximum(-diff, 0.0)), 0.0)
                    dec_ref[d, hd, 1] = jnp.exp(lg * (c - ri))
                    dec_ref[d, hd, 2] = jnp.exp(lg * ri)

    for d, (q_ref, k_ref, v_ref, o_ref) in enumerate(((qf_ref, kf_ref, vf_ref, of_ref),
                                                      (qb_ref, kb_ref, vb_ref, ob_ref))):
        for hd in range(n_heads):
            sl = slice(hd * LANE, (hd + 1) * LANE)
            q, k, v = q_ref[0, :, sl], k_ref[0, :, sl], v_ref[0, :, sl]
            s = lax.dot_general(q, k, _NT, preferred_element_type=F32) * dec_ref[d, hd, 0]
            inner = jnp.dot(s.astype(BF16), v, preferred_element_type=F32)
            st = st_ref[d, hd]
            cross = jnp.dot(q, st.astype(BF16), preferred_element_type=F32) * dec_ref[d, hd, 1]
            o_ref[0, :, sl] = inner + cross
            kd_t = (k.astype(F32) * dec_ref[d, hd, 2]).T.astype(BF16)
            st_ref[d, hd] = st * jnp.exp(lg_ref[d, hd] * c) + jnp.dot(kd_t, v, preferred_element_type=F32)


def _retention(qr, kr, vr, lg, ctx_len):
    nb, total, w = qr.shape
    nh = w // LANE
    c = CHUNK
    nc, nctx = total // c, ctx_len // c
    fwd = pl.BlockSpec((1, c, w), lambda b, s: (b, s, 0))

    def bmap(b, s):
        return (b, jnp.where(s < nctx, nctx - 1 - s, nc - 1 - (s - nctx)), 0)

    bwd = pl.BlockSpec((1, c, w), bmap)
    out = jax.ShapeDtypeStruct((nb, total, w), F32)
    return pl.pallas_call(
        functools.partial(_retention_kernel, n_heads=nh),
        grid=(nb, nc),
        in_specs=[pl.BlockSpec(memory_space=pltpu.SMEM), fwd, fwd, fwd, bwd, bwd, bwd],
        out_specs=[fwd, bwd],
        out_shape=[out, out],
        scratch_shapes=[pltpu.VMEM((2, nh, LANE, LANE), F32), pltpu.VMEM((2, nh, 3, c, LANE), F32)],
        compiler_params=_cparams("arbitrary", "arbitrary"),
        name="retention",
    )(lg, qr, kr, vr, qr, kr, vr)


def _out1_kernel(of_ref, ob_ref, gr_ref, gn_ref, od_ref, wor_ref, wod_ref, x_ref, g1_ref, n2_ref, sh2_ref, sc2_ref,
                 xn_ref, h2_ref, *, d_model, n_heads):
    o = of_ref[0] + ob_ref[0]
    g = gr_ref[0].astype(F32)
    gate = g * jax.nn.sigmoid(g)
    gn = gn_ref[...]
    ys = []
    for hd in range(n_heads):
        sl = slice(hd * LANE, (hd + 1) * LANE)
        oh = o[:, sl]
        mu = jnp.mean(oh, axis=-1, keepdims=True)
        var = jnp.mean(jnp.square(oh - mu), axis=-1, keepdims=True)
        ys.append((gate[:, sl] * ((oh - mu) * lax.rsqrt(var + EPS) * gn[:, sl])).astype(BF16))
    y_ret = jnp.concatenate(ys, axis=1)
    y = (jnp.dot(y_ret, wor_ref[...], preferred_element_type=F32)
         + jnp.dot(od_ref[0], wod_ref[...], preferred_element_type=F32))
    xn = x_ref[0] + g1_ref[0] * y
    xn_ref[0] = xn
    h2_ref[0] = _modulate(xn, n2_ref, sc2_ref, sh2_ref, d_model).astype(BF16)


def _out1(o_f, o_b, g_r, gn_w, o_d, w_o, x_all, mods, rows, norm2, ctx_len):
    nb, total, d = x_all.shape
    tm = TOK_TILE
    skip = ctx_len // tm
    seq = total - ctx_len
    wor = w_o[:C_HEADS * C_DV].astype(BF16)
    wod = _pad_rows(w_o[C_HEADS * C_DV:], D_Q_HEADS, HEAD_DIM).astype(BF16)
    gn = gn_w.astype(F32).reshape(1, -1)
    tile_in = lambda w: pl.BlockSpec((1, tm, w), lambda b, t: (b, t + skip, 0))
    tile_out = pl.BlockSpec((1, tm, d), lambda b, t: (b, t, 0))
    mod = lambda chunk: pl.BlockSpec((1, 1, d), lambda b, t: ((rows + b) * 6 + chunk, 0, 0))
    return pl.pallas_call(
        functools.partial(_out1_kernel, d_model=d, n_heads=C_HEADS),
        grid=(nb, seq // tm),
        in_specs=[tile_in(o_f.shape[2]), tile_in(o_b.shape[2]), tile_in(g_r.shape[2]), _full(gn),
                  tile_in(o_d.shape[2]), _full(wor), _full(wod), tile_in(d), mod(2), _full(norm2), mod(3), mod(4)],
        out_specs=[tile_out, tile_out],
        out_shape=[jax.ShapeDtypeStruct((nb, seq, d), F32), jax.ShapeDtypeStruct((nb, seq, d), BF16)],
        compiler_params=_cparams("arbitrary", "arbitrary"),
        name="out_proj1",
    )(o_f, o_b, g_r, gn, o_d, wor, wod, x_all, mods, norm2, mods, mods)


def kernel(x, c, ctx, c_ctx, ada_w, ada_b, norm1_w, norm2_w, ab_w_in, ab_w_o, a_q_norm, a_k_norm, a_sink,
           b_q_lora_norm, b_kv_lora_norm, b_w_uq, b_w_ukv, b_q_norm, b_k_norm, cd_w_in, cd_w_o, c_decay_fwd,
           c_decay_bwd, c_gn_w, d_q_norm, d_k_norm, peer_w_q, peer_keys, peer_u, peer_v):
    nb, seq, d = x.shape
    ctx_len = ctx.shape[1]
    total = ctx_len + seq
    assert ctx_len == TOK_TILE and seq % TOK_TILE == 0 and seq % GRID_W == 0

    rows = -(-(nb + 1) // SUBLANE) * SUBLANE
    c_rows = jnp.concatenate([c, c_ctx[None, :], jnp.zeros((rows - nb - 1, d), c.dtype)], axis=0).astype(F32)
    mods = _ada(c_rows, ada_w, ada_b).reshape(-1, 1, d)

    tabs_a, half_a = _rope_tables(ctx_len, seq, 0, HEAD_DIM)
    tabs_b, half_b = _rope_tables(ctx_len, seq, B_NOPE, B_ROPE)
    n1 = norm1_w.astype(F32).reshape(-1, 1, d)
    n2 = norm2_w.astype(F32).reshape(-1, 1, d)

    x_all = jnp.concatenate([ctx, x], axis=1).astype(F32)
    qa, ka, va, qb, kb, vb = _proj0(x_all, mods, rows, n1[0], ab_w_in[0], b_w_uq[0], b_w_ukv[0], a_q_norm[0],
                                    a_k_norm[0], b_q_lora_norm[0], b_kv_lora_norm[0], b_q_norm[0], b_k_norm[0],
                                    tabs_a, tabs_b, half_a, half_b)
    o_a = _attn_window(qa, ka, va, a_sink[0], ctx_len=ctx_len)
    o_b = _attn_dense(qb, kb, vb, ctx_len=ctx_len)
    x_all, h2 = _out0(o_a, o_b, ab_w_o[0], x_all, mods, rows, n2[0])
    h2 = h2.reshape(nb * total, d)
    sel = _peer_select(h2, peer_w_q[0], peer_keys[0])
    f = _peer_apply(h2, sel, peer_u[0], peer_v[0]).reshape(nb, total, d)

    x_all, qr, kr, vr, gr, qd, kd, vd = _proj1(x_all, f, mods, rows, n1[1], cd_w_in[0], d_q_norm[0], d_k_norm[0],
                                               tabs_a, half_a)
    lg = jnp.stack([jax.nn.log_sigmoid(c_decay_fwd[0].astype(F32)), jax.nn.log_sigmoid(c_decay_bwd[0].astype(F32))])
    o_f, o_bw = _retention(qr, kr, vr, lg, ctx_len)
    o_d = _attn_dense(qd, kd, vd, ctx_len=ctx_len)
    x_lat, h2 = _out1(o_f, o_bw, gr, c_gn_w[0], o_d, cd_w_o[0], x_all, mods, rows, n2[1], ctx_len)
    h2 = h2.reshape(nb * seq, d)
    sel = _peer_select(h2, peer_w_q[1], peer_keys[1])
    out = _peer_apply(h2, sel, peer_u[1], peer_v[1], x=x_lat.reshape(nb * seq, d), mods=mods,
                      mod_index=lambda b: (rows + b) * 6 + 5, tokens_per_batch=seq)
    return out.reshape(nb, seq, d).astype(x.dtype)
```

```python
import functools

import numpy as np
import jax
import jax.numpy as jnp
from jax import lax
from jax.experimental import pallas as pl
from jax.experimental.pallas import tpu as pltpu

F32 = jnp.float32
BF16 = jnp.bfloat16

GRID_W = 64
ROPE_THETA = 10000.0
EPS = 1e-6
NEG_INF = -1e30
HEAD_DIM = 64
A_Q_HEADS, A_KV_HEADS, A_WINDOW = 8, 2, 128
B_HEADS, B_NOPE, B_ROPE, B_V, B_Q_RANK, B_KV_RANK = 8, 64, 32, 64, 256, 256
B_QK = B_NOPE + B_ROPE
C_HEADS, C_DK, C_DV = 4, 64, 128
D_Q_HEADS, D_KV_HEADS = 8, 2
PEER_HEADS, PEER_N_KEYS, PEER_D_KEY, PEER_TOPK = 8, 128, 256, 16

LANE = 128
SUBLANE = 8
BF16_ROWS = 16
ONES_LANE = LANE - 1
VMEM_LIMIT = 56 * 1024 * 1024

TOK_TILE = 256
CHUNK = 128
ATTN_KV_PER_STEP = 2
ATTN_CHAINS_PER_STEP = 4
PEER_SEL_TILE = 256
SELECT_HEADS = 4
PEER_TOK_TILE = 512
PEER_EXP_TILE = 2048


def _cparams(*sem):
    return pltpu.CompilerParams(dimension_semantics=sem, vmem_limit_bytes=VMEM_LIMIT)


def _full(arr):
    nd = arr.ndim
    return pl.BlockSpec(arr.shape, lambda *_: (0,) * nd)


def _pad_cols(w, n_heads, d):
    lead = w.shape[:-1]
    w = w.reshape(lead + (n_heads, d))
    w = jnp.pad(w, [(0, 0)] * len(lead) + [(0, 0), (0, LANE - d)])
    return w.reshape(lead + (n_heads * LANE,))


def _pad_rows(w, n_heads, d):
    n = w.shape[-1]
    w = w.reshape(n_heads, d, n)
    w = jnp.pad(w, [(0, 0), (0, LANE - d), (0, 0)])
    return w.reshape(n_heads * LANE, n)


def _pad_gain(g, d):
    return jnp.pad(g.astype(F32), (0, LANE - d)).reshape(1, LANE)


def _rope_tables(ctx_len, seq, lane_off, d_rot):
    blk = d_rot // 2
    half = blk // 2
    freqs = ROPE_THETA ** (-np.arange(half, dtype=np.float64) / half)
    pos = np.arange(seq)
    total = ctx_len + seq
    cos = np.ones((total, LANE), np.float64)
    sup = np.zeros((total, LANE), np.float64)
    sdn = np.zeros((total, LANE), np.float64)
    for axis, p in enumerate((pos // GRID_W, pos % GRID_W)):
        ang = p[:, None].astype(np.float64) * freqs[None, :]
        c, s = np.cos(ang), np.sin(ang)
        base = lane_off + axis * blk
        cos[ctx_len:, base:base + half] = c
        cos[ctx_len:, base + half:base + blk] = c
        sdn[ctx_len:, base:base + half] = -s
        sup[ctx_len:, base + half:base + blk] = s
    return (jnp.asarray(cos, F32), jnp.asarray(sup, F32), jnp.asarray(sdn, F32)), half


def _rms_rows(x, true_dim):
    return x * lax.rsqrt(jnp.sum(x * x, axis=-1, keepdims=True) * (1.0 / true_dim) + EPS)


def _rope(y, cos, sup, sdn, half):
    return y * cos + pltpu.roll(y, half, 1) * sup + pltpu.roll(y, LANE - half, 1) * sdn


def _ada_kernel(c_ref, w_ref, b_ref, o_ref):
    c = c_ref[...]
    s = c * jax.nn.sigmoid(c)
    o_ref[0] = jnp.dot(s.astype(BF16), w_ref[0].astype(BF16), preferred_element_type=F32) + b_ref[0]


def _ada(c_rows, ada_w, ada_b):
    depth, d, n = ada_w.shape
    rows = c_rows.shape[0]
    tn = 1536
    return pl.pallas_call(
        _ada_kernel,
        grid=(depth, n // tn),
        in_specs=[pl.BlockSpec((rows, d), lambda l, j: (0, 0)),
                  pl.BlockSpec((1, d, tn), lambda l, j: (l, 0, j)),
                  pl.BlockSpec((1, 1, tn), lambda l, j: (l, 0, j))],
        out_specs=pl.BlockSpec((1, rows, tn), lambda l, j: (l, 0, j)),
        out_shape=jax.ShapeDtypeStruct((depth, rows, n), F32),
        compiler_params=_cparams("arbitrary", "arbitrary"),
        name="ada_mod",
    )(c_rows, ada_w, ada_b.reshape(depth, 1, n))


def _mod_spec(layer, chunk, rows, nb, d, tile_axis=1):
    def imap(*ids):
        b, t = ids[0], ids[tile_axis]
        r = jnp.where(t == 0, nb, b)
        return ((layer * rows + r) * 6 + chunk, 0, 0)
    return pl.BlockSpec((1, 1, d), imap)


def _modulate(x, n_ref, sc_ref, sh_ref, d):
    return _rms_rows(x, d) * n_ref[...] * (1.0 + sc_ref[0]) + sh_ref[0]


def _proj0_kernel(x_ref, n1_ref, sh_ref, sc_ref, w_ref, wuq_ref, wuk_ref, wuv_ref,
                  aqn_ref, akn_ref, bqln_ref, bkvln_ref, bqn_ref, bkn_ref,
                  ca_ref, ua_ref, da_ref, cb_ref, ub_ref, db_ref,
                  qa_ref, ka_ref, va_ref, qb_ref, kb_ref, vb_ref, *, d_model, half_a, half_b, offs):
    h = _modulate(x_ref[0], n1_ref, sc_ref, sh_ref, d_model).astype(BF16)
    ca, ua, da = ca_ref[...], ua_ref[...], da_ref[...]
    cb, ub, db = cb_ref[...], ub_ref[...], db_ref[...]
    o_qa, o_ka, o_va, o_cq, o_ckv, o_kr, o_end = offs

    z = jnp.dot(h, w_ref[:, o_qa:o_ka], preferred_element_type=F32)
    for i in range(A_Q_HEADS):
        y = _rms_rows(z[:, i * LANE:(i + 1) * LANE], HEAD_DIM) * aqn_ref[...]
        qa_ref[0, :, i * LANE:(i + 1) * LANE] = (_rope(y, ca, ua, da, half_a) * HEAD_DIM ** -0.5).astype(BF16)
    z = jnp.dot(h, w_ref[:, o_ka:o_va], preferred_element_type=F32)
    for i in range(A_KV_HEADS):
        y = _rms_rows(z[:, i * LANE:(i + 1) * LANE], HEAD_DIM) * akn_ref[...]
        ka_ref[0, :, i * LANE:(i + 1) * LANE] = _rope(y, ca, ua, da, half_a).astype(BF16)
    va = jnp.dot(h, w_ref[:, o_va:o_cq], preferred_element_type=F32)
    va_ref[0] = _with_ones_lane(va, A_KV_HEADS).astype(BF16)

    cq = jnp.dot(h, w_ref[:, o_cq:o_ckv], preferred_element_type=F32)
    cq = (_rms_rows(cq, B_Q_RANK) * bqln_ref[...]).astype(BF16)
    z = jnp.dot(cq, wuq_ref[...], preferred_element_type=F32)
    for i in range(B_HEADS):
        y = _rms_rows(z[:, i * LANE:(i + 1) * LANE], B_QK) * bqn_ref[...]
        qb_ref[0, :, i * LANE:(i + 1) * LANE] = (_rope(y, cb, ub, db, half_b) * B_QK ** -0.5).astype(BF16)

    ckv = jnp.dot(h, w_ref[:, o_ckv:o_kr], preferred_element_type=F32)
    ckv = (_rms_rows(ckv, B_KV_RANK) * bkvln_ref[...]).astype(BF16)
    kr = jnp.dot(h, w_ref[:, o_kr:o_end], preferred_element_type=F32)
    z = jnp.dot(ckv, wuk_ref[...], preferred_element_type=F32)
    for i in range(B_HEADS):
        y = _rms_rows(z[:, i * LANE:(i + 1) * LANE] + kr, B_QK) * bkn_ref[...]
        kb_ref[0, :, i * LANE:(i + 1) * LANE] = _rope(y, cb, ub, db, half_b).astype(BF16)
    vb = jnp.dot(ckv, wuv_ref[...], preferred_element_type=F32)
    vb_ref[0] = _with_ones_lane(vb, B_HEADS).astype(BF16)


def _proj0(x_all, mods, rows, norm1, w_in, b_wuq, b_wukv, a_qn, a_kn, b_qln, b_kvln, b_qn, b_kn, tabs_a, tabs_b,
           half_a, half_b):
    nb, total, d = x_all.shape
    tm = TOK_TILE
    wq, wk, wv, wcq, wckv, wkr = jnp.split(
        w_in, np.cumsum([A_Q_HEADS * HEAD_DIM, A_KV_HEADS * HEAD_DIM, A_KV_HEADS * HEAD_DIM, B_Q_RANK, B_KV_RANK])
        .tolist(), axis=1)
    kr_pad = jnp.pad(wkr, ((0, 0), (B_NOPE, LANE - B_QK)))
    parts = [_pad_cols(wq, A_Q_HEADS, HEAD_DIM), _pad_cols(wk, A_KV_HEADS, HEAD_DIM),
             _pad_cols(wv, A_KV_HEADS, HEAD_DIM), wcq, wckv, kr_pad]
    offs = tuple(int(o) for o in np.cumsum([0] + [p.shape[1] for p in parts]))
    w_all = jnp.concatenate(parts, axis=1).astype(BF16)
    wuq = _pad_cols(b_wuq, B_HEADS, B_QK).astype(BF16)
    wukv = b_wukv.reshape(B_KV_RANK, B_HEADS, B_NOPE + B_V)
    wuk = _pad_cols(wukv[..., :B_NOPE].reshape(B_KV_RANK, -1), B_HEADS, B_NOPE).astype(BF16)
    wuv = _pad_cols(wukv[..., B_NOPE:].reshape(B_KV_RANK, -1), B_HEADS, B_V).astype(BF16)
    consts = [w_all, wuq, wuk, wuv, _pad_gain(a_qn, HEAD_DIM), _pad_gain(a_kn, HEAD_DIM),
              b_qln.astype(F32).reshape(1, -1), b_kvln.astype(F32).reshape(1, -1),
              _pad_gain(b_qn, B_QK), _pad_gain(b_kn, B_QK)]
    tab_spec = pl.BlockSpec((tm, LANE), lambda b, t: (t, 0))
    wide = lambda nh: pl.BlockSpec((1, tm, nh * LANE), lambda b, t: (b, t, 0))
    shp = lambda nh: jax.ShapeDtypeStruct((nb, total, nh * LANE), BF16)
    return pl.pallas_call(
        functools.partial(_proj0_kernel, d_model=d, half_a=half_a, half_b=half_b, offs=offs),
        grid=(nb, total // tm),
        in_specs=[pl.BlockSpec((1, tm, d), lambda b, t: (b, t, 0)), _full(norm1),
                  _mod_spec(0, 0, rows, nb, d), _mod_spec(0, 1, rows, nb, d)]
                 + [_full(c) for c in consts] + [tab_spec] * 6,
        out_specs=[wide(A_Q_HEADS), wide(A_KV_HEADS), wide(A_KV_HEADS), wide(B_HEADS), wide(B_HEADS), wide(B_HEADS)],
        out_shape=[shp(A_Q_HEADS), shp(A_KV_HEADS), shp(A_KV_HEADS), shp(B_HEADS), shp(B_HEADS), shp(B_HEADS)],
        compiler_params=_cparams("arbitrary", "arbitrary"),
        name="proj0",
    )(x_all, norm1, mods, mods, *consts, *tabs_a, *tabs_b)


_NT = (((1,), (1,)), ((), ()))


def _stack_heads(q_ref, grp, j=0):
    h0 = j * grp
    if grp == 1:
        return q_ref[0, :, h0 * LANE:(h0 + 1) * LANE]
    return jnp.concatenate([q_ref[0, :, (h0 + g) * LANE:(h0 + g + 1) * LANE] for g in range(grp)], axis=0)


def _softmax_av(scores, values, sink=None):
    m = None
    for s in scores:
        ms = jnp.max(s, axis=-1, keepdims=True)
        m = ms if m is None else jnp.maximum(m, ms)
    if sink is not None:
        m = jnp.maximum(m, sink)
    o = None
    for s, v in zip(scores, values):
        os_ = jnp.dot(jnp.exp(s - m).astype(BF16), v, preferred_element_type=F32)
        o = os_ if o is None else o + os_
    den = o[:, ONES_LANE:ONES_LANE + 1]
    if sink is not None:
        den = den + jnp.exp(sink - m)
    return o / den


def _with_ones_lane(v, n_heads):
    lane = lax.broadcasted_iota(jnp.int32, (1, n_heads * LANE), 1) % LANE
    return v + jnp.where(lane == ONES_LANE, 1.0, 0.0)


def _unstack_store(o, o_ref, grp, tq, j=0):
    for g in range(grp):
        h = j * grp + g
        o_ref[0, :, h * LANE:(h + 1) * LANE] = o[g * tq:(g + 1) * tq].astype(BF16)


def _attn_dense_kernel(q_ref, k_ref, v_ref, o_ref, *, grp, nq, ctx_len, tq):
    qt = pl.program_id(2)

    def chains(keys):
        for j in range(nq):
            qj = slice(j * LANE, (j + 1) * LANE)
            kj = slice((j // grp) * LANE, (j // grp + 1) * LANE)
            s = lax.dot_general(q_ref[0, :, qj], k_ref[0, keys, kj], _NT, preferred_element_type=F32)
            o_ref[0, :, qj] = _softmax_av([s], [v_ref[0, keys, kj]]).astype(BF16)

    @pl.when(qt * tq < ctx_len)
    def _():
        chains(slice(0, ctx_len))

    @pl.when(qt * tq >= ctx_len)
    def _():
        chains(slice(None))


def _attn_dense(q, k, v, *, ctx_len):
    nb, total, qw = q.shape
    hq, hkv = qw // LANE, k.shape[2] // LANE
    grp = hq // hkv
    nq = min(ATTN_CHAINS_PER_STEP, hq)
    nkv = max(1, nq // grp)
    tq = TOK_TILE
    assert ctx_len % tq == 0 and hq % nq == 0 and (nq % grp == 0 or grp % nq == 0)
    return pl.pallas_call(
        functools.partial(_attn_dense_kernel, grp=grp, nq=nq, ctx_len=ctx_len, tq=tq),
        grid=(nb, hq // nq, total // tq),
        in_specs=[pl.BlockSpec((1, tq, nq * LANE), lambda b, h, t: (b, t, h)),
                  pl.BlockSpec((1, total, nkv * LANE), lambda b, h, t: (b, 0, (h * nq // grp) // nkv)),
                  pl.BlockSpec((1, total, nkv * LANE), lambda b, h, t: (b, 0, (h * nq // grp) // nkv))],
        out_specs=pl.BlockSpec((1, tq, nq * LANE), lambda b, h, t: (b, t, h)),
        out_shape=jax.ShapeDtypeStruct(q.shape, BF16),
        compiler_params=_cparams("arbitrary", "arbitrary", "arbitrary"),
        name="attn_dense",
    )(q, k, v)


def _attn_window_kernel(q_ref, k_ref, v_ref, sink_ref, o_ref, *, grp, kvs, ctx_len, tq, total, window):
    qt = pl.program_id(2)

    def sink_col(j):
        return jnp.concatenate([jnp.broadcast_to(sink_ref[j * grp + g][:, 0:1], (tq, 1)) for g in range(grp)], axis=0)

    @pl.when(qt * tq < ctx_len)
    def _():
        for j in range(kvs):
            kj = slice(j * LANE, (j + 1) * LANE)
            s_c = lax.dot_general(_stack_heads(q_ref, grp, j), k_ref[0, 0:ctx_len, kj], _NT,
                                  preferred_element_type=F32)
            _unstack_store(_softmax_av([s_c], [v_ref[0, 0:ctx_len, kj]], sink_col(j)), o_ref, grp, tq, j)

    @pl.when(qt * tq >= ctx_len)
    def _():
        slab = 3 * tq
        start = pl.multiple_of(jnp.clip((qt - 1) * tq, ctx_len, total - slab), tq)
        qpos = qt * tq + lax.broadcasted_iota(jnp.int32, (tq, slab), 0)
        kpos = start + lax.broadcasted_iota(jnp.int32, (tq, slab), 1)
        bias = jnp.where(jnp.abs(qpos - kpos) <= window, 0.0, NEG_INF)
        bias = jnp.concatenate([bias] * grp, axis=0)
        for j in range(kvs):
            kj = slice(j * LANE, (j + 1) * LANE)
            q = _stack_heads(q_ref, grp, j)
            s_c = lax.dot_general(q, k_ref[0, 0:ctx_len, kj], _NT, preferred_element_type=F32)
            s_l = lax.dot_general(q, k_ref[0, pl.ds(start, slab), kj], _NT, preferred_element_type=F32) + bias
            o = _softmax_av([s_c, s_l], [v_ref[0, 0:ctx_len, kj], v_ref[0, pl.ds(start, slab), kj]], sink_col(j))
            _unstack_store(o, o_ref, grp, tq, j)


def _attn_window(q, k, v, sink, *, ctx_len):
    nb, total, qw = q.shape
    hq, hkv = qw // LANE, k.shape[2] // LANE
    grp = hq // hkv
    kvs = ATTN_KV_PER_STEP
    tq = CHUNK
    assert hkv % kvs == 0
    sink_rows = jnp.broadcast_to(sink.astype(F32).reshape(hq, 1, 1), (hq, 1, LANE))
    return pl.pallas_call(
        functools.partial(_attn_window_kernel, grp=grp, kvs=kvs, ctx_len=ctx_len, tq=tq, total=total,
                          window=A_WINDOW),
        grid=(nb, hkv // kvs, total // tq),
        in_specs=[pl.BlockSpec((1, tq, kvs * grp * LANE), lambda b, h, t: (b, t, h)),
                  pl.BlockSpec((1, total, kvs * LANE), lambda b, h, t: (b, 0, h)),
                  pl.BlockSpec((1, total, kvs * LANE), lambda b, h, t: (b, 0, h)),
                  pl.BlockSpec((kvs * grp, 1, LANE), lambda b, h, t: (h, 0, 0))],
        out_specs=pl.BlockSpec((1, tq, kvs * grp * LANE), lambda b, h, t: (b, t, h)),
        out_shape=jax.ShapeDtypeStruct(q.shape, BF16),
        compiler_params=_cparams("arbitrary", "arbitrary", "arbitrary"),
        name="attn_window",
    )(q, k, v, sink_rows)


def _out0_kernel(oa_ref, ob_ref, woa_ref, wob_ref, x_ref, g1_ref, n2_ref, sh2_ref, sc2_ref, xn_ref, h2_ref, *, d_model):
    y = (jnp.dot(oa_ref[0], woa_ref[...], preferred_element_type=F32)
         + jnp.dot(ob_ref[0], wob_ref[...], preferred_element_type=F32))
    xn = x_ref[0] + g1_ref[0] * y
    xn_ref[0] = xn
    h2_ref[0] = _modulate(xn, n2_ref, sc2_ref, sh2_ref, d_model).astype(BF16)


def _out0(oa, ob, w_o, x_all, mods, rows, norm2):
    nb, total, d = x_all.shape
    tm = TOK_TILE
    woa = _pad_rows(w_o[:A_Q_HEADS * HEAD_DIM], A_Q_HEADS, HEAD_DIM).astype(BF16)
    wob = _pad_rows(w_o[A_Q_HEADS * HEAD_DIM:], B_HEADS, B_V).astype(BF16)
    tile = lambda w: pl.BlockSpec((1, tm, w), lambda b, t: (b, t, 0))
    return pl.pallas_call(
        functools.partial(_out0_kernel, d_model=d),
        grid=(nb, total // tm),
        in_specs=[tile(oa.shape[2]), tile(ob.shape[2]), _full(woa), _full(wob), tile(d),
                  _mod_spec(0, 2, rows, nb, d), _full(norm2), _mod_spec(0, 3, rows, nb, d),
                  _mod_spec(0, 4, rows, nb, d)],
        out_specs=[tile(d), tile(d)],
        out_shape=[jax.ShapeDtypeStruct((nb, total, d), F32), jax.ShapeDtypeStruct((nb, total, d), BF16)],
        compiler_params=_cparams("arbitrary", "arbitrary"),
        name="out_proj0",
    )(oa, ob, woa, wob, x_all, mods, norm2, mods, mods)


def _top_rows(sc, rowf, k):
    n = sc.shape[0]
    vals, idxs = [], []
    work = sc
    for _ in range(k):
        m = jnp.max(work, axis=0, keepdims=True)
        idx = jnp.min(jnp.where(work == m, rowf, float(n)), axis=0, keepdims=True)
        vals.append(m)
        idxs.append(idx)
        work = jnp.where(rowf == idx, -jnp.inf, work)
    return vals, idxs


def _stack_rows(rows, row16):
    out = jnp.zeros(row16.shape, F32)
    for k, r in enumerate(rows):
        out = jnp.where(row16 == float(k), r, out)
    return out


def _candidates(v0, s1, slab_rows):
    return jnp.concatenate([v0[k1] + s1[0:slab_rows[k1], :] for k1 in range(len(v0))], axis=0)


def _select_exact(sc0, sc1, rowf, row16, flat, slab_rows, topk):
    nk, ts = sc0.shape
    v0, i0 = _top_rows(sc0, rowf, topk)
    v1, i1 = _top_rows(sc1, rowf, topk)
    work = _candidates(v0, _stack_rows(v1, row16), slab_rows)
    cnt = jnp.zeros((topk, ts), F32)
    zsum = jnp.zeros((1, ts), F32)
    best0 = None
    for k in range(topk):
        m = jnp.max(work, axis=0, keepdims=True)
        idx = jnp.min(jnp.where(work == m, flat, 1e9), axis=0, keepdims=True)
        work = jnp.where(flat == idx, -jnp.inf, work)
        best0 = m if best0 is None else best0
        zsum = zsum + jnp.exp(m - best0)
        cnt = cnt + jnp.where(row16 == jnp.floor(idx * (1.0 / topk)), 1.0, 0.0)
    cc = jnp.zeros((nk, ts), F32)
    rb = jnp.full((nk, ts), 99.0, F32)
    for k in range(topk):
        ck = jnp.sum(jnp.where(row16 == float(k), cnt, 0.0), axis=0, keepdims=True)
        cc = jnp.where(rowf == i0[k], ck, cc)
        rb = jnp.where(rowf == i1[k], float(k), rb)
    return cc, rb, zsum


def _select_fast(sc0, sc1, row16, slab_rows, topk):
    nk, ts = sc0.shape
    ninf = -jnp.inf
    count = lambda hit: jnp.sum(jnp.where(hit, 1.0, 0.0), axis=0, keepdims=True)
    work, v0 = sc0, []
    for _ in range(topk):
        m = jnp.max(work, axis=0, keepdims=True)
        v0.append(m)
        work = jnp.where(work == m, ninf, work)
    bad = count(work == ninf) != float(topk)
    work, v1 = sc1, []
    rb = jnp.full((nk, ts), 99.0, F32)
    for k in range(topk):
        m = jnp.max(work, axis=0, keepdims=True)
        v1.append(m)
        hit = work == m
        work = jnp.where(hit, ninf, work)
        rb = jnp.where(hit, float(k), rb)
    bad = jnp.logical_or(bad, count(rb < 99.0) != float(topk))
    work = _candidates(v0, _stack_rows(v1, row16), slab_rows)
    zsum = jnp.zeros((1, ts), F32)
    best0 = None
    for _ in range(topk):
        m = jnp.max(work, axis=0, keepdims=True)
        work = jnp.where(work == m, ninf, work)
        best0 = m if best0 is None else best0
        zsum = zsum + jnp.exp(m - best0)
    chosen = jnp.where(work == ninf, 1.0, 0.0)
    cc = jnp.zeros((nk, ts), F32)
    total = jnp.zeros((1, ts), F32)
    off = 0
    for k1 in range(topk):
        ck = jnp.sum(chosen[off:off + slab_rows[k1], :], axis=0, keepdims=True)
        off += slab_rows[k1]
        total = total + ck
        cc = jnp.where(sc0 == v0[k1], ck, cc)
    bad = jnp.logical_or(bad, total != float(topk))
    return cc, rb, zsum, jnp.max(jnp.where(bad, 1.0, 0.0))


def _peer_select_kernel(h_ref, wq_ref, keys_ref, cc_ref, e0_ref, rb_ref, e1_ref, q_sc, *, n_heads, topk):
    nk = PEER_N_KEYS
    ts = h_ref.shape[0]
    q_sc[...] = lax.dot_general(wq_ref[...], h_ref[...], (((1,), (1,)), ((), ())), preferred_element_type=F32)
    rowf = lax.broadcasted_iota(jnp.int32, (nk, ts), 0).astype(F32)
    row16 = lax.broadcasted_iota(jnp.int32, (topk, ts), 0).astype(F32)
    slab_rows = [topk] + [SUBLANE] * (topk - 1)
    n_cand = sum(slab_rows)
    ci = lax.broadcasted_iota(jnp.int32, (n_cand, ts), 0)
    rest = ci - topk
    flat = jnp.where(ci < topk, ci, (1 + (rest >> 3)) * topk + (rest & 7)).astype(F32)

    def scores(hp):
        qhp = q_sc[pl.ds(pl.multiple_of(hp * nk, nk), nk), :].astype(BF16)
        return jnp.dot(keys_ref[hp], qhp, preferred_element_type=F32)

    def heads_body(it, carry):
        heads = [it * SELECT_HEADS + u for u in range(SELECT_HEADS)]
        scs = [(scores(hd * 2), scores(hd * 2 + 1)) for hd in heads]
        fast = [_select_fast(sc0, sc1, row16, slab_rows, topk) for sc0, sc1 in scs]
        for hd, (sc0, sc1), (cc, rb, zsum, tie) in zip(heads, scs, fast):
            cc, rb, zsum = lax.cond(tie > 0.0,
                                    lambda: _select_exact(sc0, sc1, rowf, row16, flat, slab_rows, topk),
                                    lambda: (cc, rb, zsum))
            cc_ref[0, hd] = cc
            rb_ref[0, hd] = rb.astype(BF16)
            e0_ref[0, hd] = jnp.exp(sc0 - jnp.max(sc0, axis=0, keepdims=True))
            e1_ref[0, hd] = (jnp.exp(sc1 - jnp.max(sc1, axis=0, keepdims=True)) / zsum).astype(BF16)
        return carry

    lax.fori_loop(0, n_heads // SELECT_HEADS, heads_body, 0)


def _peer_select(h2, w_q, keys):
    t, d = h2.shape
    ts = PEER_SEL_TILE
    nh, nk = PEER_HEADS, PEER_N_KEYS
    wq_t = w_q.T.astype(BF16)
    keys2 = keys.reshape(nh * 2, nk, PEER_D_KEY // 2).astype(BF16)
    row_out = jax.ShapeDtypeStruct((t // ts, nh, nk, ts), F32)
    col_out = jax.ShapeDtypeStruct((t // ts, nh, nk, ts), BF16)
    ospec = pl.BlockSpec((1, nh, nk, ts), lambda i: (i, 0, 0, 0))
    return pl.pallas_call(
        functools.partial(_peer_select_kernel, n_heads=nh, topk=PEER_TOPK),
        grid=(t // ts,),
        in_specs=[pl.BlockSpec((ts, d), lambda i: (i, 0)), _full(wq_t), _full(keys2)],
        out_specs=[ospec] * 4,
        out_shape=[row_out, row_out, col_out, col_out],
        scratch_shapes=[pltpu.VMEM((wq_t.shape[0], ts), F32)],
        compiler_params=_cparams("arbitrary"),
        name="peer_select",
    )(h2, wq_t, keys2)


def _peer_apply_kernel(*refs, n_heads, final):
    if final:
        (h_ref, u_hbm, vt_hbm, cc_ref, e0_ref, rb_ref, e1_ref, x_ref, g_ref, o_ref,
         acc_ref, g_sc, p_sc, ht_sc, u_buf, vt_buf, sem) = refs
    else:
        (h_ref, u_hbm, vt_hbm, cc_ref, e0_ref, rb_ref, e1_ref, o_ref,
         acc_ref, g_sc, p_sc, ht_sc, u_buf, vt_buf, sem) = refs
    tp, et, hh = pl.program_id(0), pl.program_id(1), pl.program_id(2)
    nk = PEER_N_KEYS
    rows_per_tile = cc_ref.shape[2]
    tw = cc_ref.shape[3]
    tok = pl.ds(pl.multiple_of(hh * tw, tw), tw)
    te = u_buf.shape[1]
    n_et = pl.num_programs(1) - 1
    last = n_et

    def u_copy(tile):
        return pltpu.make_async_copy(u_hbm.at[pl.ds(pl.multiple_of(tile * te, te), te), :], u_buf.at[tile % 2],
                                     sem.at[0, tile % 2])

    def vt_copy(tile):
        return pltpu.make_async_copy(vt_hbm.at[tile], vt_buf.at[tile % 2], sem.at[1, tile % 2])

    @pl.when(hh == 0)
    def _():
        @pl.when(jnp.logical_and(tp == 0, et == 0))
        def _():
            u_copy(0).start()

        @pl.when(et < n_et)
        def _():
            u_copy(et).wait()

        @pl.when(et >= 1)
        def _():
            vt_copy(et - 1).wait()

        @pl.when(et + 1 < n_et)
        def _():
            u_copy(et + 1).start()

        @pl.when(et < n_et)
        def _():
            vt_copy(et).start()

        @pl.when(jnp.logical_and(et == n_et, tp + 1 < pl.num_programs(0)))
        def _():
            u_copy(0).start()

    @pl.when(et == 0)
    def _():
        ht_sc[hh] = h_ref[tok, :].astype(F32).T.astype(BF16)

    def stage1():
        return jnp.dot(u_buf[et % 2], ht_sc[hh], preferred_element_type=F32)

    def stage1_store(at):
        g_sc[hh * 2 + et % 2] = jax.nn.gelu(at.astype(BF16))

    def stage2():
        zero = jnp.zeros((), BF16)
        prev = hh * 2 + (et + 1) % 2
        for ii in range(rows_per_tile):
            ccr = [jnp.broadcast_to(cc_ref[hh, hd, ii:ii + 1, :], (BF16_ROWS, tw)).astype(BF16)
                   for hd in range(n_heads)]
            e0r = [jnp.broadcast_to(e0_ref[hh, hd, ii:ii + 1, :], (BF16_ROWS, tw)).astype(BF16)
                   for hd in range(n_heads)]
            for s0 in range(0, nk, BF16_ROWS):
                rws = slice(s0, s0 + BF16_ROWS)
                w = None
                for hd in range(n_heads):
                    term = jnp.where(rb_ref[hh, hd, rws, :] < ccr[hd], e1_ref[hh, hd, rws, :], zero) * e0r[hd]
                    w = term if w is None else w + term
                r0 = ii * nk + s0
                p_sc[r0:r0 + BF16_ROWS, :] = w * g_sc[prev, r0:r0 + BF16_ROWS, :]
        acc_ref[hh] += jnp.dot(vt_buf[(et + 1) % 2], p_sc[...], preferred_element_type=F32)

    @pl.when(et == 0)
    def _():
        acc_ref[hh] = jnp.zeros(acc_ref.shape[1:], F32)
        stage1_store(stage1())

    @pl.when(jnp.logical_and(et > 0, et < last))
    def _():
        at = stage1()
        stage2()
        stage1_store(at)

    @pl.when(et == last)
    def _():
        stage2()

    @pl.when(et == last)
    def _():
        f = acc_ref[hh].T
        if final:
            o_ref[tok, :] = x_ref[tok, :] + g_ref[0] * f
        else:
            o_ref[tok, :] = f


def _peer_apply(h2, sel, u_tab, v_tab, x=None, mods=None, mod_index=None, tokens_per_batch=None):
    t, d = h2.shape
    cc, e0, rb, e1 = sel
    nh, nk = PEER_HEADS, PEER_N_KEYS
    tt, te = PEER_TOK_TILE, PEER_EXP_TILE
    ti = te // nk
    n_et = u_tab.shape[0] // te
    u = u_tab.astype(BF16)
    vt = v_tab.reshape(n_et, te, d).transpose(0, 2, 1).astype(BF16)
    final = x is not None
    tw = cc.shape[3]
    halves = tt // tw
    assert tt % tw == 0 and tw == PEER_SEL_TILE
    prv = lambda e: jnp.maximum(e - 1, 0)
    row_spec = pl.BlockSpec((halves, nh, ti, tw), lambda i, e, hh: (i, 0, prv(e), 0))
    col_spec = pl.BlockSpec((halves, nh, nk, tw), lambda i, e, hh: (i, 0, 0, 0))
    in_specs = [pl.BlockSpec((tt, d), lambda i, e, hh: (i, 0)),
                pl.BlockSpec(memory_space=pl.ANY), pl.BlockSpec(memory_space=pl.ANY),
                row_spec, row_spec, col_spec, col_spec]
    args = [h2, u, vt, cc, e0, rb, e1]
    if final:
        per = tokens_per_batch // tt
        in_specs += [pl.BlockSpec((tt, d), lambda i, e, hh: (i, 0)),
                     pl.BlockSpec((1, 1, d), lambda i, e, hh: (mod_index(i // per), 0, 0))]
        args += [x, mods]
    return pl.pallas_call(
        functools.partial(_peer_apply_kernel, n_heads=nh, final=final),
        grid=(t // tt, n_et + 1, halves),
        in_specs=in_specs,
        out_specs=pl.BlockSpec((tt, d), lambda i, e, hh: (i, 0)),
        out_shape=jax.ShapeDtypeStruct((t, d), F32),
        scratch_shapes=[pltpu.VMEM((halves, d, tw), F32), pltpu.VMEM((halves * 2, te, tw), BF16),
                        pltpu.VMEM((te, tw), BF16), pltpu.VMEM((halves, d, tw), BF16),
                        pltpu.VMEM((2, te, d), BF16), pltpu.VMEM((2, d, te), BF16),
                        pltpu.SemaphoreType.DMA((2, 2))],
        compiler_params=_cparams("arbitrary", "arbitrary", "arbitrary"),
        name="peer_apply_final" if final else "peer_apply",
    )(*args)


def _proj1_kernel(x_ref, f_ref, g2_ref, n1_ref, sh_ref, sc_ref, w_ref, dqn_ref, dkn_ref, ca_ref, ua_ref, da_ref,
                  xn_ref, qr_ref, kr_ref, vr_ref, gr_ref, qd_ref, kd_ref, vd_ref, *, d_model, half_a, offs):
    xn = x_ref[0] + g2_ref[0] * f_ref[0]
    xn_ref[0] = xn
    h = _modulate(xn, n1_ref, sc_ref, sh_ref, d_model).astype(BF16)
    ca, ua, da = ca_ref[...], ua_ref[...], da_ref[...]
    o_qr, o_kr, o_vr, o_gr, o_qd, o_kd, o_vd, o_end = offs

    z = jnp.dot(h, w_ref[:, o_qr:o_kr], preferred_element_type=F32)
    for i in range(C_HEADS):
        qr_ref[0, :, i * LANE:(i + 1) * LANE] = _rope(z[:, i * LANE:(i + 1) * LANE], ca, ua, da, half_a).astype(BF16)
    z = jnp.dot(h, w_ref[:, o_kr:o_vr], preferred_element_type=F32) * (C_DK ** -0.5)
    for i in range(C_HEADS):
        kr_ref[0, :, i * LANE:(i + 1) * LANE] = _rope(z[:, i * LANE:(i + 1) * LANE], ca, ua, da, half_a).astype(BF16)
    vr_ref[0] = jnp.dot(h, w_ref[:, o_vr:o_gr], preferred_element_type=F32).astype(BF16)
    gr_ref[0] = jnp.dot(h, w_ref[:, o_gr:o_qd], preferred_element_type=F32).astype(BF16)
    z = jnp.dot(h, w_ref[:, o_qd:o_kd], preferred_element_type=F32)
    for i in range(D_Q_HEADS):
        y = _rms_rows(z[:, i * LANE:(i + 1) * LANE], HEAD_DIM) * dqn_ref[...]
        qd_ref[0, :, i * LANE:(i + 1) * LANE] = (_rope(y, ca, ua, da, half_a) * HEAD_DIM ** -0.5).astype(BF16)
    z = jnp.dot(h, w_ref[:, o_kd:o_vd], preferred_element_type=F32)
    for i in range(D_KV_HEADS):
        y = _rms_rows(z[:, i * LANE:(i + 1) * LANE], HEAD_DIM) * dkn_ref[...]
        kd_ref[0, :, i * LANE:(i + 1) * LANE] = _rope(y, ca, ua, da, half_a).astype(BF16)
    vd = jnp.dot(h, w_ref[:, o_vd:o_end], preferred_element_type=F32)
    vd_ref[0] = _with_ones_lane(vd, D_KV_HEADS).astype(BF16)


def _proj1(x_all, f_all, mods, rows, norm1, w_in, d_qn, d_kn, tabs_a, half_a):
    nb, total, d = x_all.shape
    tm = TOK_TILE
    cqk, cv = C_HEADS * C_DK, C_HEADS * C_DV
    wqr, wkr, wvr, wgr, wqd, wkd, wvd = jnp.split(
        w_in, np.cumsum([cqk, cqk, cv, cv, D_Q_HEADS * HEAD_DIM, D_KV_HEADS * HEAD_DIM]).tolist(), axis=1)
    parts = [_pad_cols(wqr, C_HEADS, C_DK), _pad_cols(wkr, C_HEADS, C_DK), wvr, wgr,
             _pad_cols(wqd, D_Q_HEADS, HEAD_DIM), _pad_cols(wkd, D_KV_HEADS, HEAD_DIM),
             _pad_cols(wvd, D_KV_HEADS, HEAD_DIM)]
    offs = tuple(int(o) for o in np.cumsum([0] + [p.shape[1] for p in parts]))
    w_all = jnp.concatenate(parts, axis=1).astype(BF16)
    consts = [w_all, _pad_gain(d_qn, HEAD_DIM), _pad_gain(d_kn, HEAD_DIM)]
    tab_spec = pl.BlockSpec((tm, LANE), lambda b, t: (t, 0))
    tile = lambda w: pl.BlockSpec((1, tm, w), lambda b, t: (b, t, 0))
    widths = [p.shape[1] for p in parts]
    return pl.pallas_call(
        functools.partial(_proj1_kernel, d_model=d, half_a=half_a, offs=offs),
        grid=(nb, total // tm),
        in_specs=[tile(d), tile(d), _mod_spec(0, 5, rows, nb, d), _full(norm1),
                  _mod_spec(1, 0, rows, nb, d), _mod_spec(1, 1, rows, nb, d)]
                 + [_full(c) for c in consts] + [tab_spec] * 3,
        out_specs=[tile(d)] + [tile(w) for w in widths],
        out_shape=[jax.ShapeDtypeStruct((nb, total, d), F32)]
                  + [jax.ShapeDtypeStruct((nb, total, w), BF16) for w in widths],
        compiler_params=_cparams("arbitrary", "arbitrary"),
        name="proj1",
    )(x_all, f_all, mods, norm1, mods, mods, *consts, *tabs_a)


def _retention_kernel(lg_ref, qf_ref, kf_ref, vf_ref, qb_ref, kb_ref, vb_ref, of_ref, ob_ref, st_ref, dec_ref, *,
                      n_heads):
    step = pl.program_id(1)
    c = CHUNK

    @pl.when(step == 0)
    def _():
        st_ref[...] = jnp.zeros_like(st_ref)
        ri = lax.broadcasted_iota(jnp.int32, (c, LANE), 0).astype(F32)
        diff = ri - lax.broadcasted_iota(jnp.int32, (c, LANE), 1).astype(F32)
        for d in range(2):
            for hd in range(n_heads):
                lg = lg_ref[d, hd]
                if d == 0:
                    dec_ref[d, hd, 0] = jnp.where(diff >= 0, jnp.exp(lg * jnp.maximum(diff, 0.0)), 0.0)
                    dec_ref[d, hd, 1] = jnp.exp(lg * (ri + 1.0))
                    dec_ref[d, hd, 2] = jnp.exp(lg * (c - 1.0 - ri))
                else:
                    dec_ref[d, hd, 0] = jnp.where(diff <= 0, jnp.exp(lg * jnp.maximum(-diff, 0.0)), 0.0)
                    dec_ref[d, hd, 1] = jnp.exp(lg * (c - ri))
                    dec_ref[d, hd, 2] = jnp.exp(lg * ri)

    for d, (q_ref, k_ref, v_ref, o_ref) in enumerate(((qf_ref, kf_ref, vf_ref, of_ref),
                                                      (qb_ref, kb_ref, vb_ref, ob_ref))):
        for hd in range(n_heads):
            sl = slice(hd * LANE, (hd + 1) * LANE)
            q, k, v = q_ref[0, :, sl], k_ref[0, :, sl], v_ref[0, :, sl]
            s = lax.dot_general(q, k, _NT, preferred_element_type=F32) * dec_ref[d, hd, 0]
            inner = jnp.dot(s.astype(BF16), v, preferred_element_type=F32)
            st = st_ref[d, hd]
            cross = jnp.dot(q, st.astype(BF16), preferred_element_type=F32) * dec_ref[d, hd, 1]
            o_ref[0, :, sl] = inner + cross
            kd_t = (k.astype(F32) * dec_ref[d, hd, 2]).T.astype(BF16)
            st_ref[d, hd] = st * jnp.exp(lg_ref[d, hd] * c) + jnp.dot(kd_t, v, preferred_element_type=F32)


def _retention(qr, kr, vr, lg, ctx_len):
    nb, total, w = qr.shape
    nh = w // LANE
    c = CHUNK
    nc, nctx = total // c, ctx_len // c
    fwd = pl.BlockSpec((1, c, w), lambda b, s: (b, s, 0))

    def bmap(b, s):
        return (b, jnp.where(s < nctx, nctx - 1 - s, nc - 1 - (s - nctx)), 0)

    bwd = pl.BlockSpec((1, c, w), bmap)
    out = jax.ShapeDtypeStruct((nb, total, w), F32)
    return pl.pallas_call(
        functools.partial(_retention_kernel, n_heads=nh),
        grid=(nb, nc),
        in_specs=[pl.BlockSpec(memory_space=pltpu.SMEM), fwd, fwd, fwd, bwd, bwd, bwd],
        out_specs=[fwd, bwd],
        out_shape=[out, out],
        scratch_shapes=[pltpu.VMEM((2, nh, LANE, LANE), F32), pltpu.VMEM((2, nh, 3, c, LANE), F32)],
        compiler_params=_cparams("arbitrary", "arbitrary"),
        name="retention",
    )(lg, qr, kr, vr, qr, kr, vr)


def _out1_kernel(of_ref, ob_ref, gr_ref, gn_ref, od_ref, wor_ref, wod_ref, x_ref, g1_ref, n2_ref, sh2_ref, sc2_ref,
                 xn_ref, h2_ref, *, d_model, n_heads):
    o = of_ref[0] + ob_ref[0]
    g = gr_ref[0].astype(F32)
    gate = g * jax.nn.sigmoid(g)
    gn = gn_ref[...]
    ys = []
    for hd in range(n_heads):
        sl = slice(hd * LANE, (hd + 1) * LANE)
        oh = o[:, sl]
        mu = jnp.mean(oh, axis=-1, keepdims=True)
        var = jnp.mean(jnp.square(oh - mu), axis=-1, keepdims=True)
        ys.append((gate[:, sl] * ((oh - mu) * lax.rsqrt(var + EPS) * gn[:, sl])).astype(BF16))
    y_ret = jnp.concatenate(ys, axis=1)
    y = (jnp.dot(y_ret, wor_ref[...], preferred_element_type=F32)
         + jnp.dot(od_ref[0], wod_ref[...], preferred_element_type=F32))
    xn = x_ref[0] + g1_ref[0] * y
    xn_ref[0] = xn
    h2_ref[0] = _modulate(xn, n2_ref, sc2_ref, sh2_ref, d_model).astype(BF16)


def _out1(o_f, o_b, g_r, gn_w, o_d, w_o, x_all, mods, rows, norm2, ctx_len):
    nb, total, d = x_all.shape
    tm = TOK_TILE
    skip = ctx_len // tm
    seq = total - ctx_len
    wor = w_o[:C_HEADS * C_DV].astype(BF16)
    wod = _pad_rows(w_o[C_HEADS * C_DV:], D_Q_HEADS, HEAD_DIM).astype(BF16)
    gn = gn_w.astype(F32).reshape(1, -1)
    tile_in = lambda w: pl.BlockSpec((1, tm, w), lambda b, t: (b, t + skip, 0))
    tile_out = pl.BlockSpec((1, tm, d), lambda b, t: (b, t, 0))
    mod = lambda chunk: pl.BlockSpec((1, 1, d), lambda b, t: ((rows + b) * 6 + chunk, 0, 0))
    return pl.pallas_call(
        functools.partial(_out1_kernel, d_model=d, n_heads=C_HEADS),
        grid=(nb, seq // tm),
        in_specs=[tile_in(o_f.shape[2]), tile_in(o_b.shape[2]), tile_in(g_r.shape[2]), _full(gn),
                  tile_in(o_d.shape[2]), _full(wor), _full(wod), tile_in(d), mod(2), _full(norm2), mod(3), mod(4)],
        out_specs=[tile_out, tile_out],
        out_shape=[jax.ShapeDtypeStruct((nb, seq, d), F32), jax.ShapeDtypeStruct((nb, seq, d), BF16)],
        compiler_params=_cparams("arbitrary", "arbitrary"),
        name="out_proj1",
    )(o_f, o_b, g_r, gn, o_d, wor, wod, x_all, mods, norm2, mods, mods)


def kernel(x, c, ctx, c_ctx, ada_w, ada_b, norm1_w, norm2_w, ab_w_in, ab_w_o, a_q_norm, a_k_norm, a_sink,
           b_q_lora_norm, b_kv_lora_norm, b_w_uq, b_w_ukv, b_q_norm, b_k_norm, cd_w_in, cd_w_o, c_decay_fwd,
           c_decay_bwd, c_gn_w, d_q_norm, d_k_norm, peer_w_q, peer_keys, peer_u, peer_v):
    nb, seq, d = x.shape
    ctx_len = ctx.shape[1]
    total = ctx_len + seq
    assert ctx_len == TOK_TILE and seq % TOK_TILE == 0 and seq % GRID_W == 0

    rows = -(-(nb + 1) // SUBLANE) * SUBLANE
    c_rows = jnp.concatenate([c, c_ctx[None, :], jnp.zeros((rows - nb - 1, d), c.dtype)], axis=0).astype(F32)
    mods = _ada(c_rows, ada_w, ada_b).reshape(-1, 1, d)

    tabs_a, half_a = _rope_tables(ctx_len, seq, 0, HEAD_DIM)
    tabs_b, half_b = _rope_tables(ctx_len, seq, B_NOPE, B_ROPE)
    n1 = norm1_w.astype(F32).reshape(-1, 1, d)
    n2 = norm2_w.astype(F32).reshape(-1, 1, d)

    x_all = jnp.concatenate([ctx, x], axis=1).astype(F32)
    qa, ka, va, qb, kb, vb = _proj0(x_all, mods, rows, n1[0], ab_w_in[0], b_w_uq[0], b_w_ukv[0], a_q_norm[0],
                                    a_k_norm[0], b_q_lora_norm[0], b_kv_lora_norm[0], b_q_norm[0], b_k_norm[0],
                                    tabs_a, tabs_b, half_a, half_b)
    o_a = _attn_window(qa, ka, va, a_sink[0], ctx_len=ctx_len)
    o_b = _attn_dense(qb, kb, vb, ctx_len=ctx_len)
    x_all, h2 = _out0(o_a, o_b, ab_w_o[0], x_all, mods, rows, n2[0])
    h2 = h2.reshape(nb * total, d)
    sel = _peer_select(h2, peer_w_q[0], peer_keys[0])
    f = _peer_apply(h2, sel, peer_u[0], peer_v[0]).reshape(nb, total, d)

    x_all, qr, kr, vr, gr, qd, kd, vd = _proj1(x_all, f, mods, rows, n1[1], cd_w_in[0], d_q_norm[0], d_k_norm[0],
                                               tabs_a, half_a)
    lg = jnp.stack([jax.nn.log_sigmoid(c_decay_fwd[0].astype(F32)), jax.nn.log_sigmoid(c_decay_bwd[0].astype(F32))])
    o_f, o_bw = _retention(qr, kr, vr, lg, ctx_len)
    o_d = _attn_dense(qd, kd, vd, ctx_len=ctx_len)
    x_lat, h2 = _out1(o_f, o_bw, gr, c_gn_w[0], o_d, cd_w_o[0], x_all, mods, rows, n2[1], ctx_len)
    h2 = h2.reshape(nb * seq, d)
    sel = _peer_select(h2, peer_w_q[1], peer_keys[1])
    out = _peer_apply(h2, sel, peer_u[1], peer_v[1], x=x_lat.reshape(nb * seq, d), mods=mods,
                      mod_index=lambda b: (rows + b) * 6 + 5, tokens_per_batch=seq)
    return out.reshape(nb, seq, d).astype(x.dtype)
```

```python
import functools

import numpy as np
import jax
import jax.numpy as jnp
from jax import lax
from jax.experimental import pallas as pl
from jax.experimental.pallas import tpu as pltpu

F32 = jnp.float32
BF16 = jnp.bfloat16

GRID_W = 64
ROPE_THETA = 10000.0
EPS = 1e-6
NEG_INF = -1e30
HEAD_DIM = 64
A_Q_HEADS, A_KV_HEADS, A_WINDOW = 8, 2, 128
B_HEADS, B_NOPE, B_ROPE, B_V, B_Q_RANK, B_KV_RANK = 8, 64, 32, 64, 256, 256
B_QK = B_NOPE + B_ROPE
C_HEADS, C_DK, C_DV = 4, 64, 128
D_Q_HEADS, D_KV_HEADS = 8, 2
PEER_HEADS, PEER_N_KEYS, PEER_D_KEY, PEER_TOPK = 8, 128, 256, 16

LANE = 128
SUBLANE = 8
BF16_ROWS = 16
ONES_LANE = LANE - 1
VMEM_LIMIT = 56 * 1024 * 1024

TOK_TILE = 256
PROJ_TILE = 256
CHUNK = 128
ATTN_KV_PER_STEP = 2
ATTN_CHAINS_PER_STEP = 4
PEER_SEL_TILE = 256
SELECT_HEADS = 4
PEER_TOK_TILE = 512
PEER_EXP_TILE = 2048


def _cparams(*sem):
    return pltpu.CompilerParams(dimension_semantics=sem, vmem_limit_bytes=VMEM_LIMIT)


def _full(arr):
    nd = arr.ndim
    return pl.BlockSpec(arr.shape, lambda *_: (0,) * nd)


def _pad_cols(w, n_heads, d):
    lead = w.shape[:-1]
    w = w.reshape(lead + (n_heads, d))
    w = jnp.pad(w, [(0, 0)] * len(lead) + [(0, 0), (0, LANE - d)])
    return w.reshape(lead + (n_heads * LANE,))


def _pad_rows(w, n_heads, d):
    n = w.shape[-1]
    w = w.reshape(n_heads, d, n)
    w = jnp.pad(w, [(0, 0), (0, LANE - d), (0, 0)])
    return w.reshape(n_heads * LANE, n)


def _pad_gain(g, d):
    return jnp.pad(g.astype(F32), (0, LANE - d)).reshape(1, LANE)


def _rope_tables(ctx_len, seq, lane_off, d_rot):
    blk = d_rot // 2
    half = blk // 2
    freqs = ROPE_THETA ** (-np.arange(half, dtype=np.float64) / half)
    pos = np.arange(seq)
    total = ctx_len + seq
    cos = np.ones((total, LANE), np.float64)
    sup = np.zeros((total, LANE), np.float64)
    sdn = np.zeros((total, LANE), np.float64)
    for axis, p in enumerate((pos // GRID_W, pos % GRID_W)):
        ang = p[:, None].astype(np.float64) * freqs[None, :]
        c, s = np.cos(ang), np.sin(ang)
        base = lane_off + axis * blk
        cos[ctx_len:, base:base + half] = c
        cos[ctx_len:, base + half:base + blk] = c
        sdn[ctx_len:, base:base + half] = -s
        sup[ctx_len:, base + half:base + blk] = s
    return (jnp.asarray(cos, F32), jnp.asarray(sup, F32), jnp.asarray(sdn, F32)), half


def _rms_rows(x, true_dim):
    return x * lax.rsqrt(jnp.sum(x * x, axis=-1, keepdims=True) * (1.0 / true_dim) + EPS)


def _rms_head(x, true_dim):
    ss = jnp.dot((x * x).astype(BF16), jnp.ones((LANE, LANE), BF16), preferred_element_type=F32)
    return x * lax.rsqrt(ss * (1.0 / true_dim) + EPS)


def _rope(y, cos, sup, sdn, half):
    return y * cos + pltpu.roll(y, half, 1) * sup + pltpu.roll(y, LANE - half, 1) * sdn


def _ada_kernel(c_ref, w_ref, b_ref, o_ref):
    c = c_ref[...]
    s = c * jax.nn.sigmoid(c)
    o_ref[0] = jnp.dot(s.astype(BF16), w_ref[0].astype(BF16), preferred_element_type=F32) + b_ref[0]


def _ada(c_rows, ada_w, ada_b):
    depth, d, n = ada_w.shape
    rows = c_rows.shape[0]
    tn = 1536
    return pl.pallas_call(
        _ada_kernel,
        grid=(depth, n // tn),
        in_specs=[pl.BlockSpec((rows, d), lambda l, j: (0, 0)),
                  pl.BlockSpec((1, d, tn), lambda l, j: (l, 0, j)),
                  pl.BlockSpec((1, 1, tn), lambda l, j: (l, 0, j))],
        out_specs=pl.BlockSpec((1, rows, tn), lambda l, j: (l, 0, j)),
        out_shape=jax.ShapeDtypeStruct((depth, rows, n), F32),
        compiler_params=_cparams("arbitrary", "arbitrary"),
        name="ada_mod",
    )(c_rows, ada_w, ada_b.reshape(depth, 1, n))


def _mod_spec(layer, chunk, rows, nb, d, ctx_tiles=1):
    def imap(b, t):
        r = jnp.where(t < ctx_tiles, nb, b)
        return ((layer * rows + r) * 6 + chunk, 0, 0)
    return pl.BlockSpec((1, 1, d), imap)


def _modulate(x, n_ref, sc_ref, sh_ref, d):
    return _rms_rows(x, d) * n_ref[...] * (1.0 + sc_ref[0]) + sh_ref[0]


def _proj0_kernel(x_ref, n1_ref, sh_ref, sc_ref, w_ref, wuq_ref, wuk_ref, wuv_ref,
                  aqn_ref, akn_ref, bqln_ref, bkvln_ref, bqn_ref, bkn_ref,
                  ca_ref, ua_ref, da_ref, cb_ref, ub_ref, db_ref,
                  qa_ref, ka_ref, va_ref, qb_ref, kb_ref, vb_ref, *, d_model, half_a, half_b, offs):
    h = _modulate(x_ref[0], n1_ref, sc_ref, sh_ref, d_model).astype(BF16)
    ca, ua, da = ca_ref[...], ua_ref[...], da_ref[...]
    cb, ub, db = cb_ref[...], ub_ref[...], db_ref[...]
    o_qa, o_ka, o_va, o_cq, o_ckv, o_kr, o_end = offs

    z = jnp.dot(h, w_ref[:, o_qa:o_ka], preferred_element_type=F32)
    for i in range(A_Q_HEADS):
        y = _rms_head(z[:, i * LANE:(i + 1) * LANE], HEAD_DIM) * aqn_ref[...]
        qa_ref[0, :, i * LANE:(i + 1) * LANE] = (_rope(y, ca, ua, da, half_a) * HEAD_DIM ** -0.5).astype(BF16)
    z = jnp.dot(h, w_ref[:, o_ka:o_va], preferred_element_type=F32)
    for i in range(A_KV_HEADS):
        y = _rms_head(z[:, i * LANE:(i + 1) * LANE], HEAD_DIM) * akn_ref[...]
        ka_ref[0, :, i * LANE:(i + 1) * LANE] = _rope(y, ca, ua, da, half_a).astype(BF16)
    va = jnp.dot(h, w_ref[:, o_va:o_cq], preferred_element_type=F32)
    va_ref[0] = _with_ones_lane(va, A_KV_HEADS).astype(BF16)

    cq = jnp.dot(h, w_ref[:, o_cq:o_ckv], preferred_element_type=F32)
    cq = (_rms_rows(cq, B_Q_RANK) * bqln_ref[...]).astype(BF16)
    z = jnp.dot(cq, wuq_ref[...], preferred_element_type=F32)
    for i in range(B_HEADS):
        y = _rms_head(z[:, i * LANE:(i + 1) * LANE], B_QK) * bqn_ref[...]
        qb_ref[0, :, i * LANE:(i + 1) * LANE] = (_rope(y, cb, ub, db, half_b) * B_QK ** -0.5).astype(BF16)

    ckv = jnp.dot(h, w_ref[:, o_ckv:o_kr], preferred_element_type=F32)
    ckv = (_rms_rows(ckv, B_KV_RANK) * bkvln_ref[...]).astype(BF16)
    kr = jnp.dot(h, w_ref[:, o_kr:o_end], preferred_element_type=F32)
    z = jnp.dot(ckv, wuk_ref[...], preferred_element_type=F32)
    for i in range(B_HEADS):
        y = _rms_head(z[:, i * LANE:(i + 1) * LANE] + kr, B_QK) * bkn_ref[...]
        kb_ref[0, :, i * LANE:(i + 1) * LANE] = _rope(y, cb, ub, db, half_b).astype(BF16)
    vb = jnp.dot(ckv, wuv_ref[...], preferred_element_type=F32)
    vb_ref[0] = _with_ones_lane(vb, B_HEADS).astype(BF16)


def _proj0(x_all, mods, rows, norm1, w_in, b_wuq, b_wukv, a_qn, a_kn, b_qln, b_kvln, b_qn, b_kn, tabs_a, tabs_b,
           half_a, half_b):
    nb, total, d = x_all.shape
    tm = PROJ_TILE
    wq, wk, wv, wcq, wckv, wkr = jnp.split(
        w_in, np.cumsum([A_Q_HEADS * HEAD_DIM, A_KV_HEADS * HEAD_DIM, A_KV_HEADS * HEAD_DIM, B_Q_RANK, B_KV_RANK])
        .tolist(), axis=1)
    kr_pad = jnp.pad(wkr, ((0, 0), (B_NOPE, LANE - B_QK)))
    parts = [_pad_cols(wq, A_Q_HEADS, HEAD_DIM), _pad_cols(wk, A_KV_HEADS, HEAD_DIM),
             _pad_cols(wv, A_KV_HEADS, HEAD_DIM), wcq, wckv, kr_pad]
    offs = tuple(int(o) for o in np.cumsum([0] + [p.shape[1] for p in parts]))
    w_all = jnp.concatenate(parts, axis=1).astype(BF16)
    wuq = _pad_cols(b_wuq, B_HEADS, B_QK).astype(BF16)
    wukv = b_wukv.reshape(B_KV_RANK, B_HEADS, B_NOPE + B_V)
    wuk = _pad_cols(wukv[..., :B_NOPE].reshape(B_KV_RANK, -1), B_HEADS, B_NOPE).astype(BF16)
    wuv = _pad_cols(wukv[..., B_NOPE:].reshape(B_KV_RANK, -1), B_HEADS, B_V).astype(BF16)
    consts = [w_all, wuq, wuk, wuv, _pad_gain(a_qn, HEAD_DIM), _pad_gain(a_kn, HEAD_DIM),
              b_qln.astype(F32).reshape(1, -1), b_kvln.astype(F32).reshape(1, -1),
              _pad_gain(b_qn, B_QK), _pad_gain(b_kn, B_QK)]
    tab_spec = pl.BlockSpec((tm, LANE), lambda b, t: (t, 0))
    wide = lambda nh: pl.BlockSpec((1, tm, nh * LANE), lambda b, t: (b, t, 0))
    shp = lambda nh: jax.ShapeDtypeStruct((nb, total, nh * LANE), BF16)
    return pl.pallas_call(
        functools.partial(_proj0_kernel, d_model=d, half_a=half_a, half_b=half_b, offs=offs),
        grid=(nb, total // tm),
        in_specs=[pl.BlockSpec((1, tm, d), lambda b, t: (b, t, 0)), _full(norm1),
                  _mod_spec(0, 0, rows, nb, d, TOK_TILE // tm), _mod_spec(0, 1, rows, nb, d, TOK_TILE // tm)]
                 + [_full(c) for c in consts] + [tab_spec] * 6,
        out_specs=[wide(A_Q_HEADS), wide(A_KV_HEADS), wide(A_KV_HEADS), wide(B_HEADS), wide(B_HEADS), wide(B_HEADS)],
        out_shape=[shp(A_Q_HEADS), shp(A_KV_HEADS), shp(A_KV_HEADS), shp(B_HEADS), shp(B_HEADS), shp(B_HEADS)],
        compiler_params=_cparams("arbitrary", "arbitrary"),
        name="proj0",
    )(x_all, norm1, mods, mods, *consts, *tabs_a, *tabs_b)


_NT = (((1,), (1,)), ((), ()))


def _stack_heads(q_ref, grp, j=0):
    h0 = j * grp
    if grp == 1:
        return q_ref[0, :, h0 * LANE:(h0 + 1) * LANE]
    return jnp.concatenate([q_ref[0, :, (h0 + g) * LANE:(h0 + g + 1) * LANE] for g in range(grp)], axis=0)


def _softmax_av(scores, values, sink=None):
    m = None
    for s in scores:
        ms = jnp.max(s, axis=-1, keepdims=True)
        m = ms if m is None else jnp.maximum(m, ms)
    if sink is not None:
        m = jnp.maximum(m, sink)
    o = None
    for s, v in zip(scores, values):
        os_ = jnp.dot(jnp.exp(s - m).astype(BF16), v, preferred_element_type=F32)
        o = os_ if o is None else o + os_
    den = o[:, ONES_LANE:ONES_LANE + 1]
    if sink is not None:
        den = den + jnp.exp(sink - m)
    return o / den


def _with_ones_lane(v, n_heads):
    lane = lax.broadcasted_iota(jnp.int32, (1, n_heads * LANE), 1) % LANE
    return v + jnp.where(lane == ONES_LANE, 1.0, 0.0)


def _unstack_store(o, o_ref, grp, tq, j=0):
    for g in range(grp):
        h = j * grp + g
        o_ref[0, :, h * LANE:(h + 1) * LANE] = o[g * tq:(g + 1) * tq].astype(BF16)


def _attn_dense_kernel(q_ref, k_ref, v_ref, o_ref, *, grp, nq, ctx_len, tq):
    qt = pl.program_id(2)

    def chains(keys):
        for j in range(nq):
            qj = slice(j * LANE, (j + 1) * LANE)
            kj = slice((j // grp) * LANE, (j // grp + 1) * LANE)
            s = lax.dot_general(q_ref[0, :, qj], k_ref[0, keys, kj], _NT, preferred_element_type=F32)
            o_ref[0, :, qj] = _softmax_av([s], [v_ref[0, keys, kj]]).astype(BF16)

    @pl.when(qt * tq < ctx_len)
    def _():
        chains(slice(0, ctx_len))

    @pl.when(qt * tq >= ctx_len)
    def _():
        chains(slice(None))


def _attn_dense(q, k, v, *, ctx_len):
    nb, total, qw = q.shape
    hq, hkv = qw // LANE, k.shape[2] // LANE
    grp = hq // hkv
    nq = min(ATTN_CHAINS_PER_STEP, hq)
    nkv = max(1, nq // grp)
    tq = TOK_TILE
    assert ctx_len % tq == 0 and hq % nq == 0 and (nq % grp == 0 or grp % nq == 0)
    return pl.pallas_call(
        functools.partial(_attn_dense_kernel, grp=grp, nq=nq, ctx_len=ctx_len, tq=tq),
        grid=(nb, hq // nq, total // tq),
        in_specs=[pl.BlockSpec((1, tq, nq * LANE), lambda b, h, t: (b, t, h)),
                  pl.BlockSpec((1, total, nkv * LANE), lambda b, h, t: (b, 0, (h * nq // grp) // nkv)),
                  pl.BlockSpec((1, total, nkv * LANE), lambda b, h, t: (b, 0, (h * nq // grp) // nkv))],
        out_specs=pl.BlockSpec((1, tq, nq * LANE), lambda b, h, t: (b, t, h)),
        out_shape=jax.ShapeDtypeStruct(q.shape, BF16),
        compiler_params=_cparams("arbitrary", "arbitrary", "arbitrary"),
        name="attn_dense",
    )(q, k, v)


def _attn_window_kernel(q_ref, k_ref, v_ref, sink_ref, o_ref, *, grp, kvs, ctx_len, tq, total, window):
    qt = pl.program_id(2)

    def sink_col(j):
        return jnp.concatenate([jnp.broadcast_to(sink_ref[j * grp + g][:, 0:1], (tq, 1)) for g in range(grp)], axis=0)

    @pl.when(qt * tq < ctx_len)
    def _():
        for j in range(kvs):
            kj = slice(j * LANE, (j + 1) * LANE)
            s_c = lax.dot_general(_stack_heads(q_ref, grp, j), k_ref[0, 0:ctx_len, kj], _NT,
                                  preferred_element_type=F32)
            _unstack_store(_softmax_av([s_c], [v_ref[0, 0:ctx_len, kj]], sink_col(j)), o_ref, grp, tq, j)

    @pl.when(qt * tq >= ctx_len)
    def _():
        slab = 3 * tq
        start = pl.multiple_of(jnp.clip((qt - 1) * tq, ctx_len, total - slab), tq)
        qpos = qt * tq + lax.broadcasted_iota(jnp.int32, (tq, slab), 0)
        kpos = start + lax.broadcasted_iota(jnp.int32, (tq, slab), 1)
        bias = jnp.where(jnp.abs(qpos - kpos) <= window, 0.0, NEG_INF)
        bias = jnp.concatenate([bias] * grp, axis=0)
        for j in range(kvs):
            kj = slice(j * LANE, (j + 1) * LANE)
            q = _stack_heads(q_ref, grp, j)
            s_c = lax.dot_general(q, k_ref[0, 0:ctx_len, kj], _NT, preferred_element_type=F32)
            s_l = lax.dot_general(q, k_ref[0, pl.ds(start, slab), kj], _NT, preferred_element_type=F32) + bias
            o = _softmax_av([s_c, s_l], [v_ref[0, 0:ctx_len, kj], v_ref[0, pl.ds(start, slab), kj]], sink_col(j))
            _unstack_store(o, o_ref, grp, tq, j)


def _attn_window(q, k, v, sink, *, ctx_len):
    nb, total, qw = q.shape
    hq, hkv = qw // LANE, k.shape[2] // LANE
    grp = hq // hkv
    kvs = ATTN_KV_PER_STEP
    tq = CHUNK
    assert hkv % kvs == 0
    sink_rows = jnp.broadcast_to(sink.astype(F32).reshape(hq, 1, 1), (hq, 1, LANE))
    return pl.pallas_call(
        functools.partial(_attn_window_kernel, grp=grp, kvs=kvs, ctx_len=ctx_len, tq=tq, total=total,
                          window=A_WINDOW),
        grid=(nb, hkv // kvs, total // tq),
        in_specs=[pl.BlockSpec((1, tq, kvs * grp * LANE), lambda b, h, t: (b, t, h)),
                  pl.BlockSpec((1, total, kvs * LANE), lambda b, h, t: (b, 0, h)),
                  pl.BlockSpec((1, total, kvs * LANE), lambda b, h, t: (b, 0, h)),
                  pl.BlockSpec((kvs * grp, 1, LANE), lambda b, h, t: (h, 0, 0))],
        out_specs=pl.BlockSpec((1, tq, kvs * grp * LANE), lambda b, h, t: (b, t, h)),
        out_shape=jax.ShapeDtypeStruct(q.shape, BF16),
        compiler_params=_cparams("arbitrary", "arbitrary", "arbitrary"),
        name="attn_window",
    )(q, k, v, sink_rows)


def _out0_kernel(oa_ref, ob_ref, woa_ref, wob_ref, x_ref, g1_ref, n2_ref, sh2_ref, sc2_ref, xn_ref, h2_ref, *, d_model):
    y = (jnp.dot(oa_ref[0], woa_ref[...], preferred_element_type=F32)
         + jnp.dot(ob_ref[0], wob_ref[...], preferred_element_type=F32))
    xn = x_ref[0] + g1_ref[0] * y
    xn_ref[0] = xn
    h2_ref[0] = _modulate(xn, n2_ref, sc2_ref, sh2_ref, d_model).astype(BF16)


def _out0(oa, ob, w_o, x_all, mods, rows, norm2):
    nb, total, d = x_all.shape
    tm = TOK_TILE
    woa = _pad_rows(w_o[:A_Q_HEADS * HEAD_DIM], A_Q_HEADS, HEAD_DIM).astype(BF16)
    wob = _pad_rows(w_o[A_Q_HEADS * HEAD_DIM:], B_HEADS, B_V).astype(BF16)
    tile = lambda w: pl.BlockSpec((1, tm, w), lambda b, t: (b, t, 0))
    return pl.pallas_call(
        functools.partial(_out0_kernel, d_model=d),
        grid=(nb, total // tm),
        in_specs=[tile(oa.shape[2]), tile(ob.shape[2]), _full(woa), _full(wob), tile(d),
                  _mod_spec(0, 2, rows, nb, d), _full(norm2), _mod_spec(0, 3, rows, nb, d),
                  _mod_spec(0, 4, rows, nb, d)],
        out_specs=[tile(d), tile(d)],
        out_shape=[jax.ShapeDtypeStruct((nb, total, d), F32), jax.ShapeDtypeStruct((nb, total, d), BF16)],
        compiler_params=_cparams("arbitrary", "arbitrary"),
        name="out_proj0",
    )(oa, ob, woa, wob, x_all, mods, norm2, mods, mods)


def _top_rows(sc, rowf, k):
    n = sc.shape[0]
    vals, idxs = [], []
    work = sc
    for _ in range(k):
        m = jnp.max(work, axis=0, keepdims=True)
        idx = jnp.min(jnp.where(work == m, rowf, float(n)), axis=0, keepdims=True)
        vals.append(m)
        idxs.append(idx)
        work = jnp.where(rowf == idx, -jnp.inf, work)
    return vals, idxs


def _stack_rows(rows, row16):
    out = jnp.zeros(row16.shape, F32)
    for k, r in enumerate(rows):
        out = jnp.where(row16 == float(k), r, out)
    return out


def _candidates(v0, s1, slab_rows):
    return jnp.concatenate([v0[k1] + s1[0:slab_rows[k1], :] for k1 in range(len(v0))], axis=0)


def _select_exact(sc0, sc1, rowf, row16, flat, slab_rows, topk):
    nk, ts = sc0.shape
    v0, i0 = _top_rows(sc0, rowf, topk)
    v1, i1 = _top_rows(sc1, rowf, topk)
    work = _candidates(v0, _stack_rows(v1, row16), slab_rows)
    cnt = jnp.zeros((topk, ts), F32)
    zsum = jnp.zeros((1, ts), F32)
    best0 = None
    for k in range(topk):
        m = jnp.max(work, axis=0, keepdims=True)
        idx = jnp.min(jnp.where(work == m, flat, 1e9), axis=0, keepdims=True)
        work = jnp.where(flat == idx, -jnp.inf, work)
        best0 = m if best0 is None else best0
        zsum = zsum + jnp.exp(m - best0)
        cnt = cnt + jnp.where(row16 == jnp.floor(idx * (1.0 / topk)), 1.0, 0.0)
    cc = jnp.zeros((nk, ts), F32)
    rb = jnp.full((nk, ts), 99.0, F32)
    for k in range(topk):
        ck = jnp.sum(jnp.where(row16 == float(k), cnt, 0.0), axis=0, keepdims=True)
        cc = jnp.where(rowf == i0[k], ck, cc)
        rb = jnp.where(rowf == i1[k], float(k), rb)
    return cc, rb, zsum


def _select_fast(sc0, sc1, row16, slab_rows, topk):
    nk, ts = sc0.shape
    ninf = -jnp.inf
    count = lambda hit: jnp.sum(jnp.where(hit, 1.0, 0.0), axis=0, keepdims=True)
    work, v0 = sc0, []
    for _ in range(topk):
        m = jnp.max(work, axis=0, keepdims=True)
        v0.append(m)
        work = jnp.where(work == m, ninf, work)
    bad = count(work == ninf) != float(topk)
    work, v1 = sc1, []
    rb = jnp.full((nk, ts), 99.0, F32)
    for k in range(topk):
        m = jnp.max(work, axis=0, keepdims=True)
        v1.append(m)
        hit = work == m
        work = jnp.where(hit, ninf, work)
        rb = jnp.where(hit, float(k), rb)
    bad = jnp.logical_or(bad, count(rb < 99.0) != float(topk))
    work = _candidates(v0, _stack_rows(v1, row16), slab_rows)
    zsum = jnp.zeros((1, ts), F32)
    best0 = None
    for _ in range(topk):
        m = jnp.max(work, axis=0, keepdims=True)
        work = jnp.where(work == m, ninf, work)
        best0 = m if best0 is None else best0
        zsum = zsum + jnp.exp(m - best0)
    chosen = jnp.where(work == ninf, 1.0, 0.0)
    cc = jnp.zeros((nk, ts), F32)
    total = jnp.zeros((1, ts), F32)
    off = 0
    for k1 in range(topk):
        ck = jnp.sum(chosen[off:off + slab_rows[k1], :], axis=0, keepdims=True)
        off += slab_rows[k1]
        total = total + ck
        cc = jnp.where(sc0 == v0[k1], ck, cc)
    bad = jnp.logical_or(bad, total != float(topk))
    return cc, rb, zsum, jnp.max(jnp.where(bad, 1.0, 0.0))


def _peer_select_kernel(h_ref, wq_ref, keys_ref, cc_ref, e0_ref, rb_ref, e1_ref, q_sc, *, n_heads, topk):
    nk = PEER_N_KEYS
    ts = h_ref.shape[0]
    q_sc[...] = lax.dot_general(wq_ref[...], h_ref[...], (((1,), (1,)), ((), ())), preferred_element_type=F32)
    rowf = lax.broadcasted_iota(jnp.int32, (nk, ts), 0).astype(F32)
    row16 = lax.broadcasted_iota(jnp.int32, (topk, ts), 0).astype(F32)
    slab_rows = [topk] + [SUBLANE] * (topk - 1)
    n_cand = sum(slab_rows)
    ci = lax.broadcasted_iota(jnp.int32, (n_cand, ts), 0)
    rest = ci - topk
    flat = jnp.where(ci < topk, ci, (1 + (rest >> 3)) * topk + (rest & 7)).astype(F32)

    def scores(hp):
        qhp = q_sc[pl.ds(pl.multiple_of(hp * nk, nk), nk), :].astype(BF16)
        return jnp.dot(keys_ref[hp], qhp, preferred_element_type=F32)

    def heads_body(it, carry):
        heads = [it * SELECT_HEADS + u for u in range(SELECT_HEADS)]
        scs = [(scores(hd * 2), scores(hd * 2 + 1)) for hd in heads]
        fast = [_select_fast(sc0, sc1, row16, slab_rows, topk) for sc0, sc1 in scs]
        for hd, (sc0, sc1), (cc, rb, zsum, tie) in zip(heads, scs, fast):
            cc, rb, zsum = lax.cond(tie > 0.0,
                                    lambda: _select_exact(sc0, sc1, rowf, row16, flat, slab_rows, topk),
                                    lambda: (cc, rb, zsum))
            cc_ref[0, hd] = cc
            rb_ref[0, hd] = rb.astype(BF16)
            e0_ref[0, hd] = jnp.exp(sc0 - jnp.max(sc0, axis=0, keepdims=True))
            e1_ref[0, hd] = (jnp.exp(sc1 - jnp.max(sc1, axis=0, keepdims=True)) / zsum).astype(BF16)
        return carry

    lax.fori_loop(0, n_heads // SELECT_HEADS, heads_body, 0)


def _peer_select(h2, w_q, keys):
    t, d = h2.shape
    ts = PEER_SEL_TILE
    nh, nk = PEER_HEADS, PEER_N_KEYS
    wq_t = w_q.T.astype(BF16)
    keys2 = keys.reshape(nh * 2, nk, PEER_D_KEY // 2).astype(BF16)
    row_out = jax.ShapeDtypeStruct((t // ts, nh, nk, ts), F32)
    col_out = jax.ShapeDtypeStruct((t // ts, nh, nk, ts), BF16)
    ospec = pl.BlockSpec((1, nh, nk, ts), lambda i: (i, 0, 0, 0))
    return pl.pallas_call(
        functools.partial(_peer_select_kernel, n_heads=nh, topk=PEER_TOPK),
        grid=(t // ts,),
        in_specs=[pl.BlockSpec((ts, d), lambda i: (i, 0)), _full(wq_t), _full(keys2)],
        out_specs=[ospec] * 4,
        out_shape=[row_out, row_out, col_out, col_out],
        scratch_shapes=[pltpu.VMEM((wq_t.shape[0], ts), F32)],
        compiler_params=_cparams("arbitrary"),
        name="peer_select",
    )(h2, wq_t, keys2)


def _peer_apply_kernel(*refs, n_heads, final):
    if final:
        (h_ref, u_hbm, vt_hbm, cc_ref, e0_ref, rb_ref, e1_ref, x_ref, g_ref, o_ref,
         acc_ref, g_sc, p_sc, ht_sc, u_buf, vt_buf, sem) = refs
    else:
        (h_ref, u_hbm, vt_hbm, cc_ref, e0_ref, rb_ref, e1_ref, o_ref,
         acc_ref, g_sc, p_sc, ht_sc, u_buf, vt_buf, sem) = refs
    tp, et, hh = pl.program_id(0), pl.program_id(1), pl.program_id(2)
    nk = PEER_N_KEYS
    rows_per_tile = cc_ref.shape[2]
    tw = cc_ref.shape[3]
    tok = pl.ds(pl.multiple_of(hh * tw, tw), tw)
    te = u_buf.shape[1]
    n_et = pl.num_programs(1) - 1
    last = n_et

    def u_copy(tile):
        return pltpu.make_async_copy(u_hbm.at[pl.ds(pl.multiple_of(tile * te, te), te), :], u_buf.at[tile % 2],
                                     sem.at[0, tile % 2])

    def vt_copy(tile):
        return pltpu.make_async_copy(vt_hbm.at[tile], vt_buf.at[tile % 2], sem.at[1, tile % 2])

    @pl.when(hh == 0)
    def _():
        @pl.when(jnp.logical_and(tp == 0, et == 0))
        def _():
            u_copy(0).start()

        @pl.when(et < n_et)
        def _():
            u_copy(et).wait()

        @pl.when(et >= 1)
        def _():
            vt_copy(et - 1).wait()

        @pl.when(et + 1 < n_et)
        def _():
            u_copy(et + 1).start()

        @pl.when(et < n_et)
        def _():
            vt_copy(et).start()

        @pl.when(jnp.logical_and(et == n_et, tp + 1 < pl.num_programs(0)))
        def _():
            u_copy(0).start()

    @pl.when(et == 0)
    def _():
        ht_sc[hh] = h_ref[tok, :].astype(F32).T.astype(BF16)

    def stage1():
        return jnp.dot(u_buf[et % 2], ht_sc[hh], preferred_element_type=F32)

    def stage1_store(at):
        g_sc[hh * 2 + et % 2] = jax.nn.gelu(at.astype(BF16))

    def stage2():
        zero = jnp.zeros((), BF16)
        prev = hh * 2 + (et + 1) % 2
        for ii in range(rows_per_tile):
            ccr = [jnp.broadcast_to(cc_ref[hh, hd, ii:ii + 1, :], (BF16_ROWS, tw)).astype(BF16)
                   for hd in range(n_heads)]
            e0r = [jnp.broadcast_to(e0_ref[hh, hd, ii:ii + 1, :], (BF16_ROWS, tw)).astype(BF16)
                   for hd in range(n_heads)]
            for s0 in range(0, nk, BF16_ROWS):
                rws = slice(s0, s0 + BF16_ROWS)
                w = None
                for hd in range(n_heads):
                    term = jnp.where(rb_ref[hh, hd, rws, :] < ccr[hd], e1_ref[hh, hd, rws, :], zero) * e0r[hd]
                    w = term if w is None else w + term
                r0 = ii * nk + s0
                p_sc[r0:r0 + BF16_ROWS, :] = w * g_sc[prev, r0:r0 + BF16_ROWS, :]
        acc_ref[hh] += jnp.dot(vt_buf[(et + 1) % 2], p_sc[...], preferred_element_type=F32)

    @pl.when(et == 0)
    def _():
        acc_ref[hh] = jnp.zeros(acc_ref.shape[1:], F32)
        stage1_store(stage1())

    @pl.when(jnp.logical_and(et > 0, et < last))
    def _():
        at = stage1()
        stage2()
        stage1_store(at)

    @pl.when(et == last)
    def _():
        stage2()

    @pl.when(et == last)
    def _():
        f = acc_ref[hh].T
        if final:
            o_ref[tok, :] = x_ref[tok, :] + g_ref[0] * f
        else:
            o_ref[tok, :] = f


def _peer_apply(h2, sel, u_tab, v_tab, x=None, mods=None, mod_index=None, tokens_per_batch=None):
    t, d = h2.shape
    cc, e0, rb, e1 = sel
    nh, nk = PEER_HEADS, PEER_N_KEYS
    tt, te = PEER_TOK_TILE, PEER_EXP_TILE
    ti = te // nk
    n_et = u_tab.shape[0] // te
    u = u_tab.astype(BF16)
    vt = v_tab.reshape(n_et, te, d).transpose(0, 2, 1).astype(BF16)
    final = x is not None
    tw = cc.shape[3]
    halves = tt // tw
    assert tt % tw == 0 and tw == PEER_SEL_TILE
    prv = lambda e: jnp.maximum(e - 1, 0)
    row_spec = pl.BlockSpec((halves, nh, ti, tw), lambda i, e, hh: (i, 0, prv(e), 0))
    col_spec = pl.BlockSpec((halves, nh, nk, tw), lambda i, e, hh: (i, 0, 0, 0))
    in_specs = [pl.BlockSpec((tt, d), lambda i, e, hh: (i, 0)),
                pl.BlockSpec(memory_space=pl.ANY), pl.BlockSpec(memory_space=pl.ANY),
                row_spec, row_spec, col_spec, col_spec]
    args = [h2, u, vt, cc, e0, rb, e1]
    if final:
        per = tokens_per_batch // tt
        in_specs += [pl.BlockSpec((tt, d), lambda i, e, hh: (i, 0)),
                     pl.BlockSpec((1, 1, d), lambda i, e, hh: (mod_index(i // per), 0, 0))]
        args += [x, mods]
    return pl.pallas_call(
        functools.partial(_peer_apply_kernel, n_heads=nh, final=final),
        grid=(t // tt, n_et + 1, halves),
        in_specs=in_specs,
        out_specs=pl.BlockSpec((tt, d), lambda i, e, hh: (i, 0)),
        out_shape=jax.ShapeDtypeStruct((t, d), F32),
        scratch_shapes=[pltpu.VMEM((halves, d, tw), F32), pltpu.VMEM((halves * 2, te, tw), BF16),
                        pltpu.VMEM((te, tw), BF16), pltpu.VMEM((halves, d, tw), BF16),
                        pltpu.VMEM((2, te, d), BF16), pltpu.VMEM((2, d, te), BF16),
                        pltpu.SemaphoreType.DMA((2, 2))],
        compiler_params=_cparams("arbitrary", "arbitrary", "arbitrary"),
        name="peer_apply_final" if final else "peer_apply",
    )(*args)


def _proj1_kernel(x_ref, f_ref, g2_ref, n1_ref, sh_ref, sc_ref, w_ref, dqn_ref, dkn_ref, ca_ref, ua_ref, da_ref,
                  xn_ref, qr_ref, kr_ref, vr_ref, gr_ref, qd_ref, kd_ref, vd_ref, *, d_model, half_a, offs):
    xn = x_ref[0] + g2_ref[0] * f_ref[0]
    xn_ref[0] = xn
    h = _modulate(xn, n1_ref, sc_ref, sh_ref, d_model).astype(BF16)
    ca, ua, da = ca_ref[...], ua_ref[...], da_ref[...]
    o_qr, o_kr, o_vr, o_gr, o_qd, o_kd, o_vd, o_end = offs

    z = jnp.dot(h, w_ref[:, o_qr:o_kr], preferred_element_type=F32)
    for i in range(C_HEADS):
        qr_ref[0, :, i * LANE:(i + 1) * LANE] = _rope(z[:, i * LANE:(i + 1) * LANE], ca, ua, da, half_a).astype(BF16)
    z = jnp.dot(h, w_ref[:, o_kr:o_vr], preferred_element_type=F32) * (C_DK ** -0.5)
    for i in range(C_HEADS):
        kr_ref[0, :, i * LANE:(i + 1) * LANE] = _rope(z[:, i * LANE:(i + 1) * LANE], ca, ua, da, half_a).astype(BF16)
    vr_ref[0] = jnp.dot(h, w_ref[:, o_vr:o_gr], preferred_element_type=F32).astype(BF16)
    gr_ref[0] = jnp.dot(h, w_ref[:, o_gr:o_qd], preferred_element_type=F32).astype(BF16)
    z = jnp.dot(h, w_ref[:, o_qd:o_kd], preferred_element_type=F32)
    for i in range(D_Q_HEADS):
        y = _rms_rows(z[:, i * LANE:(i + 1) * LANE], HEAD_DIM) * dqn_ref[...]
        qd_ref[0, :, i * LANE:(i + 1) * LANE] = (_rope(y, ca, ua, da, half_a) * HEAD_DIM ** -0.5).astype(BF16)
    z = jnp.dot(h, w_ref[:, o_kd:o_vd], preferred_element_type=F32)
    for i in range(D_KV_HEADS):
        y = _rms_rows(z[:, i * LANE:(i + 1) * LANE], HEAD_DIM) * dkn_ref[...]
        kd_ref[0, :, i * LANE:(i + 1) * LANE] = _rope(y, ca, ua, da, half_a).astype(BF16)
    vd = jnp.dot(h, w_ref[:, o_vd:o_end], preferred_element_type=F32)
    vd_ref[0] = _with_ones_lane(vd, D_KV_HEADS).astype(BF16)


def _proj1(x_all, f_all, mods, rows, norm1, w_in, d_qn, d_kn, tabs_a, half_a):
    nb, total, d = x_all.shape
    tm = PROJ_TILE
    cqk, cv = C_HEADS * C_DK, C_HEADS * C_DV
    wqr, wkr, wvr, wgr, wqd, wkd, wvd = jnp.split(
        w_in, np.cumsum([cqk, cqk, cv, cv, D_Q_HEADS * HEAD_DIM, D_KV_HEADS * HEAD_DIM]).tolist(), axis=1)
    parts = [_pad_cols(wqr, C_HEADS, C_DK), _pad_cols(wkr, C_HEADS, C_DK), wvr, wgr,
             _pad_cols(wqd, D_Q_HEADS, HEAD_DIM), _pad_cols(wkd, D_KV_HEADS, HEAD_DIM),
             _pad_cols(wvd, D_KV_HEADS, HEAD_DIM)]
    offs = tuple(int(o) for o in np.cumsum([0] + [p.shape[1] for p in parts]))
    w_all = jnp.concatenate(parts, axis=1).astype(BF16)
    consts = [w_all, _pad_gain(d_qn, HEAD_DIM), _pad_gain(d_kn, HEAD_DIM)]
    tab_spec = pl.BlockSpec((tm, LANE), lambda b, t: (t, 0))
    tile = lambda w: pl.BlockSpec((1, tm, w), lambda b, t: (b, t, 0))
    widths = [p.shape[1] for p in parts]
    return pl.pallas_call(
        functools.partial(_proj1_kernel, d_model=d, half_a=half_a, offs=offs),
        grid=(nb, total // tm),
        in_specs=[tile(d), tile(d), _mod_spec(0, 5, rows, nb, d, TOK_TILE // tm), _full(norm1),
                  _mod_spec(1, 0, rows, nb, d, TOK_TILE // tm), _mod_spec(1, 1, rows, nb, d, TOK_TILE // tm)]
                 + [_full(c) for c in consts] + [tab_spec] * 3,
        out_specs=[tile(d)] + [tile(w) for w in widths],
        out_shape=[jax.ShapeDtypeStruct((nb, total, d), F32)]
                  + [jax.ShapeDtypeStruct((nb, total, w), BF16) for w in widths],
        compiler_params=_cparams("arbitrary", "arbitrary"),
        name="proj1",
    )(x_all, f_all, mods, norm1, mods, mods, *consts, *tabs_a)


def _retention_kernel(lg_ref, qf_ref, kf_ref, vf_ref, qb_ref, kb_ref, vb_ref, of_ref, ob_ref, st_ref, dec_ref, *,
                      n_heads):
    step = pl.program_id(1)
    c = CHUNK

    @pl.when(step == 0)
    def _():
        st_ref[...] = jnp.zeros_like(st_ref)
        ri = lax.broadcasted_iota(jnp.int32, (c, LANE), 0).astype(F32)
        diff = ri - lax.broadcasted_iota(jnp.int32, (c, LANE), 1).astype(F32)
        for d in range(2):
            for hd in range(n_heads):
                lg = lg_ref[d, hd]
                if d == 0:
                    dec_ref[d, hd, 0] = jnp.where(diff >= 0, jnp.exp(lg * jnp.maximum(diff, 0.0)), 0.0)
                    dec_ref[d, hd, 1] = jnp.exp(lg * (ri + 1.0))
                    dec_ref[d, hd, 2] = jnp.exp(lg * (c - 1.0 - ri))
                else:
                    dec_ref[d, hd, 0] = jnp.where(diff <= 0, jnp.exp(lg * jnp.maximum(-diff, 0.0)), 0.0)
                    dec_ref[d, hd, 1] = jnp.exp(lg * (c - ri))
                    dec_ref[d, hd, 2] = jnp.exp(lg * ri)

    for d, (q_ref, k_ref, v_ref, o_ref) in enumerate(((qf_ref, kf_ref, vf_ref, of_ref),
                                                      (qb_ref, kb_ref, vb_ref, ob_ref))):
        for hd in range(n_heads):
            sl = slice(hd * LANE, (hd + 1) * LANE)
            q, k, v = q_ref[0, :, sl], k_ref[0, :, sl], v_ref[0, :, sl]
            s = lax.dot_general(q, k, _NT, preferred_element_type=F32) * dec_ref[d, hd, 0]
            inner = jnp.dot(s.astype(BF16), v, preferred_element_type=F32)
            st = st_ref[d, hd]
            cross = jnp.dot(q, st.astype(BF16), preferred_element_type=F32) * dec_ref[d, hd, 1]
            o_ref[0, :, sl] = inner + cross
            kd_t = (k.astype(F32) * dec_ref[d, hd, 2]).T.astype(BF16)
            st_ref[d, hd] = st * jnp.exp(lg_ref[d, hd] * c) + jnp.dot(kd_t, v, preferred_element_type=F32)


def _retention(qr, kr, vr, lg, ctx_len):
    nb, total, w = qr.shape
    nh = w // LANE
    c = CHUNK
    nc, nctx = total // c, ctx_len // c
    fwd = pl.BlockSpec((1, c, w), lambda b, s: (b, s, 0))

    def bmap(b, s):
        return (b, jnp.where(s < nctx, nctx - 1 - s, nc - 1 - (s - nctx)), 0)

    bwd = pl.BlockSpec((1, c, w), bmap)
    out = jax.ShapeDtypeStruct((nb, total, w), F32)
    return pl.pallas_call(
        functools.partial(_retention_kernel, n_heads=nh),
        grid=(nb, nc),
        in_specs=[pl.BlockSpec(memory_space=pltpu.SMEM), fwd, fwd, fwd, bwd, bwd, bwd],
        out_specs=[fwd, bwd],
        out_shape=[out, out],
        scratch_shapes=[pltpu.VMEM((2, nh, LANE, LANE), F32), pltpu.VMEM((2, nh, 3, c, LANE), F32)],
        compiler_params=_cparams("arbitrary", "arbitrary"),
        name="retention",
    )(lg, qr, kr, vr, qr, kr, vr)


def _out1_kernel(of_ref, ob_ref, gr_ref, gn_ref, od_ref, wor_ref, wod_ref, x_ref, g1_ref, n2_ref, sh2_ref, sc2_ref,
                 xn_ref, h2_ref, *, d_model, n_heads):
    o = of_ref[0] + ob_ref[0]
    g = gr_ref[0].astype(F32)
    gate = g * jax.nn.sigmoid(g)
    gn = gn_ref[...]
    ys = []
    for hd in range(n_heads):
        sl = slice(hd * LANE, (hd + 1) * LANE)
        oh = o[:, sl]
        mu = jnp.mean(oh, axis=-1, keepdims=True)
        var = jnp.mean(jnp.square(oh - mu), axis=-1, keepdims=True)
        ys.append((gate[:, sl] * ((oh - mu) * lax.rsqrt(var + EPS) * gn[:, sl])).astype(BF16))
    y_ret = jnp.concatenate(ys, axis=1)
    y = (jnp.dot(y_ret, wor_ref[...], preferred_element_type=F32)
         + jnp.dot(od_ref[0], wod_ref[...], preferred_element_type=F32))
    xn = x_ref[0] + g1_ref[0] * y
    xn_ref[0] = xn
    h2_ref[0] = _modulate(xn, n2_ref, sc2_ref, sh2_ref, d_model).astype(BF16)


def _out1(o_f, o_b, g_r, gn_w, o_d, w_o, x_all, mods, rows, norm2, ctx_len):
    nb, total, d = x_all.shape
    tm = TOK_TILE
    skip = ctx_len // tm
    seq = total - ctx_len
    wor = w_o[:C_HEADS * C_DV].astype(BF16)
    wod = _pad_rows(w_o[C_HEADS * C_DV:], D_Q_HEADS, HEAD_DIM).astype(BF16)
    gn = gn_w.astype(F32).reshape(1, -1)
    tile_in = lambda w: pl.BlockSpec((1, tm, w), lambda b, t: (b, t + skip, 0))
    tile_out = pl.BlockSpec((1, tm, d), lambda b, t: (b, t, 0))
    mod = lambda chunk: pl.BlockSpec((1, 1, d), lambda b, t: ((rows + b) * 6 + chunk, 0, 0))
    return pl.pallas_call(
        functools.partial(_out1_kernel, d_model=d, n_heads=C_HEADS),
        grid=(nb, seq // tm),
        in_specs=[tile_in(o_f.shape[2]), tile_in(o_b.shape[2]), tile_in(g_r.shape[2]), _full(gn),
                  tile_in(o_d.shape[2]), _full(wor), _full(wod), tile_in(d), mod(2), _full(norm2), mod(3), mod(4)],
        out_specs=[tile_out, tile_out],
        out_shape=[jax.ShapeDtypeStruct((nb, seq, d), F32), jax.ShapeDtypeStruct((nb, seq, d), BF16)],
        compiler_params=_cparams("arbitrary", "arbitrary"),
        name="out_proj1",
    )(o_f, o_b, g_r, gn, o_d, wor, wod, x_all, mods, norm2, mods, mods)


def kernel(x, c, ctx, c_ctx, ada_w, ada_b, norm1_w, norm2_w, ab_w_in, ab_w_o, a_q_norm, a_k_norm, a_sink,
           b_q_lora_norm, b_kv_lora_norm, b_w_uq, b_w_ukv, b_q_norm, b_k_norm, cd_w_in, cd_w_o, c_decay_fwd,
           c_decay_bwd, c_gn_w, d_q_norm, d_k_norm, peer_w_q, peer_keys, peer_u, peer_v):
    nb, seq, d = x.shape
    ctx_len = ctx.shape[1]
    total = ctx_len + seq
    assert ctx_len == TOK_TILE and seq % TOK_TILE == 0 and seq % GRID_W == 0

    rows = -(-(nb + 1) // SUBLANE) * SUBLANE
    c_rows = jnp.concatenate([c, c_ctx[None, :], jnp.zeros((rows - nb - 1, d), c.dtype)], axis=0).astype(F32)
    mods = _ada(c_rows, ada_w, ada_b).reshape(-1, 1, d)

    tabs_a, half_a = _rope_tables(ctx_len, seq, 0, HEAD_DIM)
    tabs_b, half_b = _rope_tables(ctx_len, seq, B_NOPE, B_ROPE)
    n1 = norm1_w.astype(F32).reshape(-1, 1, d)
    n2 = norm2_w.astype(F32).reshape(-1, 1, d)

    x_all = jnp.concatenate([ctx, x], axis=1).astype(F32)
    qa, ka, va, qb, kb, vb = _proj0(x_all, mods, rows, n1[0], ab_w_in[0], b_w_uq[0], b_w_ukv[0], a_q_norm[0],
                                    a_k_norm[0], b_q_lora_norm[0], b_kv_lora_norm[0], b_q_norm[0], b_k_norm[0],
                                    tabs_a, tabs_b, half_a, half_b)
    o_a = _attn_window(qa, ka, va, a_sink[0], ctx_len=ctx_len)
    o_b = _attn_dense(qb, kb, vb, ctx_len=ctx_len)
    x_all, h2 = _out0(o_a, o_b, ab_w_o[0], x_all, mods, rows, n2[0])
    h2 = h2.reshape(nb * total, d)
    sel = _peer_select(h2, peer_w_q[0], peer_keys[0])
    f = _peer_apply(h2, sel, peer_u[0], peer_v[0]).reshape(nb, total, d)

    x_all, qr, kr, vr, gr, qd, kd, vd = _proj1(x_all, f, mods, rows, n1[1], cd_w_in[0], d_q_norm[0], d_k_norm[0],
                                               tabs_a, half_a)
    lg = jnp.stack([jax.nn.log_sigmoid(c_decay_fwd[0].astype(F32)), jax.nn.log_sigmoid(c_decay_bwd[0].astype(F32))])
    o_f, o_bw = _retention(qr, kr, vr, lg, ctx_len)
    o_d = _attn_dense(qd, kd, vd, ctx_len=ctx_len)
    x_lat, h2 = _out1(o_f, o_bw, gr, c_gn_w[0], o_d, cd_w_o[0], x_all, mods, rows, n2[1], ctx_len)
    h2 = h2.reshape(nb * seq, d)
    sel = _peer_select(h2, peer_w_q[1], peer_keys[1])
    out = _peer_apply(h2, sel, peer_u[1], peer_v[1], x=x_lat.reshape(nb * seq, d), mods=mods,
                      mod_index=lambda b: (rows + b) * 6 + 5, tokens_per_batch=seq)
    return out.reshape(nb, seq, d).astype(x.dtype)
```

```python
import functools

import numpy as np
import jax
import jax.numpy as jnp
from jax import lax
from jax.experimental import pallas as pl
from jax.experimental.pallas import tpu as pltpu

F32 = jnp.float32
BF16 = jnp.bfloat16

GRID_W = 64
ROPE_THETA = 10000.0
EPS = 1e-6
NEG_INF = -1e30
HEAD_DIM = 64
A_Q_HEADS, A_KV_HEADS, A_WINDOW = 8, 2, 128
B_HEADS, B_NOPE, B_ROPE, B_V, B_Q_RANK, B_KV_RANK = 8, 64, 32, 64, 256, 256
B_QK = B_NOPE + B_ROPE
C_HEADS, C_DK, C_DV = 4, 64, 128
D_Q_HEADS, D_KV_HEADS = 8, 2
PEER_HEADS, PEER_N_KEYS, PEER_D_KEY, PEER_TOPK = 8, 128, 256, 16

LANE = 128
SUBLANE = 8
BF16_ROWS = 16
ONES_LANE = LANE - 1
VMEM_LIMIT = 56 * 1024 * 1024

TOK_TILE = 256
PROJ_TILE = 256
CHUNK = 128
ATTN_KV_PER_STEP = 2
ATTN_CHAINS_PER_STEP = 4
ADA_COL_TILE = 1536
RANK_ABSENT = 99.0
PEER_SEL_TILE = 256
SELECT_HEADS = 4
PEER_TOK_TILE = 512
PEER_EXP_TILE = 2048


def _cparams(*sem):
    return pltpu.CompilerParams(dimension_semantics=sem, vmem_limit_bytes=VMEM_LIMIT)


def _full(arr):
    nd = arr.ndim
    return pl.BlockSpec(arr.shape, lambda *_: (0,) * nd)


def _pad_cols(w, n_heads, d):
    lead = w.shape[:-1]
    w = w.reshape(lead + (n_heads, d))
    w = jnp.pad(w, [(0, 0)] * len(lead) + [(0, 0), (0, LANE - d)])
    return w.reshape(lead + (n_heads * LANE,))


def _pad_rows(w, n_heads, d):
    n = w.shape[-1]
    w = w.reshape(n_heads, d, n)
    w = jnp.pad(w, [(0, 0), (0, LANE - d), (0, 0)])
    return w.reshape(n_heads * LANE, n)


def _pad_gain(g, d):
    return jnp.pad(g.astype(F32), (0, LANE - d)).reshape(1, LANE)


def _rope_tables(ctx_len, seq, lane_off, d_rot):
    blk = d_rot // 2
    half = blk // 2
    freqs = ROPE_THETA ** (-np.arange(half, dtype=np.float64) / half)
    pos = np.arange(seq)
    total = ctx_len + seq
    cos = np.ones((total, LANE), np.float64)
    sup = np.zeros((total, LANE), np.float64)
    sdn = np.zeros((total, LANE), np.float64)
    for axis, p in enumerate((pos // GRID_W, pos % GRID_W)):
        ang = p[:, None].astype(np.float64) * freqs[None, :]
        c, s = np.cos(ang), np.sin(ang)
        base = lane_off + axis * blk
        cos[ctx_len:, base:base + half] = c
        cos[ctx_len:, base + half:base + blk] = c
        sdn[ctx_len:, base:base + half] = -s
        sup[ctx_len:, base + half:base + blk] = s
    return (jnp.asarray(cos, F32), jnp.asarray(sup, F32), jnp.asarray(sdn, F32)), half


def _rms_rows(x, true_dim):
    return x * lax.rsqrt(jnp.sum(x * x, axis=-1, keepdims=True) * (1.0 / true_dim) + EPS)


def _rms_head(x, true_dim):
    ss = jnp.dot((x * x).astype(BF16), jnp.ones((LANE, LANE), BF16), preferred_element_type=F32)
    return x * lax.rsqrt(ss * (1.0 / true_dim) + EPS)


def _rope(y, cos, sup, sdn, half):
    return y * cos + pltpu.roll(y, half, 1) * sup + pltpu.roll(y, LANE - half, 1) * sdn


def _ada_kernel(c_ref, w_ref, b_ref, o_ref):
    c = c_ref[...]
    s = c * jax.nn.sigmoid(c)
    o_ref[0] = jnp.dot(s.astype(BF16), w_ref[0].astype(BF16), preferred_element_type=F32) + b_ref[0]


def _ada(c_rows, ada_w, ada_b):
    depth, d, n = ada_w.shape
    rows = c_rows.shape[0]
    tn = ADA_COL_TILE
    assert n % tn == 0
    return pl.pallas_call(
        _ada_kernel,
        grid=(depth, n // tn),
        in_specs=[pl.BlockSpec((rows, d), lambda l, j: (0, 0)),
                  pl.BlockSpec((1, d, tn), lambda l, j: (l, 0, j)),
                  pl.BlockSpec((1, 1, tn), lambda l, j: (l, 0, j))],
        out_specs=pl.BlockSpec((1, rows, tn), lambda l, j: (l, 0, j)),
        out_shape=jax.ShapeDtypeStruct((depth, rows, n), F32),
        compiler_params=_cparams("arbitrary", "arbitrary"),
        name="ada_mod",
    )(c_rows, ada_w, ada_b.reshape(depth, 1, n))


def _mod_spec(layer, chunk, rows, nb, d, ctx_tiles=1):
    def imap(b, t):
        r = jnp.where(t < ctx_tiles, nb, b)
        return ((layer * rows + r) * 6 + chunk, 0, 0)
    return pl.BlockSpec((1, 1, d), imap)


def _modulate(x, n_ref, sc_ref, sh_ref, d):
    return _rms_rows(x, d) * n_ref[...] * (1.0 + sc_ref[0]) + sh_ref[0]


def _proj0_kernel(x_ref, n1_ref, sh_ref, sc_ref, w_ref, wuq_ref, wuk_ref, wuv_ref,
                  aqn_ref, akn_ref, bqln_ref, bkvln_ref, bqn_ref, bkn_ref,
                  ca_ref, ua_ref, da_ref, cb_ref, ub_ref, db_ref,
                  qa_ref, ka_ref, va_ref, qb_ref, kb_ref, vb_ref, *, d_model, half_a, half_b, offs):
    h = _modulate(x_ref[0], n1_ref, sc_ref, sh_ref, d_model).astype(BF16)
    ca, ua, da = ca_ref[...], ua_ref[...], da_ref[...]
    cb, ub, db = cb_ref[...], ub_ref[...], db_ref[...]
    o_qa, o_ka, o_va, o_cq, o_ckv, o_kr, o_end = offs

    z = jnp.dot(h, w_ref[:, o_qa:o_ka], preferred_element_type=F32)
    for i in range(A_Q_HEADS):
        y = _rms_head(z[:, i * LANE:(i + 1) * LANE], HEAD_DIM) * aqn_ref[...]
        qa_ref[0, :, i * LANE:(i + 1) * LANE] = (_rope(y, ca, ua, da, half_a) * HEAD_DIM ** -0.5).astype(BF16)
    z = jnp.dot(h, w_ref[:, o_ka:o_va], preferred_element_type=F32)
    for i in range(A_KV_HEADS):
        y = _rms_head(z[:, i * LANE:(i + 1) * LANE], HEAD_DIM) * akn_ref[...]
        ka_ref[0, :, i * LANE:(i + 1) * LANE] = _rope(y, ca, ua, da, half_a).astype(BF16)
    va = jnp.dot(h, w_ref[:, o_va:o_cq], preferred_element_type=F32)
    va_ref[0] = _with_ones_lane(va, A_KV_HEADS).astype(BF16)

    cq = jnp.dot(h, w_ref[:, o_cq:o_ckv], preferred_element_type=F32)
    cq = (_rms_rows(cq, B_Q_RANK) * bqln_ref[...]).astype(BF16)
    z = jnp.dot(cq, wuq_ref[...], preferred_element_type=F32)
    for i in range(B_HEADS):
        y = _rms_head(z[:, i * LANE:(i + 1) * LANE], B_QK) * bqn_ref[...]
        qb_ref[0, :, i * LANE:(i + 1) * LANE] = (_rope(y, cb, ub, db, half_b) * B_QK ** -0.5).astype(BF16)

    ckv = jnp.dot(h, w_ref[:, o_ckv:o_kr], preferred_element_type=F32)
    ckv = (_rms_rows(ckv, B_KV_RANK) * bkvln_ref[...]).astype(BF16)
    kr = jnp.dot(h, w_ref[:, o_kr:o_end], preferred_element_type=F32)
    z = jnp.dot(ckv, wuk_ref[...], preferred_element_type=F32)
    for i in range(B_HEADS):
        y = _rms_head(z[:, i * LANE:(i + 1) * LANE] + kr, B_QK) * bkn_ref[...]
        kb_ref[0, :, i * LANE:(i + 1) * LANE] = _rope(y, cb, ub, db, half_b).astype(BF16)
    vb = jnp.dot(ckv, wuv_ref[...], preferred_element_type=F32)
    vb_ref[0] = _with_ones_lane(vb, B_HEADS).astype(BF16)


def _proj0(x_all, mods, rows, norm1, w_in, b_wuq, b_wukv, a_qn, a_kn, b_qln, b_kvln, b_qn, b_kn, tabs_a, tabs_b,
           half_a, half_b):
    nb, total, d = x_all.shape
    tm = PROJ_TILE
    wq, wk, wv, wcq, wckv, wkr = jnp.split(
        w_in, np.cumsum([A_Q_HEADS * HEAD_DIM, A_KV_HEADS * HEAD_DIM, A_KV_HEADS * HEAD_DIM, B_Q_RANK, B_KV_RANK])
        .tolist(), axis=1)
    kr_pad = jnp.pad(wkr, ((0, 0), (B_NOPE, LANE - B_QK)))
    parts = [_pad_cols(wq, A_Q_HEADS, HEAD_DIM), _pad_cols(wk, A_KV_HEADS, HEAD_DIM),
             _pad_cols(wv, A_KV_HEADS, HEAD_DIM), wcq, wckv, kr_pad]
    offs = tuple(int(o) for o in np.cumsum([0] + [p.shape[1] for p in parts]))
    w_all = jnp.concatenate(parts, axis=1).astype(BF16)
    wuq = _pad_cols(b_wuq, B_HEADS, B_QK).astype(BF16)
    wukv = b_wukv.reshape(B_KV_RANK, B_HEADS, B_NOPE + B_V)
    wuk = _pad_cols(wukv[..., :B_NOPE].reshape(B_KV_RANK, -1), B_HEADS, B_NOPE).astype(BF16)
    wuv = _pad_cols(wukv[..., B_NOPE:].reshape(B_KV_RANK, -1), B_HEADS, B_V).astype(BF16)
    consts = [w_all, wuq, wuk, wuv, _pad_gain(a_qn, HEAD_DIM), _pad_gain(a_kn, HEAD_DIM),
              b_qln.astype(F32).reshape(1, -1), b_kvln.astype(F32).reshape(1, -1),
              _pad_gain(b_qn, B_QK), _pad_gain(b_kn, B_QK)]
    tab_spec = pl.BlockSpec((tm, LANE), lambda b, t: (t, 0))
    wide = lambda nh: pl.BlockSpec((1, tm, nh * LANE), lambda b, t: (b, t, 0))
    shp = lambda nh: jax.ShapeDtypeStruct((nb, total, nh * LANE), BF16)
    return pl.pallas_call(
        functools.partial(_proj0_kernel, d_model=d, half_a=half_a, half_b=half_b, offs=offs),
        grid=(nb, total // tm),
        in_specs=[pl.BlockSpec((1, tm, d), lambda b, t: (b, t, 0)), _full(norm1),
                  _mod_spec(0, 0, rows, nb, d, TOK_TILE // tm), _mod_spec(0, 1, rows, nb, d, TOK_TILE // tm)]
                 + [_full(c) for c in consts] + [tab_spec] * 6,
        out_specs=[wide(A_Q_HEADS), wide(A_KV_HEADS), wide(A_KV_HEADS), wide(B_HEADS), wide(B_HEADS), wide(B_HEADS)],
        out_shape=[shp(A_Q_HEADS), shp(A_KV_HEADS), shp(A_KV_HEADS), shp(B_HEADS), shp(B_HEADS), shp(B_HEADS)],
        compiler_params=_cparams("arbitrary", "arbitrary"),
        name="proj0",
    )(x_all, norm1, mods, mods, *consts, *tabs_a, *tabs_b)


_NT = (((1,), (1,)), ((), ()))


def _stack_heads(q_ref, grp, j=0):
    h0 = j * grp
    if grp == 1:
        return q_ref[0, :, h0 * LANE:(h0 + 1) * LANE]
    return jnp.concatenate([q_ref[0, :, (h0 + g) * LANE:(h0 + g + 1) * LANE] for g in range(grp)], axis=0)


def _softmax_av(scores, values, sink=None):
    m = None
    for s in scores:
        ms = jnp.max(s, axis=-1, keepdims=True)
        m = ms if m is None else jnp.maximum(m, ms)
    if sink is not None:
        m = jnp.maximum(m, sink)
    o = None
    for s, v in zip(scores, values):
        os_ = jnp.dot(jnp.exp(s - m).astype(BF16), v, preferred_element_type=F32)
        o = os_ if o is None else o + os_
    den = o[:, ONES_LANE:ONES_LANE + 1]
    if sink is not None:
        den = den + jnp.exp(sink - m)
    return o / den


def _with_ones_lane(v, n_heads):
    lane = lax.broadcasted_iota(jnp.int32, (1, n_heads * LANE), 1) % LANE
    return v + jnp.where(lane == ONES_LANE, 1.0, 0.0)


def _unstack_store(o, o_ref, grp, tq, j=0):
    for g in range(grp):
        h = j * grp + g
        o_ref[0, :, h * LANE:(h + 1) * LANE] = o[g * tq:(g + 1) * tq].astype(BF16)


def _attn_dense_kernel(q_ref, k_ref, v_ref, o_ref, *, grp, nq, ctx_len, tq):
    qt = pl.program_id(2)

    def chains(keys):
        for j in range(nq):
            qj = slice(j * LANE, (j + 1) * LANE)
            kj = slice((j // grp) * LANE, (j // grp + 1) * LANE)
            s = lax.dot_general(q_ref[0, :, qj], k_ref[0, keys, kj], _NT, preferred_element_type=F32)
            o_ref[0, :, qj] = _softmax_av([s], [v_ref[0, keys, kj]]).astype(BF16)

    @pl.when(qt * tq < ctx_len)
    def _():
        chains(slice(0, ctx_len))

    @pl.when(qt * tq >= ctx_len)
    def _():
        chains(slice(None))


def _attn_dense(q, k, v, *, ctx_len):
    nb, total, qw = q.shape
    hq, hkv = qw // LANE, k.shape[2] // LANE
    grp = hq // hkv
    nq = min(ATTN_CHAINS_PER_STEP, hq)
    nkv = max(1, nq // grp)
    tq = TOK_TILE
    assert ctx_len % tq == 0 and hq % nq == 0 and (nq % grp == 0 or grp % nq == 0)
    return pl.pallas_call(
        functools.partial(_attn_dense_kernel, grp=grp, nq=nq, ctx_len=ctx_len, tq=tq),
        grid=(nb, hq // nq, total // tq),
        in_specs=[pl.BlockSpec((1, tq, nq * LANE), lambda b, h, t: (b, t, h)),
                  pl.BlockSpec((1, total, nkv * LANE), lambda b, h, t: (b, 0, (h * nq // grp) // nkv)),
                  pl.BlockSpec((1, total, nkv * LANE), lambda b, h, t: (b, 0, (h * nq // grp) // nkv))],
        out_specs=pl.BlockSpec((1, tq, nq * LANE), lambda b, h, t: (b, t, h)),
        out_shape=jax.ShapeDtypeStruct(q.shape, BF16),
        compiler_params=_cparams("arbitrary", "arbitrary", "arbitrary"),
        name="attn_dense",
    )(q, k, v)


def _attn_window_kernel(q_ref, k_ref, v_ref, sink_ref, o_ref, *, grp, kvs, ctx_len, tq, total, window):
    qt = pl.program_id(2)

    def sink_col(j):
        return jnp.concatenate([jnp.broadcast_to(sink_ref[j * grp + g][:, 0:1], (tq, 1)) for g in range(grp)], axis=0)

    @pl.when(qt * tq < ctx_len)
    def _():
        for j in range(kvs):
            kj = slice(j * LANE, (j + 1) * LANE)
            s_c = lax.dot_general(_stack_heads(q_ref, grp, j), k_ref[0, 0:ctx_len, kj], _NT,
                                  preferred_element_type=F32)
            _unstack_store(_softmax_av([s_c], [v_ref[0, 0:ctx_len, kj]], sink_col(j)), o_ref, grp, tq, j)

    @pl.when(qt * tq >= ctx_len)
    def _():
        slab = 3 * tq
        start = pl.multiple_of(jnp.clip((qt - 1) * tq, ctx_len, total - slab), tq)
        qpos = qt * tq + lax.broadcasted_iota(jnp.int32, (tq, slab), 0)
        kpos = start + lax.broadcasted_iota(jnp.int32, (tq, slab), 1)
        bias = jnp.where(jnp.abs(qpos - kpos) <= window, 0.0, NEG_INF)
        bias = jnp.concatenate([bias] * grp, axis=0)
        for j in range(kvs):
            kj = slice(j * LANE, (j + 1) * LANE)
            q = _stack_heads(q_ref, grp, j)
            s_c = lax.dot_general(q, k_ref[0, 0:ctx_len, kj], _NT, preferred_element_type=F32)
            s_l = lax.dot_general(q, k_ref[0, pl.ds(start, slab), kj], _NT, preferred_element_type=F32) + bias
            o = _softmax_av([s_c, s_l], [v_ref[0, 0:ctx_len, kj], v_ref[0, pl.ds(start, slab), kj]], sink_col(j))
            _unstack_store(o, o_ref, grp, tq, j)


def _attn_window(q, k, v, sink, *, ctx_len):
    nb, total, qw = q.shape
    hq, hkv = qw // LANE, k.shape[2] // LANE
    grp = hq // hkv
    kvs = ATTN_KV_PER_STEP
    tq = CHUNK
    assert hkv % kvs == 0
    sink_rows = jnp.broadcast_to(sink.astype(F32).reshape(hq, 1, 1), (hq, 1, LANE))
    return pl.pallas_call(
        functools.partial(_attn_window_kernel, grp=grp, kvs=kvs, ctx_len=ctx_len, tq=tq, total=total,
                          window=A_WINDOW),
        grid=(nb, hkv // kvs, total // tq),
        in_specs=[pl.BlockSpec((1, tq, kvs * grp * LANE), lambda b, h, t: (b, t, h)),
                  pl.BlockSpec((1, total, kvs * LANE), lambda b, h, t: (b, 0, h)),
                  pl.BlockSpec((1, total, kvs * LANE), lambda b, h, t: (b, 0, h)),
                  pl.BlockSpec((kvs * grp, 1, LANE), lambda b, h, t: (h, 0, 0))],
        out_specs=pl.BlockSpec((1, tq, kvs * grp * LANE), lambda b, h, t: (b, t, h)),
        out_shape=jax.ShapeDtypeStruct(q.shape, BF16),
        compiler_params=_cparams("arbitrary", "arbitrary", "arbitrary"),
        name="attn_window",
    )(q, k, v, sink_rows)


def _out0_kernel(oa_ref, ob_ref, woa_ref, wob_ref, x_ref, g1_ref, n2_ref, sh2_ref, sc2_ref, xn_ref, h2_ref, *, d_model):
    y = (jnp.dot(oa_ref[0], woa_ref[...], preferred_element_type=F32)
         + jnp.dot(ob_ref[0], wob_ref[...], preferred_element_type=F32))
    xn = x_ref[0] + g1_ref[0] * y
    xn_ref[0] = xn
    h2_ref[0] = _modulate(xn, n2_ref, sc2_ref, sh2_ref, d_model).astype(BF16)


def _out0(oa, ob, w_o, x_all, mods, rows, norm2):
    nb, total, d = x_all.shape
    tm = TOK_TILE
    woa = _pad_rows(w_o[:A_Q_HEADS * HEAD_DIM], A_Q_HEADS, HEAD_DIM).astype(BF16)
    wob = _pad_rows(w_o[A_Q_HEADS * HEAD_DIM:], B_HEADS, B_V).astype(BF16)
    tile = lambda w: pl.BlockSpec((1, tm, w), lambda b, t: (b, t, 0))
    return pl.pallas_call(
        functools.partial(_out0_kernel, d_model=d),
        grid=(nb, total // tm),
        in_specs=[tile(oa.shape[2]), tile(ob.shape[2]), _full(woa), _full(wob), tile(d),
                  _mod_spec(0, 2, rows, nb, d), _full(norm2), _mod_spec(0, 3, rows, nb, d),
                  _mod_spec(0, 4, rows, nb, d)],
        out_specs=[tile(d), tile(d)],
        out_shape=[jax.ShapeDtypeStruct((nb, total, d), F32), jax.ShapeDtypeStruct((nb, total, d), BF16)],
        compiler_params=_cparams("arbitrary", "arbitrary"),
        name="out_proj0",
    )(oa, ob, woa, wob, x_all, mods, norm2, mods, mods)


def _top_rows(sc, rowf, k):
    n = sc.shape[0]
    vals, idxs = [], []
    work = sc
    for _ in range(k):
        m = jnp.max(work, axis=0, keepdims=True)
        idx = jnp.min(jnp.where(work == m, rowf, float(n)), axis=0, keepdims=True)
        vals.append(m)
        idxs.append(idx)
        work = jnp.where(rowf == idx, -jnp.inf, work)
    return vals, idxs


def _stack_rows(rows, row16):
    out = jnp.zeros(row16.shape, F32)
    for k, r in enumerate(rows):
        out = jnp.where(row16 == float(k), r, out)
    return out


def _candidates(v0, s1, slab_rows):
    return jnp.concatenate([v0[k1] + s1[0:slab_rows[k1], :] for k1 in range(len(v0))], axis=0)


def _select_exact(sc0, sc1, rowf, row16, flat, slab_rows, topk):
    nk, ts = sc0.shape
    v0, i0 = _top_rows(sc0, rowf, topk)
    v1, i1 = _top_rows(sc1, rowf, topk)
    work = _candidates(v0, _stack_rows(v1, row16), slab_rows)
    cnt = jnp.zeros((topk, ts), F32)
    zsum = jnp.zeros((1, ts), F32)
    best0 = None
    for k in range(topk):
        m = jnp.max(work, axis=0, keepdims=True)
        idx = jnp.min(jnp.where(work == m, flat, float(topk * topk)), axis=0, keepdims=True)
        work = jnp.where(flat == idx, -jnp.inf, work)
        best0 = m if best0 is None else best0
        zsum = zsum + jnp.exp(m - best0)
        cnt = cnt + jnp.where(row16 == jnp.floor(idx * (1.0 / topk)), 1.0, 0.0)
    cc = jnp.zeros((nk, ts), F32)
    rb = jnp.full((nk, ts), RANK_ABSENT, F32)
    for k in range(topk):
        ck = jnp.sum(jnp.where(row16 == float(k), cnt, 0.0), axis=0, keepdims=True)
        cc = jnp.where(rowf == i0[k], ck, cc)
        rb = jnp.where(rowf == i1[k], float(k), rb)
    return cc, rb, zsum


def _select_fast(sc0, sc1, row16, slab_rows, topk):
    nk, ts = sc0.shape
    ninf = -jnp.inf
    count = lambda hit: jnp.sum(jnp.where(hit, 1.0, 0.0), axis=0, keepdims=True)
    work, v0 = sc0, []
    for _ in range(topk):
        m = jnp.max(work, axis=0, keepdims=True)
        v0.append(m)
        work = jnp.where(work == m, ninf, work)
    bad = count(work == ninf) != float(topk)
    work, v1 = sc1, []
    rb = jnp.full((nk, ts), RANK_ABSENT, F32)
    for k in range(topk):
        m = jnp.max(work, axis=0, keepdims=True)
        v1.append(m)
        hit = work == m
        work = jnp.where(hit, ninf, work)
        rb = jnp.where(hit, float(k), rb)
    bad = jnp.logical_or(bad, count(rb < RANK_ABSENT) != float(topk))
    work = _candidates(v0, _stack_rows(v1, row16), slab_rows)
    zsum = jnp.zeros((1, ts), F32)
    best0 = None
    for _ in range(topk):
        m = jnp.max(work, axis=0, keepdims=True)
        work = jnp.where(work == m, ninf, work)
        best0 = m if best0 is None else best0
        zsum = zsum + jnp.exp(m - best0)
    chosen = jnp.where(work == ninf, 1.0, 0.0)
    cc = jnp.zeros((nk, ts), F32)
    total = jnp.zeros((1, ts), F32)
    off = 0
    for k1 in range(topk):
        ck = jnp.sum(chosen[off:off + slab_rows[k1], :], axis=0, keepdims=True)
        off += slab_rows[k1]
        total = total + ck
        cc = jnp.where(sc0 == v0[k1], ck, cc)
    bad = jnp.logical_or(bad, total != float(topk))
    return cc, rb, zsum, jnp.max(jnp.where(bad, 1.0, 0.0))


def _peer_select_kernel(h_ref, wq_ref, keys_ref, cc_ref, e0_ref, rb_ref, e1_ref, q_sc, *, n_heads, topk):
    nk = PEER_N_KEYS
    ts = h_ref.shape[0]
    q_sc[...] = lax.dot_general(wq_ref[...], h_ref[...], (((1,), (1,)), ((), ())), preferred_element_type=F32)
    rowf = lax.broadcasted_iota(jnp.int32, (nk, ts), 0).astype(F32)
    row16 = lax.broadcasted_iota(jnp.int32, (topk, ts), 0).astype(F32)
    slab_rows = [topk] + [SUBLANE] * (topk - 1)
    n_cand = sum(slab_rows)
    ci = lax.broadcasted_iota(jnp.int32, (n_cand, ts), 0)
    rest = ci - topk
    flat = jnp.where(ci < topk, ci, (1 + (rest >> 3)) * topk + (rest & 7)).astype(F32)

    def scores(hp):
        qhp = q_sc[pl.ds(pl.multiple_of(hp * nk, nk), nk), :].astype(BF16)
        return jnp.dot(keys_ref[hp], qhp, preferred_element_type=F32)

    def heads_body(it, carry):
        heads = [it * SELECT_HEADS + u for u in range(SELECT_HEADS)]
        scs = [(scores(hd * 2), scores(hd * 2 + 1)) for hd in heads]
        fast = [_select_fast(sc0, sc1, row16, slab_rows, topk) for sc0, sc1 in scs]
        for hd, (sc0, sc1), (cc, rb, zsum, tie) in zip(heads, scs, fast):
            cc, rb, zsum = lax.cond(tie > 0.0,
                                    lambda: _select_exact(sc0, sc1, rowf, row16, flat, slab_rows, topk),
                                    lambda: (cc, rb, zsum))
            cc_ref[0, hd] = cc
            rb_ref[0, hd] = rb.astype(BF16)
            e0_ref[0, hd] = jnp.exp(sc0 - jnp.max(sc0, axis=0, keepdims=True))
            e1_ref[0, hd] = (jnp.exp(sc1 - jnp.max(sc1, axis=0, keepdims=True)) / zsum).astype(BF16)
        return carry

    lax.fori_loop(0, n_heads // SELECT_HEADS, heads_body, 0)


def _peer_select(h2, w_q, keys):
    t, d = h2.shape
    ts = PEER_SEL_TILE
    nh, nk = PEER_HEADS, PEER_N_KEYS
    wq_t = w_q.T.astype(BF16)
    keys2 = keys.reshape(nh * 2, nk, PEER_D_KEY // 2).astype(BF16)
    row_out = jax.ShapeDtypeStruct((t // ts, nh, nk, ts), F32)
    col_out = jax.ShapeDtypeStruct((t // ts, nh, nk, ts), BF16)
    ospec = pl.BlockSpec((1, nh, nk, ts), lambda i: (i, 0, 0, 0))
    return pl.pallas_call(
        functools.partial(_peer_select_kernel, n_heads=nh, topk=PEER_TOPK),
        grid=(t // ts,),
        in_specs=[pl.BlockSpec((ts, d), lambda i: (i, 0)), _full(wq_t), _full(keys2)],
        out_specs=[ospec] * 4,
        out_shape=[row_out, row_out, col_out, col_out],
        scratch_shapes=[pltpu.VMEM((wq_t.shape[0], ts), F32)],
        compiler_params=_cparams("arbitrary"),
        name="peer_select",
    )(h2, wq_t, keys2)


def _peer_apply_kernel(*refs, n_heads, final):
    if final:
        (h_ref, u_hbm, vt_hbm, cc_ref, e0_ref, rb_ref, e1_ref, x_ref, g_ref, o_ref,
         acc_ref, g_sc, p_sc, ht_sc, u_buf, vt_buf, sem) = refs
    else:
        (h_ref, u_hbm, vt_hbm, cc_ref, e0_ref, rb_ref, e1_ref, o_ref,
         acc_ref, g_sc, p_sc, ht_sc, u_buf, vt_buf, sem) = refs
    tp, et, hh = pl.program_id(0), pl.program_id(1), pl.program_id(2)
    nk = PEER_N_KEYS
    rows_per_tile = cc_ref.shape[2]
    tw = cc_ref.shape[3]
    tok = pl.ds(pl.multiple_of(hh * tw, tw), tw)
    te = u_buf.shape[1]
    n_et = pl.num_programs(1) - 1
    last = n_et

    def u_copy(tile):
        return pltpu.make_async_copy(u_hbm.at[pl.ds(pl.multiple_of(tile * te, te), te), :], u_buf.at[tile % 2],
                                     sem.at[0, tile % 2])

    def vt_copy(tile):
        return pltpu.make_async_copy(vt_hbm.at[tile], vt_buf.at[tile % 2], sem.at[1, tile % 2])

    @pl.when(hh == 0)
    def _():
        @pl.when(jnp.logical_and(tp == 0, et == 0))
        def _():
            u_copy(0).start()

        @pl.when(et < n_et)
        def _():
            u_copy(et).wait()

        @pl.when(et >= 1)
        def _():
            vt_copy(et - 1).wait()

        @pl.when(et + 1 < n_et)
        def _():
            u_copy(et + 1).start()

        @pl.when(et < n_et)
        def _():
            vt_copy(et).start()

        @pl.when(jnp.logical_and(et == n_et, tp + 1 < pl.num_programs(0)))
        def _():
            u_copy(0).start()

    @pl.when(et == 0)
    def _():
        ht_sc[hh] = h_ref[tok, :].astype(F32).T.astype(BF16)

    def stage1():
        return jnp.dot(u_buf[et % 2], ht_sc[hh], preferred_element_type=F32)

    def stage1_store(at):
        g_sc[hh * 2 + et % 2] = jax.nn.gelu(at.astype(BF16))

    def stage2():
        zero = jnp.zeros((), BF16)
        prev = hh * 2 + (et + 1) % 2
        for ii in range(rows_per_tile):
            ccr = [jnp.broadcast_to(cc_ref[hh, hd, ii:ii + 1, :], (BF16_ROWS, tw)).astype(BF16)
                   for hd in range(n_heads)]
            e0r = [jnp.broadcast_to(e0_ref[hh, hd, ii:ii + 1, :], (BF16_ROWS, tw)).astype(BF16)
                   for hd in range(n_heads)]
            for s0 in range(0, nk, BF16_ROWS):
                rws = slice(s0, s0 + BF16_ROWS)
                w = None
                for hd in range(n_heads):
                    term = jnp.where(rb_ref[hh, hd, rws, :] < ccr[hd], e1_ref[hh, hd, rws, :], zero) * e0r[hd]
                    w = term if w is None else w + term
                r0 = ii * nk + s0
                p_sc[r0:r0 + BF16_ROWS, :] = w * g_sc[prev, r0:r0 + BF16_ROWS, :]
        acc_ref[hh] += jnp.dot(vt_buf[(et + 1) % 2], p_sc[...], preferred_element_type=F32)

    @pl.when(et == 0)
    def _():
        acc_ref[hh] = jnp.zeros(acc_ref.shape[1:], F32)
        stage1_store(stage1())

    @pl.when(jnp.logical_and(et > 0, et < last))
    def _():
        at = stage1()
        stage2()
        stage1_store(at)

    @pl.when(et == last)
    def _():
        stage2()

    @pl.when(et == last)
    def _():
        f = acc_ref[hh].T
        if final:
            o_ref[tok, :] = x_ref[tok, :] + g_ref[0] * f
        else:
            o_ref[tok, :] = f


def _peer_apply(h2, sel, u_tab, v_tab, x=None, mods=None, mod_index=None, tokens_per_batch=None):
    t, d = h2.shape
    cc, e0, rb, e1 = sel
    nh, nk = PEER_HEADS, PEER_N_KEYS
    tt, te = PEER_TOK_TILE, PEER_EXP_TILE
    ti = te // nk
    n_et = u_tab.shape[0] // te
    u = u_tab.astype(BF16)
    vt = v_tab.reshape(n_et, te, d).transpose(0, 2, 1).astype(BF16)
    final = x is not None
    tw = cc.shape[3]
    halves = tt // tw
    assert tt % tw == 0 and tw == PEER_SEL_TILE
    prv = lambda e: jnp.maximum(e - 1, 0)
    row_spec = pl.BlockSpec((halves, nh, ti, tw), lambda i, e, hh: (i, 0, prv(e), 0))
    col_spec = pl.BlockSpec((halves, nh, nk, tw), lambda i, e, hh: (i, 0, 0, 0))
    in_specs = [pl.BlockSpec((tt, d), lambda i, e, hh: (i, 0)),
                pl.BlockSpec(memory_space=pl.ANY), pl.BlockSpec(memory_space=pl.ANY),
                row_spec, row_spec, col_spec, col_spec]
    args = [h2, u, vt, cc, e0, rb, e1]
    if final:
        per = tokens_per_batch // tt
        in_specs += [pl.BlockSpec((tt, d), lambda i, e, hh: (i, 0)),
                     pl.BlockSpec((1, 1, d), lambda i, e, hh: (mod_index(i // per), 0, 0))]
        args += [x, mods]
    return pl.pallas_call(
        functools.partial(_peer_apply_kernel, n_heads=nh, final=final),
        grid=(t // tt, n_et + 1, halves),
        in_specs=in_specs,
        out_specs=pl.BlockSpec((tt, d), lambda i, e, hh: (i, 0)),
        out_shape=jax.ShapeDtypeStruct((t, d), F32),
        scratch_shapes=[pltpu.VMEM((halves, d, tw), F32), pltpu.VMEM((halves * 2, te, tw), BF16),
                        pltpu.VMEM((te, tw), BF16), pltpu.VMEM((halves, d, tw), BF16),
                        pltpu.VMEM((2, te, d), BF16), pltpu.VMEM((2, d, te), BF16),
                        pltpu.SemaphoreType.DMA((2, 2))],
        compiler_params=_cparams("arbitrary", "arbitrary", "arbitrary"),
        name="peer_apply_final" if final else "peer_apply",
    )(*args)


def _proj1_kernel(x_ref, f_ref, g2_ref, n1_ref, sh_ref, sc_ref, w_ref, dqn_ref, dkn_ref, ca_ref, ua_ref, da_ref,
                  xn_ref, qr_ref, kr_ref, vr_ref, gr_ref, qd_ref, kd_ref, vd_ref, *, d_model, half_a, offs):
    xn = x_ref[0] + g2_ref[0] * f_ref[0]
    xn_ref[0] = xn
    h = _modulate(xn, n1_ref, sc_ref, sh_ref, d_model).astype(BF16)
    ca, ua, da = ca_ref[...], ua_ref[...], da_ref[...]
    o_qr, o_kr, o_vr, o_gr, o_qd, o_kd, o_vd, o_end = offs

    z = jnp.dot(h, w_ref[:, o_qr:o_kr], preferred_element_type=F32)
    for i in range(C_HEADS):
        qr_ref[0, :, i * LANE:(i + 1) * LANE] = _rope(z[:, i * LANE:(i + 1) * LANE], ca, ua, da, half_a).astype(BF16)
    z = jnp.dot(h, w_ref[:, o_kr:o_vr], preferred_element_type=F32) * (C_DK ** -0.5)
    for i in range(C_HEADS):
        kr_ref[0, :, i * LANE:(i + 1) * LANE] = _rope(z[:, i * LANE:(i + 1) * LANE], ca, ua, da, half_a).astype(BF16)
    vr_ref[0] = jnp.dot(h, w_ref[:, o_vr:o_gr], preferred_element_type=F32).astype(BF16)
    gr_ref[0] = jnp.dot(h, w_ref[:, o_gr:o_qd], preferred_element_type=F32).astype(BF16)
    z = jnp.dot(h, w_ref[:, o_qd:o_kd], preferred_element_type=F32)
    for i in range(D_Q_HEADS):
        y = _rms_rows(z[:, i * LANE:(i + 1) * LANE], HEAD_DIM) * dqn_ref[...]
        qd_ref[0, :, i * LANE:(i + 1) * LANE] = (_rope(y, ca, ua, da, half_a) * HEAD_DIM ** -0.5).astype(BF16)
    z = jnp.dot(h, w_ref[:, o_kd:o_vd], preferred_element_type=F32)
    for i in range(D_KV_HEADS):
        y = _rms_rows(z[:, i * LANE:(i + 1) * LANE], HEAD_DIM) * dkn_ref[...]
        kd_ref[0, :, i * LANE:(i + 1) * LANE] = _rope(y, ca, ua, da, half_a).astype(BF16)
    vd = jnp.dot(h, w_ref[:, o_vd:o_end], preferred_element_type=F32)
    vd_ref[0] = _with_ones_lane(vd, D_KV_HEADS).astype(BF16)


def _proj1(x_all, f_all, mods, rows, norm1, w_in, d_qn, d_kn, tabs_a, half_a):
    nb, total, d = x_all.shape
    tm = PROJ_TILE
    cqk, cv = C_HEADS * C_DK, C_HEADS * C_DV
    wqr, wkr, wvr, wgr, wqd, wkd, wvd = jnp.split(
        w_in, np.cumsum([cqk, cqk, cv, cv, D_Q_HEADS * HEAD_DIM, D_KV_HEADS * HEAD_DIM]).tolist(), axis=1)
    parts = [_pad_cols(wqr, C_HEADS, C_DK), _pad_cols(wkr, C_HEADS, C_DK), wvr, wgr,
             _pad_cols(wqd, D_Q_HEADS, HEAD_DIM), _pad_cols(wkd, D_KV_HEADS, HEAD_DIM),
             _pad_cols(wvd, D_KV_HEADS, HEAD_DIM)]
    offs = tuple(int(o) for o in np.cumsum([0] + [p.shape[1] for p in parts]))
    w_all = jnp.concatenate(parts, axis=1).astype(BF16)
    consts = [w_all, _pad_gain(d_qn, HEAD_DIM), _pad_gain(d_kn, HEAD_DIM)]
    tab_spec = pl.BlockSpec((tm, LANE), lambda b, t: (t, 0))
    tile = lambda w: pl.BlockSpec((1, tm, w), lambda b, t: (b, t, 0))
    widths = [p.shape[1] for p in parts]
    return pl.pallas_call(
        functools.partial(_proj1_kernel, d_model=d, half_a=half_a, offs=offs),
        grid=(nb, total // tm),
        in_specs=[tile(d), tile(d), _mod_spec(0, 5, rows, nb, d, TOK_TILE // tm), _full(norm1),
                  _mod_spec(1, 0, rows, nb, d, TOK_TILE // tm), _mod_spec(1, 1, rows, nb, d, TOK_TILE // tm)]
                 + [_full(c) for c in consts] + [tab_spec] * 3,
        out_specs=[tile(d)] + [tile(w) for w in widths],
        out_shape=[jax.ShapeDtypeStruct((nb, total, d), F32)]
                  + [jax.ShapeDtypeStruct((nb, total, w), BF16) for w in widths],
        compiler_params=_cparams("arbitrary", "arbitrary"),
        name="proj1",
    )(x_all, f_all, mods, norm1, mods, mods, *consts, *tabs_a)


def _retention_kernel(lg_ref, qf_ref, kf_ref, vf_ref, qb_ref, kb_ref, vb_ref, of_ref, ob_ref, st_ref, dec_ref, *,
                      n_heads):
    step = pl.program_id(1)
    c = CHUNK

    @pl.when(step == 0)
    def _():
        st_ref[...] = jnp.zeros_like(st_ref)
        ri = lax.broadcasted_iota(jnp.int32, (c, LANE), 0).astype(F32)
        diff = ri - lax.broadcasted_iota(jnp.int32, (c, LANE), 1).astype(F32)
        for d in range(2):
            for hd in range(n_heads):
                lg = lg_ref[d, hd]
                if d == 0:
                    dec_ref[d, hd, 0] = jnp.where(diff >= 0, jnp.exp(lg * jnp.maximum(diff, 0.0)), 0.0)
                    dec_ref[d, hd, 1] = jnp.exp(lg * (ri + 1.0))
                    dec_ref[d, hd, 2] = jnp.exp(lg * (c - 1.0 - ri))
                else:
                    dec_ref[d, hd, 0] = jnp.where(diff <= 0, jnp.exp(lg * jnp.maximum(-diff, 0.0)), 0.0)
                    dec_ref[d, hd, 1] = jnp.exp(lg * (c - ri))
                    dec_ref[d, hd, 2] = jnp.exp(lg * ri)

    for d, (q_ref, k_ref, v_ref, o_ref) in enumerate(((qf_ref, kf_ref, vf_ref, of_ref),
                                                      (qb_ref, kb_ref, vb_ref, ob_ref))):
        for hd in range(n_heads):
            sl = slice(hd * LANE, (hd + 1) * LANE)
            q, k, v = q_ref[0, :, sl], k_ref[0, :, sl], v_ref[0, :, sl]
            s = lax.dot_general(q, k, _NT, preferred_element_type=F32) * dec_ref[d, hd, 0]
            inner = jnp.dot(s.astype(BF16), v, preferred_element_type=F32)
            st = st_ref[d, hd]
            cross = jnp.dot(q, st.astype(BF16), preferred_element_type=F32) * dec_ref[d, hd, 1]
            o_ref[0, :, sl] = inner + cross
            kd_t = (k.astype(F32) * dec_ref[d, hd, 2]).T.astype(BF16)
            st_ref[d, hd] = st * jnp.exp(lg_ref[d, hd] * c) + jnp.dot(kd_t, v, preferred_element_type=F32)


def _retention(qr, kr, vr, lg, ctx_len):
    nb, total, w = qr.shape
    nh = w // LANE
    c = CHUNK
    nc, nctx = total // c, ctx_len // c
    fwd = pl.BlockSpec((1, c, w), lambda b, s: (b, s, 0))

    def bmap(b, s):
        return (b, jnp.where(s < nctx, nctx - 1 - s, nc - 1 - (s - nctx)), 0)

    bwd = pl.BlockSpec((1, c, w), bmap)
    out = jax.ShapeDtypeStruct((nb, total, w), F32)
    return pl.pallas_call(
        functools.partial(_retention_kernel, n_heads=nh),
        grid=(nb, nc),
        in_specs=[pl.BlockSpec(memory_space=pltpu.SMEM), fwd, fwd, fwd, bwd, bwd, bwd],
        out_specs=[fwd, bwd],
        out_shape=[out, out],
        scratch_shapes=[pltpu.VMEM((2, nh, LANE, LANE), F32), pltpu.VMEM((2, nh, 3, c, LANE), F32)],
        compiler_params=_cparams("arbitrary", "arbitrary"),
        name="retention",
    )(lg, qr, kr, vr, qr, kr, vr)


def _out1_kernel(of_ref, ob_ref, gr_ref, gn_ref, od_ref, wor_ref, wod_ref, x_ref, g1_ref, n2_ref, sh2_ref, sc2_ref,
                 xn_ref, h2_ref, *, d_model, n_heads):
    o = of_ref[0] + ob_ref[0]
    g = gr_ref[0].astype(F32)
    gate = g * jax.nn.sigmoid(g)
    gn = gn_ref[...]
    ys = []
    for hd in range(n_heads):
        sl = slice(hd * LANE, (hd + 1) * LANE)
        oh = o[:, sl]
        mu = jnp.mean(oh, axis=-1, keepdims=True)
        var = jnp.mean(jnp.square(oh - mu), axis=-1, keepdims=True)
        ys.append((gate[:, sl] * ((oh - mu) * lax.rsqrt(var + EPS) * gn[:, sl])).astype(BF16))
    y_ret = jnp.concatenate(ys, axis=1)
    y = (jnp.dot(y_ret, wor_ref[...], preferred_element_type=F32)
         + jnp.dot(od_ref[0], wod_ref[...], preferred_element_type=F32))
    xn = x_ref[0] + g1_ref[0] * y
    xn_ref[0] = xn
    h2_ref[0] = _modulate(xn, n2_ref, sc2_ref, sh2_ref, d_model).astype(BF16)


def _out1(o_f, o_b, g_r, gn_w, o_d, w_o, x_all, mods, rows, norm2, ctx_len):
    nb, total, d = x_all.shape
    tm = TOK_TILE
    skip = ctx_len // tm
    seq = total - ctx_len
    wor = w_o[:C_HEADS * C_DV].astype(BF16)
    wod = _pad_rows(w_o[C_HEADS * C_DV:], D_Q_HEADS, HEAD_DIM).astype(BF16)
    gn = gn_w.astype(F32).reshape(1, -1)
    tile_in = lambda w: pl.BlockSpec((1, tm, w), lambda b, t: (b, t + skip, 0))
    tile_out = pl.BlockSpec((1, tm, d), lambda b, t: (b, t, 0))
    mod = lambda chunk: pl.BlockSpec((1, 1, d), lambda b, t: ((rows + b) * 6 + chunk, 0, 0))
    return pl.pallas_call(
        functools.partial(_out1_kernel, d_model=d, n_heads=C_HEADS),
        grid=(nb, seq // tm),
        in_specs=[tile_in(o_f.shape[2]), tile_in(o_b.shape[2]), tile_in(g_r.shape[2]), _full(gn),
                  tile_in(o_d.shape[2]), _full(wor), _full(wod), tile_in(d), mod(2), _full(norm2), mod(3), mod(4)],
        out_specs=[tile_out, tile_out],
        out_shape=[jax.ShapeDtypeStruct((nb, seq, d), F32), jax.ShapeDtypeStruct((nb, seq, d), BF16)],
        compiler_params=_cparams("arbitrary", "arbitrary"),
        name="out_proj1",
    )(o_f, o_b, g_r, gn, o_d, wor, wod, x_all, mods, norm2, mods, mods)


def kernel(x, c, ctx, c_ctx, ada_w, ada_b, norm1_w, norm2_w, ab_w_in, ab_w_o, a_q_norm, a_k_norm, a_sink,
           b_q_lora_norm, b_kv_lora_norm, b_w_uq, b_w_ukv, b_q_norm, b_k_norm, cd_w_in, cd_w_o, c_decay_fwd,
           c_decay_bwd, c_gn_w, d_q_norm, d_k_norm, peer_w_q, peer_keys, peer_u, peer_v):
    nb, seq, d = x.shape
    ctx_len = ctx.shape[1]
    total = ctx_len + seq
    assert ctx_len == TOK_TILE and seq % TOK_TILE == 0 and seq % GRID_W == 0

    rows = -(-(nb + 1) // SUBLANE) * SUBLANE
    c_rows = jnp.concatenate([c, c_ctx[None, :], jnp.zeros((rows - nb - 1, d), c.dtype)], axis=0).astype(F32)
    mods = _ada(c_rows, ada_w, ada_b).reshape(-1, 1, d)

    tabs_a, half_a = _rope_tables(ctx_len, seq, 0, HEAD_DIM)
    tabs_b, half_b = _rope_tables(ctx_len, seq, B_NOPE, B_ROPE)
    n1 = norm1_w.astype(F32).reshape(-1, 1, d)
    n2 = norm2_w.astype(F32).reshape(-1, 1, d)

    x_all = jnp.concatenate([ctx, x], axis=1).astype(F32)
    qa, ka, va, qb, kb, vb = _proj0(x_all, mods, rows, n1[0], ab_w_in[0], b_w_uq[0], b_w_ukv[0], a_q_norm[0],
                                    a_k_norm[0], b_q_lora_norm[0], b_kv_lora_norm[0], b_q_norm[0], b_k_norm[0],
                                    tabs_a, tabs_b, half_a, half_b)
    o_a = _attn_window(qa, ka, va, a_sink[0], ctx_len=ctx_len)
    o_b = _attn_dense(qb, kb, vb, ctx_len=ctx_len)
    x_all, h2 = _out0(o_a, o_b, ab_w_o[0], x_all, mods, rows, n2[0])
    h2 = h2.reshape(nb * total, d)
    sel = _peer_select(h2, peer_w_q[0], peer_keys[0])
    f = _peer_apply(h2, sel, peer_u[0], peer_v[0]).reshape(nb, total, d)

    x_all, qr, kr, vr, gr, qd, kd, vd = _proj1(x_all, f, mods, rows, n1[1], cd_w_in[0], d_q_norm[0], d_k_norm[0],
                                               tabs_a, half_a)
    lg = jnp.stack([jax.nn.log_sigmoid(c_decay_fwd[0].astype(F32)), jax.nn.log_sigmoid(c_decay_bwd[0].astype(F32))])
    o_f, o_bw = _retention(qr, kr, vr, lg, ctx_len)
    o_d = _attn_dense(qd, kd, vd, ctx_len=ctx_len)
    x_lat, h2 = _out1(o_f, o_bw, gr, c_gn_w[0], o_d, cd_w_o[0], x_all, mods, rows, n2[1], ctx_len)
    h2 = h2.reshape(nb * seq, d)
    sel = _peer_select(h2, peer_w_q[1], peer_keys[1])
    out = _peer_apply(h2, sel, peer_u[1], peer_v[1], x=x_lat.reshape(nb * seq, d), mods=mods,
                      mod_index=lambda b: (rows + b) * 6 + 5, tokens_per_batch=seq)
    return out.reshape(nb, seq, d).astype(x.dtype)
```

```python
import functools

import numpy as np
import jax
import jax.numpy as jnp
from jax import lax
from jax.experimental import pallas as pl
from jax.experimental.pallas import tpu as pltpu

F32 = jnp.float32
BF16 = jnp.bfloat16

GRID_W = 64
ROPE_THETA = 10000.0
EPS = 1e-6
NEG_INF = -1e30
HEAD_DIM = 64
A_Q_HEADS, A_KV_HEADS, A_WINDOW = 8, 2, 128
B_HEADS, B_NOPE, B_ROPE, B_V, B_Q_RANK, B_KV_RANK = 8, 64, 32, 64, 256, 256
B_QK = B_NOPE + B_ROPE
C_HEADS, C_DK, C_DV = 4, 64, 128
D_Q_HEADS, D_KV_HEADS = 8, 2
PEER_HEADS, PEER_N_KEYS, PEER_D_KEY, PEER_TOPK = 8, 128, 256, 16

LANE = 128
SUBLANE = 8
BF16_ROWS = 16
ONES_LANE = LANE - 1
VMEM_LIMIT = 56 * 1024 * 1024

TOK_TILE = 256
PROJ_TILE = 256
CHUNK = 128
ATTN_KV_PER_STEP = 2
ATTN_CHAINS_PER_STEP = 8
ADA_COL_TILE = 1536
RANK_ABSENT = 99.0
PEER_SEL_TILE = 256
SELECT_HEADS = 8
PEER_TOK_TILE = 512
PEER_EXP_TILE = 2048


def _cparams(*sem):
    return pltpu.CompilerParams(dimension_semantics=sem, vmem_limit_bytes=VMEM_LIMIT)


def _full(arr):
    nd = arr.ndim
    return pl.BlockSpec(arr.shape, lambda *_: (0,) * nd)


def _pad_cols(w, n_heads, d):
    lead = w.shape[:-1]
    w = w.reshape(lead + (n_heads, d))
    w = jnp.pad(w, [(0, 0)] * len(lead) + [(0, 0), (0, LANE - d)])
    return w.reshape(lead + (n_heads * LANE,))


def _pad_rows(w, n_heads, d):
    n = w.shape[-1]
    w = w.reshape(n_heads, d, n)
    w = jnp.pad(w, [(0, 0), (0, LANE - d), (0, 0)])
    return w.reshape(n_heads * LANE, n)


def _pad_gain(g, d):
    return jnp.pad(g.astype(F32), (0, LANE - d)).reshape(1, LANE)


def _rope_tables(ctx_len, seq, lane_off, d_rot):
    blk = d_rot // 2
    half = blk // 2
    freqs = ROPE_THETA ** (-np.arange(half, dtype=np.float64) / half)
    pos = np.arange(seq)
    total = ctx_len + seq
    cos = np.ones((total, LANE), np.float64)
    sup = np.zeros((total, LANE), np.float64)
    sdn = np.zeros((total, LANE), np.float64)
    for axis, p in enumerate((pos // GRID_W, pos % GRID_W)):
        ang = p[:, None].astype(np.float64) * freqs[None, :]
        c, s = np.cos(ang), np.sin(ang)
        base = lane_off + axis * blk
        cos[ctx_len:, base:base + half] = c
        cos[ctx_len:, base + half:base + blk] = c
        sdn[ctx_len:, base:base + half] = -s
        sup[ctx_len:, base + half:base + blk] = s
    return (jnp.asarray(cos, F32), jnp.asarray(sup, F32), jnp.asarray(sdn, F32)), half


def _rms_rows(x, true_dim):
    return x * lax.rsqrt(jnp.sum(x * x, axis=-1, keepdims=True) * (1.0 / true_dim) + EPS)


def _rms_head(x, true_dim):
    ss = jnp.dot((x * x).astype(BF16), jnp.ones((LANE, LANE), BF16), preferred_element_type=F32)
    return x * lax.rsqrt(ss * (1.0 / true_dim) + EPS)


def _rope(y, cos, sup, sdn, half):
    return y * cos + pltpu.roll(y, half, 1) * sup + pltpu.roll(y, LANE - half, 1) * sdn


def _ada_kernel(c_ref, w_ref, b_ref, o_ref):
    c = c_ref[...]
    s = c * jax.nn.sigmoid(c)
    o_ref[0] = jnp.dot(s.astype(BF16), w_ref[0].astype(BF16), preferred_element_type=F32) + b_ref[0]


def _ada(c_rows, ada_w, ada_b):
    depth, d, n = ada_w.shape
    rows = c_rows.shape[0]
    tn = ADA_COL_TILE
    assert n % tn == 0
    return pl.pallas_call(
        _ada_kernel,
        grid=(depth, n // tn),
        in_specs=[pl.BlockSpec((rows, d), lambda l, j: (0, 0)),
                  pl.BlockSpec((1, d, tn), lambda l, j: (l, 0, j)),
                  pl.BlockSpec((1, 1, tn), lambda l, j: (l, 0, j))],
        out_specs=pl.BlockSpec((1, rows, tn), lambda l, j: (l, 0, j)),
        out_shape=jax.ShapeDtypeStruct((depth, rows, n), F32),
        compiler_params=_cparams("arbitrary", "arbitrary"),
        name="ada_mod",
    )(c_rows, ada_w, ada_b.reshape(depth, 1, n))


def _mod_spec(layer, chunk, rows, nb, d, ctx_tiles=1):
    def imap(b, t):
        r = jnp.where(t < ctx_tiles, nb, b)
        return ((layer * rows + r) * 6 + chunk, 0, 0)
    return pl.BlockSpec((1, 1, d), imap)


def _modulate(x, n_ref, sc_ref, sh_ref, d):
    return _rms_rows(x, d) * n_ref[...] * (1.0 + sc_ref[0]) + sh_ref[0]


def _proj0_kernel(x_ref, n1_ref, sh_ref, sc_ref, w_ref, wuq_ref, wuk_ref, wuv_ref,
                  aqn_ref, akn_ref, bqln_ref, bkvln_ref, bqn_ref, bkn_ref,
                  ca_ref, ua_ref, da_ref, cb_ref, ub_ref, db_ref,
                  qa_ref, ka_ref, va_ref, qb_ref, kb_ref, vb_ref, *, d_model, half_a, half_b, offs):
    h = _modulate(x_ref[0], n1_ref, sc_ref, sh_ref, d_model).astype(BF16)
    ca, ua, da = ca_ref[...], ua_ref[...], da_ref[...]
    cb, ub, db = cb_ref[...], ub_ref[...], db_ref[...]
    o_qa, o_ka, o_va, o_cq, o_ckv, o_kr, o_end = offs

    z = jnp.dot(h, w_ref[:, o_qa:o_ka], preferred_element_type=F32)
    for i in range(A_Q_HEADS):
        y = _rms_head(z[:, i * LANE:(i + 1) * LANE], HEAD_DIM) * aqn_ref[...]
        qa_ref[0, :, i * LANE:(i + 1) * LANE] = (_rope(y, ca, ua, da, half_a) * HEAD_DIM ** -0.5).astype(BF16)
    z = jnp.dot(h, w_ref[:, o_ka:o_va], preferred_element_type=F32)
    for i in range(A_KV_HEADS):
        y = _rms_head(z[:, i * LANE:(i + 1) * LANE], HEAD_DIM) * akn_ref[...]
        ka_ref[0, :, i * LANE:(i + 1) * LANE] = _rope(y, ca, ua, da, half_a).astype(BF16)
    va = jnp.dot(h, w_ref[:, o_va:o_cq], preferred_element_type=F32)
    va_ref[0] = _with_ones_lane(va, A_KV_HEADS).astype(BF16)

    cq = jnp.dot(h, w_ref[:, o_cq:o_ckv], preferred_element_type=F32)
    cq = (_rms_rows(cq, B_Q_RANK) * bqln_ref[...]).astype(BF16)
    z = jnp.dot(cq, wuq_ref[...], preferred_element_type=F32)
    for i in range(B_HEADS):
        y = _rms_head(z[:, i * LANE:(i + 1) * LANE], B_QK) * bqn_ref[...]
        qb_ref[0, :, i * LANE:(i + 1) * LANE] = (_rope(y, cb, ub, db, half_b) * B_QK ** -0.5).astype(BF16)

    ckv = jnp.dot(h, w_ref[:, o_ckv:o_kr], preferred_element_type=F32)
    ckv = (_rms_rows(ckv, B_KV_RANK) * bkvln_ref[...]).astype(BF16)
    kr = jnp.dot(h, w_ref[:, o_kr:o_end], preferred_element_type=F32)
    z = jnp.dot(ckv, wuk_ref[...], preferred_element_type=F32)
    for i in range(B_HEADS):
        y = _rms_head(z[:, i * LANE:(i + 1) * LANE] + kr, B_QK) * bkn_ref[...]
        kb_ref[0, :, i * LANE:(i + 1) * LANE] = _rope(y, cb, ub, db, half_b).astype(BF16)
    vb = jnp.dot(ckv, wuv_ref[...], preferred_element_type=F32)
    vb_ref[0] = _with_ones_lane(vb, B_HEADS).astype(BF16)


def _proj0(x_all, mods, rows, norm1, w_in, b_wuq, b_wukv, a_qn, a_kn, b_qln, b_kvln, b_qn, b_kn, tabs_a, tabs_b,
           half_a, half_b):
    nb, total, d = x_all.shape
    tm = PROJ_TILE
    wq, wk, wv, wcq, wckv, wkr = jnp.split(
        w_in, np.cumsum([A_Q_HEADS * HEAD_DIM, A_KV_HEADS * HEAD_DIM, A_KV_HEADS * HEAD_DIM, B_Q_RANK, B_KV_RANK])
        .tolist(), axis=1)
    kr_pad = jnp.pad(wkr, ((0, 0), (B_NOPE, LANE - B_QK)))
    parts = [_pad_cols(wq, A_Q_HEADS, HEAD_DIM), _pad_cols(wk, A_KV_HEADS, HEAD_DIM),
             _pad_cols(wv, A_KV_HEADS, HEAD_DIM), wcq, wckv, kr_pad]
    offs = tuple(int(o) for o in np.cumsum([0] + [p.shape[1] for p in parts]))
    w_all = jnp.concatenate(parts, axis=1).astype(BF16)
    wuq = _pad_cols(b_wuq, B_HEADS, B_QK).astype(BF16)
    wukv = b_wukv.reshape(B_KV_RANK, B_HEADS, B_NOPE + B_V)
    wuk = _pad_cols(wukv[..., :B_NOPE].reshape(B_KV_RANK, -1), B_HEADS, B_NOPE).astype(BF16)
    wuv = _pad_cols(wukv[..., B_NOPE:].reshape(B_KV_RANK, -1), B_HEADS, B_V).astype(BF16)
    consts = [w_all, wuq, wuk, wuv, _pad_gain(a_qn, HEAD_DIM), _pad_gain(a_kn, HEAD_DIM),
              b_qln.astype(F32).reshape(1, -1), b_kvln.astype(F32).reshape(1, -1),
              _pad_gain(b_qn, B_QK), _pad_gain(b_kn, B_QK)]
    tab_spec = pl.BlockSpec((tm, LANE), lambda b, t: (t, 0))
    wide = lambda nh: pl.BlockSpec((1, tm, nh * LANE), lambda b, t: (b, t, 0))
    shp = lambda nh: jax.ShapeDtypeStruct((nb, total, nh * LANE), BF16)
    return pl.pallas_call(
        functools.partial(_proj0_kernel, d_model=d, half_a=half_a, half_b=half_b, offs=offs),
        grid=(nb, total // tm),
        in_specs=[pl.BlockSpec((1, tm, d), lambda b, t: (b, t, 0)), _full(norm1),
                  _mod_spec(0, 0, rows, nb, d, TOK_TILE // tm), _mod_spec(0, 1, rows, nb, d, TOK_TILE // tm)]
                 + [_full(c) for c in consts] + [tab_spec] * 6,
        out_specs=[wide(A_Q_HEADS), wide(A_KV_HEADS), wide(A_KV_HEADS), wide(B_HEADS), wide(B_HEADS), wide(B_HEADS)],
        out_shape=[shp(A_Q_HEADS), shp(A_KV_HEADS), shp(A_KV_HEADS), shp(B_HEADS), shp(B_HEADS), shp(B_HEADS)],
        compiler_params=_cparams("arbitrary", "arbitrary"),
        name="proj0",
    )(x_all, norm1, mods, mods, *consts, *tabs_a, *tabs_b)


_NT = (((1,), (1,)), ((), ()))


def _stack_heads(q_ref, grp, j=0):
    h0 = j * grp
    if grp == 1:
        return q_ref[0, :, h0 * LANE:(h0 + 1) * LANE]
    return jnp.concatenate([q_ref[0, :, (h0 + g) * LANE:(h0 + g + 1) * LANE] for g in range(grp)], axis=0)


def _softmax_av(scores, values, sink=None):
    m = None
    for s in scores:
        ms = jnp.max(s, axis=-1, keepdims=True)
        m = ms if m is None else jnp.maximum(m, ms)
    if sink is not None:
        m = jnp.maximum(m, sink)
    o = None
    for s, v in zip(scores, values):
        os_ = jnp.dot(jnp.exp(s - m).astype(BF16), v, preferred_element_type=F32)
        o = os_ if o is None else o + os_
    den = o[:, ONES_LANE:ONES_LANE + 1]
    if sink is not None:
        den = den + jnp.exp(sink - m)
    return o / den


def _with_ones_lane(v, n_heads):
    lane = lax.broadcasted_iota(jnp.int32, (1, n_heads * LANE), 1) % LANE
    return v + jnp.where(lane == ONES_LANE, 1.0, 0.0)


def _unstack_store(o, o_ref, grp, tq, j=0):
    for g in range(grp):
        h = j * grp + g
        o_ref[0, :, h * LANE:(h + 1) * LANE] = o[g * tq:(g + 1) * tq].astype(BF16)


def _attn_dense_kernel(q_ref, k_ref, v_ref, o_ref, *, grp, nq, ctx_len, tq):
    qt = pl.program_id(2)

    def chains(keys):
        for j in range(nq):
            qj = slice(j * LANE, (j + 1) * LANE)
            kj = slice((j // grp) * LANE, (j // grp + 1) * LANE)
            s = lax.dot_general(q_ref[0, :, qj], k_ref[0, keys, kj], _NT, preferred_element_type=F32)
            o_ref[0, :, qj] = _softmax_av([s], [v_ref[0, keys, kj]]).astype(BF16)

    @pl.when(qt * tq < ctx_len)
    def _():
        chains(slice(0, ctx_len))

    @pl.when(qt * tq >= ctx_len)
    def _():
        chains(slice(None))


def _attn_dense(q, k, v, *, ctx_len):
    nb, total, qw = q.shape
    hq, hkv = qw // LANE, k.shape[2] // LANE
    grp = hq // hkv
    nq = min(ATTN_CHAINS_PER_STEP, hq)
    nkv = max(1, nq // grp)
    tq = TOK_TILE
    assert ctx_len % tq == 0 and hq % nq == 0 and (nq % grp == 0 or grp % nq == 0)
    return pl.pallas_call(
        functools.partial(_attn_dense_kernel, grp=grp, nq=nq, ctx_len=ctx_len, tq=tq),
        grid=(nb, hq // nq, total // tq),
        in_specs=[pl.BlockSpec((1, tq, nq * LANE), lambda b, h, t: (b, t, h)),
                  pl.BlockSpec((1, total, nkv * LANE), lambda b, h, t: (b, 0, (h * nq // grp) // nkv)),
                  pl.BlockSpec((1, total, nkv * LANE), lambda b, h, t: (b, 0, (h * nq // grp) // nkv))],
        out_specs=pl.BlockSpec((1, tq, nq * LANE), lambda b, h, t: (b, t, h)),
        out_shape=jax.ShapeDtypeStruct(q.shape, BF16),
        compiler_params=_cparams("arbitrary", "arbitrary", "arbitrary"),
        name="attn_dense",
    )(q, k, v)


def _attn_window_kernel(q_ref, k_ref, v_ref, sink_ref, o_ref, *, grp, kvs, ctx_len, tq, total, window):
    qt = pl.program_id(2)

    def sink_col(j):
        return jnp.concatenate([jnp.broadcast_to(sink_ref[j * grp + g][:, 0:1], (tq, 1)) for g in range(grp)], axis=0)

    @pl.when(qt * tq < ctx_len)
    def _():
        for j in range(kvs):
            kj = slice(j * LANE, (j + 1) * LANE)
            s_c = lax.dot_general(_stack_heads(q_ref, grp, j), k_ref[0, 0:ctx_len, kj], _NT,
                                  preferred_element_type=F32)
            _unstack_store(_softmax_av([s_c], [v_ref[0, 0:ctx_len, kj]], sink_col(j)), o_ref, grp, tq, j)

    @pl.when(qt * tq >= ctx_len)
    def _():
        slab = 3 * tq
        start = pl.multiple_of(jnp.clip((qt - 1) * tq, ctx_len, total - slab), tq)
        qpos = qt * tq + lax.broadcasted_iota(jnp.int32, (tq, slab), 0)
        kpos = start + lax.broadcasted_iota(jnp.int32, (tq, slab), 1)
        bias = jnp.where(jnp.abs(qpos - kpos) <= window, 0.0, NEG_INF)
        bias = jnp.concatenate([bias] * grp, axis=0)
        for j in range(kvs):
            kj = slice(j * LANE, (j + 1) * LANE)
            q = _stack_heads(q_ref, grp, j)
            s_c = lax.dot_general(q, k_ref[0, 0:ctx_len, kj], _NT, preferred_element_type=F32)
            s_l = lax.dot_general(q, k_ref[0, pl.ds(start, slab), kj], _NT, preferred_element_type=F32) + bias
            o = _softmax_av([s_c, s_l], [v_ref[0, 0:ctx_len, kj], v_ref[0, pl.ds(start, slab), kj]], sink_col(j))
            _unstack_store(o, o_ref, grp, tq, j)


def _attn_window(q, k, v, sink, *, ctx_len):
    nb, total, qw = q.shape
    hq, hkv = qw // LANE, k.shape[2] // LANE
    grp = hq // hkv
    kvs = ATTN_KV_PER_STEP
    tq = CHUNK
    assert hkv % kvs == 0
    sink_rows = jnp.broadcast_to(sink.astype(F32).reshape(hq, 1, 1), (hq, 1, LANE))
    return pl.pallas_call(
        functools.partial(_attn_window_kernel, grp=grp, kvs=kvs, ctx_len=ctx_len, tq=tq, total=total,
                          window=A_WINDOW),
        grid=(nb, hkv // kvs, total // tq),
        in_specs=[pl.BlockSpec((1, tq, kvs * grp * LANE), lambda b, h, t: (b, t, h)),
                  pl.BlockSpec((1, total, kvs * LANE), lambda b, h, t: (b, 0, h)),
                  pl.BlockSpec((1, total, kvs * LANE), lambda b, h, t: (b, 0, h)),
                  pl.BlockSpec((kvs * grp, 1, LANE), lambda b, h, t: (h, 0, 0))],
        out_specs=pl.BlockSpec((1, tq, kvs * grp * LANE), lambda b, h, t: (b, t, h)),
        out_shape=jax.ShapeDtypeStruct(q.shape, BF16),
        compiler_params=_cparams("arbitrary", "arbitrary", "arbitrary"),
        name="attn_window",
    )(q, k, v, sink_rows)


def _out0_kernel(oa_ref, ob_ref, woa_ref, wob_ref, x_ref, g1_ref, n2_ref, sh2_ref, sc2_ref, xn_ref, h2_ref, *, d_model):
    y = (jnp.dot(oa_ref[0], woa_ref[...], preferred_element_type=F32)
         + jnp.dot(ob_ref[0], wob_ref[...], preferred_element_type=F32))
    xn = x_ref[0] + g1_ref[0] * y
    xn_ref[0] = xn
    h2_ref[0] = _modulate(xn, n2_ref, sc2_ref, sh2_ref, d_model).astype(BF16)


def _out0(oa, ob, w_o, x_all, mods, rows, norm2):
    nb, total, d = x_all.shape
    tm = TOK_TILE
    woa = _pad_rows(w_o[:A_Q_HEADS * HEAD_DIM], A_Q_HEADS, HEAD_DIM).astype(BF16)
    wob = _pad_rows(w_o[A_Q_HEADS * HEAD_DIM:], B_HEADS, B_V).astype(BF16)
    tile = lambda w: pl.BlockSpec((1, tm, w), lambda b, t: (b, t, 0))
    return pl.pallas_call(
        functools.partial(_out0_kernel, d_model=d),
        grid=(nb, total // tm),
        in_specs=[tile(oa.shape[2]), tile(ob.shape[2]), _full(woa), _full(wob), tile(d),
                  _mod_spec(0, 2, rows, nb, d), _full(norm2), _mod_spec(0, 3, rows, nb, d),
                  _mod_spec(0, 4, rows, nb, d)],
        out_specs=[tile(d), tile(d)],
        out_shape=[jax.ShapeDtypeStruct((nb, total, d), F32), jax.ShapeDtypeStruct((nb, total, d), BF16)],
        compiler_params=_cparams("arbitrary", "arbitrary"),
        name="out_proj0",
    )(oa, ob, woa, wob, x_all, mods, norm2, mods, mods)


def _top_rows(sc, rowf, k):
    n = sc.shape[0]
    vals, idxs = [], []
    work = sc
    for _ in range(k):
        m = jnp.max(work, axis=0, keepdims=True)
        idx = jnp.min(jnp.where(work == m, rowf, float(n)), axis=0, keepdims=True)
        vals.append(m)
        idxs.append(idx)
        work = jnp.where(rowf == idx, -jnp.inf, work)
    return vals, idxs


def _stack_rows(rows, row16):
    out = jnp.zeros(row16.shape, F32)
    for k, r in enumerate(rows):
        out = jnp.where(row16 == float(k), r, out)
    return out


def _candidates(v0, s1, slab_rows):
    return jnp.concatenate([v0[k1] + s1[0:slab_rows[k1], :] for k1 in range(len(v0))], axis=0)


def _select_exact(sc0, sc1, rowf, row16, flat, slab_rows, topk):
    nk, ts = sc0.shape
    v0, i0 = _top_rows(sc0, rowf, topk)
    v1, i1 = _top_rows(sc1, rowf, topk)
    work = _candidates(v0, _stack_rows(v1, row16), slab_rows)
    cnt = jnp.zeros((topk, ts), F32)
    zsum = jnp.zeros((1, ts), F32)
    best0 = None
    for k in range(topk):
        m = jnp.max(work, axis=0, keepdims=True)
        idx = jnp.min(jnp.where(work == m, flat, float(topk * topk)), axis=0, keepdims=True)
        work = jnp.where(flat == idx, -jnp.inf, work)
        best0 = m if best0 is None else best0
        zsum = zsum + jnp.exp(m - best0)
        cnt = cnt + jnp.where(row16 == jnp.floor(idx * (1.0 / topk)), 1.0, 0.0)
    cc = jnp.zeros((nk, ts), F32)
    rb = jnp.full((nk, ts), RANK_ABSENT, F32)
    for k in range(topk):
        ck = jnp.sum(jnp.where(row16 == float(k), cnt, 0.0), axis=0, keepdims=True)
        cc = jnp.where(rowf == i0[k], ck, cc)
        rb = jnp.where(rowf == i1[k], float(k), rb)
    return cc, rb, zsum


def _select_fast(sc0, sc1, row16, slab_rows, topk):
    nk, ts = sc0.shape
    ninf = -jnp.inf
    count = lambda hit: jnp.sum(jnp.where(hit, 1.0, 0.0), axis=0, keepdims=True)
    work, v0 = sc0, []
    for _ in range(topk):
        m = jnp.max(work, axis=0, keepdims=True)
        v0.append(m)
        work = jnp.where(work == m, ninf, work)
    bad = count(work == ninf) != float(topk)
    work, v1 = sc1, []
    rb = jnp.full((nk, ts), RANK_ABSENT, F32)
    for k in range(topk):
        m = jnp.max(work, axis=0, keepdims=True)
        v1.append(m)
        hit = work == m
        work = jnp.where(hit, ninf, work)
        rb = jnp.where(hit, float(k), rb)
    bad = jnp.logical_or(bad, count(rb < RANK_ABSENT) != float(topk))
    work = _candidates(v0, _stack_rows(v1, row16), slab_rows)
    zsum = jnp.zeros((1, ts), F32)
    best0 = None
    for _ in range(topk):
        m = jnp.max(work, axis=0, keepdims=True)
        work = jnp.where(work == m, ninf, work)
        best0 = m if best0 is None else best0
        zsum = zsum + jnp.exp(m - best0)
    chosen = jnp.where(work == ninf, 1.0, 0.0)
    cc = jnp.zeros((nk, ts), F32)
    total = jnp.zeros((1, ts), F32)
    off = 0
    for k1 in range(topk):
        ck = jnp.sum(chosen[off:off + slab_rows[k1], :], axis=0, keepdims=True)
        off += slab_rows[k1]
        total = total + ck
        cc = jnp.where(sc0 == v0[k1], ck, cc)
    bad = jnp.logical_or(bad, total != float(topk))
    return cc, rb, zsum, jnp.max(jnp.where(bad, 1.0, 0.0))


def _peer_select_kernel(h_ref, wq_ref, keys_ref, cc_ref, e0_ref, rb_ref, e1_ref, q_sc, *, n_heads, topk):
    nk = PEER_N_KEYS
    ts = h_ref.shape[0]
    q_sc[...] = lax.dot_general(wq_ref[...], h_ref[...], (((1,), (1,)), ((), ())), preferred_element_type=F32)
    rowf = lax.broadcasted_iota(jnp.int32, (nk, ts), 0).astype(F32)
    row16 = lax.broadcasted_iota(jnp.int32, (topk, ts), 0).astype(F32)
    slab_rows = [topk] + [SUBLANE] * (topk - 1)
    n_cand = sum(slab_rows)
    ci = lax.broadcasted_iota(jnp.int32, (n_cand, ts), 0)
    rest = ci - topk
    flat = jnp.where(ci < topk, ci, (1 + (rest >> 3)) * topk + (rest & 7)).astype(F32)

    def scores(hp):
        qhp = q_sc[pl.ds(pl.multiple_of(hp * nk, nk), nk), :].astype(BF16)
        return jnp.dot(keys_ref[hp], qhp, preferred_element_type=F32)

    def heads_body(it, carry):
        heads = [it * SELECT_HEADS + u for u in range(SELECT_HEADS)]
        scs = [(scores(hd * 2), scores(hd * 2 + 1)) for hd in heads]
        fast = [_select_fast(sc0, sc1, row16, slab_rows, topk) for sc0, sc1 in scs]
        for hd, (sc0, sc1), (cc, rb, zsum, tie) in zip(heads, scs, fast):
            cc, rb, zsum = lax.cond(tie > 0.0,
                                    lambda: _select_exact(sc0, sc1, rowf, row16, flat, slab_rows, topk),
                                    lambda: (cc, rb, zsum))
            cc_ref[0, hd] = cc
            rb_ref[0, hd] = rb.astype(BF16)
            e0_ref[0, hd] = jnp.exp(sc0 - jnp.max(sc0, axis=0, keepdims=True))
            e1_ref[0, hd] = (jnp.exp(sc1 - jnp.max(sc1, axis=0, keepdims=True)) / zsum).astype(BF16)
        return carry

    lax.fori_loop(0, n_heads // SELECT_HEADS, heads_body, 0)


def _peer_select(h2, w_q, keys):
    t, d = h2.shape
    ts = PEER_SEL_TILE
    nh, nk = PEER_HEADS, PEER_N_KEYS
    wq_t = w_q.T.astype(BF16)
    keys2 = keys.reshape(nh * 2, nk, PEER_D_KEY // 2).astype(BF16)
    row_out = jax.ShapeDtypeStruct((t // ts, nh, nk, ts), F32)
    col_out = jax.ShapeDtypeStruct((t // ts, nh, nk, ts), BF16)
    ospec = pl.BlockSpec((1, nh, nk, ts), lambda i: (i, 0, 0, 0))
    return pl.pallas_call(
        functools.partial(_peer_select_kernel, n_heads=nh, topk=PEER_TOPK),
        grid=(t // ts,),
        in_specs=[pl.BlockSpec((ts, d), lambda i: (i, 0)), _full(wq_t), _full(keys2)],
        out_specs=[ospec] * 4,
        out_shape=[row_out, row_out, col_out, col_out],
        scratch_shapes=[pltpu.VMEM((wq_t.shape[0], ts), F32)],
        compiler_params=_cparams("arbitrary"),
        name="peer_select",
    )(h2, wq_t, keys2)


def _peer_apply_kernel(*refs, n_heads, final):
    if final:
        (h_ref, u_hbm, vt_hbm, cc_ref, e0_ref, rb_ref, e1_ref, x_ref, g_ref, o_ref,
         acc_ref, g_sc, p_sc, ht_sc, u_buf, vt_buf, sem) = refs
    else:
        (h_ref, u_hbm, vt_hbm, cc_ref, e0_ref, rb_ref, e1_ref, o_ref,
         acc_ref, g_sc, p_sc, ht_sc, u_buf, vt_buf, sem) = refs
    tp, et, hh = pl.program_id(0), pl.program_id(1), pl.program_id(2)
    nk = PEER_N_KEYS
    rows_per_tile = cc_ref.shape[2]
    tw = cc_ref.shape[3]
    tok = pl.ds(pl.multiple_of(hh * tw, tw), tw)
    te = u_buf.shape[1]
    n_et = pl.num_programs(1) - 1
    last = n_et

    def u_copy(tile):
        return pltpu.make_async_copy(u_hbm.at[pl.ds(pl.multiple_of(tile * te, te), te), :], u_buf.at[tile % 2],
                                     sem.at[0, tile % 2])

    def vt_copy(tile):
        return pltpu.make_async_copy(vt_hbm.at[tile], vt_buf.at[tile % 2], sem.at[1, tile % 2])

    @pl.when(hh == 0)
    def _():
        @pl.when(jnp.logical_and(tp == 0, et == 0))
        def _():
            u_copy(0).start()

        @pl.when(et < n_et)
        def _():
            u_copy(et).wait()

        @pl.when(et >= 1)
        def _():
            vt_copy(et - 1).wait()

        @pl.when(et + 1 < n_et)
        def _():
            u_copy(et + 1).start()

        @pl.when(et < n_et)
        def _():
            vt_copy(et).start()

        @pl.when(jnp.logical_and(et == n_et, tp + 1 < pl.num_programs(0)))
        def _():
            u_copy(0).start()

    @pl.when(et == 0)
    def _():
        ht_sc[hh] = h_ref[tok, :].astype(F32).T.astype(BF16)

    def stage1():
        return jnp.dot(u_buf[et % 2], ht_sc[hh], preferred_element_type=F32)

    def stage1_store(at):
        g_sc[hh * 2 + et % 2] = jax.nn.gelu(at.astype(BF16))

    def stage2():
        zero = jnp.zeros((), BF16)
        prev = hh * 2 + (et + 1) % 2
        for ii in range(rows_per_tile):
            ccr = [jnp.broadcast_to(cc_ref[hh, hd, ii:ii + 1, :], (BF16_ROWS, tw)).astype(BF16)
                   for hd in range(n_heads)]
            e0r = [jnp.broadcast_to(e0_ref[hh, hd, ii:ii + 1, :], (BF16_ROWS, tw)).astype(BF16)
                   for hd in range(n_heads)]
            for s0 in range(0, nk, BF16_ROWS):
                rws = slice(s0, s0 + BF16_ROWS)
                w = None
                for hd in range(n_heads):
                    term = jnp.where(rb_ref[hh, hd, rws, :] < ccr[hd], e1_ref[hh, hd, rws, :], zero) * e0r[hd]
                    w = term if w is None else w + term
                r0 = ii * nk + s0
                p_sc[r0:r0 + BF16_ROWS, :] = w * g_sc[prev, r0:r0 + BF16_ROWS, :]
        acc_ref[hh] += jnp.dot(vt_buf[(et + 1) % 2], p_sc[...], preferred_element_type=F32)

    @pl.when(et == 0)
    def _():
        acc_ref[hh] = jnp.zeros(acc_ref.shape[1:], F32)
        stage1_store(stage1())

    @pl.when(jnp.logical_and(et > 0, et < last))
    def _():
        at = stage1()
        stage2()
        stage1_store(at)

    @pl.when(et == last)
    def _():
        stage2()

    @pl.when(et == last)
    def _():
        f = acc_ref[hh].T
        if final:
            o_ref[tok, :] = x_ref[tok, :] + g_ref[0] * f
        else:
            o_ref[tok, :] = f


def _peer_apply(h2, sel, u_tab, v_tab, x=None, mods=None, mod_index=None, tokens_per_batch=None):
    t, d = h2.shape
    cc, e0, rb, e1 = sel
    nh, nk = PEER_HEADS, PEER_N_KEYS
    tt, te = PEER_TOK_TILE, PEER_EXP_TILE
    ti = te // nk
    n_et = u_tab.shape[0] // te
    u = u_tab.astype(BF16)
    vt = v_tab.reshape(n_et, te, d).transpose(0, 2, 1).astype(BF16)
    final = x is not None
    tw = cc.shape[3]
    halves = tt // tw
    assert tt % tw == 0 and tw == PEER_SEL_TILE
    prv = lambda e: jnp.maximum(e - 1, 0)
    row_spec = pl.BlockSpec((halves, nh, ti, tw), lambda i, e, hh: (i, 0, prv(e), 0))
    col_spec = pl.BlockSpec((halves, nh, nk, tw), lambda i, e, hh: (i, 0, 0, 0))
    in_specs = [pl.BlockSpec((tt, d), lambda i, e, hh: (i, 0)),
                pl.BlockSpec(memory_space=pl.ANY), pl.BlockSpec(memory_space=pl.ANY),
                row_spec, row_spec, col_spec, col_spec]
    args = [h2, u, vt, cc, e0, rb, e1]
    if final:
        per = tokens_per_batch // tt
        in_specs += [pl.BlockSpec((tt, d), lambda i, e, hh: (i, 0)),
                     pl.BlockSpec((1, 1, d), lambda i, e, hh: (mod_index(i // per), 0, 0))]
        args += [x, mods]
    return pl.pallas_call(
        functools.partial(_peer_apply_kernel, n_heads=nh, final=final),
        grid=(t // tt, n_et + 1, halves),
        in_specs=in_specs,
        out_specs=pl.BlockSpec((tt, d), lambda i, e, hh: (i, 0)),
        out_shape=jax.ShapeDtypeStruct((t, d), F32),
        scratch_shapes=[pltpu.VMEM((halves, d, tw), F32), pltpu.VMEM((halves * 2, te, tw), BF16),
                        pltpu.VMEM((te, tw), BF16), pltpu.VMEM((halves, d, tw), BF16),
                        pltpu.VMEM((2, te, d), BF16), pltpu.VMEM((2, d, te), BF16),
                        pltpu.SemaphoreType.DMA((2, 2))],
        compiler_params=_cparams("arbitrary", "arbitrary", "arbitrary"),
        name="peer_apply_final" if final else "peer_apply",
    )(*args)


def _proj1_kernel(x_ref, f_ref, g2_ref, n1_ref, sh_ref, sc_ref, w_ref, dqn_ref, dkn_ref, ca_ref, ua_ref, da_ref,
                  xn_ref, qr_ref, kr_ref, vr_ref, gr_ref, qd_ref, kd_ref, vd_ref, *, d_model, half_a, offs):
    xn = x_ref[0] + g2_ref[0] * f_ref[0]
    xn_ref[0] = xn
    h = _modulate(xn, n1_ref, sc_ref, sh_ref, d_model).astype(BF16)
    ca, ua, da = ca_ref[...], ua_ref[...], da_ref[...]
    o_qr, o_kr, o_vr, o_gr, o_qd, o_kd, o_vd, o_end = offs

    z = jnp.dot(h, w_ref[:, o_qr:o_kr], preferred_element_type=F32)
    for i in range(C_HEADS):
        qr_ref[0, :, i * LANE:(i + 1) * LANE] = _rope(z[:, i * LANE:(i + 1) * LANE], ca, ua, da, half_a).astype(BF16)
    z = jnp.dot(h, w_ref[:, o_kr:o_vr], preferred_element_type=F32) * (C_DK ** -0.5)
    for i in range(C_HEADS):
        kr_ref[0, :, i * LANE:(i + 1) * LANE] = _rope(z[:, i * LANE:(i + 1) * LANE], ca, ua, da, half_a).astype(BF16)
    vr_ref[0] = jnp.dot(h, w_ref[:, o_vr:o_gr], preferred_element_type=F32).astype(BF16)
    gr_ref[0] = jnp.dot(h, w_ref[:, o_gr:o_qd], preferred_element_type=F32).astype(BF16)
    z = jnp.dot(h, w_ref[:, o_qd:o_kd], preferred_element_type=F32)
    for i in range(D_Q_HEADS):
        y = _rms_rows(z[:, i * LANE:(i + 1) * LANE], HEAD_DIM) * dqn_ref[...]
        qd_ref[0, :, i * LANE:(i + 1) * LANE] = (_rope(y, ca, ua, da, half_a) * HEAD_DIM ** -0.5).astype(BF16)
    z = jnp.dot(h, w_ref[:, o_kd:o_vd], preferred_element_type=F32)
    for i in range(D_KV_HEADS):
        y = _rms_rows(z[:, i * LANE:(i + 1) * LANE], HEAD_DIM) * dkn_ref[...]
        kd_ref[0, :, i * LANE:(i + 1) * LANE] = _rope(y, ca, ua, da, half_a).astype(BF16)
    vd = jnp.dot(h, w_ref[:, o_vd:o_end], preferred_element_type=F32)
    vd_ref[0] = _with_ones_lane(vd, D_KV_HEADS).astype(BF16)


def _proj1(x_all, f_all, mods, rows, norm1, w_in, d_qn, d_kn, tabs_a, half_a):
    nb, total, d = x_all.shape
    tm = PROJ_TILE
    cqk, cv = C_HEADS * C_DK, C_HEADS * C_DV
    wqr, wkr, wvr, wgr, wqd, wkd, wvd = jnp.split(
        w_in, np.cumsum([cqk, cqk, cv, cv, D_Q_HEADS * HEAD_DIM, D_KV_HEADS * HEAD_DIM]).tolist(), axis=1)
    parts = [_pad_cols(wqr, C_HEADS, C_DK), _pad_cols(wkr, C_HEADS, C_DK), wvr, wgr,
             _pad_cols(wqd, D_Q_HEADS, HEAD_DIM), _pad_cols(wkd, D_KV_HEADS, HEAD_DIM),
             _pad_cols(wvd, D_KV_HEADS, HEAD_DIM)]
    offs = tuple(int(o) for o in np.cumsum([0] + [p.shape[1] for p in parts]))
    w_all = jnp.concatenate(parts, axis=1).astype(BF16)
    consts = [w_all, _pad_gain(d_qn, HEAD_DIM), _pad_gain(d_kn, HEAD_DIM)]
    tab_spec = pl.BlockSpec((tm, LANE), lambda b, t: (t, 0))
    tile = lambda w: pl.BlockSpec((1, tm, w), lambda b, t: (b, t, 0))
    widths = [p.shape[1] for p in parts]
    return pl.pallas_call(
        functools.partial(_proj1_kernel, d_model=d, half_a=half_a, offs=offs),
        grid=(nb, total // tm),
        in_specs=[tile(d), tile(d), _mod_spec(0, 5, rows, nb, d, TOK_TILE // tm), _full(norm1),
                  _mod_spec(1, 0, rows, nb, d, TOK_TILE // tm), _mod_spec(1, 1, rows, nb, d, TOK_TILE // tm)]
                 + [_full(c) for c in consts] + [tab_spec] * 3,
        out_specs=[tile(d)] + [tile(w) for w in widths],
        out_shape=[jax.ShapeDtypeStruct((nb, total, d), F32)]
                  + [jax.ShapeDtypeStruct((nb, total, w), BF16) for w in widths],
        compiler_params=_cparams("arbitrary", "arbitrary"),
        name="proj1",
    )(x_all, f_all, mods, norm1, mods, mods, *consts, *tabs_a)


def _retention_kernel(lg_ref, qf_ref, kf_ref, vf_ref, qb_ref, kb_ref, vb_ref, of_ref, ob_ref, st_ref, dec_ref, *,
                      n_heads):
    step = pl.program_id(1)
    c = CHUNK

    @pl.when(step == 0)
    def _():
        st_ref[...] = jnp.zeros_like(st_ref)
        ri = lax.broadcasted_iota(jnp.int32, (c, LANE), 0).astype(F32)
        diff = ri - lax.broadcasted_iota(jnp.int32, (c, LANE), 1).astype(F32)
        for d in range(2):
            for hd in range(n_heads):
                lg = lg_ref[d, hd]
                if d == 0:
                    dec_ref[d, hd, 0] = jnp.where(diff >= 0, jnp.exp(lg * jnp.maximum(diff, 0.0)), 0.0)
                    dec_ref[d, hd, 1] = jnp.exp(lg * (ri + 1.0))
                    dec_ref[d, hd, 2] = jnp.exp(lg * (c - 1.0 - ri))
                else:
                    dec_ref[d, hd, 0] = jnp.where(diff <= 0, jnp.exp(lg * jnp.maximum(-diff, 0.0)), 0.0)
                    dec_ref[d, hd, 1] = jnp.exp(lg * (c - ri))
                    dec_ref[d, hd, 2] = jnp.exp(lg * ri)

    for d, (q_ref, k_ref, v_ref, o_ref) in enumerate(((qf_ref, kf_ref, vf_ref, of_ref),
                                                      (qb_ref, kb_ref, vb_ref, ob_ref))):
        for hd in range(n_heads):
            sl = slice(hd * LANE, (hd + 1) * LANE)
            q, k, v = q_ref[0, :, sl], k_ref[0, :, sl], v_ref[0, :, sl]
            s = lax.dot_general(q, k, _NT, preferred_element_type=F32) * dec_ref[d, hd, 0]
            inner = jnp.dot(s.astype(BF16), v, preferred_element_type=F32)
            st = st_ref[d, hd]
            cross = jnp.dot(q, st.astype(BF16), preferred_element_type=F32) * dec_ref[d, hd, 1]
            o_ref[0, :, sl] = inner + cross
            kd_t = (k.astype(F32) * dec_ref[d, hd, 2]).T.astype(BF16)
            st_ref[d, hd] = st * jnp.exp(lg_ref[d, hd] * c) + jnp.dot(kd_t, v, preferred_element_type=F32)


def _retention(qr, kr, vr, lg, ctx_len):
    nb, total, w = qr.shape
    nh = w // LANE
    c = CHUNK
    nc, nctx = total // c, ctx_len // c
    fwd = pl.BlockSpec((1, c, w), lambda b, s: (b, s, 0))

    def bmap(b, s):
        return (b, jnp.where(s < nctx, nctx - 1 - s, nc - 1 - (s - nctx)), 0)

    bwd = pl.BlockSpec((1, c, w), bmap)
    out = jax.ShapeDtypeStruct((nb, total, w), F32)
    return pl.pallas_call(
        functools.partial(_retention_kernel, n_heads=nh),
        grid=(nb, nc),
        in_specs=[pl.BlockSpec(memory_space=pltpu.SMEM), fwd, fwd, fwd, bwd, bwd, bwd],
        out_specs=[fwd, bwd],
        out_shape=[out, out],
        scratch_shapes=[pltpu.VMEM((2, nh, LANE, LANE), F32), pltpu.VMEM((2, nh, 3, c, LANE), F32)],
        compiler_params=_cparams("arbitrary", "arbitrary"),
        name="retention",
    )(lg, qr, kr, vr, qr, kr, vr)


def _out1_kernel(of_ref, ob_ref, gr_ref, gn_ref, od_ref, wor_ref, wod_ref, x_ref, g1_ref, n2_ref, sh2_ref, sc2_ref,
                 xn_ref, h2_ref, *, d_model, n_heads):
    o = of_ref[0] + ob_ref[0]
    g = gr_ref[0].astype(F32)
    gate = g * jax.nn.sigmoid(g)
    gn = gn_ref[...]
    ys = []
    for hd in range(n_heads):
        sl = slice(hd * LANE, (hd + 1) * LANE)
        oh = o[:, sl]
        mu = jnp.mean(oh, axis=-1, keepdims=True)
        var = jnp.mean(jnp.square(oh - mu), axis=-1, keepdims=True)
        ys.append((gate[:, sl] * ((oh - mu) * lax.rsqrt(var + EPS) * gn[:, sl])).astype(BF16))
    y_ret = jnp.concatenate(ys, axis=1)
    y = (jnp.dot(y_ret, wor_ref[...], preferred_element_type=F32)
         + jnp.dot(od_ref[0], wod_ref[...], preferred_element_type=F32))
    xn = x_ref[0] + g1_ref[0] * y
    xn_ref[0] = xn
    h2_ref[0] = _modulate(xn, n2_ref, sc2_ref, sh2_ref, d_model).astype(BF16)


def _out1(o_f, o_b, g_r, gn_w, o_d, w_o, x_all, mods, rows, norm2, ctx_len):
    nb, total, d = x_all.shape
    tm = TOK_TILE
    skip = ctx_len // tm
    seq = total - ctx_len
    wor = w_o[:C_HEADS * C_DV].astype(BF16)
    wod = _pad_rows(w_o[C_HEADS * C_DV:], D_Q_HEADS, HEAD_DIM).astype(BF16)
    gn = gn_w.astype(F32).reshape(1, -1)
    tile_in = lambda w: pl.BlockSpec((1, tm, w), lambda b, t: (b, t + skip, 0))
    tile_out = pl.BlockSpec((1, tm, d), lambda b, t: (b, t, 0))
    mod = lambda chunk: pl.BlockSpec((1, 1, d), lambda b, t: ((rows + b) * 6 + chunk, 0, 0))
    return pl.pallas_call(
        functools.partial(_out1_kernel, d_model=d, n_heads=C_HEADS),
        grid=(nb, seq // tm),
        in_specs=[tile_in(o_f.shape[2]), tile_in(o_b.shape[2]), tile_in(g_r.shape[2]), _full(gn),
                  tile_in(o_d.shape[2]), _full(wor), _full(wod), tile_in(d), mod(2), _full(norm2), mod(3), mod(4)],
        out_specs=[tile_out, tile_out],
        out_shape=[jax.ShapeDtypeStruct((nb, seq, d), F32), jax.ShapeDtypeStruct((nb, seq, d), BF16)],
        compiler_params=_cparams("arbitrary", "arbitrary"),
        name="out_proj1",
    )(o_f, o_b, g_r, gn, o_d, wor, wod, x_all, mods, norm2, mods, mods)


def kernel(x, c, ctx, c_ctx, ada_w, ada_b, norm1_w, norm2_w, ab_w_in, ab_w_o, a_q_norm, a_k_norm, a_sink,
           b_q_lora_norm, b_kv_lora_norm, b_w_uq, b_w_ukv, b_q_norm, b_k_norm, cd_w_in, cd_w_o, c_decay_fwd,
           c_decay_bwd, c_gn_w, d_q_norm, d_k_norm, peer_w_q, peer_keys, peer_u, peer_v):
    nb, seq, d = x.shape
    ctx_len = ctx.shape[1]
    total = ctx_len + seq
    assert ctx_len == TOK_TILE and seq % TOK_TILE == 0 and seq % GRID_W == 0

    rows = -(-(nb + 1) // SUBLANE) * SUBLANE
    c_rows = jnp.concatenate([c, c_ctx[None, :], jnp.zeros((rows - nb - 1, d), c.dtype)], axis=0).astype(F32)
    mods = _ada(c_rows, ada_w, ada_b).reshape(-1, 1, d)

    tabs_a, half_a = _rope_tables(ctx_len, seq, 0, HEAD_DIM)
    tabs_b, half_b = _rope_tables(ctx_len, seq, B_NOPE, B_ROPE)
    n1 = norm1_w.astype(F32).reshape(-1, 1, d)
    n2 = norm2_w.astype(F32).reshape(-1, 1, d)

    x_all = jnp.concatenate([ctx, x], axis=1).astype(F32)
    qa, ka, va, qb, kb, vb = _proj0(x_all, mods, rows, n1[0], ab_w_in[0], b_w_uq[0], b_w_ukv[0], a_q_norm[0],
                                    a_k_norm[0], b_q_lora_norm[0], b_kv_lora_norm[0], b_q_norm[0], b_k_norm[0],
                                    tabs_a, tabs_b, half_a, half_b)
    o_a = _attn_window(qa, ka, va, a_sink[0], ctx_len=ctx_len)
    o_b = _attn_dense(qb, kb, vb, ctx_len=ctx_len)
    x_all, h2 = _out0(o_a, o_b, ab_w_o[0], x_all, mods, rows, n2[0])
    h2 = h2.reshape(nb * total, d)
    sel = _peer_select(h2, peer_w_q[0], peer_keys[0])
    f = _peer_apply(h2, sel, peer_u[0], peer_v[0]).reshape(nb, total, d)

    x_all, qr, kr, vr, gr, qd, kd, vd = _proj1(x_all, f, mods, rows, n1[1], cd_w_in[0], d_q_norm[0], d_k_norm[0],
                                               tabs_a, half_a)
    lg = jnp.stack([jax.nn.log_sigmoid(c_decay_fwd[0].astype(F32)), jax.nn.log_sigmoid(c_decay_bwd[0].astype(F32))])
    o_f, o_bw = _retention(qr, kr, vr, lg, ctx_len)
    o_d = _attn_dense(qd, kd, vd, ctx_len=ctx_len)
    x_lat, h2 = _out1(o_f, o_bw, gr, c_gn_w[0], o_d, cd_w_o[0], x_all, mods, rows, n2[1], ctx_len)
    h2 = h2.reshape(nb * seq, d)
    sel = _peer_select(h2, peer_w_q[1], peer_keys[1])
    out = _peer_apply(h2, sel, peer_u[1], peer_v[1], x=x_lat.reshape(nb * seq, d), mods=mods,
                      mod_index=lambda b: (rows + b) * 6 + 5, tokens_per_batch=seq)
    return out.reshape(nb, seq, d).astype(x.dtype)
```

```python
import functools

import numpy as np
import jax
import jax.numpy as jnp
from jax import lax
from jax.experimental import pallas as pl
from jax.experimental.pallas import tpu as pltpu

F32 = jnp.float32
BF16 = jnp.bfloat16

GRID_W = 64
ROPE_THETA = 10000.0
EPS = 1e-6
NEG_INF = -1e30
HEAD_DIM = 64
A_Q_HEADS, A_KV_HEADS, A_WINDOW = 8, 2, 128
B_HEADS, B_NOPE, B_ROPE, B_V, B_Q_RANK, B_KV_RANK = 8, 64, 32, 64, 256, 256
B_QK = B_NOPE + B_ROPE
C_HEADS, C_DK, C_DV = 4, 64, 128
D_Q_HEADS, D_KV_HEADS = 8, 2
PEER_HEADS, PEER_N_KEYS, PEER_D_KEY, PEER_TOPK = 8, 128, 256, 16

LANE = 128
SUBLANE = 8
BF16_ROWS = 16
ONES_LANE = LANE - 1
VMEM_LIMIT = 56 * 1024 * 1024

TOK_TILE = 256
PROJ_TILE = 256
CHUNK = 128
ATTN_KV_PER_STEP = 2
ATTN_CHAINS_PER_STEP = 8
ADA_COL_TILE = 1536
RANK_ABSENT = 99.0
PEER_SEL_TILE = 256
SELECT_HEADS = 4
PEER_TOK_TILE = 512
PEER_EXP_TILE = 2048


def _cparams(*sem):
    return pltpu.CompilerParams(dimension_semantics=sem, vmem_limit_bytes=VMEM_LIMIT)


def _full(arr):
    nd = arr.ndim
    return pl.BlockSpec(arr.shape, lambda *_: (0,) * nd)


def _pad_cols(w, n_heads, d):
    lead = w.shape[:-1]
    w = w.reshape(lead + (n_heads, d))
    w = jnp.pad(w, [(0, 0)] * len(lead) + [(0, 0), (0, LANE - d)])
    return w.reshape(lead + (n_heads * LANE,))


def _pad_rows(w, n_heads, d):
    n = w.shape[-1]
    w = w.reshape(n_heads, d, n)
    w = jnp.pad(w, [(0, 0), (0, LANE - d), (0, 0)])
    return w.reshape(n_heads * LANE, n)


def _pad_gain(g, d):
    return jnp.pad(g.astype(F32), (0, LANE - d)).reshape(1, LANE)


def _rope_tables(ctx_len, seq, lane_off, d_rot):
    blk = d_rot // 2
    half = blk // 2
    freqs = ROPE_THETA ** (-np.arange(half, dtype=np.float64) / half)
    pos = np.arange(seq)
    total = ctx_len + seq
    cos = np.ones((total, LANE), np.float64)
    sup = np.zeros((total, LANE), np.float64)
    sdn = np.zeros((total, LANE), np.float64)
    for axis, p in enumerate((pos // GRID_W, pos % GRID_W)):
        ang = p[:, None].astype(np.float64) * freqs[None, :]
        c, s = np.cos(ang), np.sin(ang)
        base = lane_off + axis * blk
        cos[ctx_len:, base:base + half] = c
        cos[ctx_len:, base + half:base + blk] = c
        sdn[ctx_len:, base:base + half] = -s
        sup[ctx_len:, base + half:base + blk] = s
    return (jnp.asarray(cos, F32), jnp.asarray(sup, F32), jnp.asarray(sdn, F32)), half


def _rms_rows(x, true_dim):
    return x * lax.rsqrt(jnp.sum(x * x, axis=-1, keepdims=True) * (1.0 / true_dim) + EPS)


def _rms_head(x, true_dim):
    ss = jnp.dot((x * x).astype(BF16), jnp.ones((LANE, LANE), BF16), preferred_element_type=F32)
    return x * lax.rsqrt(ss * (1.0 / true_dim) + EPS)


def _rope(y, cos, sup, sdn, half):
    return y * cos + pltpu.roll(y, half, 1) * sup + pltpu.roll(y, LANE - half, 1) * sdn


def _ada_kernel(c_ref, w_ref, b_ref, o_ref):
    c = c_ref[...]
    s = c * jax.nn.sigmoid(c)
    o_ref[0] = jnp.dot(s.astype(BF16), w_ref[0].astype(BF16), preferred_element_type=F32) + b_ref[0]


def _ada(c_rows, ada_w, ada_b):
    depth, d, n = ada_w.shape
    rows = c_rows.shape[0]
    tn = ADA_COL_TILE
    assert n % tn == 0
    return pl.pallas_call(
        _ada_kernel,
        grid=(depth, n // tn),
        in_specs=[pl.BlockSpec((rows, d), lambda l, j: (0, 0)),
                  pl.BlockSpec((1, d, tn), lambda l, j: (l, 0, j)),
                  pl.BlockSpec((1, 1, tn), lambda l, j: (l, 0, j))],
        out_specs=pl.BlockSpec((1, rows, tn), lambda l, j: (l, 0, j)),
        out_shape=jax.ShapeDtypeStruct((depth, rows, n), F32),
        compiler_params=_cparams("arbitrary", "arbitrary"),
        name="ada_mod",
    )(c_rows, ada_w, ada_b.reshape(depth, 1, n))


def _mod_spec(layer, chunk, rows, nb, d, ctx_tiles=1):
    def imap(b, t):
        r = jnp.where(t < ctx_tiles, nb, b)
        return ((layer * rows + r) * 6 + chunk, 0, 0)
    return pl.BlockSpec((1, 1, d), imap)


def _modulate(x, n_ref, sc_ref, sh_ref, d):
    return _rms_rows(x, d) * n_ref[...] * (1.0 + sc_ref[0]) + sh_ref[0]


def _proj0_kernel(x_ref, n1_ref, sh_ref, sc_ref, w_ref, wuq_ref, wuk_ref, wuv_ref,
                  aqn_ref, akn_ref, bqln_ref, bkvln_ref, bqn_ref, bkn_ref,
                  ca_ref, ua_ref, da_ref, cb_ref, ub_ref, db_ref,
                  qa_ref, ka_ref, va_ref, qb_ref, kb_ref, vb_ref, *, d_model, half_a, half_b, offs):
    h = _modulate(x_ref[0], n1_ref, sc_ref, sh_ref, d_model).astype(BF16)
    ca, ua, da = ca_ref[...], ua_ref[...], da_ref[...]
    cb, ub, db = cb_ref[...], ub_ref[...], db_ref[...]
    o_qa, o_ka, o_va, o_cq, o_ckv, o_kr, o_end = offs

    z = jnp.dot(h, w_ref[:, o_qa:o_ka], preferred_element_type=F32)
    for i in range(A_Q_HEADS):
        y = _rms_head(z[:, i * LANE:(i + 1) * LANE], HEAD_DIM) * aqn_ref[...]
        qa_ref[0, :, i * LANE:(i + 1) * LANE] = (_rope(y, ca, ua, da, half_a) * HEAD_DIM ** -0.5).astype(BF16)
    z = jnp.dot(h, w_ref[:, o_ka:o_va], preferred_element_type=F32)
    for i in range(A_KV_HEADS):
        y = _rms_head(z[:, i * LANE:(i + 1) * LANE], HEAD_DIM) * akn_ref[...]
        ka_ref[0, :, i * LANE:(i + 1) * LANE] = _rope(y, ca, ua, da, half_a).astype(BF16)
    va = jnp.dot(h, w_ref[:, o_va:o_cq], preferred_element_type=F32)
    va_ref[0] = _with_ones_lane(va, A_KV_HEADS).astype(BF16)

    cq = jnp.dot(h, w_ref[:, o_cq:o_ckv], preferred_element_type=F32)
    cq = (_rms_rows(cq, B_Q_RANK) * bqln_ref[...]).astype(BF16)
    z = jnp.dot(cq, wuq_ref[...], preferred_element_type=F32)
    for i in range(B_HEADS):
        y = _rms_head(z[:, i * LANE:(i + 1) * LANE], B_QK) * bqn_ref[...]
        qb_ref[0, :, i * LANE:(i + 1) * LANE] = (_rope(y, cb, ub, db, half_b) * B_QK ** -0.5).astype(BF16)

    ckv = jnp.dot(h, w_ref[:, o_ckv:o_kr], preferred_element_type=F32)
    ckv = (_rms_rows(ckv, B_KV_RANK) * bkvln_ref[...]).astype(BF16)
    kr = jnp.dot(h, w_ref[:, o_kr:o_end], preferred_element_type=F32)
    z = jnp.dot(ckv, wuk_ref[...], preferred_element_type=F32)
    for i in range(B_HEADS):
        y = _rms_head(z[:, i * LANE:(i + 1) * LANE] + kr, B_QK) * bkn_ref[...]
        kb_ref[0, :, i * LANE:(i + 1) * LANE] = _rope(y, cb, ub, db, half_b).astype(BF16)
    vb = jnp.dot(ckv, wuv_ref[...], preferred_element_type=F32)
    vb_ref[0] = _with_ones_lane(vb, B_HEADS).astype(BF16)


def _proj0(x_all, mods, rows, norm1, w_in, b_wuq, b_wukv, a_qn, a_kn, b_qln, b_kvln, b_qn, b_kn, tabs_a, tabs_b,
           half_a, half_b):
    nb, total, d = x_all.shape
    tm = PROJ_TILE
    wq, wk, wv, wcq, wckv, wkr = jnp.split(
        w_in, np.cumsum([A_Q_HEADS * HEAD_DIM, A_KV_HEADS * HEAD_DIM, A_KV_HEADS * HEAD_DIM, B_Q_RANK, B_KV_RANK])
        .tolist(), axis=1)
    kr_pad = jnp.pad(wkr, ((0, 0), (B_NOPE, LANE - B_QK)))
    parts = [_pad_cols(wq, A_Q_HEADS, HEAD_DIM), _pad_cols(wk, A_KV_HEADS, HEAD_DIM),
             _pad_cols(wv, A_KV_HEADS, HEAD_DIM), wcq, wckv, kr_pad]
    offs = tuple(int(o) for o in np.cumsum([0] + [p.shape[1] for p in parts]))
    w_all = jnp.concatenate(parts, axis=1).astype(BF16)
    wuq = _pad_cols(b_wuq, B_HEADS, B_QK).astype(BF16)
    wukv = b_wukv.reshape(B_KV_RANK, B_HEADS, B_NOPE + B_V)
    wuk = _pad_cols(wukv[..., :B_NOPE].reshape(B_KV_RANK, -1), B_HEADS, B_NOPE).astype(BF16)
    wuv = _pad_cols(wukv[..., B_NOPE:].reshape(B_KV_RANK, -1), B_HEADS, B_V).astype(BF16)
    consts = [w_all, wuq, wuk, wuv, _pad_gain(a_qn, HEAD_DIM), _pad_gain(a_kn, HEAD_DIM),
              b_qln.astype(F32).reshape(1, -1), b_kvln.astype(F32).reshape(1, -1),
              _pad_gain(b_qn, B_QK), _pad_gain(b_kn, B_QK)]
    tab_spec = pl.BlockSpec((tm, LANE), lambda b, t: (t, 0))
    wide = lambda nh: pl.BlockSpec((1, tm, nh * LANE), lambda b, t: (b, t, 0))
    shp = lambda nh: jax.ShapeDtypeStruct((nb, total, nh * LANE), BF16)
    return pl.pallas_call(
        functools.partial(_proj0_kernel, d_model=d, half_a=half_a, half_b=half_b, offs=offs),
        grid=(nb, total // tm),
        in_specs=[pl.BlockSpec((1, tm, d), lambda b, t: (b, t, 0)), _full(norm1),
                  _mod_spec(0, 0, rows, nb, d, TOK_TILE // tm), _mod_spec(0, 1, rows, nb, d, TOK_TILE // tm)]
                 + [_full(c) for c in consts] + [tab_spec] * 6,
        out_specs=[wide(A_Q_HEADS), wide(A_KV_HEADS), wide(A_KV_HEADS), wide(B_HEADS), wide(B_HEADS), wide(B_HEADS)],
        out_shape=[shp(A_Q_HEADS), shp(A_KV_HEADS), shp(A_KV_HEADS), shp(B_HEADS), shp(B_HEADS), shp(B_HEADS)],
        compiler_params=_cparams("arbitrary", "arbitrary"),
        name="proj0",
    )(x_all, norm1, mods, mods, *consts, *tabs_a, *tabs_b)


_NT = (((1,), (1,)), ((), ()))


def _stack_heads(q_ref, grp, j=0):
    h0 = j * grp
    if grp == 1:
        return q_ref[0, :, h0 * LANE:(h0 + 1) * LANE]
    return jnp.concatenate([q_ref[0, :, (h0 + g) * LANE:(h0 + g + 1) * LANE] for g in range(grp)], axis=0)


def _softmax_av(scores, values, sink=None):
    m = None
    for s in scores:
        ms = jnp.max(s, axis=-1, keepdims=True)
        m = ms if m is None else jnp.maximum(m, ms)
    if sink is not None:
        m = jnp.maximum(m, sink)
    o = None
    for s, v in zip(scores, values):
        os_ = jnp.dot(jnp.exp(s - m).astype(BF16), v, preferred_element_type=F32)
        o = os_ if o is None else o + os_
    den = o[:, ONES_LANE:ONES_LANE + 1]
    if sink is not None:
        den = den + jnp.exp(sink - m)
    return o / den


def _with_ones_lane(v, n_heads):
    lane = lax.broadcasted_iota(jnp.int32, (1, n_heads * LANE), 1) % LANE
    return v + jnp.where(lane == ONES_LANE, 1.0, 0.0)


def _unstack_store(o, o_ref, grp, tq, j=0):
    for g in range(grp):
        h = j * grp + g
        o_ref[0, :, h * LANE:(h + 1) * LANE] = o[g * tq:(g + 1) * tq].astype(BF16)


def _attn_dense_kernel(q_ref, k_ref, v_ref, o_ref, *, grp, nq, ctx_len, tq):
    qt = pl.program_id(2)

    def chains(keys):
        for j in range(nq):
            qj = slice(j * LANE, (j + 1) * LANE)
            kj = slice((j // grp) * LANE, (j // grp + 1) * LANE)
            s = lax.dot_general(q_ref[0, :, qj], k_ref[0, keys, kj], _NT, preferred_element_type=F32)
            o_ref[0, :, qj] = _softmax_av([s], [v_ref[0, keys, kj]]).astype(BF16)

    @pl.when(qt * tq < ctx_len)
    def _():
        chains(slice(0, ctx_len))

    @pl.when(qt * tq >= ctx_len)
    def _():
        chains(slice(None))


def _attn_dense(q, k, v, *, ctx_len):
    nb, total, qw = q.shape
    hq, hkv = qw // LANE, k.shape[2] // LANE
    grp = hq // hkv
    nq = min(ATTN_CHAINS_PER_STEP, hq)
    nkv = max(1, nq // grp)
    tq = TOK_TILE
    assert ctx_len % tq == 0 and hq % nq == 0 and (nq % grp == 0 or grp % nq == 0)
    return pl.pallas_call(
        functools.partial(_attn_dense_kernel, grp=grp, nq=nq, ctx_len=ctx_len, tq=tq),
        grid=(nb, hq // nq, total // tq),
        in_specs=[pl.BlockSpec((1, tq, nq * LANE), lambda b, h, t: (b, t, h)),
                  pl.BlockSpec((1, total, nkv * LANE), lambda b, h, t: (b, 0, (h * nq // grp) // nkv)),
                  pl.BlockSpec((1, total, nkv * LANE), lambda b, h, t: (b, 0, (h * nq // grp) // nkv))],
        out_specs=pl.BlockSpec((1, tq, nq * LANE), lambda b, h, t: (b, t, h)),
        out_shape=jax.ShapeDtypeStruct(q.shape, BF16),
        compiler_params=_cparams("arbitrary", "arbitrary", "arbitrary"),
        name="attn_dense",
    )(q, k, v)


def _attn_window_kernel(q_ref, k_ref, v_ref, sink_ref, o_ref, *, grp, kvs, ctx_len, tq, total, window):
    qt = pl.program_id(2)

    def sink_col(j):
        return jnp.concatenate([jnp.broadcast_to(sink_ref[j * grp + g][:, 0:1], (tq, 1)) for g in range(grp)], axis=0)

    @pl.when(qt * tq < ctx_len)
    def _():
        for j in range(kvs):
            kj = slice(j * LANE, (j + 1) * LANE)
            s_c = lax.dot_general(_stack_heads(q_ref, grp, j), k_ref[0, 0:ctx_len, kj], _NT,
                                  preferred_element_type=F32)
            _unstack_store(_softmax_av([s_c], [v_ref[0, 0:ctx_len, kj]], sink_col(j)), o_ref, grp, tq, j)

    @pl.when(qt * tq >= ctx_len)
    def _():
        slab = 3 * tq
        start = pl.multiple_of(jnp.clip((qt - 1) * tq, ctx_len, total - slab), tq)
        qpos = qt * tq + lax.broadcasted_iota(jnp.int32, (tq, slab), 0)
        kpos = start + lax.broadcasted_iota(jnp.int32, (tq, slab), 1)
        bias = jnp.where(jnp.abs(qpos - kpos) <= window, 0.0, NEG_INF)
        bias = jnp.concatenate([bias] * grp, axis=0)
        for j in range(kvs):
            kj = slice(j * LANE, (j + 1) * LANE)
            q = _stack_heads(q_ref, grp, j)
            s_c = lax.dot_general(q, k_ref[0, 0:ctx_len, kj], _NT, preferred_element_type=F32)
            s_l = lax.dot_general(q, k_ref[0, pl.ds(start, slab), kj], _NT, preferred_element_type=F32) + bias
            o = _softmax_av([s_c, s_l], [v_ref[0, 0:ctx_len, kj], v_ref[0, pl.ds(start, slab), kj]], sink_col(j))
            _unstack_store(o, o_ref, grp, tq, j)


def _attn_window(q, k, v, sink, *, ctx_len):
    nb, total, qw = q.shape
    hq, hkv = qw // LANE, k.shape[2] // LANE
    grp = hq // hkv
    kvs = ATTN_KV_PER_STEP
    tq = CHUNK
    assert hkv % kvs == 0
    sink_rows = jnp.broadcast_to(sink.astype(F32).reshape(hq, 1, 1), (hq, 1, LANE))
    return pl.pallas_call(
        functools.partial(_attn_window_kernel, grp=grp, kvs=kvs, ctx_len=ctx_len, tq=tq, total=total,
                          window=A_WINDOW),
        grid=(nb, hkv // kvs, total // tq),
        in_specs=[pl.BlockSpec((1, tq, kvs * grp * LANE), lambda b, h, t: (b, t, h)),
                  pl.BlockSpec((1, total, kvs * LANE), lambda b, h, t: (b, 0, h)),
                  pl.BlockSpec((1, total, kvs * LANE), lambda b, h, t: (b, 0, h)),
                  pl.BlockSpec((kvs * grp, 1, LANE), lambda b, h, t: (h, 0, 0))],
        out_specs=pl.BlockSpec((1, tq, kvs * grp * LANE), lambda b, h, t: (b, t, h)),
        out_shape=jax.ShapeDtypeStruct(q.shape, BF16),
        compiler_params=_cparams("arbitrary", "arbitrary", "arbitrary"),
        name="attn_window",
    )(q, k, v, sink_rows)


def _out0_kernel(oa_ref, ob_ref, woa_ref, wob_ref, x_ref, g1_ref, n2_ref, sh2_ref, sc2_ref, xn_ref, h2_ref, *, d_model):
    y = (jnp.dot(oa_ref[0], woa_ref[...], preferred_element_type=F32)
         + jnp.dot(ob_ref[0], wob_ref[...], preferred_element_type=F32))
    xn = x_ref[0] + g1_ref[0] * y
    xn_ref[0] = xn
    h2_ref[0] = _modulate(xn, n2_ref, sc2_ref, sh2_ref, d_model).astype(BF16)


def _out0(oa, ob, w_o, x_all, mods, rows, norm2):
    nb, total, d = x_all.shape
    tm = TOK_TILE
    woa = _pad_rows(w_o[:A_Q_HEADS * HEAD_DIM], A_Q_HEADS, HEAD_DIM).astype(BF16)
    wob = _pad_rows(w_o[A_Q_HEADS * HEAD_DIM:], B_HEADS, B_V).astype(BF16)
    tile = lambda w: pl.BlockSpec((1, tm, w), lambda b, t: (b, t, 0))
    return pl.pallas_call(
        functools.partial(_out0_kernel, d_model=d),
        grid=(nb, total // tm),
        in_specs=[tile(oa.shape[2]), tile(ob.shape[2]), _full(woa), _full(wob), tile(d),
                  _mod_spec(0, 2, rows, nb, d), _full(norm2), _mod_spec(0, 3, rows, nb, d),
                  _mod_spec(0, 4, rows, nb, d)],
        out_specs=[tile(d), tile(d)],
        out_shape=[jax.ShapeDtypeStruct((nb, total, d), F32), jax.ShapeDtypeStruct((nb, total, d), BF16)],
        compiler_params=_cparams("arbitrary", "arbitrary"),
        name="out_proj0",
    )(oa, ob, woa, wob, x_all, mods, norm2, mods, mods)


def _top_rows(sc, rowf, k):
    n = sc.shape[0]
    vals, idxs = [], []
    work = sc
    for _ in range(k):
        m = jnp.max(work, axis=0, keepdims=True)
        idx = jnp.min(jnp.where(work == m, rowf, float(n)), axis=0, keepdims=True)
        vals.append(m)
        idxs.append(idx)
        work = jnp.where(rowf == idx, -jnp.inf, work)
    return vals, idxs


def _stack_rows(rows, row16):
    out = jnp.zeros(row16.shape, F32)
    for k, r in enumerate(rows):
        out = jnp.where(row16 == float(k), r, out)
    return out


def _candidates(v0, s1, slab_rows):
    return jnp.concatenate([v0[k1] + s1[0:slab_rows[k1], :] for k1 in range(len(v0))], axis=0)


def _select_exact(sc0, sc1, rowf, row16, flat, slab_rows, topk):
    nk, ts = sc0.shape
    v0, i0 = _top_rows(sc0, rowf, topk)
    v1, i1 = _top_rows(sc1, rowf, topk)
    work = _candidates(v0, _stack_rows(v1, row16), slab_rows)
    cnt = jnp.zeros((topk, ts), F32)
    zsum = jnp.zeros((1, ts), F32)
    best0 = None
    for k in range(topk):
        m = jnp.max(work, axis=0, keepdims=True)
        idx = jnp.min(jnp.where(work == m, flat, float(topk * topk)), axis=0, keepdims=True)
        work = jnp.where(flat == idx, -jnp.inf, work)
        best0 = m if best0 is None else best0
        zsum = zsum + jnp.exp(m - best0)
        cnt = cnt + jnp.where(row16 == jnp.floor(idx * (1.0 / topk)), 1.0, 0.0)
    cc = jnp.zeros((nk, ts), F32)
    rb = jnp.full((nk, ts), RANK_ABSENT, F32)
    for k in range(topk):
        ck = jnp.sum(jnp.where(row16 == float(k), cnt, 0.0), axis=0, keepdims=True)
        cc = jnp.where(rowf == i0[k], ck, cc)
        rb = jnp.where(rowf == i1[k], float(k), rb)
    return cc, rb, zsum


def _select_fast(sc0, sc1, row16, slab_rows, topk):
    nk, ts = sc0.shape
    ninf = -jnp.inf
    count = lambda hit: jnp.sum(jnp.where(hit, 1.0, 0.0), axis=0, keepdims=True)
    work, v0 = sc0, []
    for _ in range(topk):
        m = jnp.max(work, axis=0, keepdims=True)
        v0.append(m)
        work = jnp.where(work == m, ninf, work)
    bad = count(work == ninf) != float(topk)
    work, v1 = sc1, []
    rb = jnp.full((nk, ts), RANK_ABSENT, F32)
    for k in range(topk):
        m = jnp.max(work, axis=0, keepdims=True)
        v1.append(m)
        hit = work == m
        work = jnp.where(hit, ninf, work)
        rb = jnp.where(hit, float(k), rb)
    bad = jnp.logical_or(bad, count(rb < RANK_ABSENT) != float(topk))
    work = _candidates(v0, _stack_rows(v1, row16), slab_rows)
    zsum = jnp.zeros((1, ts), F32)
    best0 = None
    for _ in range(topk):
        m = jnp.max(work, axis=0, keepdims=True)
        work = jnp.where(work == m, ninf, work)
        best0 = m if best0 is None else best0
        zsum = zsum + jnp.exp(m - best0)
    chosen = jnp.where(work == ninf, 1.0, 0.0)
    cc = jnp.zeros((nk, ts), F32)
    total = jnp.zeros((1, ts), F32)
    off = 0
    for k1 in range(topk):
        ck = jnp.sum(chosen[off:off + slab_rows[k1], :], axis=0, keepdims=True)
        off += slab_rows[k1]
        total = total + ck
        cc = jnp.where(sc0 == v0[k1], ck, cc)
    bad = jnp.logical_or(bad, total != float(topk))
    return cc, rb, zsum, jnp.max(jnp.where(bad, 1.0, 0.0))


def _peer_select_kernel(h_ref, wq_ref, keys_ref, cc_ref, e0_ref, rb_ref, e1_ref, q_sc, *, n_heads, topk):
    nk = PEER_N_KEYS
    ts = h_ref.shape[0]
    q_sc[...] = lax.dot_general(wq_ref[...], h_ref[...], (((1,), (1,)), ((), ())), preferred_element_type=F32)
    rowf = lax.broadcasted_iota(jnp.int32, (nk, ts), 0).astype(F32)
    row16 = lax.broadcasted_iota(jnp.int32, (topk, ts), 0).astype(F32)
    slab_rows = [topk] + [SUBLANE] * (topk - 1)
    n_cand = sum(slab_rows)
    ci = lax.broadcasted_iota(jnp.int32, (n_cand, ts), 0)
    rest = ci - topk
    flat = jnp.where(ci < topk, ci, (1 + (rest >> 3)) * topk + (rest & 7)).astype(F32)

    def scores(hp):
        qhp = q_sc[pl.ds(pl.multiple_of(hp * nk, nk), nk), :].astype(BF16)
        return jnp.dot(keys_ref[hp], qhp, preferred_element_type=F32)

    def heads_body(it, carry):
        heads = [it * SELECT_HEADS + u for u in range(SELECT_HEADS)]
        scs = [(scores(hd * 2), scores(hd * 2 + 1)) for hd in heads]
        fast = [_select_fast(sc0, sc1, row16, slab_rows, topk) for sc0, sc1 in scs]
        for hd, (sc0, sc1), (cc, rb, zsum, tie) in zip(heads, scs, fast):
            cc, rb, zsum = lax.cond(tie > 0.0,
                                    lambda: _select_exact(sc0, sc1, rowf, row16, flat, slab_rows, topk),
                                    lambda: (cc, rb, zsum))
            cc_ref[0, hd] = cc
            rb_ref[0, hd] = rb.astype(BF16)
            e0_ref[0, hd] = jnp.exp(sc0 - jnp.max(sc0, axis=0, keepdims=True))
            e1_ref[0, hd] = (jnp.exp(sc1 - jnp.max(sc1, axis=0, keepdims=True)) / zsum).astype(BF16)
        return carry

    lax.fori_loop(0, n_heads // SELECT_HEADS, heads_body, 0)


def _peer_select(h2, w_q, keys):
    t, d = h2.shape
    ts = PEER_SEL_TILE
    nh, nk = PEER_HEADS, PEER_N_KEYS
    wq_t = w_q.T.astype(BF16)
    keys2 = keys.reshape(nh * 2, nk, PEER_D_KEY // 2).astype(BF16)
    row_out = jax.ShapeDtypeStruct((t // ts, nh, nk, ts), F32)
    col_out = jax.ShapeDtypeStruct((t // ts, nh, nk, ts), BF16)
    ospec = pl.BlockSpec((1, nh, nk, ts), lambda i: (i, 0, 0, 0))
    return pl.pallas_call(
        functools.partial(_peer_select_kernel, n_heads=nh, topk=PEER_TOPK),
        grid=(t // ts,),
        in_specs=[pl.BlockSpec((ts, d), lambda i: (i, 0)), _full(wq_t), _full(keys2)],
        out_specs=[ospec] * 4,
        out_shape=[row_out, row_out, col_out, col_out],
        scratch_shapes=[pltpu.VMEM((wq_t.shape[0], ts), F32)],
        compiler_params=_cparams("arbitrary"),
        name="peer_select",
    )(h2, wq_t, keys2)


def _peer_apply_kernel(*refs, n_heads, final):
    if final:
        (h_ref, u_hbm, vt_hbm, cc_ref, e0_ref, rb_ref, e1_ref, x_ref, g_ref, o_ref,
         acc_ref, g_sc, p_sc, ht_sc, u_buf, vt_buf, sem) = refs
    else:
        (h_ref, u_hbm, vt_hbm, cc_ref, e0_ref, rb_ref, e1_ref, o_ref,
         acc_ref, g_sc, p_sc, ht_sc, u_buf, vt_buf, sem) = refs
    tp, et, hh = pl.program_id(0), pl.program_id(1), pl.program_id(2)
    nk = PEER_N_KEYS
    rows_per_tile = cc_ref.shape[2]
    tw = cc_ref.shape[3]
    tok = pl.ds(pl.multiple_of(hh * tw, tw), tw)
    te = u_buf.shape[1]
    n_et = pl.num_programs(1) - 1
    last = n_et

    def u_copy(tile):
        return pltpu.make_async_copy(u_hbm.at[pl.ds(pl.multiple_of(tile * te, te), te), :], u_buf.at[tile % 2],
                                     sem.at[0, tile % 2])

    def vt_copy(tile):
        return pltpu.make_async_copy(vt_hbm.at[tile], vt_buf.at[tile % 2], sem.at[1, tile % 2])

    @pl.when(hh == 0)
    def _():
        @pl.when(jnp.logical_and(tp == 0, et == 0))
        def _():
            u_copy(0).start()

        @pl.when(et < n_et)
        def _():
            u_copy(et).wait()

        @pl.when(et >= 1)
        def _():
            vt_copy(et - 1).wait()

        @pl.when(et + 1 < n_et)
        def _():
            u_copy(et + 1).start()

        @pl.when(et < n_et)
        def _():
            vt_copy(et).start()

        @pl.when(jnp.logical_and(et == n_et, tp + 1 < pl.num_programs(0)))
        def _():
            u_copy(0).start()

    @pl.when(et == 0)
    def _():
        ht_sc[hh] = h_ref[tok, :].astype(F32).T.astype(BF16)

    def stage1():
        return jnp.dot(u_buf[et % 2], ht_sc[hh], preferred_element_type=F32)

    def stage1_store(at):
        g_sc[hh * 2 + et % 2] = jax.nn.gelu(at.astype(BF16))

    def stage2():
        zero = jnp.zeros((), BF16)
        prev = hh * 2 + (et + 1) % 2
        for ii in range(rows_per_tile):
            ccr = [jnp.broadcast_to(cc_ref[hh, hd, ii:ii + 1, :], (BF16_ROWS, tw)).astype(BF16)
                   for hd in range(n_heads)]
            e0r = [jnp.broadcast_to(e0_ref[hh, hd, ii:ii + 1, :], (BF16_ROWS, tw)).astype(BF16)
                   for hd in range(n_heads)]
            for s0 in range(0, nk, BF16_ROWS):
                rws = slice(s0, s0 + BF16_ROWS)
                w = None
                for hd in range(n_heads):
                    term = jnp.where(rb_ref[hh, hd, rws, :] < ccr[hd], e1_ref[hh, hd, rws, :], zero) * e0r[hd]
                    w = term if w is None else w + term
                r0 = ii * nk + s0
                p_sc[r0:r0 + BF16_ROWS, :] = w * g_sc[prev, r0:r0 + BF16_ROWS, :]
        acc_ref[hh] += jnp.dot(vt_buf[(et + 1) % 2], p_sc[...], preferred_element_type=F32)

    @pl.when(et == 0)
    def _():
        acc_ref[hh] = jnp.zeros(acc_ref.shape[1:], F32)
        stage1_store(stage1())

    @pl.when(jnp.logical_and(et > 0, et < last))
    def _():
        at = stage1()
        stage2()
        stage1_store(at)

    @pl.when(et == last)
    def _():
        stage2()

    @pl.when(et == last)
    def _():
        f = acc_ref[hh].T
        if final:
            o_ref[tok, :] = x_ref[tok, :] + g_ref[0] * f
        else:
            o_ref[tok, :] = f


def _peer_apply(h2, sel, u_tab, v_tab, x=None, mods=None, mod_index=None, tokens_per_batch=None):
    t, d = h2.shape
    cc, e0, rb, e1 = sel
    nh, nk = PEER_HEADS, PEER_N_KEYS
    tt, te = PEER_TOK_TILE, PEER_EXP_TILE
    ti = te // nk
    n_et = u_tab.shape[0] // te
    u = u_tab.astype(BF16)
    vt = v_tab.reshape(n_et, te, d).transpose(0, 2, 1).astype(BF16)
    final = x is not None
    tw = cc.shape[3]
    halves = tt // tw
    assert tt % tw == 0 and tw == PEER_SEL_TILE
    prv = lambda e: jnp.maximum(e - 1, 0)
    row_spec = pl.BlockSpec((halves, nh, ti, tw), lambda i, e, hh: (i, 0, prv(e), 0))
    col_spec = pl.BlockSpec((halves, nh, nk, tw), lambda i, e, hh: (i, 0, 0, 0))
    in_specs = [pl.BlockSpec((tt, d), lambda i, e, hh: (i, 0)),
                pl.BlockSpec(memory_space=pl.ANY), pl.BlockSpec(memory_space=pl.ANY),
                row_spec, row_spec, col_spec, col_spec]
    args = [h2, u, vt, cc, e0, rb, e1]
    if final:
        per = tokens_per_batch // tt
        in_specs += [pl.BlockSpec((tt, d), lambda i, e, hh: (i, 0)),
                     pl.BlockSpec((1, 1, d), lambda i, e, hh: (mod_index(i // per), 0, 0))]
        args += [x, mods]
    return pl.pallas_call(
        functools.partial(_peer_apply_kernel, n_heads=nh, final=final),
        grid=(t // tt, n_et + 1, halves),
        in_specs=in_specs,
        out_specs=pl.BlockSpec((tt, d), lambda i, e, hh: (i, 0)),
        out_shape=jax.ShapeDtypeStruct((t, d), F32),
        scratch_shapes=[pltpu.VMEM((halves, d, tw), F32), pltpu.VMEM((halves * 2, te, tw), BF16),
                        pltpu.VMEM((te, tw), BF16), pltpu.VMEM((halves, d, tw), BF16),
                        pltpu.VMEM((2, te, d), BF16), pltpu.VMEM((2, d, te), BF16),
                        pltpu.SemaphoreType.DMA((2, 2))],
        compiler_params=_cparams("arbitrary", "arbitrary", "arbitrary"),
        name="peer_apply_final" if final else "peer_apply",
    )(*args)


def _proj1_kernel(x_ref, f_ref, g2_ref, n1_ref, sh_ref, sc_ref, w_ref, dqn_ref, dkn_ref, ca_ref, ua_ref, da_ref,
                  xn_ref, qr_ref, kr_ref, vr_ref, gr_ref, qd_ref, kd_ref, vd_ref, *, d_model, half_a, offs):
    xn = x_ref[0] + g2_ref[0] * f_ref[0]
    xn_ref[0] = xn
    h = _modulate(xn, n1_ref, sc_ref, sh_ref, d_model).astype(BF16)
    ca, ua, da = ca_ref[...], ua_ref[...], da_ref[...]
    o_qr, o_kr, o_vr, o_gr, o_qd, o_kd, o_vd, o_end = offs

    z = jnp.dot(h, w_ref[:, o_qr:o_kr], preferred_element_type=F32)
    for i in range(C_HEADS):
        qr_ref[0, :, i * LANE:(i + 1) * LANE] = _rope(z[:, i * LANE:(i + 1) * LANE], ca, ua, da, half_a).astype(BF16)
    z = jnp.dot(h, w_ref[:, o_kr:o_vr], preferred_element_type=F32) * (C_DK ** -0.5)
    for i in range(C_HEADS):
        kr_ref[0, :, i * LANE:(i + 1) * LANE] = _rope(z[:, i * LANE:(i + 1) * LANE], ca, ua, da, half_a).astype(BF16)
    vr_ref[0] = jnp.dot(h, w_ref[:, o_vr:o_gr], preferred_element_type=F32).astype(BF16)
    gr_ref[0] = jnp.dot(h, w_ref[:, o_gr:o_qd], preferred_element_type=F32).astype(BF16)
    z = jnp.dot(h, w_ref[:, o_qd:o_kd], preferred_element_type=F32)
    for i in range(D_Q_HEADS):
        y = _rms_rows(z[:, i * LANE:(i + 1) * LANE], HEAD_DIM) * dqn_ref[...]
        qd_ref[0, :, i * LANE:(i + 1) * LANE] = (_rope(y, ca, ua, da, half_a) * HEAD_DIM ** -0.5).astype(BF16)
    z = jnp.dot(h, w_ref[:, o_kd:o_vd], preferred_element_type=F32)
    for i in range(D_KV_HEADS):
        y = _rms_rows(z[:, i * LANE:(i + 1) * LANE], HEAD_DIM) * dkn_ref[...]
        kd_ref[0, :, i * LANE:(i + 1) * LANE] = _rope(y, ca, ua, da, half_a).astype(BF16)
    vd = jnp.dot(h, w_ref[:, o_vd:o_end], preferred_element_type=F32)
    vd_ref[0] = _with_ones_lane(vd, D_KV_HEADS).astype(BF16)


def _proj1(x_all, f_all, mods, rows, norm1, w_in, d_qn, d_kn, tabs_a, half_a):
    nb, total, d = x_all.shape
    tm = PROJ_TILE
    cqk, cv = C_HEADS * C_DK, C_HEADS * C_DV
    wqr, wkr, wvr, wgr, wqd, wkd, wvd = jnp.split(
        w_in, np.cumsum([cqk, cqk, cv, cv, D_Q_HEADS * HEAD_DIM, D_KV_HEADS * HEAD_DIM]).tolist(), axis=1)
    parts = [_pad_cols(wqr, C_HEADS, C_DK), _pad_cols(wkr, C_HEADS, C_DK), wvr, wgr,
             _pad_cols(wqd, D_Q_HEADS, HEAD_DIM), _pad_cols(wkd, D_KV_HEADS, HEAD_DIM),
             _pad_cols(wvd, D_KV_HEADS, HEAD_DIM)]
    offs = tuple(int(o) for o in np.cumsum([0] + [p.shape[1] for p in parts]))
    w_all = jnp.concatenate(parts, axis=1).astype(BF16)
    consts = [w_all, _pad_gain(d_qn, HEAD_DIM), _pad_gain(d_kn, HEAD_DIM)]
    tab_spec = pl.BlockSpec((tm, LANE), lambda b, t: (t, 0))
    tile = lambda w: pl.BlockSpec((1, tm, w), lambda b, t: (b, t, 0))
    widths = [p.shape[1] for p in parts]
    return pl.pallas_call(
        functools.partial(_proj1_kernel, d_model=d, half_a=half_a, offs=offs),
        grid=(nb, total // tm),
        in_specs=[tile(d), tile(d), _mod_spec(0, 5, rows, nb, d, TOK_TILE // tm), _full(norm1),
                  _mod_spec(1, 0, rows, nb, d, TOK_TILE // tm), _mod_spec(1, 1, rows, nb, d, TOK_TILE // tm)]
                 + [_full(c) for c in consts] + [tab_spec] * 3,
        out_specs=[tile(d)] + [tile(w) for w in widths],
        out_shape=[jax.ShapeDtypeStruct((nb, total, d), F32)]
                  + [jax.ShapeDtypeStruct((nb, total, w), BF16) for w in widths],
        compiler_params=_cparams("arbitrary", "arbitrary"),
        name="proj1",
    )(x_all, f_all, mods, norm1, mods, mods, *consts, *tabs_a)


def _retention_kernel(lg_ref, qf_ref, kf_ref, vf_ref, qb_ref, kb_ref, vb_ref, of_ref, ob_ref, st_ref, dec_ref, *,
                      n_heads):
    step = pl.program_id(1)
    c = CHUNK

    @pl.when(step == 0)
    def _():
        st_ref[...] = jnp.zeros_like(st_ref)
        ri = lax.broadcasted_iota(jnp.int32, (c, LANE), 0).astype(F32)
        diff = ri - lax.broadcasted_iota(jnp.int32, (c, LANE), 1).astype(F32)
        for d in range(2):
            for hd in range(n_heads):
                lg = lg_ref[d, hd]
                if d == 0:
                    dec_ref[d, hd, 0] = jnp.where(diff >= 0, jnp.exp(lg * jnp.maximum(diff, 0.0)), 0.0)
                    dec_ref[d, hd, 1] = jnp.exp(lg * (ri + 1.0))
                    dec_ref[d, hd, 2] = jnp.exp(lg * (c - 1.0 - ri))
                else:
                    dec_ref[d, hd, 0] = jnp.where(diff <= 0, jnp.exp(lg * jnp.maximum(-diff, 0.0)), 0.0)
                    dec_ref[d, hd, 1] = jnp.exp(lg * (c - ri))
                    dec_ref[d, hd, 2] = jnp.exp(lg * ri)

    for d, (q_ref, k_ref, v_ref, o_ref) in enumerate(((qf_ref, kf_ref, vf_ref, of_ref),
                                                      (qb_ref, kb_ref, vb_ref, ob_ref))):
        for hd in range(n_heads):
            sl = slice(hd * LANE, (hd + 1) * LANE)
            q, k, v = q_ref[0, :, sl], k_ref[0, :, sl], v_ref[0, :, sl]
            s = lax.dot_general(q, k, _NT, preferred_element_type=F32) * dec_ref[d, hd, 0]
            inner = jnp.dot(s.astype(BF16), v, preferred_element_type=F32)
            st = st_ref[d, hd]
            cross = jnp.dot(q, st.astype(BF16), preferred_element_type=F32) * dec_ref[d, hd, 1]
            o_ref[0, :, sl] = inner + cross
            kd_t = (k.astype(F32) * dec_ref[d, hd, 2]).T.astype(BF16)
            st_ref[d, hd] = st * jnp.exp(lg_ref[d, hd] * c) + jnp.dot(kd_t, v, preferred_element_type=F32)


def _retention(qr, kr, vr, lg, ctx_len):
    nb, total, w = qr.shape
    nh = w // LANE
    c = CHUNK
    nc, nctx = total // c, ctx_len // c
    fwd = pl.BlockSpec((1, c, w), lambda b, s: (b, s, 0))

    def bmap(b, s):
        return (b, jnp.where(s < nctx, nctx - 1 - s, nc - 1 - (s - nctx)), 0)

    bwd = pl.BlockSpec((1, c, w), bmap)
    out = jax.ShapeDtypeStruct((nb, total, w), F32)
    return pl.pallas_call(
        functools.partial(_retention_kernel, n_heads=nh),
        grid=(nb, nc),
        in_specs=[pl.BlockSpec(memory_space=pltpu.SMEM), fwd, fwd, fwd, bwd, bwd, bwd],
        out_specs=[fwd, bwd],
        out_shape=[out, out],
        scratch_shapes=[pltpu.VMEM((2, nh, LANE, LANE), F32), pltpu.VMEM((2, nh, 3, c, LANE), F32)],
        compiler_params=_cparams("arbitrary", "arbitrary"),
        name="retention",
    )(lg, qr, kr, vr, qr, kr, vr)


def _out1_kernel(of_ref, ob_ref, gr_ref, gn_ref, od_ref, wor_ref, wod_ref, x_ref, g1_ref, n2_ref, sh2_ref, sc2_ref,
                 xn_ref, h2_ref, *, d_model, n_heads):
    o = of_ref[0] + ob_ref[0]
    g = gr_ref[0].astype(F32)
    gate = g * jax.nn.sigmoid(g)
    gn = gn_ref[...]
    ys = []
    for hd in range(n_heads):
        sl = slice(hd * LANE, (hd + 1) * LANE)
        oh = o[:, sl]
        mu = jnp.mean(oh, axis=-1, keepdims=True)
        var = jnp.mean(jnp.square(oh - mu), axis=-1, keepdims=True)
        ys.append((gate[:, sl] * ((oh - mu) * lax.rsqrt(var + EPS) * gn[:, sl])).astype(BF16))
    y_ret = jnp.concatenate(ys, axis=1)
    y = (jnp.dot(y_ret, wor_ref[...], preferred_element_type=F32)
         + jnp.dot(od_ref[0], wod_ref[...], preferred_element_type=F32))
    xn = x_ref[0] + g1_ref[0] * y
    xn_ref[0] = xn
    h2_ref[0] = _modulate(xn, n2_ref, sc2_ref, sh2_ref, d_model).astype(BF16)


def _out1(o_f, o_b, g_r, gn_w, o_d, w_o, x_all, mods, rows, norm2, ctx_len):
    nb, total, d = x_all.shape
    tm = TOK_TILE
    skip = ctx_len // tm
    seq = total - ctx_len
    wor = w_o[:C_HEADS * C_DV].astype(BF16)
    wod = _pad_rows(w_o[C_HEADS * C_DV:], D_Q_HEADS, HEAD_DIM).astype(BF16)
    gn = gn_w.astype(F32).reshape(1, -1)
    tile_in = lambda w: pl.BlockSpec((1, tm, w), lambda b, t: (b, t + skip, 0))
    tile_out = pl.BlockSpec((1, tm, d), lambda b, t: (b, t, 0))
    mod = lambda chunk: pl.BlockSpec((1, 1, d), lambda b, t: ((rows + b) * 6 + chunk, 0, 0))
    return pl.pallas_call(
        functools.partial(_out1_kernel, d_model=d, n_heads=C_HEADS),
        grid=(nb, seq // tm),
        in_specs=[tile_in(o_f.shape[2]), tile_in(o_b.shape[2]), tile_in(g_r.shape[2]), _full(gn),
                  tile_in(o_d.shape[2]), _full(wor), _full(wod), tile_in(d), mod(2), _full(norm2), mod(3), mod(4)],
        out_specs=[tile_out, tile_out],
        out_shape=[jax.ShapeDtypeStruct((nb, seq, d), F32), jax.ShapeDtypeStruct((nb, seq, d), BF16)],
        compiler_params=_cparams("arbitrary", "arbitrary"),
        name="out_proj1",
    )(o_f, o_b, g_r, gn, o_d, wor, wod, x_all, mods, norm2, mods, mods)


def kernel(x, c, ctx, c_ctx, ada_w, ada_b, norm1_w, norm2_w, ab_w_in, ab_w_o, a_q_norm, a_k_norm, a_sink,
           b_q_lora_norm, b_kv_lora_norm, b_w_uq, b_w_ukv, b_q_norm, b_k_norm, cd_w_in, cd_w_o, c_decay_fwd,
           c_decay_bwd, c_gn_w, d_q_norm, d_k_norm, peer_w_q, peer_keys, peer_u, peer_v):
    nb, seq, d = x.shape
    ctx_len = ctx.shape[1]
    total = ctx_len + seq
    assert ctx_len == TOK_TILE and seq % TOK_TILE == 0 and seq % GRID_W == 0

    rows = -(-(nb + 1) // SUBLANE) * SUBLANE
    c_rows = jnp.concatenate([c, c_ctx[None, :], jnp.zeros((rows - nb - 1, d), c.dtype)], axis=0).astype(F32)
    mods = _ada(c_rows, ada_w, ada_b).reshape(-1, 1, d)

    tabs_a, half_a = _rope_tables(ctx_len, seq, 0, HEAD_DIM)
    tabs_b, half_b = _rope_tables(ctx_len, seq, B_NOPE, B_ROPE)
    n1 = norm1_w.astype(F32).reshape(-1, 1, d)
    n2 = norm2_w.astype(F32).reshape(-1, 1, d)

    x_all = jnp.concatenate([ctx, x], axis=1).astype(F32)
    qa, ka, va, qb, kb, vb = _proj0(x_all, mods, rows, n1[0], ab_w_in[0], b_w_uq[0], b_w_ukv[0], a_q_norm[0],
                                    a_k_norm[0], b_q_lora_norm[0], b_kv_lora_norm[0], b_q_norm[0], b_k_norm[0],
                                    tabs_a, tabs_b, half_a, half_b)
    o_a = _attn_window(qa, ka, va, a_sink[0], ctx_len=ctx_len)
    o_b = _attn_dense(qb, kb, vb, ctx_len=ctx_len)
    x_all, h2 = _out0(o_a, o_b, ab_w_o[0], x_all, mods, rows, n2[0])
    h2 = h2.reshape(nb * total, d)
    sel = _peer_select(h2, peer_w_q[0], peer_keys[0])
    f = _peer_apply(h2, sel, peer_u[0], peer_v[0]).reshape(nb, total, d)

    x_all, qr, kr, vr, gr, qd, kd, vd = _proj1(x_all, f, mods, rows, n1[1], cd_w_in[0], d_q_norm[0], d_k_norm[0],
                                               tabs_a, half_a)
    lg = jnp.stack([jax.nn.log_sigmoid(c_decay_fwd[0].astype(F32)), jax.nn.log_sigmoid(c_decay_bwd[0].astype(F32))])
    o_f, o_bw = _retention(qr, kr, vr, lg, ctx_len)
    o_d = _attn_dense(qd, kd, vd, ctx_len=ctx_len)
    x_lat, h2 = _out1(o_f, o_bw, gr, c_gn_w[0], o_d, cd_w_o[0], x_all, mods, rows, n2[1], ctx_len)
    h2 = h2.reshape(nb * seq, d)
    sel = _peer_select(h2, peer_w_q[1], peer_keys[1])
    out = _peer_apply(h2, sel, peer_u[1], peer_v[1], x=x_lat.reshape(nb * seq, d), mods=mods,
                      mod_index=lambda b: (rows + b) * 6 + 5, tokens_per_batch=seq)
    return out.reshape(nb, seq, d).astype(x.dtype)
```
